```python
import math
import jax, jax.numpy as jnp
from jax import lax
import numpy as np

D_MODEL = 1024
BATCH = 8
SEQ = 4096
DEPTH = 2

ATTN_HEADS = 8
ATTN_HEAD_DIM = 64
ATTN_WIDTH = ATTN_HEADS * ATTN_HEAD_DIM
KV_RANK = 256
IDX_HEADS = 8
IDX_DIM = 64
TOPK_MAX = 256
ATTN_QBLOCK = 64
N_BUCKETS = 32
MAX_DISTANCE = 128
SSM_HEADS = 16
SSM_HEAD_DIM = 64
SSM_INNER = SSM_HEADS * SSM_HEAD_DIM
SSM_GROUPS = 2
SSM_STATE = 128
CONV_WIDTH = 4
CONV_CH = SSM_INNER + 2 * SSM_GROUPS * SSM_STATE
SSD_CHUNK = 128
N_EXPERTS = 32
TOP_K = 4
D_EXPERT = D_MODEL
SWIGLU_LIMIT = 7.0
SWIGLU_ALPHA = 1.702
MOE_BLOCK = 128
EPS = 1e-6

SPLIT_SIZES = (ATTN_WIDTH, KV_RANK, IDX_HEADS * IDX_DIM, IDX_DIM, IDX_HEADS, SSM_INNER, CONV_CH, SSM_HEADS, 2 * D_MODEL)
D_IN = sum(SPLIT_SIZES)

kernel_name = 'hybrid_dsa_ssd_moe_adaln'


def _rms(x, g, eps=EPS):
    xf = x.astype(jnp.float32)
    y = xf * lax.rsqrt(jnp.mean(xf * xf, axis=-1, keepdims=True) + eps)
    return (y * g.astype(jnp.float32)).astype(x.dtype)


def _layernorm(x, g, b, eps=EPS):
    xf = x.astype(jnp.float32)
    mu = jnp.mean(xf, axis=-1, keepdims=True)
    var = jnp.mean(jnp.square(xf - mu), axis=-1, keepdims=True)
    y = (xf - mu) * lax.rsqrt(var + eps) * g.astype(jnp.float32) + b.astype(jnp.float32)
    return y.astype(x.dtype)


def _t5_bucket(dist):
    n = jnp.maximum(dist, 0)
    max_exact = N_BUCKETS // 2
    nf = jnp.maximum(n, 1).astype(jnp.float32)
    large = max_exact + (jnp.log(nf / max_exact) / math.log(MAX_DISTANCE / max_exact) * (N_BUCKETS - max_exact)).astype(jnp.int32)
    large = jnp.minimum(large, N_BUCKETS - 1)
    return jnp.where(n < max_exact, n, large)


def _split(proj):
    idx = [int(i) for i in np.cumsum(SPLIT_SIZES)[:-1]]
    return jnp.split(proj, idx, axis=-1)


def _dsa_attention(q, k, v, q_idx, w_idx, k_idx, rel_bias):
    Bsz, L = q.shape[0], q.shape[1]
    topk = min(TOPK_MAX, L // 4)
    nblk = L // ATTN_QBLOCK
    scale = ATTN_HEAD_DIM ** -0.5
    key_pos = jnp.arange(L, dtype=jnp.int32)

    def to_blocks(a):
        return jnp.moveaxis(a.reshape(Bsz, nblk, ATTN_QBLOCK, *a.shape[2:]), 1, 0)

    def block(args):
        qb, qib, wib, start = args
        t = start + jnp.arange(ATTN_QBLOCK, dtype=jnp.int32)
        dots = jnp.einsum('bqhd,bsd->bqhs', qib, k_idx)
        score = jnp.einsum('bqh,bqhs->bqs', wib, jax.nn.relu(dots)).astype(jnp.float32)
        causal = key_pos[None, :] <= t[:, None]
        score = jnp.where(causal[None], score, -jnp.inf)
        _, sel = lax.top_k(score, topk)
        kg = jax.vmap(lambda kk, ii: kk[ii])(k, sel)
        vg = jax.vmap(lambda vv, ii: vv[ii])(v, sel)
        dist = t[None, :, None] - sel
        bias = jnp.moveaxis(rel_bias[_t5_bucket(dist)], -1, 2).astype(jnp.float32)
        logits = jnp.einsum('bqhd,bqkhd->bqhk', qb, kg).astype(jnp.float32) * scale + bias
        logits = jnp.where((dist >= 0)[:, :, None, :], logits, -jnp.inf)
        p = jax.nn.softmax(logits, axis=-1).astype(vg.dtype)
        return jnp.einsum('bqhk,bqkhd->bqhd', p, vg)

    starts = jnp.arange(nblk, dtype=jnp.int32) * ATTN_QBLOCK
    out = lax.map(block, (to_blocks(q), to_blocks(q_idx), to_blocks(w_idx), starts))
    return jnp.moveaxis(out, 0, 1).reshape(Bsz, L, ATTN_WIDTH)


def _mamba2_ssd(z, xbc, dt_raw, conv_w, conv_b, dt_bias, a_log, d_skip, norm_w):
    Bsz, L = xbc.shape[0], xbc.shape[1]
    nc = L // SSD_CHUNK
    hg = SSM_HEADS // SSM_GROUPS
    xbc = lax.conv_general_dilated(xbc, conv_w[:, None, :].astype(xbc.dtype), window_strides=(1,),
                                   padding=[(CONV_WIDTH - 1, 0)], dimension_numbers=('NWC', 'WIO', 'NWC'),
                                   feature_group_count=CONV_CH) + conv_b
    xbc = jax.nn.silu(xbc).astype(jnp.float32)
    xs, bm, cm = jnp.split(xbc, [SSM_INNER, SSM_INNER + SSM_GROUPS * SSM_STATE], axis=-1)
    dt = jax.nn.softplus(dt_raw.astype(jnp.float32) + dt_bias.astype(jnp.float32))
    a = -jnp.exp(a_log.astype(jnp.float32)).reshape(SSM_GROUPS, hg)

    def chunks(arr, *tail):
        return jnp.moveaxis(arr.reshape(Bsz, nc, SSD_CHUNK, *tail), 1, 0)

    x_c = chunks(xs, SSM_GROUPS, hg, SSM_HEAD_DIM)
    dt_c = chunks(dt, SSM_GROUPS, hg)
    b_c = chunks(bm, SSM_GROUPS, SSM_STATE)
    c_c = chunks(cm, SSM_GROUPS, SSM_STATE)
    causal = jnp.tril(jnp.ones((SSD_CHUNK, SSD_CHUNK), dtype=bool))[None, :, :, None, None]

    def step(state, inp):
        xc, dtc, bc, cc = inp
        acum = jnp.cumsum(dtc * a, axis=1)
        seg = acum[:, :, None] - acum[:, None, :]
        lmat = jnp.exp(jnp.where(causal, seg, -jnp.inf))
        cb = jnp.einsum('bign,bjgn->bijg', cc, bc)
        wmat = cb[..., None] * lmat * dtc[:, None]
        y = jnp.einsum('bijgh,bjghp->bighp', wmat, xc)
        y = y + jnp.einsum('bign,bghpn->bighp', cc, state) * jnp.exp(acum)[..., None]
        decay = jnp.exp(acum[:, -1:] - acum) * dtc
        state = state * jnp.exp(acum[:, -1])[..., None, None] + jnp.einsum('bjgn,bjgh,bjghp->bghpn', bc, decay, xc)
        return state, y

    h0 = jnp.zeros((Bsz, SSM_GROUPS, hg, SSM_HEAD_DIM, SSM_STATE), jnp.float32)
    _, ys = lax.scan(step, h0, (x_c, dt_c, b_c, c_c))
    y = jnp.moveaxis(ys, 0, 1).reshape(Bsz, L, SSM_HEADS, SSM_HEAD_DIM)
    y = y + d_skip.astype(jnp.float32)[:, None] * xs.reshape(Bsz, L, SSM_HEADS, SSM_HEAD_DIM)
    y = y.reshape(Bsz, L, SSM_INNER) * jax.nn.silu(z.astype(jnp.float32))
    yg = y.reshape(Bsz, L, SSM_GROUPS, SSM_INNER // SSM_GROUPS)
    yg = yg * lax.rsqrt(jnp.mean(yg * yg, axis=-1, keepdims=True) + EPS)
    y = yg.reshape(Bsz, L, SSM_INNER) * norm_w.astype(jnp.float32)
    return y.astype(z.dtype)


def _moe(h, w_router, b_router, w_gu, b_gu, w_dn, b_dn):
    Bsz, L, D = h.shape
    T = Bsz * L
    hf = h.reshape(T, D)
    logits = (hf @ w_router + b_router).astype(jnp.float32)
    top_val, top_idx = lax.top_k(logits, TOP_K)
    gates = jax.nn.softmax(top_val, axis=-1)
    n_assign = T * TOP_K
    e_flat = top_idx.reshape(n_assign)
    g_flat = gates.reshape(n_assign)
    tok_flat = jnp.arange(n_assign, dtype=jnp.int32) // TOP_K
    order = jnp.argsort(e_flat)
    e_sorted = e_flat[order]
    counts = jnp.bincount(e_flat, length=N_EXPERTS)
    padded = (counts + MOE_BLOCK - 1) // MOE_BLOCK * MOE_BLOCK
    pad_end = jnp.cumsum(padded)
    pad_start = pad_end - padded
    start = jnp.cumsum(counts) - counts
    rank = jnp.arange(n_assign, dtype=jnp.int32) - start[e_sorted]
    dest = pad_start[e_sorted] + rank
    n_rows = n_assign + N_EXPERTS * MOE_BLOCK
    row_tok = jnp.full((n_rows,), T, dtype=jnp.int32).at[dest].set(tok_flat[order])
    row_gate = jnp.zeros((n_rows,), jnp.float32).at[dest].set(g_flat[order])
    n_blk = n_rows // MOE_BLOCK
    blk_start = jnp.arange(n_blk, dtype=jnp.int32) * MOE_BLOCK
    blk_exp = jnp.minimum(jnp.sum(blk_start[:, None] >= pad_end[None, :], axis=1), N_EXPERTS - 1)
    h_pad = jnp.concatenate([hf, jnp.zeros((1, D), hf.dtype)], axis=0)

    def run_block(args):
        toks, e = args
        xb = h_pad[toks]
        gu = xb @ w_gu[e] + b_gu[e]
        g, u = gu[:, :D_EXPERT], gu[:, D_EXPERT:]
        g = jnp.minimum(g, SWIGLU_LIMIT)
        u = jnp.clip(u, -SWIGLU_LIMIT, SWIGLU_LIMIT)
        act = (u + 1.0) * (g * jax.nn.sigmoid(SWIGLU_ALPHA * g))
        return act @ w_dn[e] + b_dn[e]

    out = lax.map(run_block, (row_tok.reshape(n_blk, MOE_BLOCK), blk_exp))
    out = out.reshape(n_rows, D) * row_gate[:, None].astype(out.dtype)
    y = jax.ops.segment_sum(out, row_tok, num_segments=T + 1)[:T]
    return y.reshape(Bsz, L, D)


def setup_inputs(seed: int = 0) -> dict:
    key = jax.random.key(seed)
    ks = jax.random.split(key, 32)
    f32 = jnp.float32

    def nrm(k, shape, scale):
        return jax.random.normal(k, shape, f32) * scale

    def gain(k, shape):
        return 1.0 + 0.05 * jax.random.normal(k, shape, f32)

    dt = jnp.exp(jax.random.uniform(ks[17], (DEPTH, SSM_HEADS), f32, math.log(1e-3), math.log(1e-1)))
    return {
        'x': nrm(ks[0], (BATCH, SEQ, D_MODEL), 1.0),
        'c': nrm(ks[1], (BATCH, D_MODEL), 1.0),
        'rel_bias': nrm(ks[2], (N_BUCKETS, ATTN_HEADS), 0.5),
        'w_ada': nrm(ks[3], (DEPTH, D_MODEL, 6 * D_MODEL), 0.5 * D_MODEL ** -0.5),
        'b_ada': nrm(ks[4], (DEPTH, 6 * D_MODEL), 0.02),
        'norm_mix': gain(ks[5], (DEPTH, D_MODEL)),
        'norm_ffn': gain(ks[6], (DEPTH, D_MODEL)),
        'w_in': nrm(ks[7], (DEPTH, D_MODEL, D_IN), D_MODEL ** -0.5),
        'kv_norm': gain(ks[8], (DEPTH, KV_RANK)),
        'w_kv_up': nrm(ks[9], (DEPTH, KV_RANK, 2 * ATTN_WIDTH), KV_RANK ** -0.5),
        'q_norm': gain(ks[10], (DEPTH, ATTN_HEAD_DIM)),
        'k_norm': gain(ks[11], (DEPTH, ATTN_HEAD_DIM)),
        'idx_k_ln_w': gain(ks[12], (DEPTH, IDX_DIM)),
        'idx_k_ln_b': nrm(ks[13], (DEPTH, IDX_DIM), 0.02),
        'w_attn_o': nrm(ks[14], (DEPTH, ATTN_WIDTH, D_MODEL), ATTN_WIDTH ** -0.5),
        'conv_w': nrm(ks[15], (DEPTH, CONV_WIDTH, CONV_CH), CONV_WIDTH ** -0.5),
        'conv_b': nrm(ks[16], (DEPTH, CONV_CH), 0.02),
        'dt_bias': dt + jnp.log(-jnp.expm1(-dt)),
        'a_log': jnp.log(jax.random.uniform(ks[18], (DEPTH, SSM_HEADS), f32, 1.0, 16.0)),
        'd_skip': gain(ks[19], (DEPTH, SSM_HEADS)),
        'ssm_norm': gain(ks[20], (DEPTH, SSM_INNER)),
        'w_ssm_o': nrm(ks[21], (DEPTH, SSM_INNER, D_MODEL), SSM_INNER ** -0.5),
        'w_out': nrm(ks[22], (DEPTH, D_MODEL, D_MODEL), D_MODEL ** -0.5),
        'w_router': nrm(ks[23], (DEPTH, D_MODEL, N_EXPERTS), D_MODEL ** -0.5),
        'b_router': nrm(ks[24], (DEPTH, N_EXPERTS), 0.01),
        'w_gu': nrm(ks[25], (DEPTH, N_EXPERTS, D_MODEL, 2 * D_EXPERT), D_MODEL ** -0.5),
        'b_gu': nrm(ks[26], (DEPTH, N_EXPERTS, 2 * D_EXPERT), 0.02),
        'w_dn': nrm(ks[27], (DEPTH, N_EXPERTS, D_EXPERT, D_MODEL), D_EXPERT ** -0.5),
        'b_dn': nrm(ks[28], (DEPTH, N_EXPERTS, D_MODEL), 0.02),
    }


def reference(x, c, rel_bias, w_ada, b_ada, norm_mix, norm_ffn, w_in, kv_norm, w_kv_up, q_norm, k_norm,
              idx_k_ln_w, idx_k_ln_b, w_attn_o, conv_w, conv_b, dt_bias, a_log, d_skip, ssm_norm, w_ssm_o,
              w_out, w_router, b_router, w_gu, b_gu, w_dn, b_dn):
    Bsz, L, _ = x.shape
    cond = jax.nn.silu(c)
    for l in range(DEPTH):
        mod = cond @ w_ada[l] + b_ada[l]
        sh_m, sc_m, g_m, sh_f, sc_f, g_f = jnp.split(mod[:, None, :], 6, axis=-1)
        h = _rms(x, norm_mix[l]) * (1.0 + sc_m) + sh_m
        proj = h @ w_in[l]
        q, kv_lat, q_i, k_i, w_i, z, xbc, dt_raw, gate_logits = _split(proj)
        q = _rms(q.reshape(Bsz, L, ATTN_HEADS, ATTN_HEAD_DIM), q_norm[l])
        kv = _rms(kv_lat, kv_norm[l]) @ w_kv_up[l]
        k, v = jnp.split(kv, 2, axis=-1)
        k = _rms(k.reshape(Bsz, L, ATTN_HEADS, ATTN_HEAD_DIM), k_norm[l])
        v = v.reshape(Bsz, L, ATTN_HEADS, ATTN_HEAD_DIM)
        q_i = q_i.reshape(Bsz, L, IDX_HEADS, IDX_DIM) * (IDX_DIM ** -0.5)
        w_i = w_i * (IDX_HEADS ** -0.5)
        k_i = _layernorm(k_i, idx_k_ln_w[l], idx_k_ln_b[l])
        y_attn = _dsa_attention(q, k, v, q_i, w_i, k_i, rel_bias) @ w_attn_o[l]
        y_ssd = _mamba2_ssd(z, xbc, dt_raw, conv_w[l], conv_b[l], dt_bias[l], a_log[l], d_skip[l], ssm_norm[l]) @ w_ssm_o[l]
        g_attn, g_ssd = jnp.split(gate_logits, 2, axis=-1)
        mixed = jax.nn.sigmoid(g_attn) * y_attn + jax.nn.sigmoid(g_ssd) * y_ssd
        x = x + g_m * (mixed @ w_out[l])
        h = _rms(x, norm_ffn[l]) * (1.0 + sc_f) + sh_f
        x = x + g_f * _moe(h, w_router[l], b_router[l], w_gu[l], b_gu[l], w_dn[l], b_dn[l])
    return x
```

```python
import functools
import math

import jax
import jax.numpy as jnp
import numpy as np
from jax import lax
from jax.experimental import pallas as pl
from jax.experimental.pallas import tpu as pltpu

D_MODEL = 1024
DEPTH = 2
ATTN_HEADS = 8
ATTN_HEAD_DIM = 64
ATTN_WIDTH = ATTN_HEADS * ATTN_HEAD_DIM
KV_RANK = 256
IDX_HEADS = 8
IDX_DIM = 64
TOPK_MAX = 256
N_BUCKETS = 32
MAX_DISTANCE = 128
SSM_HEADS = 16
SSM_HEAD_DIM = 64
SSM_INNER = SSM_HEADS * SSM_HEAD_DIM
SSM_GROUPS = 2
SSM_STATE = 128
CONV_WIDTH = 4
CONV_CH = SSM_INNER + 2 * SSM_GROUPS * SSM_STATE
SSD_CHUNK = 128
N_EXPERTS = 32
TOP_K = 4
D_EXPERT = D_MODEL
SWIGLU_LIMIT = 7.0
SWIGLU_ALPHA = 1.702
EPS = 1e-6

COL_Q = 0
COL_KV = 512
COL_QI = 768
COL_SMALL = 1280
COL_XBC = 1536
COL_Z = 3072
COL_GATE = 4096
PROJ_COLS = 6144
PREP_COLS = 1408
SMALL_KI, SMALL_WI, SMALL_DT = 0, 64, 72

QB = 256
INT_MIN = -2 ** 31
NEG = -1e30
VMEM_LIMIT = 56 * 1024 * 1024
MOE_TM = 512
HIGHEST = lax.Precision.HIGHEST


def _pack_w_in(w):
    o = np.cumsum((0, ATTN_WIDTH, KV_RANK, IDX_HEADS * IDX_DIM, IDX_DIM, IDX_HEADS, SSM_INNER, CONV_CH, SSM_HEADS, 2 * D_MODEL))
    q, kv, qi, ki, wi, z, xbc, dt, gate = (w[:, int(o[n]):int(o[n + 1])] for n in range(9))
    zeros = lambda n: jnp.zeros((w.shape[0], n), w.dtype)
    small = jnp.concatenate([ki, wi, dt, zeros(128 - 88)], axis=1)
    packed = jnp.concatenate([q, kv, qi, small, zeros(COL_XBC - PREP_COLS), xbc, z, gate], axis=1)
    assert packed.shape[1] == PROJ_COLS
    return packed.astype(jnp.bfloat16)


def _in_proj_kernel(x_ref, g_ref, sc_ref, sh_ref, w_ref, o_ref, h_ref):
    @pl.when(pl.program_id(1) == 0)
    def _():
        x = x_ref[...]
        y = x * lax.rsqrt(jnp.mean(x * x, axis=-1, keepdims=True) + EPS) * g_ref[...]
        h_ref[...] = (y * (1.0 + sc_ref[0]) + sh_ref[0]).astype(jnp.bfloat16)
    o_ref[...] = jnp.dot(h_ref[...], w_ref[...], preferred_element_type=jnp.float32)


def _in_proj(x2, gain, sc, sh, w_packed, seq, tm=1024, tn=512):
    t, d = x2.shape
    per_b = seq // tm
    return pl.pallas_call(
        _in_proj_kernel,
        grid=(t // tm, PROJ_COLS // tn),
        in_specs=[pl.BlockSpec((tm, d), lambda i, j: (i, 0)),
                  pl.BlockSpec((1, d), lambda i, j: (0, 0)),
                  pl.BlockSpec((1, 1, d), lambda i, j: (i // per_b, 0, 0)),
                  pl.BlockSpec((1, 1, d), lambda i, j: (i // per_b, 0, 0)),
                  pl.BlockSpec((d, tn), lambda i, j: (0, j))],
        out_specs=pl.BlockSpec((tm, tn), lambda i, j: (i, j)),
        out_shape=jax.ShapeDtypeStruct((t, PROJ_COLS), jnp.float32),
        scratch_shapes=[pltpu.VMEM((tm, d), jnp.bfloat16)],
        compiler_params=pltpu.CompilerParams(dimension_semantics=("parallel", "arbitrary"),
                                             vmem_limit_bytes=VMEM_LIMIT),
        name="in_proj",
    )(x2, gain.reshape(1, d), sc[:, None, :], sh[:, None, :], w_packed)


def _head_rms_t(xt):
    x3 = xt.reshape(ATTN_HEADS, ATTN_HEAD_DIM, xt.shape[1])
    return lax.rsqrt(jnp.mean(x3 * x3, axis=1, keepdims=True) + EPS)


def _prep_kernel(p_ref, qg_ref, kvg_ref, wkv_ref, kg_ref, lng_ref, lnb_ref,
                 qT_ref, k_ref, vT_ref, qiT_ref, ki_ref, wT_ref):
    n = p_ref.shape[0]
    q = p_ref[:, COL_Q:COL_Q + ATTN_WIDTH]
    lat = p_ref[:, COL_KV:COL_KV + KV_RANK]
    qi = p_ref[:, COL_QI:COL_QI + IDX_HEADS * IDX_DIM]
    sm = p_ref[:, COL_SMALL:COL_SMALL + 128]

    scale = ATTN_HEAD_DIM ** -0.5
    qt = q.T
    qn = qt.reshape(ATTN_HEADS, ATTN_HEAD_DIM, n) * _head_rms_t(qt)
    qT_ref[0] = (qn.reshape(ATTN_WIDTH, n) * qg_ref[...] * scale).astype(jnp.bfloat16)

    latn = lat * lax.rsqrt(jnp.mean(lat * lat, axis=-1, keepdims=True) + EPS) * kvg_ref[...]
    kv = jnp.dot(latn.astype(jnp.bfloat16), wkv_ref[...], preferred_element_type=jnp.float32)
    kt = kv[:, :ATTN_WIDTH].T
    kn = (kt.reshape(ATTN_HEADS, ATTN_HEAD_DIM, n) * _head_rms_t(kt)).reshape(ATTN_WIDTH, n) * kg_ref[...]
    k_ref[0] = kn.T.astype(jnp.bfloat16)
    vT_ref[0] = kv[:, ATTN_WIDTH:].T.astype(jnp.bfloat16)

    qiT_ref[0] = (qi * (IDX_DIM ** -0.5)).T.astype(jnp.bfloat16)

    lane = lax.broadcasted_iota(jnp.int32, sm.shape, 1)
    kid = jnp.where(lane < IDX_DIM, sm, pltpu.roll(sm, IDX_DIM, 1))
    mu = jnp.mean(kid, axis=-1, keepdims=True)
    var = jnp.mean(jnp.square(kid - mu), axis=-1, keepdims=True)
    ki_ref[0] = ((kid - mu) * lax.rsqrt(var + EPS) * lng_ref[...] + lnb_ref[...]).astype(jnp.bfloat16)

    wT_ref[0] = sm.T[SMALL_WI:SMALL_WI + IDX_HEADS, :] * (IDX_HEADS ** -0.5)


def _prep(proj, bsz, seq, q_norm, kv_norm, w_kv_up, k_norm, ln_w, ln_b, tp=512):
    nb = seq // tp
    tile8 = lambda g: jnp.tile(g, ATTN_HEADS).reshape(ATTN_WIDTH, 1)
    const = lambda shape: pl.BlockSpec(shape, lambda b, i: (0,) * len(shape))
    bf = jnp.bfloat16
    return pl.pallas_call(
        _prep_kernel,
        grid=(bsz, nb),
        in_specs=[pl.BlockSpec((tp, PREP_COLS), lambda b, i: (b * nb + i, 0)),
                  const((ATTN_WIDTH, 1)), const((1, KV_RANK)), const((KV_RANK, 2 * ATTN_WIDTH)),
                  const((ATTN_WIDTH, 1)), const((1, 128)), const((1, 128))],
        out_specs=[pl.BlockSpec((1, ATTN_WIDTH, tp), lambda b, i: (b, 0, i)),
                   pl.BlockSpec((1, tp, ATTN_WIDTH), lambda b, i: (b, i, 0)),
                   pl.BlockSpec((1, ATTN_WIDTH, tp), lambda b, i: (b, 0, i)),
                   pl.BlockSpec((1, ATTN_WIDTH, tp), lambda b, i: (b, 0, i)),
                   pl.BlockSpec((1, tp, 128), lambda b, i: (b, i, 0)),
                   pl.BlockSpec((1, IDX_HEADS, tp), lambda b, i: (b, 0, i))],
        out_shape=[jax.ShapeDtypeStruct((bsz, ATTN_WIDTH, seq), bf),
                   jax.ShapeDtypeStruct((bsz, seq, ATTN_WIDTH), bf),
                   jax.ShapeDtypeStruct((bsz, ATTN_WIDTH, seq), bf),
                   jax.ShapeDtypeStruct((bsz, ATTN_WIDTH, seq), bf),
                   jax.ShapeDtypeStruct((bsz, seq, 128), bf),
                   jax.ShapeDtypeStruct((bsz, IDX_HEADS, seq), jnp.float32)],
        compiler_params=pltpu.CompilerParams(dimension_semantics=("parallel", "parallel"),
                                             vmem_limit_bytes=VMEM_LIMIT),
        name="attn_prep",
    )(proj, tile8(q_norm), kv_norm.reshape(1, KV_RANK), w_kv_up.astype(bf), tile8(k_norm),
      jnp.tile(ln_w, 2).reshape(1, 128), jnp.tile(ln_b, 2).reshape(1, 128))


def _t5_bucket(dist):
    n = jnp.maximum(dist, 0)
    max_exact = N_BUCKETS // 2
    nf = jnp.maximum(n, 1).astype(jnp.float32)
    large = max_exact + (jnp.log(nf / max_exact) / math.log(MAX_DISTANCE / max_exact) * (N_BUCKETS - max_exact)).astype(jnp.int32)
    large = jnp.minimum(large, N_BUCKETS - 1)
    return jnp.where(n < max_exact, n, large)


def _bias_tables(rel_bias):
    s = jnp.arange(QB, dtype=jnp.int32)[:, None]
    q = jnp.arange(QB, dtype=jnp.int32)[None, :]
    tabs = []
    for off in (2 * QB, QB, 0):
        dist = q - s + off
        b = jnp.moveaxis(rel_bias[_t5_bucket(dist)], -1, 0).astype(jnp.float32)
        tabs.append(jnp.where((dist >= 0)[None], b, NEG))
    return jnp.stack(tabs)


def _count_ge(keys_ref, n_tiles, cand):
    def body(j, acc):
        blk = keys_ref[pl.ds(pl.multiple_of(j * 64, 64), 64), :]
        hit = jnp.where(blk >= cand, 1, 0).astype(jnp.int32)
        return acc + jnp.sum(hit.reshape(8, 8, QB), axis=0)
    acc = lax.fori_loop(0, n_tiles * (QB // 64), body, jnp.zeros((8, QB), jnp.int32))
    return jnp.sum(acc, axis=0, keepdims=True)


def _attn_kernel(qT_ref, qiT_ref, wT_ref, k_ref, vT_ref, ki_ref, tab_ref, o_ref,
                 keys_ref, acc_ref, m_ref, l_ref, *, topk):
    i = pl.program_id(1)
    n_tiles = i + 1
    row_hi = lax.broadcasted_iota(jnp.int32, (128, QB), 0) >= 64

    def head_rows(ref, h):
        pair = ref[0, (h // 2) * 128:(h // 2) * 128 + 128, :]
        return jnp.where(row_hi == bool(h % 2), pair, jnp.zeros_like(pair))

    def score_tile(kt, carry):
        r0 = pl.multiple_of(kt * QB, QB)
        ki = ki_ref[0, pl.ds(r0, QB), :]
        sc = jnp.zeros((QB, QB), jnp.float32)
        for h in range(IDX_HEADS):
            d = jnp.dot(ki, head_rows(qiT_ref, h), preferred_element_type=jnp.float32)
            sc = sc + wT_ref[0, h:h + 1, :] * jnp.maximum(d, 0.0)
        sc = sc + 0.0
        bits = pltpu.bitcast(sc, jnp.int32)
        key = bits ^ ((bits >> 31) & 0x7FFFFFFF)
        srow = lax.broadcasted_iota(jnp.int32, (QB, QB), 0)
        qcol = lax.broadcasted_iota(jnp.int32, (QB, QB), 1)
        key = jnp.where((kt == i) & (srow > qcol), INT_MIN, key)
        keys_ref[pl.ds(r0, QB), :] = key
        return carry
    lax.fori_loop(0, n_tiles, score_tile, 0)

    def bit_step(it, r):
        cand = jnp.where(it == 0, jnp.zeros_like(r), r | (1 << (31 - it)))
        cnt = _count_ge(keys_ref, n_tiles, cand)
        return jnp.where(cnt >= topk, cand, r)
    thr = lax.fori_loop(0, 32, bit_step, jnp.full((1, QB), INT_MIN, jnp.int32))

    cnt_gt = _count_ge(keys_ref, n_tiles, thr + 1)
    cnt_ge = _count_ge(keys_ref, n_tiles, thr)
    need = topk - cnt_gt
    tie = (cnt_ge - cnt_gt > need) & (thr > INT_MIN)

    @pl.when(jnp.max(tie.astype(jnp.int32)) > 0)
    def _():
        def count_eq_below(cand):
            def body(j, acc):
                r0 = pl.multiple_of(j * 64, 64)
                blk = keys_ref[pl.ds(r0, 64), :]
                idx = lax.broadcasted_iota(jnp.int32, (64, QB), 0) + r0
                hit = jnp.where((blk == thr) & (idx < cand), 1, 0).astype(jnp.int32)
                return acc + jnp.sum(hit.reshape(8, 8, QB), axis=0)
            acc = lax.fori_loop(0, n_tiles * (QB // 64), body, jnp.zeros((8, QB), jnp.int32))
            return jnp.sum(acc, axis=0, keepdims=True)

        def idx_step(it, r):
            cand = r | (1 << (15 - it))
            return jnp.where(count_eq_below(cand) < need, cand, r)
        last = lax.fori_loop(0, 16, idx_step, jnp.zeros((1, QB), jnp.int32))

        def drop(j, carry):
            r0 = pl.multiple_of(j * 64, 64)
            blk = keys_ref[pl.ds(r0, 64), :]
            idx = lax.broadcasted_iota(jnp.int32, (64, QB), 0) + r0
            keys_ref[pl.ds(r0, 64), :] = jnp.where(tie & (blk == thr) & (idx > last), INT_MIN, blk)
            return carry
        lax.fori_loop(0, n_tiles * (QB // 64), drop, 0)

    m_ref[...] = jnp.full(m_ref.shape, NEG, jnp.float32)
    l_ref[...] = jnp.zeros(l_ref.shape, jnp.float32)
    acc_ref[...] = jnp.zeros(acc_ref.shape, jnp.float32)

    def attn_tile(kt, carry):
        r0 = pl.multiple_of(kt * QB, QB)
        sel = keys_ref[pl.ds(r0, QB), :] >= thr
        tsel = jnp.clip(kt - (i - 2), 0, 2)
        for h in range(ATTN_HEADS):
            kp = k_ref[0, pl.ds(r0, QB), (h // 2) * 128:(h // 2) * 128 + 128]
            s = jnp.dot(kp, head_rows(qT_ref, h), preferred_element_type=jnp.float32) + tab_ref[tsel, h]
            s = jnp.where(sel, s, NEG)
            m_old = m_ref[h:h + 1, :]
            m_new = jnp.maximum(m_old, jnp.max(s, axis=0, keepdims=True))
            alpha = jnp.exp(m_old - m_new)
            p = jnp.exp(s - m_new)
            l_ref[h:h + 1, :] = alpha * l_ref[h:h + 1, :] + jnp.sum(p, axis=0, keepdims=True)
            m_ref[h:h + 1, :] = m_new
            vh = vT_ref[0, h * 64:(h + 1) * 64, pl.ds(r0, QB)]
            pv = jnp.dot(vh, p.astype(jnp.bfloat16), preferred_element_type=jnp.float32)
            acc_ref[h * 64:(h + 1) * 64, :] = alpha * acc_ref[h * 64:(h + 1) * 64, :] + pv
        return carry
    lax.fori_loop(0, n_tiles, attn_tile, 0)

    for h in range(ATTN_HEADS):
        acc_ref[h * 64:(h + 1) * 64, :] = acc_ref[h * 64:(h + 1) * 64, :] / l_ref[h:h + 1, :]
    o_ref[0] = acc_ref[...].T


def _dsa_attention(qT, qiT, wT, k, vT, ki2, rel_bias):
    bsz, _, seq = qT.shape
    topk = min(TOPK_MAX, seq // 4)
    assert seq % QB == 0 and topk <= QB
    return pl.pallas_call(
        functools.partial(_attn_kernel, topk=topk),
        grid=(bsz, seq // QB),
        in_specs=[
            pl.BlockSpec((1, ATTN_WIDTH, QB), lambda b, i: (b, 0, i)),
            pl.BlockSpec((1, IDX_HEADS * IDX_DIM, QB), lambda b, i: (b, 0, i)),
            pl.BlockSpec((1, IDX_HEADS, QB), lambda b, i: (b, 0, i)),
            pl.BlockSpec((1, seq, ATTN_WIDTH), lambda b, i: (b, 0, 0)),
            pl.BlockSpec((1, ATTN_WIDTH, seq), lambda b, i: (b, 0, 0)),
            pl.BlockSpec((1, seq, 128), lambda b, i: (b, 0, 0)),
            pl.BlockSpec((3, ATTN_HEADS, QB, QB), lambda b, i: (0, 0, 0, 0)),
        ],
        out_specs=pl.BlockSpec((1, QB, ATTN_WIDTH), lambda b, i: (b, i, 0)),
        out_shape=jax.ShapeDtypeStruct((bsz, seq, ATTN_WIDTH), jnp.float32),
        scratch_shapes=[
            pltpu.VMEM((seq, QB), jnp.int32),
            pltpu.VMEM((ATTN_WIDTH, QB), jnp.float32),
            pltpu.VMEM((ATTN_HEADS, QB), jnp.float32),
            pltpu.VMEM((ATTN_HEADS, QB), jnp.float32),
        ],
        compiler_params=pltpu.CompilerParams(dimension_semantics=("parallel", "arbitrary"),
                                             vmem_limit_bytes=VMEM_LIMIT),
        name="dsa_attention",
    )(qT, qiT, wT, k, vT, ki2, _bias_tables(rel_bias))


def _mamba2_ssd(z, xbc, dt_raw, conv_w, conv_b, dt_bias, a_log, d_skip, norm_w):
    Bsz, L = xbc.shape[0], xbc.shape[1]
    nc = L // SSD_CHUNK
    hg = SSM_HEADS // SSM_GROUPS
    xbc = lax.conv_general_dilated(xbc, conv_w[:, None, :].astype(xbc.dtype), window_strides=(1,),
                                   padding=[(CONV_WIDTH - 1, 0)], dimension_numbers=('NWC', 'WIO', 'NWC'),
                                   feature_group_count=CONV_CH) + conv_b
    xbc = jax.nn.silu(xbc).astype(jnp.float32)
    xs, bm, cm = jnp.split(xbc, [SSM_INNER, SSM_INNER + SSM_GROUPS * SSM_STATE], axis=-1)
    dt = jax.nn.softplus(dt_raw.astype(jnp.float32) + dt_bias.astype(jnp.float32))
    a = -jnp.exp(a_log.astype(jnp.float32)).reshape(SSM_GROUPS, hg)

    def chunks(arr, *tail):
        return jnp.moveaxis(arr.reshape(Bsz, nc, SSD_CHUNK, *tail), 1, 0)

    x_c = chunks(xs, SSM_GROUPS, hg, SSM_HEAD_DIM)
    dt_c = chunks(dt, SSM_GROUPS, hg)
    b_c = chunks(bm, SSM_GROUPS, SSM_STATE)
    c_c = chunks(cm, SSM_GROUPS, SSM_STATE)
    causal = jnp.tril(jnp.ones((SSD_CHUNK, SSD_CHUNK), dtype=bool))[None, :, :, None, None]

    def step(state, inp):
        xc, dtc, bc, cc = inp
        acum = jnp.cumsum(dtc * a, axis=1)
        seg = acum[:, :, None] - acum[:, None, :]
        lmat = jnp.exp(jnp.where(causal, seg, -jnp.inf))
        cb = jnp.einsum('bign,bjgn->bijg', cc, bc)
        wmat = cb[..., None] * lmat * dtc[:, None]
        y = jnp.einsum('bijgh,bjghp->bighp', wmat, xc)
        y = y + jnp.einsum('bign,bghpn->bighp', cc, state) * jnp.exp(acum)[..., None]
        decay = jnp.exp(acum[:, -1:] - acum) * dtc
        state = state * jnp.exp(acum[:, -1])[..., None, None] + jnp.einsum('bjgn,bjgh,bjghp->bghpn', bc, decay, xc)
        return state, y

    h0 = jnp.zeros((Bsz, SSM_GROUPS, hg, SSM_HEAD_DIM, SSM_STATE), jnp.float32)
    _, ys = lax.scan(step, h0, (x_c, dt_c, b_c, c_c))
    y = jnp.moveaxis(ys, 0, 1).reshape(Bsz, L, SSM_HEADS, SSM_HEAD_DIM)
    y = y + d_skip.astype(jnp.float32)[:, None] * xs.reshape(Bsz, L, SSM_HEADS, SSM_HEAD_DIM)
    y = y.reshape(Bsz, L, SSM_INNER) * jax.nn.silu(z.astype(jnp.float32))
    yg = y.reshape(Bsz, L, SSM_GROUPS, SSM_INNER // SSM_GROUPS)
    yg = yg * lax.rsqrt(jnp.mean(yg * yg, axis=-1, keepdims=True) + EPS)
    y = yg.reshape(Bsz, L, SSM_INNER) * norm_w.astype(jnp.float32)
    return y.astype(z.dtype)


def _mix_out_kernel(a_ref, s_ref, gl_ref, x_ref, gm_ref, wo_ref, ws_ref, wout_ref,
                    nf_ref, scf_ref, shf_ref, wr_ref, br_ref, xo_ref, h_ref, lg_ref):
    bf = jnp.bfloat16
    ya = jnp.dot(a_ref[...].astype(bf), wo_ref[...], preferred_element_type=jnp.float32)
    ys = jnp.dot(s_ref[...].astype(bf), ws_ref[...], preferred_element_type=jnp.float32)
    mixed = jax.nn.sigmoid(gl_ref[:, :D_MODEL]) * ya + jax.nn.sigmoid(gl_ref[:, D_MODEL:]) * ys
    x = x_ref[...] + gm_ref[0] * jnp.dot(mixed.astype(bf), wout_ref[...], preferred_element_type=jnp.float32)
    xo_ref[...] = x
    y = x * lax.rsqrt(jnp.mean(x * x, axis=-1, keepdims=True) + EPS) * nf_ref[...]
    h = y * (1.0 + scf_ref[0]) + shf_ref[0]
    h_ref[...] = h.astype(bf)
    lg_ref[...] = jnp.dot(h, wr_ref[...], preferred_element_type=jnp.float32, precision=HIGHEST) + br_ref[...]


def _mix_out(attn2, ssd2, proj, x2, g_m, w_attn_o, w_ssm_o, w_out, norm_ffn, sc_f, sh_f, w_router, b_router, seq, tm=512):
    t, d = x2.shape
    per_b = seq // tm
    bf = jnp.bfloat16
    const = lambda shape: pl.BlockSpec(shape, lambda i: (0,) * len(shape))
    perb = pl.BlockSpec((1, 1, d), lambda i: (i // per_b, 0, 0))
    wr = jnp.pad(w_router, ((0, 0), (0, 128 - N_EXPERTS)))
    br = jnp.pad(b_router, (0, 128 - N_EXPERTS)).reshape(1, 128)
    return pl.pallas_call(
        _mix_out_kernel,
        grid=(t // tm,),
        in_specs=[pl.BlockSpec((tm, ATTN_WIDTH), lambda i: (i, 0)),
                  pl.BlockSpec((tm, SSM_INNER), lambda i: (i, 0)),
                  pl.BlockSpec((tm, 2 * d), lambda i: (i, COL_GATE // (2 * d))),
                  pl.BlockSpec((tm, d), lambda i: (i, 0)),
                  perb,
                  const((ATTN_WIDTH, d)), const((SSM_INNER, d)), const((d, d)),
                  const((1, d)), perb, perb, const((d, 128)), const((1, 128))],
        out_specs=[pl.BlockSpec((tm, d), lambda i: (i, 0)),
                   pl.BlockSpec((tm, d), lambda i: (i, 0)),
                   pl.BlockSpec((tm, 128), lambda i: (i, 0))],
        out_shape=[jax.ShapeDtypeStruct((t, d), jnp.float32),
                   jax.ShapeDtypeStruct((t, d), bf),
                   jax.ShapeDtypeStruct((t, 128), jnp.float32)],
        compiler_params=pltpu.CompilerParams(dimension_semantics=("parallel",), vmem_limit_bytes=VMEM_LIMIT),
        name="mix_out",
    )(attn2, ssd2, proj, x2, g_m[:, None, :], w_attn_o.astype(bf), w_ssm_o.astype(bf), w_out.astype(bf),
      norm_ffn.reshape(1, d), sc_f[:, None, :], sh_f[:, None, :], wr, br)


def _moe_kernel(be_ref, nb_ref, x_ref, g_ref, wgu_ref, bgu_ref, wdn_ref, bdn_ref, o_ref):
    i = pl.program_id(0)

    @pl.when(i < nb_ref[0])
    def _():
        gu = jnp.dot(x_ref[...], wgu_ref[0], preferred_element_type=jnp.float32) + bgu_ref[0]
        g = jnp.minimum(gu[:, :D_EXPERT], SWIGLU_LIMIT)
        u = jnp.clip(gu[:, D_EXPERT:], -SWIGLU_LIMIT, SWIGLU_LIMIT)
        act = (u + 1.0) * (g * jax.nn.sigmoid(SWIGLU_ALPHA * g))
        out = jnp.dot(act.astype(jnp.bfloat16), wdn_ref[0], preferred_element_type=jnp.float32) + bdn_ref[0]
        o_ref[...] = out * g_ref[...]

    @pl.when(i >= nb_ref[0])
    def _():
        o_ref[...] = jnp.zeros_like(o_ref)


def _moe_ffn(xs, row_gate, blk_exp, n_used, w_gu, b_gu, w_dn, b_dn):
    n_rows, d = xs.shape
    tm = MOE_TM
    bf = jnp.bfloat16
    grid_spec = pltpu.PrefetchScalarGridSpec(
        num_scalar_prefetch=2,
        grid=(n_rows // tm,),
        in_specs=[pl.BlockSpec((tm, d), lambda i, be, nb: (i, 0)),
                  pl.BlockSpec((tm, 1), lambda i, be, nb: (i, 0)),
                  pl.BlockSpec((1, d, 2 * D_EXPERT), lambda i, be, nb: (be[i], 0, 0)),
                  pl.BlockSpec((1, 1, 2 * D_EXPERT), lambda i, be, nb: (be[i], 0, 0)),
                  pl.BlockSpec((1, D_EXPERT, d), lambda i, be, nb: (be[i], 0, 0)),
                  pl.BlockSpec((1, 1, d), lambda i, be, nb: (be[i], 0, 0))],
        out_specs=pl.BlockSpec((tm, d), lambda i, be, nb: (i, 0)),
    )
    return pl.pallas_call(
        _moe_kernel,
        grid_spec=grid_spec,
        out_shape=jax.ShapeDtypeStruct((n_rows, d), jnp.float32),
        compiler_params=pltpu.CompilerParams(dimension_semantics=("arbitrary",), vmem_limit_bytes=VMEM_LIMIT),
        name="moe_ffn",
    )(blk_exp, n_used, xs, row_gate[:, None], w_gu.astype(bf), b_gu[:, None, :], w_dn.astype(bf), b_dn[:, None, :])


def _moe(h2, logits, w_gu, b_gu, w_dn, b_dn):
    t, d = h2.shape
    tm = MOE_TM
    top_val, top_idx = lax.top_k(logits, TOP_K)
    gates = jax.nn.softmax(top_val, axis=-1)
    n_assign = t * TOP_K
    e_flat = top_idx.reshape(n_assign).astype(jnp.int32)
    e_sorted, order = lax.sort((e_flat, jnp.arange(n_assign, dtype=jnp.int32)), num_keys=1)
    counts = jnp.sum((e_flat[:, None] == jnp.arange(N_EXPERTS, dtype=jnp.int32)[None, :]).astype(jnp.int32), axis=0)
    padded = (counts + tm - 1) // tm * tm
    pad_end = jnp.cumsum(padded)
    pad_start = pad_end - padded
    start = jnp.cumsum(counts) - counts
    dest_sorted = pad_start[e_sorted] + jnp.arange(n_assign, dtype=jnp.int32) - start[e_sorted]
    n_rows = n_assign + N_EXPERTS * tm
    row_tok = jnp.zeros((n_rows,), jnp.int32).at[dest_sorted].set(order // TOP_K)
    dest = jnp.zeros((n_assign,), jnp.int32).at[order].set(dest_sorted)
    row_gate = jnp.zeros((n_rows,), jnp.float32).at[dest].set(gates.reshape(n_assign))
    blk_start = jnp.arange(n_rows // tm, dtype=jnp.int32) * tm
    blk_exp = jnp.minimum(jnp.sum((blk_start[:, None] >= pad_end[None, :]).astype(jnp.int32), axis=1), N_EXPERTS - 1)
    n_used = (pad_end[-1:] // tm).astype(jnp.int32)
    out = _moe_ffn(h2[row_tok], row_gate, blk_exp, n_used, w_gu, b_gu, w_dn, b_dn)
    return jnp.sum(out[dest].reshape(t, TOP_K, d), axis=1)


def kernel(x, c, rel_bias, w_ada, b_ada, norm_mix, norm_ffn, w_in, kv_norm, w_kv_up, q_norm, k_norm,
           idx_k_ln_w, idx_k_ln_b, w_attn_o, conv_w, conv_b, dt_bias, a_log, d_skip, ssm_norm, w_ssm_o,
           w_out, w_router, b_router, w_gu, b_gu, w_dn, b_dn):
    bsz, seq, d = x.shape
    t = bsz * seq
    cond = jax.nn.silu(c)
    x2 = x.reshape(t, d)
    for l in range(DEPTH):
        mod = cond @ w_ada[l] + b_ada[l]
        sh_m, sc_m, g_m, sh_f, sc_f, g_f = jnp.split(mod, 6, axis=-1)
        proj = _in_proj(x2, norm_mix[l], sc_m, sh_m, _pack_w_in(w_in[l]), seq)
        qT, k, vT, qiT, ki2, wT = _prep(proj, bsz, seq, q_norm[l], kv_norm[l], w_kv_up[l], k_norm[l],
                                        idx_k_ln_w[l], idx_k_ln_b[l])
        attn = _dsa_attention(qT, qiT, wT, k, vT, ki2, rel_bias)
        p3 = proj.reshape(bsz, seq, PROJ_COLS)
        small = p3[..., COL_SMALL:COL_SMALL + 128]
        y_ssd = _mamba2_ssd(p3[..., COL_Z:COL_Z + SSM_INNER], p3[..., COL_XBC:COL_XBC + CONV_CH],
                            small[..., SMALL_DT:SMALL_DT + SSM_HEADS],
                            conv_w[l], conv_b[l], dt_bias[l], a_log[l], d_skip[l], ssm_norm[l])
        x2, h2, logits = _mix_out(attn.reshape(t, ATTN_WIDTH), y_ssd.reshape(t, SSM_INNER), proj, x2, g_m,
                                  w_attn_o[l], w_ssm_o[l], w_out[l], norm_ffn[l], sc_f, sh_f,
                                  w_router[l], b_router[l], seq)
        y = _moe(h2, logits[:, :N_EXPERTS], w_gu[l], b_gu[l], w_dn[l], b_dn[l])
        x2 = x2 + jnp.repeat(g_f, seq, axis=0) * y
    return x2.reshape(bsz, seq, d)
```

```python
import functools
import math

import jax
import jax.numpy as jnp
import numpy as np
from jax import lax
from jax.experimental import pallas as pl
from jax.experimental.pallas import tpu as pltpu

D_MODEL = 1024
DEPTH = 2
ATTN_HEADS = 8
ATTN_HEAD_DIM = 64
ATTN_WIDTH = ATTN_HEADS * ATTN_HEAD_DIM
KV_RANK = 256
IDX_HEADS = 8
IDX_DIM = 64
TOPK_MAX = 256
N_BUCKETS = 32
MAX_DISTANCE = 128
SSM_HEADS = 16
SSM_HEAD_DIM = 64
SSM_INNER = SSM_HEADS * SSM_HEAD_DIM
SSM_GROUPS = 2
SSM_STATE = 128
CONV_WIDTH = 4
CONV_CH = SSM_INNER + 2 * SSM_GROUPS * SSM_STATE
SSD_CHUNK = 128
N_EXPERTS = 32
TOP_K = 4
D_EXPERT = D_MODEL
SWIGLU_LIMIT = 7.0
SWIGLU_ALPHA = 1.702
EPS = 1e-6

COL_Q = 0
COL_KV = 512
COL_QI = 768
COL_SMALL = 1280
COL_XBC = 1536
COL_Z = 3072
COL_GATE = 4096
PROJ_COLS = 6144
PREP_COLS = 1408
SMALL_KI, SMALL_WI, SMALL_DT = 0, 64, 72

QB = 256
VROWS = 80
INT_MIN = -2 ** 31
KEY_NEG_INF = (0xFF800000 ^ 0x7FFFFFFF) - 2 ** 32
NEG = -1e30
TINY = 2.0 ** -126
LOG2E = math.log2(math.e)
VMEM_LIMIT = 56 * 1024 * 1024
MOE_TM = 512
HIGHEST = lax.Precision.HIGHEST


def _pack_w_in(w):
    o = np.cumsum((0, ATTN_WIDTH, KV_RANK, IDX_HEADS * IDX_DIM, IDX_DIM, IDX_HEADS, SSM_INNER, CONV_CH, SSM_HEADS, 2 * D_MODEL))
    q, kv, qi, ki, wi, z, xbc, dt, gate = (w[:, int(o[n]):int(o[n + 1])] for n in range(9))
    zeros = lambda n: jnp.zeros((w.shape[0], n), w.dtype)
    small = jnp.concatenate([ki, wi, dt, zeros(128 - 88)], axis=1)
    packed = jnp.concatenate([q, kv, qi, small, zeros(COL_XBC - PREP_COLS), xbc, z, gate], axis=1)
    assert packed.shape[1] == PROJ_COLS
    return packed.astype(jnp.bfloat16)


def _in_proj_kernel(x_ref, g_ref, sc_ref, sh_ref, w_ref, o_ref, h_ref):
    @pl.when(pl.program_id(1) == 0)
    def _():
        x = x_ref[...]
        y = x * lax.rsqrt(jnp.mean(x * x, axis=-1, keepdims=True) + EPS) * g_ref[...]
        h_ref[...] = (y * (1.0 + sc_ref[0]) + sh_ref[0]).astype(jnp.bfloat16)
    o_ref[...] = jnp.dot(h_ref[...], w_ref[...], preferred_element_type=jnp.float32)


def _in_proj(x2, gain, sc, sh, w_packed, seq, tm=1024, tn=512):
    t, d = x2.shape
    per_b = seq // tm
    return pl.pallas_call(
        _in_proj_kernel,
        grid=(t // tm, PROJ_COLS // tn),
        in_specs=[pl.BlockSpec((tm, d), lambda i, j: (i, 0)),
                  pl.BlockSpec((1, d), lambda i, j: (0, 0)),
                  pl.BlockSpec((1, 1, d), lambda i, j: (i // per_b, 0, 0)),
                  pl.BlockSpec((1, 1, d), lambda i, j: (i // per_b, 0, 0)),
                  pl.BlockSpec((d, tn), lambda i, j: (0, j))],
        out_specs=pl.BlockSpec((tm, tn), lambda i, j: (i, j)),
        out_shape=jax.ShapeDtypeStruct((t, PROJ_COLS), jnp.float32),
        scratch_shapes=[pltpu.VMEM((tm, d), jnp.bfloat16)],
        compiler_params=pltpu.CompilerParams(dimension_semantics=("parallel", "arbitrary"),
                                             vmem_limit_bytes=VMEM_LIMIT),
        name="in_proj",
    )(x2, gain.reshape(1, d), sc[:, None, :], sh[:, None, :], w_packed)


def _head_rms_t(xt):
    x3 = xt.reshape(ATTN_HEADS, ATTN_HEAD_DIM, xt.shape[1])
    return lax.rsqrt(jnp.mean(x3 * x3, axis=1, keepdims=True) + EPS)


def _prep_kernel(p_ref, qg_ref, kvg_ref, wkv_ref, kg_ref, lng_ref, lnb_ref,
                 qT_ref, k_ref, vT_ref, qiT_ref, ki_ref, wT_ref):
    n = p_ref.shape[0]
    q = p_ref[:, COL_Q:COL_Q + ATTN_WIDTH]
    lat = p_ref[:, COL_KV:COL_KV + KV_RANK]
    qi = p_ref[:, COL_QI:COL_QI + IDX_HEADS * IDX_DIM]
    sm = p_ref[:, COL_SMALL:COL_SMALL + 128]

    scale = ATTN_HEAD_DIM ** -0.5 * LOG2E
    qt = q.T
    qn = qt.reshape(ATTN_HEADS, ATTN_HEAD_DIM, n) * _head_rms_t(qt)
    qT_ref[0] = (qn.reshape(ATTN_WIDTH, n) * qg_ref[...] * scale).astype(jnp.bfloat16)

    latn = lat * lax.rsqrt(jnp.mean(lat * lat, axis=-1, keepdims=True) + EPS) * kvg_ref[...]
    kv = jnp.dot(latn.astype(jnp.bfloat16), wkv_ref[...], preferred_element_type=jnp.float32)
    kt = kv[:, :ATTN_WIDTH].T
    kn = (kt.reshape(ATTN_HEADS, ATTN_HEAD_DIM, n) * _head_rms_t(kt)).reshape(ATTN_WIDTH, n) * kg_ref[...]
    k_ref[0] = kn.T.astype(jnp.bfloat16)
    vt = kv[:, ATTN_WIDTH:].T.reshape(ATTN_HEADS, ATTN_HEAD_DIM, n)
    ones = jnp.ones((ATTN_HEADS, VROWS - ATTN_HEAD_DIM, n), jnp.float32)
    vT_ref[0] = jnp.concatenate([vt, ones], axis=1).reshape(ATTN_HEADS * VROWS, n).astype(jnp.bfloat16)

    qiT_ref[0] = (qi * (IDX_DIM ** -0.5)).T.astype(jnp.bfloat16)

    lane = lax.broadcasted_iota(jnp.int32, sm.shape, 1)
    kid = jnp.where(lane < IDX_DIM, sm, pltpu.roll(sm, IDX_DIM, 1))
    mu = jnp.mean(kid, axis=-1, keepdims=True)
    var = jnp.mean(jnp.square(kid - mu), axis=-1, keepdims=True)
    ki_ref[0] = ((kid - mu) * lax.rsqrt(var + EPS) * lng_ref[...] + lnb_ref[...]).astype(jnp.bfloat16)

    wT_ref[0] = sm.T[SMALL_WI:SMALL_WI + IDX_HEADS, :] * (IDX_HEADS ** -0.5)


def _prep(proj, bsz, seq, q_norm, kv_norm, w_kv_up, k_norm, ln_w, ln_b, tp=512):
    nb = seq // tp
    tile8 = lambda g: jnp.tile(g, ATTN_HEADS).reshape(ATTN_WIDTH, 1)
    const = lambda shape: pl.BlockSpec(shape, lambda b, i: (0,) * len(shape))
    bf = jnp.bfloat16
    return pl.pallas_call(
        _prep_kernel,
        grid=(bsz, nb),
        in_specs=[pl.BlockSpec((tp, PREP_COLS), lambda b, i: (b * nb + i, 0)),
                  const((ATTN_WIDTH, 1)), const((1, KV_RANK)), const((KV_RANK, 2 * ATTN_WIDTH)),
                  const((ATTN_WIDTH, 1)), const((1, 128)), const((1, 128))],
        out_specs=[pl.BlockSpec((1, ATTN_WIDTH, tp), lambda b, i: (b, 0, i)),
                   pl.BlockSpec((1, tp, ATTN_WIDTH), lambda b, i: (b, i, 0)),
                   pl.BlockSpec((1, ATTN_HEADS * VROWS, tp), lambda b, i: (b, 0, i)),
                   pl.BlockSpec((1, ATTN_WIDTH, tp), lambda b, i: (b, 0, i)),
                   pl.BlockSpec((1, tp, 128), lambda b, i: (b, i, 0)),
                   pl.BlockSpec((1, IDX_HEADS, tp), lambda b, i: (b, 0, i))],
        out_shape=[jax.ShapeDtypeStruct((bsz, ATTN_WIDTH, seq), bf),
                   jax.ShapeDtypeStruct((bsz, seq, ATTN_WIDTH), bf),
                   jax.ShapeDtypeStruct((bsz, ATTN_HEADS * VROWS, seq), bf),
                   jax.ShapeDtypeStruct((bsz, ATTN_WIDTH, seq), bf),
                   jax.ShapeDtypeStruct((bsz, seq, 128), bf),
                   jax.ShapeDtypeStruct((bsz, IDX_HEADS, seq), jnp.float32)],
        compiler_params=pltpu.CompilerParams(dimension_semantics=("parallel", "parallel"),
                                             vmem_limit_bytes=VMEM_LIMIT),
        name="attn_prep",
    )(proj, tile8(q_norm), kv_norm.reshape(1, KV_RANK), w_kv_up.astype(bf), tile8(k_norm),
      jnp.tile(ln_w, 2).reshape(1, 128), jnp.tile(ln_b, 2).reshape(1, 128))


def _t5_bucket(dist):
    n = jnp.maximum(dist, 0)
    max_exact = N_BUCKETS // 2
    nf = jnp.maximum(n, 1).astype(jnp.float32)
    large = max_exact + (jnp.log(nf / max_exact) / math.log(MAX_DISTANCE / max_exact) * (N_BUCKETS - max_exact)).astype(jnp.int32)
    large = jnp.minimum(large, N_BUCKETS - 1)
    return jnp.where(n < max_exact, n, large)


def _bias_tables(rel_bias):
    s = jnp.arange(QB, dtype=jnp.int32)[:, None]
    q = jnp.arange(QB, dtype=jnp.int32)[None, :]
    tabs = []
    for off in (QB, 0):
        dist = q - s + off
        b = jnp.moveaxis(rel_bias[_t5_bucket(dist)], -1, 0).astype(jnp.float32) * LOG2E
        tabs.append(jnp.where((dist >= 0)[None], b, NEG))
    far = rel_bias[_t5_bucket(jnp.full((), 2 * QB, jnp.int32))].astype(jnp.float32) * LOG2E
    return jnp.stack(tabs), jnp.broadcast_to(far[:, None, None], (ATTN_HEADS, 8, QB))


def _attn_kernel(qT_ref, qiT_ref, wT_ref, k_ref, vT_ref, ki_ref, tab_ref, far_ref, o_ref,
                 keys_ref, hi_ref, msk_ref, acc_ref, mp_ref, *, topk):
    i = pl.program_id(1)
    n_tiles = i + 1
    row_hi = lax.broadcasted_iota(jnp.int32, (128, QB), 0) >= 64
    ones_rows = jnp.ones((16, QB), jnp.bfloat16)

    def head_rows(ref, h):
        pair = ref[0, (h // 2) * 128:(h // 2) * 128 + 128, :]
        return jnp.where(row_hi == bool(h % 2), pair, jnp.zeros_like(pair))

    def tile_rows(kt):
        return pl.ds(pl.multiple_of(kt * QB, QB), QB)

    def score_tile(kt, carry):
        ki = ki_ref[0, tile_rows(kt), :]
        sc = jnp.zeros((QB, QB), jnp.float32)
        for h in range(IDX_HEADS):
            d = jnp.dot(ki, head_rows(qiT_ref, h), preferred_element_type=jnp.float32)
            sc = sc + wT_ref[0, h:h + 1, :] * jnp.maximum(d, 0.0)
        srow = lax.broadcasted_iota(jnp.int32, (QB, QB), 0)
        qcol = lax.broadcasted_iota(jnp.int32, (QB, QB), 1)
        sc = jnp.where(jnp.abs(sc) < TINY, 0.0, sc)
        sc = jnp.where((kt == i) & (srow > qcol), -jnp.inf, sc)
        bits = pltpu.bitcast(sc, jnp.int32)
        keys_ref[tile_rows(kt), :] = bits ^ ((bits >> 31) & 0x7FFFFFFF)
        hi_ref[tile_rows(kt), :] = pltpu.bitcast(bits & jnp.int32(-65536), jnp.float32).astype(jnp.bfloat16)
        return carry
    lax.fori_loop(0, n_tiles, score_tile, 0)

    def count(indicator_of_tile):
        def body(kt, acc):
            return acc + jnp.dot(ones_rows, indicator_of_tile(kt), preferred_element_type=jnp.float32)
        return lax.fori_loop(0, n_tiles, body, jnp.zeros((16, QB), jnp.float32))[0:1, :]

    def count_hi_ge(cand16):
        b = cand16 ^ ((cand16 >> 15) & 0x7FFF)
        snap = jnp.where(((b & 0x8000) != 0) | ((b & 0x7F) == 0), 0, 0x0080)
        b = jnp.where((b & 0x7F80) == 0, snap, b)
        cb = pltpu.bitcast(b << 16, jnp.float32).astype(jnp.bfloat16)
        one, zero = jnp.ones((), jnp.bfloat16), jnp.zeros((), jnp.bfloat16)
        return count(lambda kt: jnp.where(hi_ref[tile_rows(kt), :] >= cb, one, zero))

    def count_ge(cand):
        return count(lambda kt: jnp.where(keys_ref[tile_rows(kt), :] >= cand, 1.0, 0.0).astype(jnp.bfloat16))

    def hi_step(it, r):
        cand = jnp.where(it == 0, jnp.zeros_like(r), r | (1 << (15 - it)))
        return jnp.where(count_hi_ge(cand) >= topk, cand, r)
    r16 = lax.fori_loop(0, 16, hi_step, jnp.full((1, QB), -32768, jnp.int32))

    def lo_step(it, r):
        cand = r | (1 << (15 - it))
        return jnp.where(count_ge(cand) >= topk, cand, r)
    thr = lax.fori_loop(0, 16, lo_step, r16 << 16)

    cnt_gt = count_ge(thr + 1)
    cnt_ge = count_ge(thr)
    need = topk - cnt_gt
    tie = (cnt_ge - cnt_gt > need) & (thr > KEY_NEG_INF)

    @pl.when(jnp.max(tie.astype(jnp.int32)) > 0)
    def _():
        def count_eq_below(cand):
            def ind(kt):
                idx = lax.broadcasted_iota(jnp.int32, (QB, QB), 0) + kt * QB
                return jnp.where((keys_ref[tile_rows(kt), :] == thr) & (idx < cand), 1.0, 0.0).astype(jnp.bfloat16)
            return count(ind)

        def idx_step(it, r):
            cand = r | (1 << (15 - it))
            return jnp.where(count_eq_below(cand) < need, cand, r)
        last = lax.fori_loop(0, 16, idx_step, jnp.zeros((1, QB), jnp.int32))

        def drop(kt, carry):
            blk = keys_ref[tile_rows(kt), :]
            idx = lax.broadcasted_iota(jnp.int32, (QB, QB), 0) + kt * QB
            keys_ref[tile_rows(kt), :] = jnp.where(tie & (blk == thr) & (idx > last), INT_MIN, blk)
            return carry
        lax.fori_loop(0, n_tiles, drop, 0)

    def logits(kt, h, band):
        kp = k_ref[0, tile_rows(kt), (h // 2) * 128:(h // 2) * 128 + 128]
        s = jnp.dot(kp, head_rows(qT_ref, h), preferred_element_type=jnp.float32) + msk_ref[...]
        return s if band is None else s + tab_ref[band, h]

    def set_mask(kt):
        msk_ref[...] = jnp.where(keys_ref[tile_rows(kt), :] >= thr, 0.0, NEG)

    def max_tile(kt, band):
        set_mask(kt)
        for h in range(ATTN_HEADS):
            s = logits(kt, h, band)
            mp_ref[h] = jnp.maximum(mp_ref[h], jnp.max(s.reshape(QB // 8, 8, QB), axis=0))

    def exp_tile(kt, band, m):
        set_mask(kt)
        for h in range(ATTN_HEADS):
            p = jnp.exp2(logits(kt, h, band) - m[h]).astype(jnp.bfloat16)
            va = vT_ref[0, h * VROWS:(h + 1) * VROWS, tile_rows(kt)]
            acc_ref[h * VROWS:(h + 1) * VROWS, :] += jnp.dot(va, p, preferred_element_type=jnp.float32)

    mp_ref[...] = jnp.full(mp_ref.shape, NEG, jnp.float32)

    def far_max(kt, carry):
        max_tile(kt, None)
        return carry
    lax.fori_loop(0, jnp.maximum(i - 1, 0), far_max, 0)
    mp_ref[...] = mp_ref[...] + far_ref[...]

    @pl.when(i >= 1)
    def _():
        max_tile(i - 1, 0)
    max_tile(i, 1)

    m_band = [jnp.max(mp_ref[h], axis=0, keepdims=True) for h in range(ATTN_HEADS)]
    m_far = [m_band[h] - far_ref[h, 0:1, :] for h in range(ATTN_HEADS)]

    acc_ref[...] = jnp.zeros(acc_ref.shape, jnp.float32)

    def far_exp(kt, carry):
        exp_tile(kt, None, m_far)
        return carry
    lax.fori_loop(0, jnp.maximum(i - 1, 0), far_exp, 0)

    @pl.when(i >= 1)
    def _():
        exp_tile(i - 1, 0, m_band)
    exp_tile(i, 1, m_band)

    outs = [acc_ref[h * VROWS:h * VROWS + ATTN_HEAD_DIM, :] / acc_ref[h * VROWS + ATTN_HEAD_DIM:h * VROWS + ATTN_HEAD_DIM + 1, :]
            for h in range(ATTN_HEADS)]
    o_ref[0] = jnp.concatenate(outs, axis=0).T


def _dsa_attention(qT, qiT, wT, k, vT, ki2, rel_bias):
    bsz, _, seq = qT.shape
    topk = min(TOPK_MAX, seq // 4)
    assert seq % QB == 0 and topk <= QB
    tabs, far = _bias_tables(rel_bias)
    return pl.pallas_call(
        functools.partial(_attn_kernel, topk=topk),
        grid=(bsz, seq // QB),
        in_specs=[
            pl.BlockSpec((1, ATTN_WIDTH, QB), lambda b, i: (b, 0, i)),
            pl.BlockSpec((1, IDX_HEADS * IDX_DIM, QB), lambda b, i: (b, 0, i)),
            pl.BlockSpec((1, IDX_HEADS, QB), lambda b, i: (b, 0, i)),
            pl.BlockSpec((1, seq, ATTN_WIDTH), lambda b, i: (b, 0, 0)),
            pl.BlockSpec((1, ATTN_HEADS * VROWS, seq), lambda b, i: (b, 0, 0)),
            pl.BlockSpec((1, seq, 128), lambda b, i: (b, 0, 0)),
            pl.BlockSpec((2, ATTN_HEADS, QB, QB), lambda b, i: (0, 0, 0, 0)),
            pl.BlockSpec((ATTN_HEADS, 8, QB), lambda b, i: (0, 0, 0)),
        ],
        out_specs=pl.BlockSpec((1, QB, ATTN_WIDTH), lambda b, i: (b, i, 0)),
        out_shape=jax.ShapeDtypeStruct((bsz, seq, ATTN_WIDTH), jnp.float32),
        scratch_shapes=[
            pltpu.VMEM((seq, QB), jnp.int32),
            pltpu.VMEM((seq, QB), jnp.bfloat16),
            pltpu.VMEM((QB, QB), jnp.float32),
            pltpu.VMEM((ATTN_HEADS * VROWS, QB), jnp.float32),
            pltpu.VMEM((ATTN_HEADS, 8, QB), jnp.float32),
        ],
        compiler_params=pltpu.CompilerParams(dimension_semantics=("parallel", "arbitrary"),
                                             vmem_limit_bytes=VMEM_LIMIT),
        name="dsa_attention",
    )(qT, qiT, wT, k, vT, ki2, tabs, far)


def _mamba2_ssd(z, xbc, dt_raw, conv_w, conv_b, dt_bias, a_log, d_skip, norm_w):
    Bsz, L = xbc.shape[0], xbc.shape[1]
    nc = L // SSD_CHUNK
    hg = SSM_HEADS // SSM_GROUPS
    xbc = lax.conv_general_dilated(xbc, conv_w[:, None, :].astype(xbc.dtype), window_strides=(1,),
                                   padding=[(CONV_WIDTH - 1, 0)], dimension_numbers=('NWC', 'WIO', 'NWC'),
                                   feature_group_count=CONV_CH) + conv_b
    xbc = jax.nn.silu(xbc).astype(jnp.float32)
    xs, bm, cm = jnp.split(xbc, [SSM_INNER, SSM_INNER + SSM_GROUPS * SSM_STATE], axis=-1)
    dt = jax.nn.softplus(dt_raw.astype(jnp.float32) + dt_bias.astype(jnp.float32))
    a = -jnp.exp(a_log.astype(jnp.float32)).reshape(SSM_GROUPS, hg)

    def chunks(arr, *tail):
        return jnp.moveaxis(arr.reshape(Bsz, nc, SSD_CHUNK, *tail), 1, 0)

    x_c = chunks(xs, SSM_GROUPS, hg, SSM_HEAD_DIM)
    dt_c = chunks(dt, SSM_GROUPS, hg)
    b_c = chunks(bm, SSM_GROUPS, SSM_STATE)
    c_c = chunks(cm, SSM_GROUPS, SSM_STATE)
    causal = jnp.tril(jnp.ones((SSD_CHUNK, SSD_CHUNK), dtype=bool))[None, :, :, None, None]

    def step(state, inp):
        xc, dtc, bc, cc = inp
        acum = jnp.cumsum(dtc * a, axis=1)
        seg = acum[:, :, None] - acum[:, None, :]
        lmat = jnp.exp(jnp.where(causal, seg, -jnp.inf))
        cb = jnp.einsum('bign,bjgn->bijg', cc, bc)
        wmat = cb[..., None] * lmat * dtc[:, None]
        y = jnp.einsum('bijgh,bjghp->bighp', wmat, xc)
        y = y + jnp.einsum('bign,bghpn->bighp', cc, state) * jnp.exp(acum)[..., None]
        decay = jnp.exp(acum[:, -1:] - acum) * dtc
        state = state * jnp.exp(acum[:, -1])[..., None, None] + jnp.einsum('bjgn,bjgh,bjghp->bghpn', bc, decay, xc)
        return state, y

    h0 = jnp.zeros((Bsz, SSM_GROUPS, hg, SSM_HEAD_DIM, SSM_STATE), jnp.float32)
    _, ys = lax.scan(step, h0, (x_c, dt_c, b_c, c_c))
    y = jnp.moveaxis(ys, 0, 1).reshape(Bsz, L, SSM_HEADS, SSM_HEAD_DIM)
    y = y + d_skip.astype(jnp.float32)[:, None] * xs.reshape(Bsz, L, SSM_HEADS, SSM_HEAD_DIM)
    y = y.reshape(Bsz, L, SSM_INNER) * jax.nn.silu(z.astype(jnp.float32))
    yg = y.reshape(Bsz, L, SSM_GROUPS, SSM_INNER // SSM_GROUPS)
    yg = yg * lax.rsqrt(jnp.mean(yg * yg, axis=-1, keepdims=True) + EPS)
    y = yg.reshape(Bsz, L, SSM_INNER) * norm_w.astype(jnp.float32)
    return y.astype(z.dtype)


def _mix_out_kernel(a_ref, s_ref, gl_ref, x_ref, gm_ref, wo_ref, ws_ref, wout_ref,
                    nf_ref, scf_ref, shf_ref, wr_ref, br_ref, xo_ref, h_ref, lg_ref):
    bf = jnp.bfloat16
    ya = jnp.dot(a_ref[...].astype(bf), wo_ref[...], preferred_element_type=jnp.float32)
    ys = jnp.dot(s_ref[...].astype(bf), ws_ref[...], preferred_element_type=jnp.float32)
    mixed = jax.nn.sigmoid(gl_ref[:, :D_MODEL]) * ya + jax.nn.sigmoid(gl_ref[:, D_MODEL:]) * ys
    x = x_ref[...] + gm_ref[0] * jnp.dot(mixed.astype(bf), wout_ref[...], preferred_element_type=jnp.float32)
    xo_ref[...] = x
    y = x * lax.rsqrt(jnp.mean(x * x, axis=-1, keepdims=True) + EPS) * nf_ref[...]
    h = y * (1.0 + scf_ref[0]) + shf_ref[0]
    h_ref[...] = h.astype(bf)
    lg_ref[...] = jnp.dot(h, wr_ref[...], preferred_element_type=jnp.float32, precision=HIGHEST) + br_ref[...]


def _mix_out(attn2, ssd2, proj, x2, g_m, w_attn_o, w_ssm_o, w_out, norm_ffn, sc_f, sh_f, w_router, b_router, seq, tm=512):
    t, d = x2.shape
    per_b = seq // tm
    bf = jnp.bfloat16
    const = lambda shape: pl.BlockSpec(shape, lambda i: (0,) * len(shape))
    perb = pl.BlockSpec((1, 1, d), lambda i: (i // per_b, 0, 0))
    wr = jnp.pad(w_router, ((0, 0), (0, 128 - N_EXPERTS)))
    br = jnp.pad(b_router, (0, 128 - N_EXPERTS)).reshape(1, 128)
    return pl.pallas_call(
        _mix_out_kernel,
        grid=(t // tm,),
        in_specs=[pl.BlockSpec((tm, ATTN_WIDTH), lambda i: (i, 0)),
                  pl.BlockSpec((tm, SSM_INNER), lambda i: (i, 0)),
                  pl.BlockSpec((tm, 2 * d), lambda i: (i, COL_GATE // (2 * d))),
                  pl.BlockSpec((tm, d), lambda i: (i, 0)),
                  perb,
                  const((ATTN_WIDTH, d)), const((SSM_INNER, d)), const((d, d)),
                  const((1, d)), perb, perb, const((d, 128)), const((1, 128))],
        out_specs=[pl.BlockSpec((tm, d), lambda i: (i, 0)),
                   pl.BlockSpec((tm, d), lambda i: (i, 0)),
                   pl.BlockSpec((tm, 128), lambda i: (i, 0))],
        out_shape=[jax.ShapeDtypeStruct((t, d), jnp.float32),
                   jax.ShapeDtypeStruct((t, d), bf),
                   jax.ShapeDtypeStruct((t, 128), jnp.float32)],
        compiler_params=pltpu.CompilerParams(dimension_semantics=("parallel",), vmem_limit_bytes=VMEM_LIMIT),
        name="mix_out",
    )(attn2, ssd2, proj, x2, g_m[:, None, :], w_attn_o.astype(bf), w_ssm_o.astype(bf), w_out.astype(bf),
      norm_ffn.reshape(1, d), sc_f[:, None, :], sh_f[:, None, :], wr, br)


def _moe_kernel(be_ref, nb_ref, x_ref, g_ref, wgu_ref, bgu_ref, wdn_ref, bdn_ref, o_ref, wgu_bf, wdn_bf):
    i = pl.program_id(0)

    @pl.when((i == 0) | (be_ref[i] != be_ref[jnp.maximum(i - 1, 0)]))
    def _():
        wgu_bf[...] = wgu_ref[0].astype(jnp.bfloat16)
        wdn_bf[...] = wdn_ref[0].astype(jnp.bfloat16)

    @pl.when(i < nb_ref[0])
    def _():
        gu = jnp.dot(x_ref[...], wgu_bf[...], preferred_element_type=jnp.float32) + bgu_ref[0]
        g = jnp.minimum(gu[:, :D_EXPERT], SWIGLU_LIMIT)
        u = jnp.clip(gu[:, D_EXPERT:], -SWIGLU_LIMIT, SWIGLU_LIMIT)
        act = (u + 1.0) * (g * jax.nn.sigmoid(SWIGLU_ALPHA * g))
        out = jnp.dot(act.astype(jnp.bfloat16), wdn_bf[...], preferred_element_type=jnp.float32) + bdn_ref[0]
        o_ref[...] = out * g_ref[...]

    @pl.when(i >= nb_ref[0])
    def _():
        o_ref[...] = jnp.zeros_like(o_ref)


def _moe_ffn(xs, row_gate, blk_exp, n_used, w_gu, b_gu, w_dn, b_dn):
    n_rows, d = xs.shape
    tm = MOE_TM
    grid_spec = pltpu.PrefetchScalarGridSpec(
        num_scalar_prefetch=2,
        grid=(n_rows // tm,),
        in_specs=[pl.BlockSpec((tm, d), lambda i, be, nb: (i, 0)),
                  pl.BlockSpec((tm, 1), lambda i, be, nb: (i, 0)),
                  pl.BlockSpec((1, d, 2 * D_EXPERT), lambda i, be, nb: (be[i], 0, 0)),
                  pl.BlockSpec((1, 1, 2 * D_EXPERT), lambda i, be, nb: (be[i], 0, 0)),
                  pl.BlockSpec((1, D_EXPERT, d), lambda i, be, nb: (be[i], 0, 0)),
                  pl.BlockSpec((1, 1, d), lambda i, be, nb: (be[i], 0, 0))],
        out_specs=pl.BlockSpec((tm, d), lambda i, be, nb: (i, 0)),
        scratch_shapes=[pltpu.VMEM((d, 2 * D_EXPERT), jnp.bfloat16), pltpu.VMEM((D_EXPERT, d), jnp.bfloat16)],
    )
    return pl.pallas_call(
        _moe_kernel,
        grid_spec=grid_spec,
        out_shape=jax.ShapeDtypeStruct((n_rows, d), jnp.float32),
        compiler_params=pltpu.CompilerParams(dimension_semantics=("arbitrary",), vmem_limit_bytes=VMEM_LIMIT),
        name="moe_ffn",
    )(blk_exp, n_used, xs, row_gate[:, None], w_gu, b_gu[:, None, :], w_dn, b_dn[:, None, :])


def _moe(h2, logits, w_gu, b_gu, w_dn, b_dn):
    t, d = h2.shape
    tm = MOE_TM
    i32 = jnp.int32
    top_val, top_idx = lax.top_k(logits, TOP_K)
    gates = jax.nn.softmax(top_val, axis=-1).reshape(-1)
    n_assign = t * TOP_K
    n_rows = n_assign + N_EXPERTS * tm
    e_flat = top_idx.reshape(n_assign).astype(i32)
    counts = jnp.sum((e_flat[:, None] == jnp.arange(N_EXPERTS, dtype=i32)[None, :]).astype(i32), axis=0)
    padded = (counts + tm - 1) // tm * tm
    filler_exp = jnp.repeat(jnp.arange(N_EXPERTS, dtype=i32), tm)
    filler_key = jnp.where(jnp.tile(jnp.arange(tm, dtype=i32), N_EXPERTS) < jnp.repeat(padded - counts, tm),
                           filler_exp, N_EXPERTS)
    keys = jnp.concatenate([e_flat, filler_key])
    gate_in = jnp.concatenate([gates, jnp.zeros((N_EXPERTS * tm,), jnp.float32)])
    rows = jnp.arange(n_rows, dtype=i32)
    row_key, row_src, row_gate = lax.sort((keys, rows, gate_in), num_keys=1)
    row_tok = jnp.where(row_src < n_assign, row_src // TOP_K, 0)
    _, row_of = lax.sort((row_src, rows), num_keys=1)
    dest = row_of[:n_assign]
    blk_exp = jnp.minimum(row_key[::tm], N_EXPERTS - 1)
    n_used = (jnp.sum(padded, keepdims=True) // tm).astype(i32)
    out = _moe_ffn(h2[row_tok], row_gate, blk_exp, n_used, w_gu, b_gu, w_dn, b_dn)
    return jnp.sum(out[dest].reshape(t, TOP_K, d), axis=1)


def kernel(x, c, rel_bias, w_ada, b_ada, norm_mix, norm_ffn, w_in, kv_norm, w_kv_up, q_norm, k_norm,
           idx_k_ln_w, idx_k_ln_b, w_attn_o, conv_w, conv_b, dt_bias, a_log, d_skip, ssm_norm, w_ssm_o,
           w_out, w_router, b_router, w_gu, b_gu, w_dn, b_dn):
    bsz, seq, d = x.shape
    t = bsz * seq
    cond = jax.nn.silu(c)
    x2 = x.reshape(t, d)
    for l in range(DEPTH):
        mod = cond @ w_ada[l] + b_ada[l]
        sh_m, sc_m, g_m, sh_f, sc_f, g_f = jnp.split(mod, 6, axis=-1)
        proj = _in_proj(x2, norm_mix[l], sc_m, sh_m, _pack_w_in(w_in[l]), seq)
        qT, k, vT, qiT, ki2, wT = _prep(proj, bsz, seq, q_norm[l], kv_norm[l], w_kv_up[l], k_norm[l],
                                        idx_k_ln_w[l], idx_k_ln_b[l])
        attn = _dsa_attention(qT, qiT, wT, k, vT, ki2, rel_bias)
        p3 = proj.reshape(bsz, seq, PROJ_COLS)
        small = p3[..., COL_SMALL:COL_SMALL + 128]
        y_ssd = _mamba2_ssd(p3[..., COL_Z:COL_Z + SSM_INNER], p3[..., COL_XBC:COL_XBC + CONV_CH],
                            small[..., SMALL_DT:SMALL_DT + SSM_HEADS],
                            conv_w[l], conv_b[l], dt_bias[l], a_log[l], d_skip[l], ssm_norm[l])
        x2, h2, logits = _mix_out(attn.reshape(t, ATTN_WIDTH), y_ssd.reshape(t, SSM_INNER), proj, x2, g_m,
                                  w_attn_o[l], w_ssm_o[l], w_out[l], norm_ffn[l], sc_f, sh_f,
                                  w_router[l], b_router[l], seq)
        y = _moe(h2, logits[:, :N_EXPERTS], w_gu[l], b_gu[l], w_dn[l], b_dn[l])
        x2 = x2 + jnp.repeat(g_f, seq, axis=0) * y
    return x2.reshape(bsz, seq, d)
```

```python
import functools
import math

import jax
import jax.numpy as jnp
import numpy as np
from jax import lax
from jax.experimental import pallas as pl
from jax.experimental.pallas import tpu as pltpu

D_MODEL = 1024
DEPTH = 2
ATTN_HEADS = 8
ATTN_HEAD_DIM = 64
ATTN_WIDTH = ATTN_HEADS * ATTN_HEAD_DIM
KV_RANK = 256
IDX_HEADS = 8
IDX_DIM = 64
TOPK_MAX = 256
N_BUCKETS = 32
MAX_DISTANCE = 128
SSM_HEADS = 16
SSM_HEAD_DIM = 64
SSM_INNER = SSM_HEADS * SSM_HEAD_DIM
SSM_GROUPS = 2
SSM_STATE = 128
CONV_WIDTH = 4
CONV_CH = SSM_INNER + 2 * SSM_GROUPS * SSM_STATE
SSD_CHUNK = 128
N_EXPERTS = 32
TOP_K = 4
D_EXPERT = D_MODEL
SWIGLU_LIMIT = 7.0
SWIGLU_ALPHA = 1.702
EPS = 1e-6

COL_Q = 0
COL_KV = 512
COL_QI = 768
COL_SMALL = 1280
COL_XBC = 1536
COL_Z = 3072
COL_GATE = 4096
PROJ_COLS = 6144
PREP_COLS = 1408
SMALL_KI, SMALL_WI, SMALL_DT = 0, 64, 72

QB = 256
VROWS = 80
INT_MIN = -2 ** 31
KEY_NEG_INF = (0xFF800000 ^ 0x7FFFFFFF) - 2 ** 32
NEG = -1e30
TINY = 2.0 ** -126
LOG2E = math.log2(math.e)
VMEM_LIMIT = 56 * 1024 * 1024
MOE_TM = 512
HIGHEST = lax.Precision.HIGHEST
NT = (((1,), (1,)), ((), ()))


def _pack_w_in(w):
    o = np.cumsum((0, ATTN_WIDTH, KV_RANK, IDX_HEADS * IDX_DIM, IDX_DIM, IDX_HEADS, SSM_INNER, CONV_CH, SSM_HEADS, 2 * D_MODEL))
    q, kv, qi, ki, wi, z, xbc, dt, gate = (w[:, int(o[n]):int(o[n + 1])] for n in range(9))
    zeros = lambda n: jnp.zeros((w.shape[0], n), w.dtype)
    small = jnp.concatenate([ki, wi, dt, zeros(128 - 88)], axis=1)
    packed = jnp.concatenate([q, kv, qi, small, zeros(COL_XBC - PREP_COLS), xbc, z, gate], axis=1)
    assert packed.shape[1] == PROJ_COLS
    return packed.astype(jnp.bfloat16)


def _in_proj_kernel(x_ref, g_ref, sc_ref, sh_ref, w_ref, o_ref, h_ref):
    @pl.when(pl.program_id(1) == 0)
    def _():
        x = x_ref[...]
        y = x * lax.rsqrt(jnp.mean(x * x, axis=-1, keepdims=True) + EPS) * g_ref[...]
        h_ref[...] = (y * (1.0 + sc_ref[0]) + sh_ref[0]).astype(jnp.bfloat16)
    o_ref[...] = jnp.dot(h_ref[...], w_ref[...], preferred_element_type=jnp.float32)


def _in_proj(x2, gain, sc, sh, w_packed, seq, tm=1024, tn=512):
    t, d = x2.shape
    per_b = seq // tm
    return pl.pallas_call(
        _in_proj_kernel,
        grid=(t // tm, PROJ_COLS // tn),
        in_specs=[pl.BlockSpec((tm, d), lambda i, j: (i, 0)),
                  pl.BlockSpec((1, d), lambda i, j: (0, 0)),
                  pl.BlockSpec((1, 1, d), lambda i, j: (i // per_b, 0, 0)),
                  pl.BlockSpec((1, 1, d), lambda i, j: (i // per_b, 0, 0)),
                  pl.BlockSpec((d, tn), lambda i, j: (0, j))],
        out_specs=pl.BlockSpec((tm, tn), lambda i, j: (i, j)),
        out_shape=jax.ShapeDtypeStruct((t, PROJ_COLS), jnp.float32),
        scratch_shapes=[pltpu.VMEM((tm, d), jnp.bfloat16)],
        compiler_params=pltpu.CompilerParams(dimension_semantics=("parallel", "arbitrary"),
                                             vmem_limit_bytes=VMEM_LIMIT),
        name="in_proj",
    )(x2, gain.reshape(1, d), sc[:, None, :], sh[:, None, :], w_packed)


def _head_rms_t(xt):
    x3 = xt.reshape(ATTN_HEADS, ATTN_HEAD_DIM, xt.shape[1])
    return lax.rsqrt(jnp.mean(x3 * x3, axis=1, keepdims=True) + EPS)


def _prep_kernel(p_ref, qg_ref, kvg_ref, wkv_ref, kg_ref, lng_ref, lnb_ref,
                 qT_ref, k_ref, vT_ref, qiT_ref, ki_ref, wT_ref):
    n = p_ref.shape[0]
    q = p_ref[:, COL_Q:COL_Q + ATTN_WIDTH]
    lat = p_ref[:, COL_KV:COL_KV + KV_RANK]
    qi = p_ref[:, COL_QI:COL_QI + IDX_HEADS * IDX_DIM]
    sm = p_ref[:, COL_SMALL:COL_SMALL + 128]

    scale = ATTN_HEAD_DIM ** -0.5 * LOG2E
    qt = q.T
    qn = qt.reshape(ATTN_HEADS, ATTN_HEAD_DIM, n) * _head_rms_t(qt)
    qT_ref[0] = (qn.reshape(ATTN_WIDTH, n) * qg_ref[...] * scale).astype(jnp.bfloat16)

    latn = lat * lax.rsqrt(jnp.mean(lat * lat, axis=-1, keepdims=True) + EPS) * kvg_ref[...]
    kv = jnp.dot(latn.astype(jnp.bfloat16), wkv_ref[...], preferred_element_type=jnp.float32)
    kt = kv[:, :ATTN_WIDTH].T
    kn = (kt.reshape(ATTN_HEADS, ATTN_HEAD_DIM, n) * _head_rms_t(kt)).reshape(ATTN_WIDTH, n) * kg_ref[...]
    k_ref[0] = kn.T.astype(jnp.bfloat16)
    vt = kv[:, ATTN_WIDTH:].T.reshape(ATTN_HEADS, ATTN_HEAD_DIM, n)
    ones = jnp.ones((ATTN_HEADS, VROWS - ATTN_HEAD_DIM, n), jnp.float32)
    vT_ref[0] = jnp.concatenate([vt, ones], axis=1).reshape(ATTN_HEADS * VROWS, n).astype(jnp.bfloat16)

    qiT_ref[0] = (qi * (IDX_DIM ** -0.5)).T.astype(jnp.bfloat16)

    lane = lax.broadcasted_iota(jnp.int32, sm.shape, 1)
    kid = jnp.where(lane < IDX_DIM, sm, pltpu.roll(sm, IDX_DIM, 1))
    mu = jnp.mean(kid, axis=-1, keepdims=True)
    var = jnp.mean(jnp.square(kid - mu), axis=-1, keepdims=True)
    ki_ref[0] = ((kid - mu) * lax.rsqrt(var + EPS) * lng_ref[...] + lnb_ref[...]).astype(jnp.bfloat16)

    wT_ref[0] = sm.T[SMALL_WI:SMALL_WI + IDX_HEADS, :] * (IDX_HEADS ** -0.5)


def _prep(proj, bsz, seq, q_norm, kv_norm, w_kv_up, k_norm, ln_w, ln_b, tp=512):
    nb = seq // tp
    tile8 = lambda g: jnp.tile(g, ATTN_HEADS).reshape(ATTN_WIDTH, 1)
    const = lambda shape: pl.BlockSpec(shape, lambda b, i: (0,) * len(shape))
    bf = jnp.bfloat16
    return pl.pallas_call(
        _prep_kernel,
        grid=(bsz, nb),
        in_specs=[pl.BlockSpec((tp, PREP_COLS), lambda b, i: (b * nb + i, 0)),
                  const((ATTN_WIDTH, 1)), const((1, KV_RANK)), const((KV_RANK, 2 * ATTN_WIDTH)),
                  const((ATTN_WIDTH, 1)), const((1, 128)), const((1, 128))],
        out_specs=[pl.BlockSpec((1, ATTN_WIDTH, tp), lambda b, i: (b, 0, i)),
                   pl.BlockSpec((1, tp, ATTN_WIDTH), lambda b, i: (b, i, 0)),
                   pl.BlockSpec((1, ATTN_HEADS * VROWS, tp), lambda b, i: (b, 0, i)),
                   pl.BlockSpec((1, ATTN_WIDTH, tp), lambda b, i: (b, 0, i)),
                   pl.BlockSpec((1, tp, 128), lambda b, i: (b, i, 0)),
                   pl.BlockSpec((1, IDX_HEADS, tp), lambda b, i: (b, 0, i))],
        out_shape=[jax.ShapeDtypeStruct((bsz, ATTN_WIDTH, seq), bf),
                   jax.ShapeDtypeStruct((bsz, seq, ATTN_WIDTH), bf),
                   jax.ShapeDtypeStruct((bsz, ATTN_HEADS * VROWS, seq), bf),
                   jax.ShapeDtypeStruct((bsz, ATTN_WIDTH, seq), bf),
                   jax.ShapeDtypeStruct((bsz, seq, 128), bf),
                   jax.ShapeDtypeStruct((bsz, IDX_HEADS, seq), jnp.float32)],
        compiler_params=pltpu.CompilerParams(dimension_semantics=("parallel", "parallel"),
                                             vmem_limit_bytes=VMEM_LIMIT),
        name="attn_prep",
    )(proj, tile8(q_norm), kv_norm.reshape(1, KV_RANK), w_kv_up.astype(bf), tile8(k_norm),
      jnp.tile(ln_w, 2).reshape(1, 128), jnp.tile(ln_b, 2).reshape(1, 128))


def _t5_bucket(dist):
    n = jnp.maximum(dist, 0)
    max_exact = N_BUCKETS // 2
    nf = jnp.maximum(n, 1).astype(jnp.float32)
    large = max_exact + (jnp.log(nf / max_exact) / math.log(MAX_DISTANCE / max_exact) * (N_BUCKETS - max_exact)).astype(jnp.int32)
    large = jnp.minimum(large, N_BUCKETS - 1)
    return jnp.where(n < max_exact, n, large)


def _bias_tables(rel_bias):
    s = jnp.arange(QB, dtype=jnp.int32)[:, None]
    q = jnp.arange(QB, dtype=jnp.int32)[None, :]
    tabs = []
    for off in (2 * QB, QB, 0):
        dist = q - s + off
        b = jnp.moveaxis(rel_bias[_t5_bucket(dist)], -1, 0).astype(jnp.float32) * LOG2E
        tabs.append(jnp.where((dist >= 0)[None], b, NEG))
    return jnp.stack(tabs)


def _attn_kernel(qT_ref, qiT_ref, wT_ref, k_ref, vT_ref, ki_ref, tab_ref, o_ref,
                 keys_ref, hi_ref, msk_ref, p_ref, acc_ref, mp_ref, *, topk):
    i = pl.program_id(1)
    n_tiles = i + 1
    row_hi = lax.broadcasted_iota(jnp.int32, (128, QB), 0) >= 64

    def head_rows(ref, h):
        pair = ref[0, (h // 2) * 128:(h // 2) * 128 + 128, :]
        return jnp.where(row_hi == bool(h % 2), pair, jnp.zeros_like(pair))

    def tile_rows(kt):
        return pl.ds(pl.multiple_of(kt * QB, QB), QB)

    def score_tile(kt, carry):
        ki = ki_ref[0, tile_rows(kt), :]
        sc = jnp.zeros((QB, QB), jnp.float32)
        for h in range(IDX_HEADS):
            d = jnp.dot(ki, head_rows(qiT_ref, h), preferred_element_type=jnp.float32)
            sc = sc + wT_ref[0, h:h + 1, :] * jnp.maximum(d, 0.0)
        srow = lax.broadcasted_iota(jnp.int32, (QB, QB), 0)
        qcol = lax.broadcasted_iota(jnp.int32, (QB, QB), 1)
        sc = jnp.where(jnp.abs(sc) < TINY, 0.0, sc)
        sc = jnp.where((kt == i) & (srow > qcol), -jnp.inf, sc)
        bits = pltpu.bitcast(sc, jnp.int32)
        keys_ref[tile_rows(kt), :] = bits ^ ((bits >> 31) & 0x7FFFFFFF)
        hi_ref[tile_rows(kt), :] = pltpu.bitcast(bits & jnp.int32(-65536), jnp.float32).astype(jnp.bfloat16)
        return carry
    lax.fori_loop(0, n_tiles, score_tile, 0)

    def count_hi_ge(cand16):
        b = cand16 ^ ((cand16 >> 15) & 0x7FFF)
        snap = jnp.where(((b & 0x8000) != 0) | ((b & 0x7F) == 0), 0, 0x0080)
        b = jnp.where((b & 0x7F80) == 0, snap, b)
        cb = pltpu.bitcast(b << 16, jnp.float32).astype(jnp.bfloat16)
        one, zero = jnp.ones((), jnp.bfloat16), jnp.zeros((), jnp.bfloat16)

        def body(kt, acc):
            hit = jnp.where(hi_ref[tile_rows(kt), :] >= cb, one, zero)
            parts = [hit[r:r + 16, :] for r in range(0, QB, 16)]
            while len(parts) > 1:
                parts = [a + b for a, b in zip(parts[::2], parts[1::2])]
            return acc + parts[0]
        acc = lax.fori_loop(0, n_tiles, body, jnp.zeros((16, QB), jnp.bfloat16))
        return jnp.sum(acc.astype(jnp.float32), axis=0, keepdims=True)

    def count(hit_of_tile):
        def body(kt, acc):
            return acc + jnp.sum(hit_of_tile(kt).reshape(QB // 8, 8, QB), axis=0)
        acc = lax.fori_loop(0, n_tiles, body, jnp.zeros((8, QB), jnp.int32))
        return jnp.sum(acc, axis=0, keepdims=True)

    def count_ge(cand):
        return count(lambda kt: jnp.where(keys_ref[tile_rows(kt), :] >= cand, 1, 0))

    def hi_step(it, r):
        cand = jnp.where(it == 0, jnp.zeros_like(r), r | (1 << (15 - it)))
        return jnp.where(count_hi_ge(cand) >= topk, cand, r)
    r16 = lax.fori_loop(0, 16, hi_step, jnp.full((1, QB), -32768, jnp.int32))

    def lo_step(it, r):
        cand = r | (1 << (15 - it))
        return jnp.where(count_ge(cand) >= topk, cand, r)
    thr = lax.fori_loop(0, 16, lo_step, r16 << 16)

    cnt_gt = count_ge(thr + 1)
    cnt_ge = count_ge(thr)
    need = topk - cnt_gt
    tie = (cnt_ge - cnt_gt > need) & (thr > KEY_NEG_INF)

    @pl.when(jnp.max(tie.astype(jnp.int32)) > 0)
    def _():
        def count_eq_below(cand):
            def ind(kt):
                idx = lax.broadcasted_iota(jnp.int32, (QB, QB), 0) + kt * QB
                return jnp.where((keys_ref[tile_rows(kt), :] == thr) & (idx < cand), 1, 0)
            return count(ind)

        def idx_step(it, r):
            cand = r | (1 << (15 - it))
            return jnp.where(count_eq_below(cand) < need, cand, r)
        last = lax.fori_loop(0, 16, idx_step, jnp.zeros((1, QB), jnp.int32))

        def drop(kt, carry):
            blk = keys_ref[tile_rows(kt), :]
            idx = lax.broadcasted_iota(jnp.int32, (QB, QB), 0) + kt * QB
            keys_ref[tile_rows(kt), :] = jnp.where(tie & (blk == thr) & (idx > last), INT_MIN, blk)
            return carry
        lax.fori_loop(0, n_tiles, drop, 0)

    def logits(kt, h):
        band = jnp.clip(kt - (i - 2), 0, 2)
        kp = k_ref[0, tile_rows(kt), (h // 2) * 128:(h // 2) * 128 + 128]
        s = jnp.dot(kp, head_rows(qT_ref, h), preferred_element_type=jnp.float32)
        return s + msk_ref[...] + tab_ref[band, h]

    def set_mask(kt):
        msk_ref[...] = jnp.where(keys_ref[tile_rows(kt), :] >= thr, 0.0, NEG)

    def max_tile(kt, carry):
        set_mask(kt)
        for h in range(ATTN_HEADS):
            s = logits(kt, h)
            mp_ref[h] = jnp.maximum(mp_ref[h], jnp.max(s.reshape(QB // 8, 8, QB), axis=0))
        return carry

    mp_ref[...] = jnp.full(mp_ref.shape, NEG, jnp.float32)
    lax.fori_loop(0, n_tiles, max_tile, 0)
    m = [jnp.max(mp_ref[h], axis=0, keepdims=True) for h in range(ATTN_HEADS)]

    def exp_tile(kt, carry):
        set_mask(kt)
        for h in range(ATTN_HEADS):
            p_ref[h] = jnp.exp2(logits(kt, h) - m[h]).astype(jnp.bfloat16)
        for h in range(ATTN_HEADS):
            va = vT_ref[0, h * VROWS:(h + 1) * VROWS, tile_rows(kt)]
            acc_ref[h * VROWS:(h + 1) * VROWS, :] += jnp.dot(va, p_ref[h], preferred_element_type=jnp.float32)
        return carry

    acc_ref[...] = jnp.zeros(acc_ref.shape, jnp.float32)
    lax.fori_loop(0, n_tiles, exp_tile, 0)

    outs = [acc_ref[h * VROWS:h * VROWS + ATTN_HEAD_DIM, :] / acc_ref[h * VROWS + ATTN_HEAD_DIM:h * VROWS + ATTN_HEAD_DIM + 1, :]
            for h in range(ATTN_HEADS)]
    o_ref[0] = jnp.concatenate(outs, axis=0).T


def _dsa_attention(qT, qiT, wT, k, vT, ki2, rel_bias):
    bsz, _, seq = qT.shape
    topk = min(TOPK_MAX, seq // 4)
    assert seq % QB == 0 and topk <= QB
    assert seq // 16 <= 256
    return pl.pallas_call(
        functools.partial(_attn_kernel, topk=topk),
        grid=(bsz, seq // QB),
        in_specs=[
            pl.BlockSpec((1, ATTN_WIDTH, QB), lambda b, i: (b, 0, i)),
            pl.BlockSpec((1, IDX_HEADS * IDX_DIM, QB), lambda b, i: (b, 0, i)),
            pl.BlockSpec((1, IDX_HEADS, QB), lambda b, i: (b, 0, i)),
            pl.BlockSpec((1, seq, ATTN_WIDTH), lambda b, i: (b, 0, 0)),
            pl.BlockSpec((1, ATTN_HEADS * VROWS, seq), lambda b, i: (b, 0, 0)),
            pl.BlockSpec((1, seq, 128), lambda b, i: (b, 0, 0)),
            pl.BlockSpec((3, ATTN_HEADS, QB, QB), lambda b, i: (0, 0, 0, 0)),
        ],
        out_specs=pl.BlockSpec((1, QB, ATTN_WIDTH), lambda b, i: (b, i, 0)),
        out_shape=jax.ShapeDtypeStruct((bsz, seq, ATTN_WIDTH), jnp.float32),
        scratch_shapes=[
            pltpu.VMEM((seq, QB), jnp.int32),
            pltpu.VMEM((seq, QB), jnp.bfloat16),
            pltpu.VMEM((QB, QB), jnp.float32),
            pltpu.VMEM((ATTN_HEADS, QB, QB), jnp.bfloat16),
            pltpu.VMEM((ATTN_HEADS * VROWS, QB), jnp.float32),
            pltpu.VMEM((ATTN_HEADS, 8, QB), jnp.float32),
        ],
        compiler_params=pltpu.CompilerParams(dimension_semantics=("parallel", "arbitrary"),
                                             vmem_limit_bytes=VMEM_LIMIT),
        name="dsa_attention",
    )(qT, qiT, wT, k, vT, ki2, _bias_tables(rel_bias))


def _ssd_kernel(xbc_ref, z_ref, sm_ref, cw_ref, cb_ref, dtb_ref, a_ref, dsk_ref, nw_ref, y_ref, prev_ref, st_ref):
    q = SSD_CHUNK
    bf = jnp.bfloat16

    @pl.when(pl.program_id(1) == 0)
    def _():
        prev_ref[...] = jnp.zeros(prev_ref.shape, jnp.float32)
        st_ref[...] = jnp.zeros(st_ref.shape, jnp.float32)

    cur = xbc_ref[...]
    prev = prev_ref[...]
    row = lax.broadcasted_iota(jnp.int32, cur.shape, 0)
    acc = cur * cw_ref[CONV_WIDTH - 1:CONV_WIDTH, :] + cb_ref[...]
    for s in range(1, CONV_WIDTH):
        shifted = jnp.where(row >= s, pltpu.roll(cur, s, 0), pltpu.roll(prev, s, 0))
        acc = acc + shifted * cw_ref[CONV_WIDTH - 1 - s:CONV_WIDTH - s, :]
    prev_ref[...] = cur
    u = acc * jax.nn.sigmoid(acc)
    xs = u[:, :SSM_INNER]
    bm = u[:, SSM_INNER:SSM_INNER + SSM_GROUPS * SSM_STATE].astype(bf)
    cm = u[:, SSM_INNER + SSM_GROUPS * SSM_STATE:].astype(bf)

    t = sm_ref[...] + dtb_ref[...]
    dt = jnp.maximum(t, 0.0) + jnp.log1p(jnp.exp(-jnp.abs(t)))
    ii = lax.broadcasted_iota(jnp.int32, (q, q), 0)
    jj = lax.broadcasted_iota(jnp.int32, (q, q), 1)
    causal = ii >= jj
    acum = jnp.dot(causal.astype(jnp.float32), dt * a_ref[...], preferred_element_type=jnp.float32, precision=HIGHEST)
    acum_t = acum.T
    dt_t = dt.T
    ea = jnp.exp(acum)
    last = acum[q - 1:q, :]
    decay = jnp.exp(last - acum) * dt
    ea_last = jnp.exp(last)

    lane_hi = lax.broadcasted_iota(jnp.int32, (q, 128), 1) >= SSM_HEAD_DIM
    row_hi = lax.broadcasted_iota(jnp.int32, (128, SSM_STATE), 0) >= SSM_HEAD_DIM

    def pair_cols(v, e):
        c0, c1 = SMALL_DT + e, SMALL_DT + e + 1
        return jnp.where(lane_hi, v[:, c1:c1 + 1], v[:, c0:c0 + 1])

    for g in range(SSM_GROUPS):
        bg = bm[:, g * SSM_STATE:(g + 1) * SSM_STATE]
        cg = cm[:, g * SSM_STATE:(g + 1) * SSM_STATE]
        cb = lax.dot_general(cg, bg, NT, preferred_element_type=jnp.float32)
        for k in range(g * 4, g * 4 + 4):
            e = 2 * k
            x_pair = xs[:, k * 128:(k + 1) * 128]
            halves = []
            for h in (e, e + 1):
                c = SMALL_DT + h
                seg = acum[:, c:c + 1] - acum_t[c:c + 1, :]
                w = cb * jnp.exp(jnp.where(causal, seg, -jnp.inf)) * dt_t[c:c + 1, :]
                halves.append(jnp.dot(w.astype(bf), x_pair.astype(bf), preferred_element_type=jnp.float32))
            y_pair = jnp.where(lane_hi, halves[1], halves[0])
            state = st_ref[k]
            y_pair = y_pair + lax.dot_general(cg, state.astype(bf), NT, preferred_element_type=jnp.float32) * pair_cols(ea, e)
            y_ref[:, k * 128:(k + 1) * 128] = y_pair
            xd_t = (x_pair * pair_cols(decay, e)).T.astype(bf)
            c0 = SMALL_DT + e
            keep = jnp.where(row_hi, ea_last[:, c0 + 1:c0 + 2], ea_last[:, c0:c0 + 1])
            st_ref[k] = state * keep + jnp.dot(xd_t, bg, preferred_element_type=jnp.float32)

    y = (y_ref[...] + dsk_ref[...] * xs) * (z_ref[...] * jax.nn.sigmoid(z_ref[...]))
    half = SSM_INNER // SSM_GROUPS
    for g in range(SSM_GROUPS):
        yg = y[:, g * half:(g + 1) * half]
        yg = yg * lax.rsqrt(jnp.mean(yg * yg, axis=-1, keepdims=True) + EPS)
        y_ref[:, g * half:(g + 1) * half] = yg * nw_ref[:, g * half:(g + 1) * half]


def _mamba2_ssd(proj, bsz, seq, conv_w, conv_b, dt_bias, a_log, d_skip, norm_w):
    q = SSD_CHUNK
    nc = seq // q
    lane_row = lambda v: jnp.zeros((1, 128), jnp.float32).at[0, SMALL_DT:SMALL_DT + SSM_HEADS].set(v)
    const = lambda shape: pl.BlockSpec(shape, lambda b, c: (0,) * len(shape))
    return pl.pallas_call(
        _ssd_kernel,
        grid=(bsz, nc),
        in_specs=[pl.BlockSpec((q, CONV_CH), lambda b, c: (b * nc + c, COL_XBC // CONV_CH)),
                  pl.BlockSpec((q, SSM_INNER), lambda b, c: (b * nc + c, COL_Z // SSM_INNER)),
                  pl.BlockSpec((q, 128), lambda b, c: (b * nc + c, COL_SMALL // 128)),
                  const((CONV_WIDTH, CONV_CH)), const((1, CONV_CH)), const((1, 128)), const((1, 128)),
                  const((1, SSM_INNER)), const((1, SSM_INNER))],
        out_specs=pl.BlockSpec((q, SSM_INNER), lambda b, c: (b * nc + c, 0)),
        out_shape=jax.ShapeDtypeStruct((bsz * seq, SSM_INNER), jnp.float32),
        scratch_shapes=[pltpu.VMEM((q, CONV_CH), jnp.float32),
                        pltpu.VMEM((SSM_HEADS // 2, 2 * SSM_HEAD_DIM, SSM_STATE), jnp.float32)],
        compiler_params=pltpu.CompilerParams(dimension_semantics=("parallel", "arbitrary"),
                                             vmem_limit_bytes=VMEM_LIMIT),
        name="mamba2_ssd",
    )(proj, proj, proj, conv_w, conv_b.reshape(1, CONV_CH), lane_row(dt_bias), lane_row(-jnp.exp(a_log)),
      jnp.repeat(d_skip, SSM_HEAD_DIM).reshape(1, SSM_INNER), norm_w.reshape(1, SSM_INNER))


def _mix_out_kernel(a_ref, s_ref, gl_ref, x_ref, gm_ref, wo_ref, ws_ref, wout_ref,
                    nf_ref, scf_ref, shf_ref, wr_ref, br_ref, xo_ref, h_ref, lg_ref):
    bf = jnp.bfloat16
    ya = jnp.dot(a_ref[...].astype(bf), wo_ref[...], preferred_element_type=jnp.float32)
    ys = jnp.dot(s_ref[...].astype(bf), ws_ref[...], preferred_element_type=jnp.float32)
    mixed = jax.nn.sigmoid(gl_ref[:, :D_MODEL]) * ya + jax.nn.sigmoid(gl_ref[:, D_MODEL:]) * ys
    x = x_ref[...] + gm_ref[0] * jnp.dot(mixed.astype(bf), wout_ref[...], preferred_element_type=jnp.float32)
    xo_ref[...] = x
    y = x * lax.rsqrt(jnp.mean(x * x, axis=-1, keepdims=True) + EPS) * nf_ref[...]
    h = y * (1.0 + scf_ref[0]) + shf_ref[0]
    h_ref[...] = h.astype(bf)
    lg_ref[...] = jnp.dot(h, wr_ref[...], preferred_element_type=jnp.float32, precision=HIGHEST) + br_ref[...]


def _mix_out(attn2, ssd2, proj, x2, g_m, w_attn_o, w_ssm_o, w_out, norm_ffn, sc_f, sh_f, w_router, b_router, seq, tm=512):
    t, d = x2.shape
    per_b = seq // tm
    bf = jnp.bfloat16
    const = lambda shape: pl.BlockSpec(shape, lambda i: (0,) * len(shape))
    perb = pl.BlockSpec((1, 1, d), lambda i: (i // per_b, 0, 0))
    wr = jnp.pad(w_router, ((0, 0), (0, 128 - N_EXPERTS)))
    br = jnp.pad(b_router, (0, 128 - N_EXPERTS)).reshape(1, 128)
    return pl.pallas_call(
        _mix_out_kernel,
        grid=(t // tm,),
        in_specs=[pl.BlockSpec((tm, ATTN_WIDTH), lambda i: (i, 0)),
                  pl.BlockSpec((tm, SSM_INNER), lambda i: (i, 0)),
                  pl.BlockSpec((tm, 2 * d), lambda i: (i, COL_GATE // (2 * d))),
                  pl.BlockSpec((tm, d), lambda i: (i, 0)),
                  perb,
                  const((ATTN_WIDTH, d)), const((SSM_INNER, d)), const((d, d)),
                  const((1, d)), perb, perb, const((d, 128)), const((1, 128))],
        out_specs=[pl.BlockSpec((tm, d), lambda i: (i, 0)),
                   pl.BlockSpec((tm, d), lambda i: (i, 0)),
                   pl.BlockSpec((tm, 128), lambda i: (i, 0))],
        out_shape=[jax.ShapeDtypeStruct((t, d), jnp.float32),
                   jax.ShapeDtypeStruct((t, d), bf),
                   jax.ShapeDtypeStruct((t, 128), jnp.float32)],
        compiler_params=pltpu.CompilerParams(dimension_semantics=("parallel",), vmem_limit_bytes=VMEM_LIMIT),
        name="mix_out",
    )(attn2, ssd2, proj, x2, g_m[:, None, :], w_attn_o.astype(bf), w_ssm_o.astype(bf), w_out.astype(bf),
      norm_ffn.reshape(1, d), sc_f[:, None, :], sh_f[:, None, :], wr, br)


def _moe_kernel(be_ref, nb_ref, x_ref, g_ref, wgu_ref, bgu_ref, wdn_ref, bdn_ref, o_ref, wgu_bf, wdn_bf):
    i = pl.program_id(0)

    @pl.when((i == 0) | (be_ref[i] != be_ref[jnp.maximum(i - 1, 0)]))
    def _():
        wgu_bf[...] = wgu_ref[0].astype(jnp.bfloat16)
        wdn_bf[...] = wdn_ref[0].astype(jnp.bfloat16)

    @pl.when(i < nb_ref[0])
    def _():
        gu = jnp.dot(x_ref[...], wgu_bf[...], preferred_element_type=jnp.float32) + bgu_ref[0]
        g = jnp.minimum(gu[:, :D_EXPERT], SWIGLU_LIMIT)
        u = jnp.clip(gu[:, D_EXPERT:], -SWIGLU_LIMIT, SWIGLU_LIMIT)
        act = (u + 1.0) * (g * jax.nn.sigmoid(SWIGLU_ALPHA * g))
        out = jnp.dot(act.astype(jnp.bfloat16), wdn_bf[...], preferred_element_type=jnp.float32) + bdn_ref[0]
        o_ref[...] = out * g_ref[...]

    @pl.when(i >= nb_ref[0])
    def _():
        o_ref[...] = jnp.zeros_like(o_ref)


def _moe_ffn(xs, row_gate, blk_exp, n_used, w_gu, b_gu, w_dn, b_dn):
    n_rows, d = xs.shape
    tm = MOE_TM
    grid_spec = pltpu.PrefetchScalarGridSpec(
        num_scalar_prefetch=2,
        grid=(n_rows // tm,),
        in_specs=[pl.BlockSpec((tm, d), lambda i, be, nb: (i, 0)),
                  pl.BlockSpec((tm, 1), lambda i, be, nb: (i, 0)),
                  pl.BlockSpec((1, d, 2 * D_EXPERT), lambda i, be, nb: (be[i], 0, 0)),
                  pl.BlockSpec((1, 1, 2 * D_EXPERT), lambda i, be, nb: (be[i], 0, 0)),
                  pl.BlockSpec((1, D_EXPERT, d), lambda i, be, nb: (be[i], 0, 0)),
                  pl.BlockSpec((1, 1, d), lambda i, be, nb: (be[i], 0, 0))],
        out_specs=pl.BlockSpec((tm, d), lambda i, be, nb: (i, 0)),
        scratch_shapes=[pltpu.VMEM((d, 2 * D_EXPERT), jnp.bfloat16), pltpu.VMEM((D_EXPERT, d), jnp.bfloat16)],
    )
    return pl.pallas_call(
        _moe_kernel,
        grid_spec=grid_spec,
        out_shape=jax.ShapeDtypeStruct((n_rows, d), jnp.float32),
        compiler_params=pltpu.CompilerParams(dimension_semantics=("arbitrary",), vmem_limit_bytes=VMEM_LIMIT),
        name="moe_ffn",
    )(blk_exp, n_used, xs, row_gate[:, None], w_gu, b_gu[:, None, :], w_dn, b_dn[:, None, :])


def _moe(h2, logits, w_gu, b_gu, w_dn, b_dn):
    t, d = h2.shape
    tm = MOE_TM
    i32 = jnp.int32
    top_val, top_idx = lax.top_k(logits, TOP_K)
    gates = jax.nn.softmax(top_val, axis=-1).reshape(-1)
    n_assign = t * TOP_K
    n_rows = n_assign + N_EXPERTS * tm
    e_flat = top_idx.reshape(n_assign).astype(i32)
    counts = jnp.sum((e_flat[:, None] == jnp.arange(N_EXPERTS, dtype=i32)[None, :]).astype(i32), axis=0)
    padded = (counts + tm - 1) // tm * tm
    filler_exp = jnp.repeat(jnp.arange(N_EXPERTS, dtype=i32), tm)
    filler_key = jnp.where(jnp.tile(jnp.arange(tm, dtype=i32), N_EXPERTS) < jnp.repeat(padded - counts, tm),
                           filler_exp, N_EXPERTS)
    keys = jnp.concatenate([e_flat, filler_key])
    gate_in = jnp.concatenate([gates, jnp.zeros((N_EXPERTS * tm,), jnp.float32)])
    rows = jnp.arange(n_rows, dtype=i32)
    row_key, row_src, row_gate = lax.sort((keys, rows, gate_in), num_keys=1)
    row_tok = jnp.where(row_src < n_assign, row_src // TOP_K, 0)
    _, row_of = lax.sort((row_src, rows), num_keys=1)
    dest = row_of[:n_assign]
    blk_exp = jnp.minimum(row_key[::tm], N_EXPERTS - 1)
    n_used = (jnp.sum(padded, keepdims=True) // tm).astype(i32)
    out = _moe_ffn(h2[row_tok], row_gate, blk_exp, n_used, w_gu, b_gu, w_dn, b_dn)
    return jnp.sum(out[dest].reshape(t, TOP_K, d), axis=1)


def kernel(x, c, rel_bias, w_ada, b_ada, norm_mix, norm_ffn, w_in, kv_norm, w_kv_up, q_norm, k_norm,
           idx_k_ln_w, idx_k_ln_b, w_attn_o, conv_w, conv_b, dt_bias, a_log, d_skip, ssm_norm, w_ssm_o,
           w_out, w_router, b_router, w_gu, b_gu, w_dn, b_dn):
    bsz, seq, d = x.shape
    t = bsz * seq
    cond = jax.nn.silu(c)
    x2 = x.reshape(t, d)
    for l in range(DEPTH):
        mod = cond @ w_ada[l] + b_ada[l]
        sh_m, sc_m, g_m, sh_f, sc_f, g_f = jnp.split(mod, 6, axis=-1)
        proj = _in_proj(x2, norm_mix[l], sc_m, sh_m, _pack_w_in(w_in[l]), seq)
        qT, k, vT, qiT, ki2, wT = _prep(proj, bsz, seq, q_norm[l], kv_norm[l], w_kv_up[l], k_norm[l],
                                        idx_k_ln_w[l], idx_k_ln_b[l])
        attn = _dsa_attention(qT, qiT, wT, k, vT, ki2, rel_bias)
        y_ssd = _mamba2_ssd(proj, bsz, seq, conv_w[l], conv_b[l], dt_bias[l], a_log[l], d_skip[l], ssm_norm[l])
        x2, h2, logits = _mix_out(attn.reshape(t, ATTN_WIDTH), y_ssd, proj, x2, g_m,
                                  w_attn_o[l], w_ssm_o[l], w_out[l], norm_ffn[l], sc_f, sh_f,
                                  w_router[l], b_router[l], seq)
        y = _moe(h2, logits[:, :N_EXPERTS], w_gu[l], b_gu[l], w_dn[l], b_dn[l])
        x2 = x2 + jnp.repeat(g_f, seq, axis=0) * y
    return x2.reshape(bsz, seq, d)
```

```python
import functools
import math

import jax
import jax.numpy as jnp
import numpy as np
from jax import lax
from jax.experimental import pallas as pl
from jax.experimental.pallas import tpu as pltpu

D_MODEL = 1024
DEPTH = 2
ATTN_HEADS = 8
ATTN_HEAD_DIM = 64
ATTN_WIDTH = ATTN_HEADS * ATTN_HEAD_DIM
KV_RANK = 256
IDX_HEADS = 8
IDX_DIM = 64
TOPK_MAX = 256
N_BUCKETS = 32
MAX_DISTANCE = 128
SSM_HEADS = 16
SSM_HEAD_DIM = 64
SSM_INNER = SSM_HEADS * SSM_HEAD_DIM
SSM_GROUPS = 2
SSM_STATE = 128
CONV_WIDTH = 4
CONV_CH = SSM_INNER + 2 * SSM_GROUPS * SSM_STATE
SSD_CHUNK = 128
N_EXPERTS = 32
TOP_K = 4
D_EXPERT = D_MODEL
SWIGLU_LIMIT = 7.0
SWIGLU_ALPHA = 1.702
EPS = 1e-6

COL_Q = 0
COL_KV = 512
COL_QI = 768
COL_SMALL = 1280
COL_XBC = 1536
COL_Z = 3072
COL_GATE = 4096
PROJ_COLS = 6144
PREP_COLS = 1408
SMALL_KI, SMALL_WI, SMALL_DT = 0, 64, 72

QB = 256
VROWS = 80
INT_MIN = -2 ** 31
KEY_NEG_INF = (0xFF800000 ^ 0x7FFFFFFF) - 2 ** 32
NEG = -1e30
TINY = 2.0 ** -126
LOG2E = math.log2(math.e)
VMEM_LIMIT = 56 * 1024 * 1024
MOE_TM = 512
HIGHEST = lax.Precision.HIGHEST
NT = (((1,), (1,)), ((), ()))


def _pack_w_in(w):
    o = np.cumsum((0, ATTN_WIDTH, KV_RANK, IDX_HEADS * IDX_DIM, IDX_DIM, IDX_HEADS, SSM_INNER, CONV_CH, SSM_HEADS, 2 * D_MODEL))
    q, kv, qi, ki, wi, z, xbc, dt, gate = (w[:, int(o[n]):int(o[n + 1])] for n in range(9))
    zeros = lambda n: jnp.zeros((w.shape[0], n), w.dtype)
    small = jnp.concatenate([ki, wi, dt, zeros(128 - 88)], axis=1)
    packed = jnp.concatenate([q, kv, qi, small, zeros(COL_XBC - PREP_COLS), xbc, z, gate], axis=1)
    assert packed.shape[1] == PROJ_COLS
    return packed.astype(jnp.bfloat16)


def _in_proj_kernel(x_ref, g_ref, sc_ref, sh_ref, w_ref, o_ref, h_ref):
    @pl.when(pl.program_id(1) == 0)
    def _():
        x = x_ref[...]
        y = x * lax.rsqrt(jnp.mean(x * x, axis=-1, keepdims=True) + EPS) * g_ref[...]
        h_ref[...] = (y * (1.0 + sc_ref[0]) + sh_ref[0]).astype(jnp.bfloat16)
    o_ref[...] = jnp.dot(h_ref[...], w_ref[...], preferred_element_type=jnp.float32)


def _in_proj(x2, gain, sc, sh, w_packed, seq, tm=1024, tn=512):
    t, d = x2.shape
    per_b = seq // tm
    return pl.pallas_call(
        _in_proj_kernel,
        grid=(t // tm, PROJ_COLS // tn),
        in_specs=[pl.BlockSpec((tm, d), lambda i, j: (i, 0)),
                  pl.BlockSpec((1, d), lambda i, j: (0, 0)),
                  pl.BlockSpec((1, 1, d), lambda i, j: (i // per_b, 0, 0)),
                  pl.BlockSpec((1, 1, d), lambda i, j: (i // per_b, 0, 0)),
                  pl.BlockSpec((d, tn), lambda i, j: (0, j))],
        out_specs=pl.BlockSpec((tm, tn), lambda i, j: (i, j)),
        out_shape=jax.ShapeDtypeStruct((t, PROJ_COLS), jnp.float32),
        scratch_shapes=[pltpu.VMEM((tm, d), jnp.bfloat16)],
        compiler_params=pltpu.CompilerParams(dimension_semantics=("parallel", "arbitrary"),
                                             vmem_limit_bytes=VMEM_LIMIT),
        name="in_proj",
    )(x2, gain.reshape(1, d), sc[:, None, :], sh[:, None, :], w_packed)


def _head_rms_t(xt):
    x3 = xt.reshape(ATTN_HEADS, ATTN_HEAD_DIM, xt.shape[1])
    return lax.rsqrt(jnp.mean(x3 * x3, axis=1, keepdims=True) + EPS)


def _prep_kernel(p_ref, qg_ref, kvg_ref, wkv_ref, kg_ref, lng_ref, lnb_ref,
                 qT_ref, k_ref, vT_ref, qiT_ref, ki_ref, wT_ref):
    n = p_ref.shape[0]
    q = p_ref[:, COL_Q:COL_Q + ATTN_WIDTH]
    lat = p_ref[:, COL_KV:COL_KV + KV_RANK]
    qi = p_ref[:, COL_QI:COL_QI + IDX_HEADS * IDX_DIM]
    sm = p_ref[:, COL_SMALL:COL_SMALL + 128]

    scale = ATTN_HEAD_DIM ** -0.5 * LOG2E
    qt = q.T
    qn = qt.reshape(ATTN_HEADS, ATTN_HEAD_DIM, n) * _head_rms_t(qt)
    qT_ref[0] = (qn.reshape(ATTN_WIDTH, n) * qg_ref[...] * scale).astype(jnp.bfloat16)

    latn = lat * lax.rsqrt(jnp.mean(lat * lat, axis=-1, keepdims=True) + EPS) * kvg_ref[...]
    kv = jnp.dot(latn.astype(jnp.bfloat16), wkv_ref[...], preferred_element_type=jnp.float32)
    kt = kv[:, :ATTN_WIDTH].T
    kn = (kt.reshape(ATTN_HEADS, ATTN_HEAD_DIM, n) * _head_rms_t(kt)).reshape(ATTN_WIDTH, n) * kg_ref[...]
    k_ref[0] = kn.T.astype(jnp.bfloat16)
    vt = kv[:, ATTN_WIDTH:].T.reshape(ATTN_HEADS, ATTN_HEAD_DIM, n)
    ones = jnp.ones((ATTN_HEADS, VROWS - ATTN_HEAD_DIM, n), jnp.float32)
    vT_ref[0] = jnp.concatenate([vt, ones], axis=1).reshape(ATTN_HEADS * VROWS, n).astype(jnp.bfloat16)

    qiT_ref[0] = (qi * (IDX_DIM ** -0.5)).T.astype(jnp.bfloat16)

    lane = lax.broadcasted_iota(jnp.int32, sm.shape, 1)
    kid = jnp.where(lane < IDX_DIM, sm, pltpu.roll(sm, IDX_DIM, 1))
    mu = jnp.mean(kid, axis=-1, keepdims=True)
    var = jnp.mean(jnp.square(kid - mu), axis=-1, keepdims=True)
    ki_ref[0] = ((kid - mu) * lax.rsqrt(var + EPS) * lng_ref[...] + lnb_ref[...]).astype(jnp.bfloat16)

    wT_ref[0] = sm.T[SMALL_WI:SMALL_WI + IDX_HEADS, :] * (IDX_HEADS ** -0.5)


def _prep(proj, bsz, seq, q_norm, kv_norm, w_kv_up, k_norm, ln_w, ln_b, tp=512):
    nb = seq // tp
    tile8 = lambda g: jnp.tile(g, ATTN_HEADS).reshape(ATTN_WIDTH, 1)
    const = lambda shape: pl.BlockSpec(shape, lambda b, i: (0,) * len(shape))
    bf = jnp.bfloat16
    return pl.pallas_call(
        _prep_kernel,
        grid=(bsz, nb),
        in_specs=[pl.BlockSpec((tp, PREP_COLS), lambda b, i: (b * nb + i, 0)),
                  const((ATTN_WIDTH, 1)), const((1, KV_RANK)), const((KV_RANK, 2 * ATTN_WIDTH)),
                  const((ATTN_WIDTH, 1)), const((1, 128)), const((1, 128))],
        out_specs=[pl.BlockSpec((1, ATTN_WIDTH, tp), lambda b, i: (b, 0, i)),
                   pl.BlockSpec((1, tp, ATTN_WIDTH), lambda b, i: (b, i, 0)),
                   pl.BlockSpec((1, ATTN_HEADS * VROWS, tp), lambda b, i: (b, 0, i)),
                   pl.BlockSpec((1, ATTN_WIDTH, tp), lambda b, i: (b, 0, i)),
                   pl.BlockSpec((1, tp, 128), lambda b, i: (b, i, 0)),
                   pl.BlockSpec((1, IDX_HEADS, tp), lambda b, i: (b, 0, i))],
        out_shape=[jax.ShapeDtypeStruct((bsz, ATTN_WIDTH, seq), bf),
                   jax.ShapeDtypeStruct((bsz, seq, ATTN_WIDTH), bf),
                   jax.ShapeDtypeStruct((bsz, ATTN_HEADS * VROWS, seq), bf),
                   jax.ShapeDtypeStruct((bsz, ATTN_WIDTH, seq), bf),
                   jax.ShapeDtypeStruct((bsz, seq, 128), bf),
                   jax.ShapeDtypeStruct((bsz, IDX_HEADS, seq), jnp.float32)],
        compiler_params=pltpu.CompilerParams(dimension_semantics=("parallel", "parallel"),
                                             vmem_limit_bytes=VMEM_LIMIT),
        name="attn_prep",
    )(proj, tile8(q_norm), kv_norm.reshape(1, KV_RANK), w_kv_up.astype(bf), tile8(k_norm),
      jnp.tile(ln_w, 2).reshape(1, 128), jnp.tile(ln_b, 2).reshape(1, 128))


def _t5_bucket(dist):
    n = jnp.maximum(dist, 0)
    max_exact = N_BUCKETS // 2
    nf = jnp.maximum(n, 1).astype(jnp.float32)
    large = max_exact + (jnp.log(nf / max_exact) / math.log(MAX_DISTANCE / max_exact) * (N_BUCKETS - max_exact)).astype(jnp.int32)
    large = jnp.minimum(large, N_BUCKETS - 1)
    return jnp.where(n < max_exact, n, large)


def _bias_tables(rel_bias):
    s = jnp.arange(QB, dtype=jnp.int32)[None, :, None]
    q = jnp.arange(QB, dtype=jnp.int32)[None, None, :]
    dist = q - s + jnp.array([2 * QB, QB, 0], jnp.int32)[:, None, None]
    onehot = (_t5_bucket(dist)[..., None] == jnp.arange(N_BUCKETS, dtype=jnp.int32)).astype(jnp.float32)
    b = jnp.einsum('tsqb,bh->thsq', onehot, rel_bias.astype(jnp.float32) * LOG2E, precision=HIGHEST)
    return jnp.where((dist >= 0)[:, None], b, NEG)


def _attn_kernel(qT_ref, qiT_ref, wT_ref, k_ref, vT_ref, ki_ref, tab_ref, o_ref,
                 keys_ref, hi_ref, msk_ref, p_ref, acc_ref, mp_ref, *, topk):
    i = pl.program_id(1)
    n_tiles = i + 1
    row_hi = lax.broadcasted_iota(jnp.int32, (128, QB), 0) >= 64

    def head_rows(ref, h):
        pair = ref[0, (h // 2) * 128:(h // 2) * 128 + 128, :]
        return jnp.where(row_hi == bool(h % 2), pair, jnp.zeros_like(pair))

    def tile_rows(kt):
        return pl.ds(pl.multiple_of(kt * QB, QB), QB)

    def score_tile(kt, carry):
        ki = ki_ref[0, tile_rows(kt), :]
        sc = jnp.zeros((QB, QB), jnp.float32)
        for h in range(IDX_HEADS):
            d = jnp.dot(ki, head_rows(qiT_ref, h), preferred_element_type=jnp.float32)
            sc = sc + wT_ref[0, h:h + 1, :] * jnp.maximum(d, 0.0)
        srow = lax.broadcasted_iota(jnp.int32, (QB, QB), 0)
        qcol = lax.broadcasted_iota(jnp.int32, (QB, QB), 1)
        sc = jnp.where(jnp.abs(sc) < TINY, 0.0, sc)
        sc = jnp.where((kt == i) & (srow > qcol), -jnp.inf, sc)
        bits = pltpu.bitcast(sc, jnp.int32)
        keys_ref[tile_rows(kt), :] = bits ^ ((bits >> 31) & 0x7FFFFFFF)
        hi_ref[tile_rows(kt), :] = pltpu.bitcast(bits & jnp.int32(-65536), jnp.float32).astype(jnp.bfloat16)
        return carry
    lax.fori_loop(0, n_tiles, score_tile, 0)

    def count_hi_ge(cand16):
        b = cand16 ^ ((cand16 >> 15) & 0x7FFF)
        snap = jnp.where(((b & 0x8000) != 0) | ((b & 0x7F) == 0), 0, 0x0080)
        b = jnp.where((b & 0x7F80) == 0, snap, b)
        cb = pltpu.bitcast(b << 16, jnp.float32).astype(jnp.bfloat16)
        one, zero = jnp.ones((), jnp.bfloat16), jnp.zeros((), jnp.bfloat16)

        def body(kt, acc):
            hit = jnp.where(hi_ref[tile_rows(kt), :] >= cb, one, zero)
            parts = [hit[r:r + 16, :] for r in range(0, QB, 16)]
            while len(parts) > 1:
                parts = [a + b for a, b in zip(parts[::2], parts[1::2])]
            return acc + parts[0]
        acc = lax.fori_loop(0, n_tiles, body, jnp.zeros((16, QB), jnp.bfloat16))
        return jnp.sum(acc.astype(jnp.float32), axis=0, keepdims=True)

    def count(hit_of_tile):
        def body(kt, acc):
            return acc + jnp.sum(hit_of_tile(kt).reshape(QB // 8, 8, QB), axis=0)
        acc = lax.fori_loop(0, n_tiles, body, jnp.zeros((8, QB), jnp.int32))
        return jnp.sum(acc, axis=0, keepdims=True)

    def count_ge(cand):
        return count(lambda kt: jnp.where(keys_ref[tile_rows(kt), :] >= cand, 1, 0))

    def hi_step(it, r):
        cand = jnp.where(it == 0, jnp.zeros_like(r), r | (1 << (15 - it)))
        return jnp.where(count_hi_ge(cand) >= topk, cand, r)
    r16 = lax.fori_loop(0, 16, hi_step, jnp.full((1, QB), -32768, jnp.int32))

    def lo_step(it, r):
        cand = r | (1 << (15 - it))
        return jnp.where(count_ge(cand) >= topk, cand, r)
    thr = lax.fori_loop(0, 16, lo_step, r16 << 16)

    cnt_gt = count_ge(thr + 1)
    cnt_ge = count_ge(thr)
    need = topk - cnt_gt
    tie = (cnt_ge - cnt_gt > need) & (thr > KEY_NEG_INF)

    @pl.when(jnp.max(tie.astype(jnp.int32)) > 0)
    def _():
        def count_eq_below(cand):
            def ind(kt):
                idx = lax.broadcasted_iota(jnp.int32, (QB, QB), 0) + kt * QB
                return jnp.where((keys_ref[tile_rows(kt), :] == thr) & (idx < cand), 1, 0)
            return count(ind)

        def idx_step(it, r):
            cand = r | (1 << (15 - it))
            return jnp.where(count_eq_below(cand) < need, cand, r)
        last = lax.fori_loop(0, 16, idx_step, jnp.zeros((1, QB), jnp.int32))

        def drop(kt, carry):
            blk = keys_ref[tile_rows(kt), :]
            idx = lax.broadcasted_iota(jnp.int32, (QB, QB), 0) + kt * QB
            keys_ref[tile_rows(kt), :] = jnp.where(tie & (blk == thr) & (idx > last), INT_MIN, blk)
            return carry
        lax.fori_loop(0, n_tiles, drop, 0)

    def logits(kt, h):
        band = jnp.clip(kt - (i - 2), 0, 2)
        kp = k_ref[0, tile_rows(kt), (h // 2) * 128:(h // 2) * 128 + 128]
        s = jnp.dot(kp, head_rows(qT_ref, h), preferred_element_type=jnp.float32)
        return s + msk_ref[...] + tab_ref[band, h]

    def set_mask(kt):
        msk_ref[...] = jnp.where(keys_ref[tile_rows(kt), :] >= thr, 0.0, NEG)

    def max_tile(kt, carry):
        set_mask(kt)
        for h in range(ATTN_HEADS):
            s = logits(kt, h)
            mp_ref[h] = jnp.maximum(mp_ref[h], jnp.max(s.reshape(QB // 8, 8, QB), axis=0))
        return carry

    mp_ref[...] = jnp.full(mp_ref.shape, NEG, jnp.float32)
    lax.fori_loop(0, n_tiles, max_tile, 0)
    m = [jnp.max(mp_ref[h], axis=0, keepdims=True) for h in range(ATTN_HEADS)]

    def exp_tile(kt, carry):
        set_mask(kt)
        for h in range(ATTN_HEADS):
            p_ref[h] = jnp.exp2(logits(kt, h) - m[h]).astype(jnp.bfloat16)
        for h in range(ATTN_HEADS):
            va = vT_ref[0, h * VROWS:(h + 1) * VROWS, tile_rows(kt)]
            acc_ref[h * VROWS:(h + 1) * VROWS, :] += jnp.dot(va, p_ref[h], preferred_element_type=jnp.float32)
        return carry

    acc_ref[...] = jnp.zeros(acc_ref.shape, jnp.float32)
    lax.fori_loop(0, n_tiles, exp_tile, 0)

    outs = [acc_ref[h * VROWS:h * VROWS + ATTN_HEAD_DIM, :] / acc_ref[h * VROWS + ATTN_HEAD_DIM:h * VROWS + ATTN_HEAD_DIM + 1, :]
            for h in range(ATTN_HEADS)]
    o_ref[0] = jnp.concatenate(outs, axis=0).T


def _dsa_attention(qT, qiT, wT, k, vT, ki2, rel_bias):
    bsz, _, seq = qT.shape
    topk = min(TOPK_MAX, seq // 4)
    assert seq % QB == 0 and topk <= QB
    assert seq // 16 <= 256
    return pl.pallas_call(
        functools.partial(_attn_kernel, topk=topk),
        grid=(bsz, seq // QB),
        in_specs=[
            pl.BlockSpec((1, ATTN_WIDTH, QB), lambda b, i: (b, 0, i)),
            pl.BlockSpec((1, IDX_HEADS * IDX_DIM, QB), lambda b, i: (b, 0, i)),
            pl.BlockSpec((1, IDX_HEADS, QB), lambda b, i: (b, 0, i)),
            pl.BlockSpec((1, seq, ATTN_WIDTH), lambda b, i: (b, 0, 0)),
            pl.BlockSpec((1, ATTN_HEADS * VROWS, seq), lambda b, i: (b, 0, 0)),
            pl.BlockSpec((1, seq, 128), lambda b, i: (b, 0, 0)),
            pl.BlockSpec((3, ATTN_HEADS, QB, QB), lambda b, i: (0, 0, 0, 0)),
        ],
        out_specs=pl.BlockSpec((1, QB, ATTN_WIDTH), lambda b, i: (b, i, 0)),
        out_shape=jax.ShapeDtypeStruct((bsz, seq, ATTN_WIDTH), jnp.float32),
        scratch_shapes=[
            pltpu.VMEM((seq, QB), jnp.int32),
            pltpu.VMEM((seq, QB), jnp.bfloat16),
            pltpu.VMEM((QB, QB), jnp.float32),
            pltpu.VMEM((ATTN_HEADS, QB, QB), jnp.bfloat16),
            pltpu.VMEM((ATTN_HEADS * VROWS, QB), jnp.float32),
            pltpu.VMEM((ATTN_HEADS, 8, QB), jnp.float32),
        ],
        compiler_params=pltpu.CompilerParams(dimension_semantics=("parallel", "arbitrary"),
                                             vmem_limit_bytes=VMEM_LIMIT),
        name="dsa_attention",
    )(qT, qiT, wT, k, vT, ki2, _bias_tables(rel_bias))


def _ssd_kernel(xbc_ref, z_ref, sm_ref, cw_ref, cb_ref, dtb_ref, a_ref, dsk_ref, nw_ref, y_ref, prev_ref, st_ref):
    q = SSD_CHUNK
    bf = jnp.bfloat16

    @pl.when(pl.program_id(1) == 0)
    def _():
        prev_ref[...] = jnp.zeros(prev_ref.shape, jnp.float32)
        st_ref[...] = jnp.zeros(st_ref.shape, jnp.float32)

    cur = xbc_ref[...]
    prev = prev_ref[...]
    row = lax.broadcasted_iota(jnp.int32, cur.shape, 0)
    acc = cur * cw_ref[CONV_WIDTH - 1:CONV_WIDTH, :] + cb_ref[...]
    for s in range(1, CONV_WIDTH):
        shifted = jnp.where(row >= s, pltpu.roll(cur, s, 0), pltpu.roll(prev, s, 0))
        acc = acc + shifted * cw_ref[CONV_WIDTH - 1 - s:CONV_WIDTH - s, :]
    prev_ref[...] = cur
    u = acc * jax.nn.sigmoid(acc)
    xs = u[:, :SSM_INNER]
    bm = u[:, SSM_INNER:SSM_INNER + SSM_GROUPS * SSM_STATE].astype(bf)
    cm = u[:, SSM_INNER + SSM_GROUPS * SSM_STATE:].astype(bf)

    t = sm_ref[...] + dtb_ref[...]
    dt = jnp.maximum(t, 0.0) + jnp.log1p(jnp.exp(-jnp.abs(t)))
    ii = lax.broadcasted_iota(jnp.int32, (q, q), 0)
    jj = lax.broadcasted_iota(jnp.int32, (q, q), 1)
    causal = ii >= jj
    acum = jnp.dot(causal.astype(jnp.float32), dt * a_ref[...], preferred_element_type=jnp.float32, precision=HIGHEST)
    acum_t = acum.T
    dt_t = dt.T
    ea = jnp.exp(acum)
    last = acum[q - 1:q, :]
    decay = jnp.exp(last - acum) * dt
    ea_last = jnp.exp(last)

    lane_hi = lax.broadcasted_iota(jnp.int32, (q, 128), 1) >= SSM_HEAD_DIM
    row_hi = lax.broadcasted_iota(jnp.int32, (128, SSM_STATE), 0) >= SSM_HEAD_DIM

    def pair_cols(v, e):
        c0, c1 = SMALL_DT + e, SMALL_DT + e + 1
        return jnp.where(lane_hi, v[:, c1:c1 + 1], v[:, c0:c0 + 1])

    for g in range(SSM_GROUPS):
        bg = bm[:, g * SSM_STATE:(g + 1) * SSM_STATE]
        cg = cm[:, g * SSM_STATE:(g + 1) * SSM_STATE]
        cb = lax.dot_general(cg, bg, NT, preferred_element_type=jnp.float32)
        for k in range(g * 4, g * 4 + 4):
            e = 2 * k
            x_pair = xs[:, k * 128:(k + 1) * 128]
            halves = []
            for h in (e, e + 1):
                c = SMALL_DT + h
                seg = acum[:, c:c + 1] - acum_t[c:c + 1, :]
                w = cb * jnp.exp(jnp.where(causal, seg, -jnp.inf)) * dt_t[c:c + 1, :]
                halves.append(jnp.dot(w.astype(bf), x_pair.astype(bf), preferred_element_type=jnp.float32))
            y_pair = jnp.where(lane_hi, halves[1], halves[0])
            state = st_ref[k]
            y_pair = y_pair + lax.dot_general(cg, state.astype(bf), NT, preferred_element_type=jnp.float32) * pair_cols(ea, e)
            y_ref[:, k * 128:(k + 1) * 128] = y_pair
            xd_t = (x_pair * pair_cols(decay, e)).T.astype(bf)
            c0 = SMALL_DT + e
            keep = jnp.where(row_hi, ea_last[:, c0 + 1:c0 + 2], ea_last[:, c0:c0 + 1])
            st_ref[k] = state * keep + jnp.dot(xd_t, bg, preferred_element_type=jnp.float32)

    y = (y_ref[...] + dsk_ref[...] * xs) * (z_ref[...] * jax.nn.sigmoid(z_ref[...]))
    half = SSM_INNER // SSM_GROUPS
    for g in range(SSM_GROUPS):
        yg = y[:, g * half:(g + 1) * half]
        yg = yg * lax.rsqrt(jnp.mean(yg * yg, axis=-1, keepdims=True) + EPS)
        y_ref[:, g * half:(g + 1) * half] = yg * nw_ref[:, g * half:(g + 1) * half]


def _mamba2_ssd(proj, bsz, seq, conv_w, conv_b, dt_bias, a_log, d_skip, norm_w):
    q = SSD_CHUNK
    nc = seq // q
    lane_row = lambda v: jnp.zeros((1, 128), jnp.float32).at[0, SMALL_DT:SMALL_DT + SSM_HEADS].set(v)
    const = lambda shape: pl.BlockSpec(shape, lambda b, c: (0,) * len(shape))
    return pl.pallas_call(
        _ssd_kernel,
        grid=(bsz, nc),
        in_specs=[pl.BlockSpec((q, CONV_CH), lambda b, c: (b * nc + c, COL_XBC // CONV_CH)),
                  pl.BlockSpec((q, SSM_INNER), lambda b, c: (b * nc + c, COL_Z // SSM_INNER)),
                  pl.BlockSpec((q, 128), lambda b, c: (b * nc + c, COL_SMALL // 128)),
                  const((CONV_WIDTH, CONV_CH)), const((1, CONV_CH)), const((1, 128)), const((1, 128)),
                  const((1, SSM_INNER)), const((1, SSM_INNER))],
        out_specs=pl.BlockSpec((q, SSM_INNER), lambda b, c: (b * nc + c, 0)),
        out_shape=jax.ShapeDtypeStruct((bsz * seq, SSM_INNER), jnp.float32),
        scratch_shapes=[pltpu.VMEM((q, CONV_CH), jnp.float32),
                        pltpu.VMEM((SSM_HEADS // 2, 2 * SSM_HEAD_DIM, SSM_STATE), jnp.float32)],
        compiler_params=pltpu.CompilerParams(dimension_semantics=("parallel", "arbitrary"),
                                             vmem_limit_bytes=VMEM_LIMIT),
        name="mamba2_ssd",
    )(proj, proj, proj, conv_w, conv_b.reshape(1, CONV_CH), lane_row(dt_bias), lane_row(-jnp.exp(a_log)),
      jnp.repeat(d_skip, SSM_HEAD_DIM).reshape(1, SSM_INNER), norm_w.reshape(1, SSM_INNER))


def _mix_out_kernel(a_ref, s_ref, gl_ref, x_ref, gm_ref, wo_ref, ws_ref, wout_ref,
                    nf_ref, scf_ref, shf_ref, wr_ref, br_ref, xo_ref, h_ref, lg_ref):
    bf = jnp.bfloat16
    ya = jnp.dot(a_ref[...].astype(bf), wo_ref[...], preferred_element_type=jnp.float32)
    ys = jnp.dot(s_ref[...].astype(bf), ws_ref[...], preferred_element_type=jnp.float32)
    mixed = jax.nn.sigmoid(gl_ref[:, :D_MODEL]) * ya + jax.nn.sigmoid(gl_ref[:, D_MODEL:]) * ys
    x = x_ref[...] + gm_ref[0] * jnp.dot(mixed.astype(bf), wout_ref[...], preferred_element_type=jnp.float32)
    xo_ref[...] = x
    y = x * lax.rsqrt(jnp.mean(x * x, axis=-1, keepdims=True) + EPS) * nf_ref[...]
    h = y * (1.0 + scf_ref[0]) + shf_ref[0]
    hb = pltpu.bitcast(h.astype(bf).astype(jnp.float32), jnp.int32)
    half = D_MODEL // 2
    h_ref[...] = (hb[:, :half] & jnp.int32(-65536)) | lax.shift_right_logical(hb[:, half:], 16)
    lg_ref[...] = jnp.dot(h, wr_ref[...], preferred_element_type=jnp.float32, precision=HIGHEST) + br_ref[...]


def _mix_out(attn2, ssd2, proj, x2, g_m, w_attn_o, w_ssm_o, w_out, norm_ffn, sc_f, sh_f, w_router, b_router, seq, tm=512):
    t, d = x2.shape
    per_b = seq // tm
    bf = jnp.bfloat16
    const = lambda shape: pl.BlockSpec(shape, lambda i: (0,) * len(shape))
    perb = pl.BlockSpec((1, 1, d), lambda i: (i // per_b, 0, 0))
    wr = jnp.pad(w_router, ((0, 0), (0, 128 - N_EXPERTS)))
    br = jnp.pad(b_router, (0, 128 - N_EXPERTS)).reshape(1, 128)
    return pl.pallas_call(
        _mix_out_kernel,
        grid=(t // tm,),
        in_specs=[pl.BlockSpec((tm, ATTN_WIDTH), lambda i: (i, 0)),
                  pl.BlockSpec((tm, SSM_INNER), lambda i: (i, 0)),
                  pl.BlockSpec((tm, 2 * d), lambda i: (i, COL_GATE // (2 * d))),
                  pl.BlockSpec((tm, d), lambda i: (i, 0)),
                  perb,
                  const((ATTN_WIDTH, d)), const((SSM_INNER, d)), const((d, d)),
                  const((1, d)), perb, perb, const((d, 128)), const((1, 128))],
        out_specs=[pl.BlockSpec((tm, d), lambda i: (i, 0)),
                   pl.BlockSpec((tm, d // 2), lambda i: (i, 0)),
                   pl.BlockSpec((tm, 128), lambda i: (i, 0))],
        out_shape=[jax.ShapeDtypeStruct((t, d), jnp.float32),
                   jax.ShapeDtypeStruct((t, d // 2), jnp.int32),
                   jax.ShapeDtypeStruct((t, 128), jnp.float32)],
        compiler_params=pltpu.CompilerParams(dimension_semantics=("parallel",), vmem_limit_bytes=VMEM_LIMIT),
        name="mix_out",
    )(attn2, ssd2, proj, x2, g_m[:, None, :], w_attn_o.astype(bf), w_ssm_o.astype(bf), w_out.astype(bf),
      norm_ffn.reshape(1, d), sc_f[:, None, :], sh_f[:, None, :], wr, br)


def _moe_kernel(be_ref, nb_ref, x_ref, g_ref, wgu_ref, bgu_ref, wdn_ref, bdn_ref, o_ref, wgu_bf, wdn_bf):
    i = pl.program_id(0)

    @pl.when((i == 0) | (be_ref[i] != be_ref[jnp.maximum(i - 1, 0)]))
    def _():
        wgu_bf[...] = wgu_ref[0, 0].astype(jnp.bfloat16)
        wdn_bf[...] = wdn_ref[0, 0].astype(jnp.bfloat16)

    @pl.when(i < nb_ref[0])
    def _():
        words = x_ref[...]
        x_hi = pltpu.bitcast(words & jnp.int32(-65536), jnp.float32).astype(jnp.bfloat16)
        x_lo = pltpu.bitcast(words << 16, jnp.float32).astype(jnp.bfloat16)
        x = jnp.concatenate([x_hi, x_lo], axis=1)
        gu = jnp.dot(x, wgu_bf[...], preferred_element_type=jnp.float32) + bgu_ref[0, 0]
        g = jnp.minimum(gu[:, :D_EXPERT], SWIGLU_LIMIT)
        u = jnp.clip(gu[:, D_EXPERT:], -SWIGLU_LIMIT, SWIGLU_LIMIT)
        act = (u + 1.0) * (g * jax.nn.sigmoid(SWIGLU_ALPHA * g))
        out = jnp.dot(act.astype(jnp.bfloat16), wdn_bf[...], preferred_element_type=jnp.float32) + bdn_ref[0, 0]
        o_ref[...] = out * g_ref[...]

    @pl.when(i >= nb_ref[0])
    def _():
        o_ref[...] = jnp.zeros_like(o_ref)


def _moe_ffn(xs, row_gate, blk_exp, n_used, w_gu, b_gu, w_dn, b_dn, layer):
    n_rows = xs.shape[0]
    d = D_MODEL
    tm = MOE_TM
    grid_spec = pltpu.PrefetchScalarGridSpec(
        num_scalar_prefetch=2,
        grid=(n_rows // tm,),
        in_specs=[pl.BlockSpec((tm, d // 2), lambda i, be, nb: (i, 0)),
                  pl.BlockSpec((tm, 1), lambda i, be, nb: (i, 0)),
                  pl.BlockSpec((1, 1, d, 2 * D_EXPERT), lambda i, be, nb: (layer, be[i], 0, 0)),
                  pl.BlockSpec((1, 1, 1, 2 * D_EXPERT), lambda i, be, nb: (layer, be[i], 0, 0)),
                  pl.BlockSpec((1, 1, D_EXPERT, d), lambda i, be, nb: (layer, be[i], 0, 0)),
                  pl.BlockSpec((1, 1, 1, d), lambda i, be, nb: (layer, be[i], 0, 0))],
        out_specs=pl.BlockSpec((tm, d), lambda i, be, nb: (i, 0)),
        scratch_shapes=[pltpu.VMEM((d, 2 * D_EXPERT), jnp.bfloat16), pltpu.VMEM((D_EXPERT, d), jnp.bfloat16)],
    )
    return pl.pallas_call(
        _moe_kernel,
        grid_spec=grid_spec,
        out_shape=jax.ShapeDtypeStruct((n_rows, d), jnp.float32),
        compiler_params=pltpu.CompilerParams(dimension_semantics=("arbitrary",), vmem_limit_bytes=VMEM_LIMIT),
        name="moe_ffn",
    )(blk_exp, n_used, xs, row_gate[:, None], w_gu, b_gu[:, :, None, :], w_dn, b_dn[:, :, None, :])


def _moe(h2, logits, w_gu, b_gu, w_dn, b_dn, layer):
    t = h2.shape[0]
    d = D_MODEL
    tm = MOE_TM
    i32 = jnp.int32
    top_val, top_idx = lax.top_k(logits, TOP_K)
    gates = jax.nn.softmax(top_val, axis=-1).reshape(-1)
    n_assign = t * TOP_K
    n_rows = n_assign + N_EXPERTS * tm
    e_flat = top_idx.reshape(n_assign).astype(i32)
    counts = jnp.sum((e_flat[:, None] == jnp.arange(N_EXPERTS, dtype=i32)[None, :]).astype(i32), axis=0)
    padded = (counts + tm - 1) // tm * tm
    filler_exp = jnp.repeat(jnp.arange(N_EXPERTS, dtype=i32), tm)
    filler_key = jnp.where(jnp.tile(jnp.arange(tm, dtype=i32), N_EXPERTS) < jnp.repeat(padded - counts, tm),
                           filler_exp, N_EXPERTS)
    keys = jnp.concatenate([e_flat, filler_key])
    gate_in = jnp.concatenate([gates, jnp.zeros((N_EXPERTS * tm,), jnp.float32)])
    rows = jnp.arange(n_rows, dtype=i32)
    row_key, row_src, row_gate = lax.sort((keys, rows, gate_in), num_keys=1)
    row_tok = jnp.where(row_src < n_assign, row_src // TOP_K, 0)
    _, row_of = lax.sort((row_src, rows), num_keys=1)
    dest = row_of[:n_assign].reshape(t, TOP_K).T.reshape(-1)
    blk_exp = jnp.minimum(row_key[::tm], N_EXPERTS - 1)
    n_used = (jnp.sum(padded, keepdims=True) // tm).astype(i32)
    out = _moe_ffn(h2[row_tok], row_gate, blk_exp, n_used, w_gu, b_gu, w_dn, b_dn, layer)
    return jnp.sum(out[dest].reshape(TOP_K, t, d), axis=0)


def kernel(x, c, rel_bias, w_ada, b_ada, norm_mix, norm_ffn, w_in, kv_norm, w_kv_up, q_norm, k_norm,
           idx_k_ln_w, idx_k_ln_b, w_attn_o, conv_w, conv_b, dt_bias, a_log, d_skip, ssm_norm, w_ssm_o,
           w_out, w_router, b_router, w_gu, b_gu, w_dn, b_dn):
    bsz, seq, d = x.shape
    t = bsz * seq
    cond = jax.nn.silu(c)
    x2 = x.reshape(t, d)
    for l in range(DEPTH):
        mod = cond @ w_ada[l] + b_ada[l]
        sh_m, sc_m, g_m, sh_f, sc_f, g_f = jnp.split(mod, 6, axis=-1)
        proj = _in_proj(x2, norm_mix[l], sc_m, sh_m, _pack_w_in(w_in[l]), seq)
        qT, k, vT, qiT, ki2, wT = _prep(proj, bsz, seq, q_norm[l], kv_norm[l], w_kv_up[l], k_norm[l],
                                        idx_k_ln_w[l], idx_k_ln_b[l])
        attn = _dsa_attention(qT, qiT, wT, k, vT, ki2, rel_bias)
        y_ssd = _mamba2_ssd(proj, bsz, seq, conv_w[l], conv_b[l], dt_bias[l], a_log[l], d_skip[l], ssm_norm[l])
        x2, h2, logits = _mix_out(attn.reshape(t, ATTN_WIDTH), y_ssd, proj, x2, g_m,
                                  w_attn_o[l], w_ssm_o[l], w_out[l], norm_ffn[l], sc_f, sh_f,
                                  w_router[l], b_router[l], seq)
        y = _moe(h2, logits[:, :N_EXPERTS], w_gu, b_gu, w_dn, b_dn, l)
        x2 = x2 + jnp.repeat(g_f, seq, axis=0) * y
    return x2.reshape(bsz, seq, d)
```

```python
import functools
import math

import jax
import jax.numpy as jnp
import numpy as np
from jax import lax
from jax.experimental import pallas as pl
from jax.experimental.pallas import tpu as pltpu

D_MODEL = 1024
DEPTH = 2
ATTN_HEADS = 8
ATTN_HEAD_DIM = 64
ATTN_WIDTH = ATTN_HEADS * ATTN_HEAD_DIM
KV_RANK = 256
IDX_HEADS = 8
IDX_DIM = 64
TOPK_MAX = 256
N_BUCKETS = 32
MAX_DISTANCE = 128
SSM_HEADS = 16
SSM_HEAD_DIM = 64
SSM_INNER = SSM_HEADS * SSM_HEAD_DIM
SSM_GROUPS = 2
SSM_STATE = 128
CONV_WIDTH = 4
CONV_CH = SSM_INNER + 2 * SSM_GROUPS * SSM_STATE
SSD_CHUNK = 128
N_EXPERTS = 32
TOP_K = 4
D_EXPERT = D_MODEL
SWIGLU_LIMIT = 7.0
SWIGLU_ALPHA = 1.702
EPS = 1e-6

COL_Q = 0
COL_KV = 512
COL_QI = 768
COL_SMALL = 1280
COL_XBC = 1536
COL_Z = 3072
COL_GATE = 4096
PROJ_COLS = 6144
PREP_COLS = 1408
SMALL_KI, SMALL_WI, SMALL_DT = 0, 64, 72

QB = 256
VROWS = 80
INT_MIN = -2 ** 31
KEY_NEG_INF = (0xFF800000 ^ 0x7FFFFFFF) - 2 ** 32
NEG = -1e30
TINY = 2.0 ** -126
LOG2E = math.log2(math.e)
NORM_SLACK = 1.02
MAX_SHIFT_ERROR = 96.0
VMEM_LIMIT = 56 * 1024 * 1024
MOE_TM = 512
HIGHEST = lax.Precision.HIGHEST
NT = (((1,), (1,)), ((), ()))


def _pack_w_in(w):
    o = np.cumsum((0, ATTN_WIDTH, KV_RANK, IDX_HEADS * IDX_DIM, IDX_DIM, IDX_HEADS, SSM_INNER, CONV_CH, SSM_HEADS, 2 * D_MODEL))
    q, kv, qi, ki, wi, z, xbc, dt, gate = (w[:, int(o[n]):int(o[n + 1])] for n in range(9))
    zeros = lambda n: jnp.zeros((w.shape[0], n), w.dtype)
    small = jnp.concatenate([ki, wi, dt, zeros(128 - 88)], axis=1)
    packed = jnp.concatenate([q, kv, qi, small, zeros(COL_XBC - PREP_COLS), xbc, z, gate], axis=1)
    assert packed.shape[1] == PROJ_COLS
    return packed.astype(jnp.bfloat16)


def _in_proj_kernel(x_ref, g_ref, sc_ref, sh_ref, w_ref, o_ref, h_ref):
    @pl.when(pl.program_id(1) == 0)
    def _():
        x = x_ref[...]
        y = x * lax.rsqrt(jnp.mean(x * x, axis=-1, keepdims=True) + EPS) * g_ref[...]
        h_ref[...] = (y * (1.0 + sc_ref[0]) + sh_ref[0]).astype(jnp.bfloat16)
    o_ref[...] = jnp.dot(h_ref[...], w_ref[...], preferred_element_type=jnp.float32)


def _in_proj(x2, gain, sc, sh, w_packed, seq, tm=1024, tn=1024):
    t, d = x2.shape
    per_b = seq // tm
    return pl.pallas_call(
        _in_proj_kernel,
        grid=(t // tm, PROJ_COLS // tn),
        in_specs=[pl.BlockSpec((tm, d), lambda i, j: (i, 0)),
                  pl.BlockSpec((1, d), lambda i, j: (0, 0)),
                  pl.BlockSpec((1, 1, d), lambda i, j: (i // per_b, 0, 0)),
                  pl.BlockSpec((1, 1, d), lambda i, j: (i // per_b, 0, 0)),
                  pl.BlockSpec((d, tn), lambda i, j: (0, j))],
        out_specs=pl.BlockSpec((tm, tn), lambda i, j: (i, j)),
        out_shape=jax.ShapeDtypeStruct((t, PROJ_COLS), jnp.float32),
        scratch_shapes=[pltpu.VMEM((tm, d), jnp.bfloat16)],
        compiler_params=pltpu.CompilerParams(dimension_semantics=("parallel", "arbitrary"),
                                             vmem_limit_bytes=VMEM_LIMIT),
        name="in_proj",
    )(x2, gain.reshape(1, d), sc[:, None, :], sh[:, None, :], w_packed)


def _head_rms_t(xt):
    x3 = xt.reshape(ATTN_HEADS, ATTN_HEAD_DIM, xt.shape[1])
    return lax.rsqrt(jnp.mean(x3 * x3, axis=1, keepdims=True) + EPS)


def _prep_kernel(p_ref, qg_ref, kvg_ref, wkv_ref, kg_ref, lng_ref, lnb_ref,
                 qT_ref, k_ref, vT_ref, qiT_ref, ki_ref, wT_ref, kn2_ref):
    n = p_ref.shape[0]
    q = p_ref[:, COL_Q:COL_Q + ATTN_WIDTH]
    lat = p_ref[:, COL_KV:COL_KV + KV_RANK]
    qi = p_ref[:, COL_QI:COL_QI + IDX_HEADS * IDX_DIM]
    sm = p_ref[:, COL_SMALL:COL_SMALL + 128]

    scale = ATTN_HEAD_DIM ** -0.5 * LOG2E
    qt = q.T
    qn = qt.reshape(ATTN_HEADS, ATTN_HEAD_DIM, n) * _head_rms_t(qt)
    qT_ref[0] = (qn.reshape(ATTN_WIDTH, n) * qg_ref[...] * scale).astype(jnp.bfloat16)

    latn = lat * lax.rsqrt(jnp.mean(lat * lat, axis=-1, keepdims=True) + EPS) * kvg_ref[...]
    kv = jnp.dot(latn.astype(jnp.bfloat16), wkv_ref[...], preferred_element_type=jnp.float32)
    kt = kv[:, :ATTN_WIDTH].T
    kn = (kt.reshape(ATTN_HEADS, ATTN_HEAD_DIM, n) * _head_rms_t(kt)).reshape(ATTN_WIDTH, n) * kg_ref[...]
    k_ref[0] = kn.T.astype(jnp.bfloat16)
    kn3 = kn.reshape(ATTN_HEADS, ATTN_HEAD_DIM, n)
    kn2_ref[0] = jnp.sum(kn3 * kn3, axis=1)
    vt = kv[:, ATTN_WIDTH:].T.reshape(ATTN_HEADS, ATTN_HEAD_DIM, n)
    ones = jnp.ones((ATTN_HEADS, VROWS - ATTN_HEAD_DIM, n), jnp.float32)
    vT_ref[0] = jnp.concatenate([vt, ones], axis=1).reshape(ATTN_HEADS * VROWS, n).astype(jnp.bfloat16)

    qiT_ref[0] = (qi * (IDX_DIM ** -0.5)).T.astype(jnp.bfloat16)

    lane = lax.broadcasted_iota(jnp.int32, sm.shape, 1)
    kid = jnp.where(lane < IDX_DIM, sm, pltpu.roll(sm, IDX_DIM, 1))
    mu = jnp.mean(kid, axis=-1, keepdims=True)
    var = jnp.mean(jnp.square(kid - mu), axis=-1, keepdims=True)
    ki_ref[0] = ((kid - mu) * lax.rsqrt(var + EPS) * lng_ref[...] + lnb_ref[...]).astype(jnp.bfloat16)

    wT_ref[0] = sm.T[SMALL_WI:SMALL_WI + IDX_HEADS, :] * (IDX_HEADS ** -0.5)


def _prep(proj, bsz, seq, q_norm, kv_norm, w_kv_up, k_norm, ln_w, ln_b, tp=512):
    nb = seq // tp
    tile8 = lambda g: jnp.tile(g, ATTN_HEADS).reshape(ATTN_WIDTH, 1)
    const = lambda shape: pl.BlockSpec(shape, lambda b, i: (0,) * len(shape))
    bf = jnp.bfloat16
    return pl.pallas_call(
        _prep_kernel,
        grid=(bsz, nb),
        in_specs=[pl.BlockSpec((tp, PREP_COLS), lambda b, i: (b * nb + i, 0)),
                  const((ATTN_WIDTH, 1)), const((1, KV_RANK)), const((KV_RANK, 2 * ATTN_WIDTH)),
                  const((ATTN_WIDTH, 1)), const((1, 128)), const((1, 128))],
        out_specs=[pl.BlockSpec((1, ATTN_WIDTH, tp), lambda b, i: (b, 0, i)),
                   pl.BlockSpec((1, tp, ATTN_WIDTH), lambda b, i: (b, i, 0)),
                   pl.BlockSpec((1, ATTN_HEADS * VROWS, tp), lambda b, i: (b, 0, i)),
                   pl.BlockSpec((1, ATTN_WIDTH, tp), lambda b, i: (b, 0, i)),
                   pl.BlockSpec((1, tp, 128), lambda b, i: (b, i, 0)),
                   pl.BlockSpec((1, IDX_HEADS, tp), lambda b, i: (b, 0, i)),
                   pl.BlockSpec((1, ATTN_HEADS, tp), lambda b, i: (b, 0, i))],
        out_shape=[jax.ShapeDtypeStruct((bsz, ATTN_WIDTH, seq), bf),
                   jax.ShapeDtypeStruct((bsz, seq, ATTN_WIDTH), bf),
                   jax.ShapeDtypeStruct((bsz, ATTN_HEADS * VROWS, seq), bf),
                   jax.ShapeDtypeStruct((bsz, ATTN_WIDTH, seq), bf),
                   jax.ShapeDtypeStruct((bsz, seq, 128), bf),
                   jax.ShapeDtypeStruct((bsz, IDX_HEADS, seq), jnp.float32),
                   jax.ShapeDtypeStruct((bsz, ATTN_HEADS, seq), jnp.float32)],
        compiler_params=pltpu.CompilerParams(dimension_semantics=("parallel", "parallel"),
                                             vmem_limit_bytes=VMEM_LIMIT),
        name="attn_prep",
    )(proj, tile8(q_norm), kv_norm.reshape(1, KV_RANK), w_kv_up.astype(bf), tile8(k_norm),
      jnp.tile(ln_w, 2).reshape(1, 128), jnp.tile(ln_b, 2).reshape(1, 128))


def _t5_bucket(dist):
    n = jnp.maximum(dist, 0)
    max_exact = N_BUCKETS // 2
    nf = jnp.maximum(n, 1).astype(jnp.float32)
    large = max_exact + (jnp.log(nf / max_exact) / math.log(MAX_DISTANCE / max_exact) * (N_BUCKETS - max_exact)).astype(jnp.int32)
    large = jnp.minimum(large, N_BUCKETS - 1)
    return jnp.where(n < max_exact, n, large)


def _bias_tables(rel_bias):
    s = jnp.arange(QB, dtype=jnp.int32)[None, :, None]
    q = jnp.arange(QB, dtype=jnp.int32)[None, None, :]
    dist = q - s + jnp.array([2 * QB, QB, 0], jnp.int32)[:, None, None]
    onehot = (_t5_bucket(dist)[..., None] == jnp.arange(N_BUCKETS, dtype=jnp.int32)).astype(jnp.float32)
    b = jnp.einsum('tsqb,bh->thsq', onehot, rel_bias.astype(jnp.float32) * LOG2E, precision=HIGHEST)
    return jnp.where((dist >= 0)[:, None], b, NEG)


def _attn_kernel(qT_ref, qiT_ref, wT_ref, k_ref, vT_ref, ki_ref, kn_ref, tab_ref, bst_ref, o_ref,
                 keys_ref, hi_ref, msk_ref, p_ref, acc_ref, mp_ref, m_ref, *, topk):
    i = pl.program_id(1)
    n_tiles = i + 1
    row_hi = lax.broadcasted_iota(jnp.int32, (128, QB), 0) >= 64

    def head_rows(ref, h):
        pair = ref[0, (h // 2) * 128:(h // 2) * 128 + 128, :]
        return jnp.where(row_hi == bool(h % 2), pair, jnp.zeros_like(pair))

    def tile_rows(kt):
        return pl.ds(pl.multiple_of(kt * QB, QB), QB)

    def score_tile(kt, carry):
        ki = ki_ref[0, tile_rows(kt), :]
        sc = jnp.zeros((QB, QB), jnp.float32)
        for h in range(IDX_HEADS):
            d = jnp.dot(ki, head_rows(qiT_ref, h), preferred_element_type=jnp.float32)
            sc = sc + wT_ref[0, h:h + 1, :] * jnp.maximum(d, 0.0)
        srow = lax.broadcasted_iota(jnp.int32, (QB, QB), 0)
        qcol = lax.broadcasted_iota(jnp.int32, (QB, QB), 1)
        sc = jnp.where(jnp.abs(sc) < TINY, 0.0, sc)
        sc = jnp.where((kt == i) & (srow > qcol), -jnp.inf, sc)
        bits = pltpu.bitcast(sc, jnp.int32)
        keys_ref[tile_rows(kt), :] = bits ^ ((bits >> 31) & 0x7FFFFFFF)
        hi_ref[tile_rows(kt), :] = pltpu.bitcast(bits & jnp.int32(-65536), jnp.float32).astype(jnp.bfloat16)
        return carry
    lax.fori_loop(0, n_tiles, score_tile, 0)

    def count_hi_ge(cand16):
        b = cand16 ^ ((cand16 >> 15) & 0x7FFF)
        snap = jnp.where(((b & 0x8000) != 0) | ((b & 0x7F) == 0), 0, 0x0080)
        b = jnp.where((b & 0x7F80) == 0, snap, b)
        cb = pltpu.bitcast(b << 16, jnp.float32).astype(jnp.bfloat16)
        one, zero = jnp.ones((), jnp.bfloat16), jnp.zeros((), jnp.bfloat16)

        def body(kt, acc):
            hit = jnp.where(hi_ref[tile_rows(kt), :] >= cb, one, zero)
            parts = [hit[r:r + 16, :] for r in range(0, QB, 16)]
            while len(parts) > 1:
                parts = [a + b for a, b in zip(parts[::2], parts[1::2])]
            return acc + parts[0]
        acc = lax.fori_loop(0, n_tiles, body, jnp.zeros((16, QB), jnp.bfloat16))
        return jnp.sum(acc.astype(jnp.float32), axis=0, keepdims=True)

    def count(hit_of_tile):
        def body(kt, acc):
            return acc + jnp.sum(hit_of_tile(kt).reshape(QB // 8, 8, QB), axis=0)
        acc = lax.fori_loop(0, n_tiles, body, jnp.zeros((8, QB), jnp.int32))
        return jnp.sum(acc, axis=0, keepdims=True)

    def count_ge(cand):
        return count(lambda kt: jnp.where(keys_ref[tile_rows(kt), :] >= cand, 1, 0))

    def hi_step(it, r):
        cand = jnp.where(it == 0, jnp.zeros_like(r), r | (1 << (15 - it)))
        return jnp.where(count_hi_ge(cand) >= topk, cand, r)
    r16 = lax.fori_loop(0, 16, hi_step, jnp.full((1, QB), -32768, jnp.int32))

    def lo_step(it, r):
        cand = r | (1 << (15 - it))
        return jnp.where(count_ge(cand) >= topk, cand, r)
    thr = lax.fori_loop(0, 16, lo_step, r16 << 16)

    cnt_gt = count_ge(thr + 1)
    cnt_ge = count_ge(thr)
    need = topk - cnt_gt
    tie = (cnt_ge - cnt_gt > need) & (thr > KEY_NEG_INF)

    @pl.when(jnp.max(tie.astype(jnp.int32)) > 0)
    def _():
        def count_eq_below(cand):
            def ind(kt):
                idx = lax.broadcasted_iota(jnp.int32, (QB, QB), 0) + kt * QB
                return jnp.where((keys_ref[tile_rows(kt), :] == thr) & (idx < cand), 1, 0)
            return count(ind)

        def idx_step(it, r):
            cand = r | (1 << (15 - it))
            return jnp.where(count_eq_below(cand) < need, cand, r)
        last = lax.fori_loop(0, 16, idx_step, jnp.zeros((1, QB), jnp.int32))

        def drop(kt, carry):
            blk = keys_ref[tile_rows(kt), :]
            idx = lax.broadcasted_iota(jnp.int32, (QB, QB), 0) + kt * QB
            keys_ref[tile_rows(kt), :] = jnp.where(tie & (blk == thr) & (idx > last), INT_MIN, blk)
            return carry
        lax.fori_loop(0, n_tiles, drop, 0)

    def logits(kt, h):
        band = jnp.clip(kt - (i - 2), 0, 2)
        kp = k_ref[0, tile_rows(kt), (h // 2) * 128:(h // 2) * 128 + 128]
        s = jnp.dot(kp, head_rows(qT_ref, h), preferred_element_type=jnp.float32)
        return s + msk_ref[...] + tab_ref[band, h]

    def set_mask(kt):
        msk_ref[...] = jnp.where(keys_ref[tile_rows(kt), :] >= thr, 0.0, NEG)

    def max_tile(kt, carry):
        set_mask(kt)
        for h in range(ATTN_HEADS):
            s = logits(kt, h)
            mp_ref[h] = jnp.maximum(mp_ref[h], jnp.max(s.reshape(QB // 8, 8, QB), axis=0))
        return carry

    seq = kn_ref.shape[2]
    in_extent = lax.broadcasted_iota(jnp.int32, (ATTN_HEADS, seq), 1) < n_tiles * QB
    k_max = jnp.max(jnp.where(in_extent, kn_ref[0], 0.0), axis=1, keepdims=True)
    spread = jnp.zeros((1, QB), jnp.float32)
    for h in range(ATTN_HEADS):
        qh = qT_ref[0, h * ATTN_HEAD_DIM:(h + 1) * ATTN_HEAD_DIM, :].astype(jnp.float32)
        reach = jnp.sqrt(jnp.sum(qh * qh, axis=0, keepdims=True) * k_max[h:h + 1, :]) * NORM_SLACK
        m_ref[h:h + 1, :] = reach + bst_ref[0, h:h + 1, :]
        spread = jnp.maximum(spread, 2.0 * reach + bst_ref[1, h:h + 1, :])
    bound_ok = jnp.max(spread) <= MAX_SHIFT_ERROR

    @pl.when(jnp.logical_not(bound_ok))
    def _():
        mp_ref[...] = jnp.full(mp_ref.shape, NEG, jnp.float32)
        lax.fori_loop(0, n_tiles, max_tile, 0)
        for h in range(ATTN_HEADS):
            m_ref[h:h + 1, :] = jnp.max(mp_ref[h], axis=0, keepdims=True)
    m = [m_ref[h:h + 1, :] for h in range(ATTN_HEADS)]

    def exp_tile(kt, carry):
        set_mask(kt)
        for h in range(ATTN_HEADS):
            p_ref[h] = jnp.exp2(logits(kt, h) - m[h]).astype(jnp.bfloat16)
        for h in range(ATTN_HEADS):
            va = vT_ref[0, h * VROWS:(h + 1) * VROWS, tile_rows(kt)]
            acc_ref[h * VROWS:(h + 1) * VROWS, :] += jnp.dot(va, p_ref[h], preferred_element_type=jnp.float32)
        return carry

    acc_ref[...] = jnp.zeros(acc_ref.shape, jnp.float32)
    lax.fori_loop(0, n_tiles, exp_tile, 0)

    outs = [acc_ref[h * VROWS:h * VROWS + ATTN_HEAD_DIM, :] / acc_ref[h * VROWS + ATTN_HEAD_DIM:h * VROWS + ATTN_HEAD_DIM + 1, :]
            for h in range(ATTN_HEADS)]
    o_ref[0] = jnp.concatenate(outs, axis=0).T


def _dsa_attention(qT, qiT, wT, k, vT, ki2, kn2, rel_bias):
    bsz, _, seq = qT.shape
    topk = min(TOPK_MAX, seq // 4)
    assert seq % QB == 0 and topk <= QB
    assert seq // 16 <= 256
    b2 = rel_bias.astype(jnp.float32) * LOG2E
    bias_stats = jnp.stack([jnp.max(b2, axis=0), jnp.max(b2, axis=0) - jnp.min(b2, axis=0)])
    bias_stats = jnp.broadcast_to(bias_stats[:, :, None], (2, ATTN_HEADS, QB))
    return pl.pallas_call(
        functools.partial(_attn_kernel, topk=topk),
        grid=(bsz, seq // QB),
        in_specs=[
            pl.BlockSpec((1, ATTN_WIDTH, QB), lambda b, i: (b, 0, i)),
            pl.BlockSpec((1, IDX_HEADS * IDX_DIM, QB), lambda b, i: (b, 0, i)),
            pl.BlockSpec((1, IDX_HEADS, QB), lambda b, i: (b, 0, i)),
            pl.BlockSpec((1, seq, ATTN_WIDTH), lambda b, i: (b, 0, 0)),
            pl.BlockSpec((1, ATTN_HEADS * VROWS, seq), lambda b, i: (b, 0, 0)),
            pl.BlockSpec((1, seq, 128), lambda b, i: (b, 0, 0)),
            pl.BlockSpec((1, ATTN_HEADS, seq), lambda b, i: (b, 0, 0)),
            pl.BlockSpec((3, ATTN_HEADS, QB, QB), lambda b, i: (0, 0, 0, 0)),
            pl.BlockSpec((2, ATTN_HEADS, QB), lambda b, i: (0, 0, 0)),
        ],
        out_specs=pl.BlockSpec((1, QB, ATTN_WIDTH), lambda b, i: (b, i, 0)),
        out_shape=jax.ShapeDtypeStruct((bsz, seq, ATTN_WIDTH), jnp.float32),
        scratch_shapes=[
            pltpu.VMEM((seq, QB), jnp.int32),
            pltpu.VMEM((seq, QB), jnp.bfloat16),
            pltpu.VMEM((QB, QB), jnp.float32),
            pltpu.VMEM((ATTN_HEADS, QB, QB), jnp.bfloat16),
            pltpu.VMEM((ATTN_HEADS * VROWS, QB), jnp.float32),
            pltpu.VMEM((ATTN_HEADS, 8, QB), jnp.float32),
            pltpu.VMEM((ATTN_HEADS, QB), jnp.float32),
        ],
        compiler_params=pltpu.CompilerParams(dimension_semantics=("parallel", "arbitrary"),
                                             vmem_limit_bytes=VMEM_LIMIT),
        name="dsa_attention",
    )(qT, qiT, wT, k, vT, ki2, kn2, _bias_tables(rel_bias), bias_stats)


def _ssd_kernel(xbc_ref, z_ref, sm_ref, cw_ref, cb_ref, dtb_ref, a_ref, dsk_ref, nw_ref, y_ref, prev_ref, st_ref):
    q = SSD_CHUNK
    bf = jnp.bfloat16

    @pl.when(pl.program_id(1) == 0)
    def _():
        prev_ref[...] = jnp.zeros(prev_ref.shape, jnp.float32)
        st_ref[...] = jnp.zeros(st_ref.shape, jnp.float32)

    cur = xbc_ref[...]
    prev = prev_ref[...]
    row = lax.broadcasted_iota(jnp.int32, cur.shape, 0)
    acc = cur * cw_ref[CONV_WIDTH - 1:CONV_WIDTH, :] + cb_ref[...]
    for s in range(1, CONV_WIDTH):
        shifted = jnp.where(row >= s, pltpu.roll(cur, s, 0), pltpu.roll(prev, s, 0))
        acc = acc + shifted * cw_ref[CONV_WIDTH - 1 - s:CONV_WIDTH - s, :]
    prev_ref[...] = cur
    u = acc * jax.nn.sigmoid(acc)
    xs = u[:, :SSM_INNER]
    bm = u[:, SSM_INNER:SSM_INNER + SSM_GROUPS * SSM_STATE].astype(bf)
    cm = u[:, SSM_INNER + SSM_GROUPS * SSM_STATE:].astype(bf)

    t = sm_ref[...] + dtb_ref[...]
    dt = jnp.maximum(t, 0.0) + jnp.log1p(jnp.exp(-jnp.abs(t)))
    ii = lax.broadcasted_iota(jnp.int32, (q, q), 0)
    jj = lax.broadcasted_iota(jnp.int32, (q, q), 1)
    causal = ii >= jj
    acum = jnp.dot(causal.astype(jnp.float32), dt * a_ref[...], preferred_element_type=jnp.float32, precision=HIGHEST)
    acum_t = acum.T
    dt_t = dt.T
    ea = jnp.exp(acum)
    last = acum[q - 1:q, :]
    decay = jnp.exp(last - acum) * dt
    ea_last = jnp.exp(last)

    lane_hi = lax.broadcasted_iota(jnp.int32, (q, 128), 1) >= SSM_HEAD_DIM
    row_hi = lax.broadcasted_iota(jnp.int32, (128, SSM_STATE), 0) >= SSM_HEAD_DIM

    def pair_cols(v, e):
        c0, c1 = SMALL_DT + e, SMALL_DT + e + 1
        return jnp.where(lane_hi, v[:, c1:c1 + 1], v[:, c0:c0 + 1])

    for g in range(SSM_GROUPS):
        bg = bm[:, g * SSM_STATE:(g + 1) * SSM_STATE]
        cg = cm[:, g * SSM_STATE:(g + 1) * SSM_STATE]
        cb = lax.dot_general(cg, bg, NT, preferred_element_type=jnp.float32)
        for k in range(g * 4, g * 4 + 4):
            e = 2 * k
            x_pair = xs[:, k * 128:(k + 1) * 128]
            halves = []
            for h in (e, e + 1):
                c = SMALL_DT + h
                seg = acum[:, c:c + 1] - acum_t[c:c + 1, :]
                w = cb * jnp.exp(jnp.where(causal, seg, -jnp.inf)) * dt_t[c:c + 1, :]
                halves.append(jnp.dot(w.astype(bf), x_pair.astype(bf), preferred_element_type=jnp.float32))
            y_pair = jnp.where(lane_hi, halves[1], halves[0])
            state = st_ref[k]
            y_pair = y_pair + lax.dot_general(cg, state.astype(bf), NT, preferred_element_type=jnp.float32) * pair_cols(ea, e)
            y_ref[:, k * 128:(k + 1) * 128] = y_pair
            xd_t = (x_pair * pair_cols(decay, e)).T.astype(bf)
            c0 = SMALL_DT + e
            keep = jnp.where(row_hi, ea_last[:, c0 + 1:c0 + 2], ea_last[:, c0:c0 + 1])
            st_ref[k] = state * keep + jnp.dot(xd_t, bg, preferred_element_type=jnp.float32)

    y = (y_ref[...] + dsk_ref[...] * xs) * (z_ref[...] * jax.nn.sigmoid(z_ref[...]))
    half = SSM_INNER // SSM_GROUPS
    for g in range(SSM_GROUPS):
        yg = y[:, g * half:(g + 1) * half]
        yg = yg * lax.rsqrt(jnp.mean(yg * yg, axis=-1, keepdims=True) + EPS)
        y_ref[:, g * half:(g + 1) * half] = yg * nw_ref[:, g * half:(g + 1) * half]


def _mamba2_ssd(proj, bsz, seq, conv_w, conv_b, dt_bias, a_log, d_skip, norm_w):
    q = SSD_CHUNK
    nc = seq // q
    lane_row = lambda v: jnp.zeros((1, 128), jnp.float32).at[0, SMALL_DT:SMALL_DT + SSM_HEADS].set(v)
    const = lambda shape: pl.BlockSpec(shape, lambda b, c: (0,) * len(shape))
    return pl.pallas_call(
        _ssd_kernel,
        grid=(bsz, nc),
        in_specs=[pl.BlockSpec((q, CONV_CH), lambda b, c: (b * nc + c, COL_XBC // CONV_CH)),
                  pl.BlockSpec((q, SSM_INNER), lambda b, c: (b * nc + c, COL_Z // SSM_INNER)),
                  pl.BlockSpec((q, 128), lambda b, c: (b * nc + c, COL_SMALL // 128)),
                  const((CONV_WIDTH, CONV_CH)), const((1, CONV_CH)), const((1, 128)), const((1, 128)),
                  const((1, SSM_INNER)), const((1, SSM_INNER))],
        out_specs=pl.BlockSpec((q, SSM_INNER), lambda b, c: (b * nc + c, 0)),
        out_shape=jax.ShapeDtypeStruct((bsz * seq, SSM_INNER), jnp.float32),
        scratch_shapes=[pltpu.VMEM((q, CONV_CH), jnp.float32),
                        pltpu.VMEM((SSM_HEADS // 2, 2 * SSM_HEAD_DIM, SSM_STATE), jnp.float32)],
        compiler_params=pltpu.CompilerParams(dimension_semantics=("parallel", "arbitrary"),
                                             vmem_limit_bytes=VMEM_LIMIT),
        name="mamba2_ssd",
    )(proj, proj, proj, conv_w, conv_b.reshape(1, CONV_CH), lane_row(dt_bias), lane_row(-jnp.exp(a_log)),
      jnp.repeat(d_skip, SSM_HEAD_DIM).reshape(1, SSM_INNER), norm_w.reshape(1, SSM_INNER))


def _mix_out_kernel(a_ref, s_ref, gl_ref, x_ref, gm_ref, wo_ref, ws_ref, wout_ref,
                    nf_ref, scf_ref, shf_ref, wr_ref, br_ref, xo_ref, h_ref, lg_ref):
    bf = jnp.bfloat16
    ya = jnp.dot(a_ref[...].astype(bf), wo_ref[...], preferred_element_type=jnp.float32)
    ys = jnp.dot(s_ref[...].astype(bf), ws_ref[...], preferred_element_type=jnp.float32)
    mixed = jax.nn.sigmoid(gl_ref[:, :D_MODEL]) * ya + jax.nn.sigmoid(gl_ref[:, D_MODEL:]) * ys
    x = x_ref[...] + gm_ref[0] * jnp.dot(mixed.astype(bf), wout_ref[...], preferred_element_type=jnp.float32)
    xo_ref[...] = x
    y = x * lax.rsqrt(jnp.mean(x * x, axis=-1, keepdims=True) + EPS) * nf_ref[...]
    h = y * (1.0 + scf_ref[0]) + shf_ref[0]
    hb = pltpu.bitcast(h.astype(bf).astype(jnp.float32), jnp.int32)
    half = D_MODEL // 2
    h_ref[...] = (hb[:, :half] & jnp.int32(-65536)) | lax.shift_right_logical(hb[:, half:], 16)
    lg_ref[...] = jnp.dot(h, wr_ref[...], preferred_element_type=jnp.float32, precision=HIGHEST) + br_ref[...]


def _mix_out(attn2, ssd2, proj, x2, g_m, w_attn_o, w_ssm_o, w_out, norm_ffn, sc_f, sh_f, w_router, b_router, seq, tm=512):
    t, d = x2.shape
    per_b = seq // tm
    bf = jnp.bfloat16
    const = lambda shape: pl.BlockSpec(shape, lambda i: (0,) * len(shape))
    perb = pl.BlockSpec((1, 1, d), lambda i: (i // per_b, 0, 0))
    wr = jnp.pad(w_router, ((0, 0), (0, 128 - N_EXPERTS)))
    br = jnp.pad(b_router, (0, 128 - N_EXPERTS)).reshape(1, 128)
    return pl.pallas_call(
        _mix_out_kernel,
        grid=(t // tm,),
        in_specs=[pl.BlockSpec((tm, ATTN_WIDTH), lambda i: (i, 0)),
                  pl.BlockSpec((tm, SSM_INNER), lambda i: (i, 0)),
                  pl.BlockSpec((tm, 2 * d), lambda i: (i, COL_GATE // (2 * d))),
                  pl.BlockSpec((tm, d), lambda i: (i, 0)),
                  perb,
                  const((ATTN_WIDTH, d)), const((SSM_INNER, d)), const((d, d)),
                  const((1, d)), perb, perb, const((d, 128)), const((1, 128))],
        out_specs=[pl.BlockSpec((tm, d), lambda i: (i, 0)),
                   pl.BlockSpec((tm, d // 2), lambda i: (i, 0)),
                   pl.BlockSpec((tm, 128), lambda i: (i, 0))],
        out_shape=[jax.ShapeDtypeStruct((t, d), jnp.float32),
                   jax.ShapeDtypeStruct((t, d // 2), jnp.int32),
                   jax.ShapeDtypeStruct((t, 128), jnp.float32)],
        compiler_params=pltpu.CompilerParams(dimension_semantics=("parallel",), vmem_limit_bytes=VMEM_LIMIT),
        name="mix_out",
    )(attn2, ssd2, proj, x2, g_m[:, None, :], w_attn_o.astype(bf), w_ssm_o.astype(bf), w_out.astype(bf),
      norm_ffn.reshape(1, d), sc_f[:, None, :], sh_f[:, None, :], wr, br)


def _moe_kernel(be_ref, nb_ref, x_ref, g_ref, wgu_ref, bgu_ref, wdn_ref, bdn_ref, o_ref, wgu_bf, wdn_bf):
    i = pl.program_id(0)

    @pl.when((i == 0) | (be_ref[i] != be_ref[jnp.maximum(i - 1, 0)]))
    def _():
        wgu_bf[...] = wgu_ref[0, 0].astype(jnp.bfloat16)
        wdn_bf[...] = wdn_ref[0, 0].astype(jnp.bfloat16)

    @pl.when(i < nb_ref[0])
    def _():
        words = x_ref[...]
        x_hi = pltpu.bitcast(words & jnp.int32(-65536), jnp.float32).astype(jnp.bfloat16)
        x_lo = pltpu.bitcast(words << 16, jnp.float32).astype(jnp.bfloat16)
        x = jnp.concatenate([x_hi, x_lo], axis=1)
        gu = jnp.dot(x, wgu_bf[...], preferred_element_type=jnp.float32) + bgu_ref[0, 0]
        g = jnp.minimum(gu[:, :D_EXPERT], SWIGLU_LIMIT)
        u = jnp.clip(gu[:, D_EXPERT:], -SWIGLU_LIMIT, SWIGLU_LIMIT)
        act = (u + 1.0) * (g * jax.nn.sigmoid(SWIGLU_ALPHA * g))
        out = jnp.dot(act.astype(jnp.bfloat16), wdn_bf[...], preferred_element_type=jnp.float32) + bdn_ref[0, 0]
        o_ref[...] = out * g_ref[...]

    @pl.when(i >= nb_ref[0])
    def _():
        o_ref[...] = jnp.zeros_like(o_ref)


def _moe_ffn(xs, row_gate, blk_exp, n_used, w_gu, b_gu, w_dn, b_dn, layer):
    n_rows = xs.shape[0]
    d = D_MODEL
    tm = MOE_TM
    grid_spec = pltpu.PrefetchScalarGridSpec(
        num_scalar_prefetch=2,
        grid=(n_rows // tm,),
        in_specs=[pl.BlockSpec((tm, d // 2), lambda i, be, nb: (i, 0)),
                  pl.BlockSpec((tm, 1), lambda i, be, nb: (i, 0)),
                  pl.BlockSpec((1, 1, d, 2 * D_EXPERT), lambda i, be, nb: (layer, be[i], 0, 0)),
                  pl.BlockSpec((1, 1, 1, 2 * D_EXPERT), lambda i, be, nb: (layer, be[i], 0, 0)),
                  pl.BlockSpec((1, 1, D_EXPERT, d), lambda i, be, nb: (layer, be[i], 0, 0)),
                  pl.BlockSpec((1, 1, 1, d), lambda i, be, nb: (layer, be[i], 0, 0))],
        out_specs=pl.BlockSpec((tm, d), lambda i, be, nb: (i, 0)),
        scratch_shapes=[pltpu.VMEM((d, 2 * D_EXPERT), jnp.bfloat16), pltpu.VMEM((D_EXPERT, d), jnp.bfloat16)],
    )
    return pl.pallas_call(
        _moe_kernel,
        grid_spec=grid_spec,
        out_shape=jax.ShapeDtypeStruct((n_rows, d), jnp.float32),
        compiler_params=pltpu.CompilerParams(dimension_semantics=("arbitrary",), vmem_limit_bytes=VMEM_LIMIT),
        name="moe_ffn",
    )(blk_exp, n_used, xs, row_gate[:, None], w_gu, b_gu[:, :, None, :], w_dn, b_dn[:, :, None, :])


def _moe(h2, logits, w_gu, b_gu, w_dn, b_dn, layer):
    t = h2.shape[0]
    d = D_MODEL
    tm = MOE_TM
    i32 = jnp.int32
    top_val, top_idx = lax.top_k(logits, TOP_K)
    gates = jax.nn.softmax(top_val, axis=-1).reshape(-1)
    n_assign = t * TOP_K
    n_rows = n_assign + N_EXPERTS * tm
    e_flat = top_idx.reshape(n_assign).astype(i32)
    counts = jnp.sum((e_flat[:, None] == jnp.arange(N_EXPERTS, dtype=i32)[None, :]).astype(i32), axis=0)
    padded = (counts + tm - 1) // tm * tm
    filler_exp = jnp.repeat(jnp.arange(N_EXPERTS, dtype=i32), tm)
    filler_key = jnp.where(jnp.tile(jnp.arange(tm, dtype=i32), N_EXPERTS) < jnp.repeat(padded - counts, tm),
                           filler_exp, N_EXPERTS)
    keys = jnp.concatenate([e_flat, filler_key])
    gate_in = jnp.concatenate([gates, jnp.zeros((N_EXPERTS * tm,), jnp.float32)])
    rows = jnp.arange(n_rows, dtype=i32)
    row_key, row_src, row_gate = lax.sort((keys, rows, gate_in), num_keys=1)
    row_tok = jnp.where(row_src < n_assign, row_src // TOP_K, rows % t)
    _, row_of = lax.sort((row_src, rows), num_keys=1)
    dest = row_of[:n_assign].reshape(t, TOP_K).T.reshape(-1)
    blk_exp = jnp.minimum(row_key[::tm], N_EXPERTS - 1)
    n_used = (jnp.sum(padded, keepdims=True) // tm).astype(i32)
    out = _moe_ffn(h2[row_tok], row_gate, blk_exp, n_used, w_gu, b_gu, w_dn, b_dn, layer)
    return jnp.sum(out[dest].reshape(TOP_K, t, d), axis=0)


def kernel(x, c, rel_bias, w_ada, b_ada, norm_mix, norm_ffn, w_in, kv_norm, w_kv_up, q_norm, k_norm,
           idx_k_ln_w, idx_k_ln_b, w_attn_o, conv_w, conv_b, dt_bias, a_log, d_skip, ssm_norm, w_ssm_o,
           w_out, w_router, b_router, w_gu, b_gu, w_dn, b_dn):
    bsz, seq, d = x.shape
    t = bsz * seq
    cond = jax.nn.silu(c)
    x2 = x.reshape(t, d)
    for l in range(DEPTH):
        mod = cond @ w_ada[l] + b_ada[l]
        sh_m, sc_m, g_m, sh_f, sc_f, g_f = jnp.split(mod, 6, axis=-1)
        proj = _in_proj(x2, norm_mix[l], sc_m, sh_m, _pack_w_in(w_in[l]), seq)
        qT, k, vT, qiT, ki2, wT, kn2 = _prep(proj, bsz, seq, q_norm[l], kv_norm[l], w_kv_up[l], k_norm[l],
                                             idx_k_ln_w[l], idx_k_ln_b[l])
        attn = _dsa_attention(qT, qiT, wT, k, vT, ki2, kn2, rel_bias)
        y_ssd = _mamba2_ssd(proj, bsz, seq, conv_w[l], conv_b[l], dt_bias[l], a_log[l], d_skip[l], ssm_norm[l])
        x2, h2, logits = _mix_out(attn.reshape(t, ATTN_WIDTH), y_ssd, proj, x2, g_m,
                                  w_attn_o[l], w_ssm_o[l], w_out[l], norm_ffn[l], sc_f, sh_f,
                                  w_router[l], b_router[l], seq)
        y = _moe(h2, logits[:, :N_EXPERTS], w_gu, b_gu, w_dn, b_dn, l)
        x2 = x2 + jnp.repeat(g_f, seq, axis=0) * y
    return x2.reshape(bsz, seq, d)
```

```python
import functools
import math

import jax
import jax.numpy as jnp
import numpy as np
from jax import lax
from jax.experimental import pallas as pl
from jax.experimental.pallas import tpu as pltpu

D_MODEL = 1024
DEPTH = 2
ATTN_HEADS = 8
ATTN_HEAD_DIM = 64
ATTN_WIDTH = ATTN_HEADS * ATTN_HEAD_DIM
KV_RANK = 256
IDX_HEADS = 8
IDX_DIM = 64
TOPK_MAX = 256
N_BUCKETS = 32
MAX_DISTANCE = 128
SSM_HEADS = 16
SSM_HEAD_DIM = 64
SSM_INNER = SSM_HEADS * SSM_HEAD_DIM
SSM_GROUPS = 2
SSM_STATE = 128
CONV_WIDTH = 4
CONV_CH = SSM_INNER + 2 * SSM_GROUPS * SSM_STATE
SSD_CHUNK = 128
N_EXPERTS = 32
TOP_K = 4
D_EXPERT = D_MODEL
SWIGLU_LIMIT = 7.0
SWIGLU_ALPHA = 1.702
EPS = 1e-6

COL_Q = 0
COL_KV = 512
COL_QI = 768
COL_SMALL = 1280
COL_XBC = 1536
COL_Z = 3072
COL_GATE = 4096
PROJ_COLS = 6144
PREP_COLS = 1408
SMALL_KI, SMALL_WI, SMALL_DT = 0, 64, 72

QB = 256
VROWS = 80
INT_MIN = -2 ** 31
KEY_NEG_INF = (0xFF800000 ^ 0x7FFFFFFF) - 2 ** 32
NEG = -1e30
TINY = 2.0 ** -126
LOG2E = math.log2(math.e)
NORM_SLACK = 1.02
MAX_SHIFT_ERROR = 96.0
VMEM_LIMIT = 56 * 1024 * 1024
MOE_TM = 512
HIGHEST = lax.Precision.HIGHEST
NT = (((1,), (1,)), ((), ()))


def _pack_w_in(w):
    o = np.cumsum((0, ATTN_WIDTH, KV_RANK, IDX_HEADS * IDX_DIM, IDX_DIM, IDX_HEADS, SSM_INNER, CONV_CH, SSM_HEADS, 2 * D_MODEL))
    q, kv, qi, ki, wi, z, xbc, dt, gate = (w[:, int(o[n]):int(o[n + 1])] for n in range(9))
    zeros = lambda n: jnp.zeros((w.shape[0], n), w.dtype)
    small = jnp.concatenate([ki, wi, dt, zeros(128 - 88)], axis=1)
    packed = jnp.concatenate([q, kv, qi, small, zeros(COL_XBC - PREP_COLS), xbc, z, gate], axis=1)
    assert packed.shape[1] == PROJ_COLS
    return packed.astype(jnp.bfloat16)


def _in_proj_kernel(x_ref, g_ref, sc_ref, sh_ref, w_ref, o_ref, h_ref):
    @pl.when(pl.program_id(1) == 0)
    def _():
        x = x_ref[...]
        y = x * lax.rsqrt(jnp.mean(x * x, axis=-1, keepdims=True) + EPS) * g_ref[...]
        h_ref[...] = (y * (1.0 + sc_ref[0]) + sh_ref[0]).astype(jnp.bfloat16)
    o_ref[...] = jnp.dot(h_ref[...], w_ref[...], preferred_element_type=jnp.float32)


def _in_proj(x2, gain, sc, sh, w_packed, seq, tm=1024, tn=1024):
    t, d = x2.shape
    per_b = seq // tm
    return pl.pallas_call(
        _in_proj_kernel,
        grid=(t // tm, PROJ_COLS // tn),
        in_specs=[pl.BlockSpec((tm, d), lambda i, j: (i, 0)),
                  pl.BlockSpec((1, d), lambda i, j: (0, 0)),
                  pl.BlockSpec((1, 1, d), lambda i, j: (i // per_b, 0, 0)),
                  pl.BlockSpec((1, 1, d), lambda i, j: (i // per_b, 0, 0)),
                  pl.BlockSpec((d, tn), lambda i, j: (0, j))],
        out_specs=pl.BlockSpec((tm, tn), lambda i, j: (i, j)),
        out_shape=jax.ShapeDtypeStruct((t, PROJ_COLS), jnp.float32),
        scratch_shapes=[pltpu.VMEM((tm, d), jnp.bfloat16)],
        compiler_params=pltpu.CompilerParams(dimension_semantics=("parallel", "arbitrary"),
                                             vmem_limit_bytes=VMEM_LIMIT),
        name="in_proj",
    )(x2, gain.reshape(1, d), sc[:, None, :], sh[:, None, :], w_packed)


def _head_rms_t(xt):
    x3 = xt.reshape(ATTN_HEADS, ATTN_HEAD_DIM, xt.shape[1])
    return lax.rsqrt(jnp.mean(x3 * x3, axis=1, keepdims=True) + EPS)


def _prep_kernel(p_ref, qg_ref, kvg_ref, wkv_ref, kg_ref, lng_ref, lnb_ref,
                 qT_ref, k_ref, vT_ref, qiT_ref, ki_ref, wT_ref, kn2_ref):
    n = p_ref.shape[0]
    q = p_ref[:, COL_Q:COL_Q + ATTN_WIDTH]
    lat = p_ref[:, COL_KV:COL_KV + KV_RANK]
    qi = p_ref[:, COL_QI:COL_QI + IDX_HEADS * IDX_DIM]
    sm = p_ref[:, COL_SMALL:COL_SMALL + 128]

    scale = ATTN_HEAD_DIM ** -0.5 * LOG2E
    qt = q.T
    qn = qt.reshape(ATTN_HEADS, ATTN_HEAD_DIM, n) * _head_rms_t(qt)
    qT_ref[0] = (qn.reshape(ATTN_WIDTH, n) * qg_ref[...] * scale).astype(jnp.bfloat16)

    latn = lat * lax.rsqrt(jnp.mean(lat * lat, axis=-1, keepdims=True) + EPS) * kvg_ref[...]
    kv = jnp.dot(latn.astype(jnp.bfloat16), wkv_ref[...], preferred_element_type=jnp.float32)
    kt = kv[:, :ATTN_WIDTH].T
    kn = (kt.reshape(ATTN_HEADS, ATTN_HEAD_DIM, n) * _head_rms_t(kt)).reshape(ATTN_WIDTH, n) * kg_ref[...]
    k_ref[0] = kn.T.astype(jnp.bfloat16)
    kn3 = kn.reshape(ATTN_HEADS, ATTN_HEAD_DIM, n)
    kn2_ref[0] = jnp.sum(kn3 * kn3, axis=1)
    vt = kv[:, ATTN_WIDTH:].T.reshape(ATTN_HEADS, ATTN_HEAD_DIM, n)
    ones = jnp.ones((ATTN_HEADS, VROWS - ATTN_HEAD_DIM, n), jnp.float32)
    vT_ref[0] = jnp.concatenate([vt, ones], axis=1).reshape(ATTN_HEADS * VROWS, n).astype(jnp.bfloat16)

    qiT_ref[0] = (qi * (IDX_DIM ** -0.5)).T.astype(jnp.bfloat16)

    lane = lax.broadcasted_iota(jnp.int32, sm.shape, 1)
    kid = jnp.where(lane < IDX_DIM, sm, pltpu.roll(sm, IDX_DIM, 1))
    mu = jnp.mean(kid, axis=-1, keepdims=True)
    var = jnp.mean(jnp.square(kid - mu), axis=-1, keepdims=True)
    ki_ref[0] = ((kid - mu) * lax.rsqrt(var + EPS) * lng_ref[...] + lnb_ref[...]).astype(jnp.bfloat16)

    wT_ref[0] = sm.T[SMALL_WI:SMALL_WI + IDX_HEADS, :] * (IDX_HEADS ** -0.5)


def _prep(proj, bsz, seq, q_norm, kv_norm, w_kv_up, k_norm, ln_w, ln_b, tp=512):
    nb = seq // tp
    tile8 = lambda g: jnp.tile(g, ATTN_HEADS).reshape(ATTN_WIDTH, 1)
    const = lambda shape: pl.BlockSpec(shape, lambda b, i: (0,) * len(shape))
    bf = jnp.bfloat16
    return pl.pallas_call(
        _prep_kernel,
        grid=(bsz, nb),
        in_specs=[pl.BlockSpec((tp, PREP_COLS), lambda b, i: (b * nb + i, 0)),
                  const((ATTN_WIDTH, 1)), const((1, KV_RANK)), const((KV_RANK, 2 * ATTN_WIDTH)),
                  const((ATTN_WIDTH, 1)), const((1, 128)), const((1, 128))],
        out_specs=[pl.BlockSpec((1, ATTN_WIDTH, tp), lambda b, i: (b, 0, i)),
                   pl.BlockSpec((1, tp, ATTN_WIDTH), lambda b, i: (b, i, 0)),
                   pl.BlockSpec((1, ATTN_HEADS * VROWS, tp), lambda b, i: (b, 0, i)),
                   pl.BlockSpec((1, ATTN_WIDTH, tp), lambda b, i: (b, 0, i)),
                   pl.BlockSpec((1, tp, 128), lambda b, i: (b, i, 0)),
                   pl.BlockSpec((1, IDX_HEADS, tp), lambda b, i: (b, 0, i)),
                   pl.BlockSpec((1, ATTN_HEADS, tp), lambda b, i: (b, 0, i))],
        out_shape=[jax.ShapeDtypeStruct((bsz, ATTN_WIDTH, seq), bf),
                   jax.ShapeDtypeStruct((bsz, seq, ATTN_WIDTH), bf),
                   jax.ShapeDtypeStruct((bsz, ATTN_HEADS * VROWS, seq), bf),
                   jax.ShapeDtypeStruct((bsz, ATTN_WIDTH, seq), bf),
                   jax.ShapeDtypeStruct((bsz, seq, 128), bf),
                   jax.ShapeDtypeStruct((bsz, IDX_HEADS, seq), jnp.float32),
                   jax.ShapeDtypeStruct((bsz, ATTN_HEADS, seq), jnp.float32)],
        compiler_params=pltpu.CompilerParams(dimension_semantics=("parallel", "parallel"),
                                             vmem_limit_bytes=VMEM_LIMIT),
        name="attn_prep",
    )(proj, tile8(q_norm), kv_norm.reshape(1, KV_RANK), w_kv_up.astype(bf), tile8(k_norm),
      jnp.tile(ln_w, 2).reshape(1, 128), jnp.tile(ln_b, 2).reshape(1, 128))


def _t5_bucket(dist):
    n = jnp.maximum(dist, 0)
    max_exact = N_BUCKETS // 2
    nf = jnp.maximum(n, 1).astype(jnp.float32)
    large = max_exact + (jnp.log(nf / max_exact) / math.log(MAX_DISTANCE / max_exact) * (N_BUCKETS - max_exact)).astype(jnp.int32)
    large = jnp.minimum(large, N_BUCKETS - 1)
    return jnp.where(n < max_exact, n, large)


def _bias_tables(rel_bias):
    s = jnp.arange(QB, dtype=jnp.int32)[None, :, None]
    q = jnp.arange(QB, dtype=jnp.int32)[None, None, :]
    dist = q - s + jnp.array([2 * QB, QB, 0], jnp.int32)[:, None, None]
    onehot = (_t5_bucket(dist)[..., None] == jnp.arange(N_BUCKETS, dtype=jnp.int32)).astype(jnp.float32)
    b = jnp.einsum('tsqb,bh->thsq', onehot, rel_bias.astype(jnp.float32) * LOG2E, precision=HIGHEST)
    return jnp.where((dist >= 0)[:, None], b, NEG)


def _attn_kernel(qT_ref, qiT_ref, wT_ref, k_ref, vT_ref, ki_ref, kn_ref, tab_ref, bst_ref, o_ref,
                 keys_ref, hi_ref, msk_ref, p_ref, acc_ref, mp_ref, m_ref, *, topk):
    i = pl.program_id(1)
    n_tiles = i + 1
    row_hi = lax.broadcasted_iota(jnp.int32, (128, QB), 0) >= 64

    def head_rows(ref, h):
        pair = ref[0, (h // 2) * 128:(h // 2) * 128 + 128, :]
        return jnp.where(row_hi == bool(h % 2), pair, jnp.zeros_like(pair))

    def tile_rows(kt):
        return pl.ds(pl.multiple_of(kt * QB, QB), QB)

    def score_tile(kt, carry):
        ki = ki_ref[0, tile_rows(kt), :]
        sc = jnp.zeros((QB, QB), jnp.float32)
        for h in range(IDX_HEADS):
            d = jnp.dot(ki, head_rows(qiT_ref, h), preferred_element_type=jnp.float32)
            sc = sc + wT_ref[0, h:h + 1, :] * jnp.maximum(d, 0.0)
        srow = lax.broadcasted_iota(jnp.int32, (QB, QB), 0)
        qcol = lax.broadcasted_iota(jnp.int32, (QB, QB), 1)
        sc = jnp.where(jnp.abs(sc) < TINY, 0.0, sc)
        sc = jnp.where((kt == i) & (srow > qcol), -jnp.inf, sc)
        bits = pltpu.bitcast(sc, jnp.int32)
        keys_ref[tile_rows(kt), :] = bits ^ ((bits >> 31) & 0x7FFFFFFF)
        hi_ref[tile_rows(kt), :] = pltpu.bitcast(bits & jnp.int32(-65536), jnp.float32).astype(jnp.bfloat16)
        return carry
    lax.fori_loop(0, n_tiles, score_tile, 0)

    def count_hi_ge(cand16):
        b = cand16 ^ ((cand16 >> 15) & 0x7FFF)
        snap = jnp.where(((b & 0x8000) != 0) | ((b & 0x7F) == 0), 0, 0x0080)
        b = jnp.where((b & 0x7F80) == 0, snap, b)
        cb = pltpu.bitcast(b << 16, jnp.float32).astype(jnp.bfloat16)
        one, zero = jnp.ones((), jnp.bfloat16), jnp.zeros((), jnp.bfloat16)

        def body(kt, acc):
            hit = jnp.where(hi_ref[tile_rows(kt), :] >= cb, one, zero)
            parts = [hit[r:r + 16, :] for r in range(0, QB, 16)]
            while len(parts) > 1:
                parts = [a + b for a, b in zip(parts[::2], parts[1::2])]
            return acc + parts[0]
        acc = lax.fori_loop(0, n_tiles, body, jnp.zeros((16, QB), jnp.bfloat16))
        return jnp.sum(acc.astype(jnp.float32), axis=0, keepdims=True)

    def count(hit_of_tile):
        def body(kt, acc):
            return acc + jnp.sum(hit_of_tile(kt).reshape(QB // 8, 8, QB), axis=0)
        acc = lax.fori_loop(0, n_tiles, body, jnp.zeros((8, QB), jnp.int32))
        return jnp.sum(acc, axis=0, keepdims=True)

    def count_ge(cand):
        return count(lambda kt: jnp.where(keys_ref[tile_rows(kt), :] >= cand, 1, 0))

    def hi_step(it, r):
        cand = jnp.where(it == 0, jnp.zeros_like(r), r | (1 << (15 - it)))
        return jnp.where(count_hi_ge(cand) >= topk, cand, r)
    r16 = lax.fori_loop(0, 16, hi_step, jnp.full((1, QB), -32768, jnp.int32))

    def lo_step(it, r):
        cand = r | (1 << (15 - it))
        return jnp.where(count_ge(cand) >= topk, cand, r)
    thr = lax.fori_loop(0, 16, lo_step, r16 << 16)

    cnt_gt = count_ge(thr + 1)
    cnt_ge = count_ge(thr)
    need = topk - cnt_gt
    tie = (cnt_ge - cnt_gt > need) & (thr > KEY_NEG_INF)

    @pl.when(jnp.max(tie.astype(jnp.int32)) > 0)
    def _():
        def count_eq_below(cand):
            def ind(kt):
                idx = lax.broadcasted_iota(jnp.int32, (QB, QB), 0) + kt * QB
                return jnp.where((keys_ref[tile_rows(kt), :] == thr) & (idx < cand), 1, 0)
            return count(ind)

        def idx_step(it, r):
            cand = r | (1 << (15 - it))
            return jnp.where(count_eq_below(cand) < need, cand, r)
        last = lax.fori_loop(0, 16, idx_step, jnp.zeros((1, QB), jnp.int32))

        def drop(kt, carry):
            blk = keys_ref[tile_rows(kt), :]
            idx = lax.broadcasted_iota(jnp.int32, (QB, QB), 0) + kt * QB
            keys_ref[tile_rows(kt), :] = jnp.where(tie & (blk == thr) & (idx > last), INT_MIN, blk)
            return carry
        lax.fori_loop(0, n_tiles, drop, 0)

    def logits(kt, h):
        band = jnp.clip(kt - (i - 2), 0, 2)
        kp = k_ref[0, tile_rows(kt), (h // 2) * 128:(h // 2) * 128 + 128]
        s = jnp.dot(kp, head_rows(qT_ref, h), preferred_element_type=jnp.float32)
        return s + msk_ref[...] + tab_ref[band, h]

    def set_mask(kt):
        msk_ref[...] = jnp.where(keys_ref[tile_rows(kt), :] >= thr, 0.0, NEG)

    def max_tile(kt, carry):
        set_mask(kt)
        for h in range(ATTN_HEADS):
            s = logits(kt, h)
            mp_ref[h] = jnp.maximum(mp_ref[h], jnp.max(s.reshape(QB // 8, 8, QB), axis=0))
        return carry

    seq = kn_ref.shape[2]
    in_extent = lax.broadcasted_iota(jnp.int32, (ATTN_HEADS, seq), 1) < n_tiles * QB
    k_max = jnp.max(jnp.where(in_extent, kn_ref[0], 0.0), axis=1, keepdims=True)
    spread = jnp.zeros((1, QB), jnp.float32)
    for h in range(ATTN_HEADS):
        qh = qT_ref[0, h * ATTN_HEAD_DIM:(h + 1) * ATTN_HEAD_DIM, :].astype(jnp.float32)
        reach = jnp.sqrt(jnp.sum(qh * qh, axis=0, keepdims=True) * k_max[h:h + 1, :]) * NORM_SLACK
        m_ref[h:h + 1, :] = reach + bst_ref[0, h:h + 1, :]
        spread = jnp.maximum(spread, 2.0 * reach + bst_ref[1, h:h + 1, :])
    bound_ok = jnp.max(spread) <= MAX_SHIFT_ERROR

    @pl.when(jnp.logical_not(bound_ok))
    def _():
        mp_ref[...] = jnp.full(mp_ref.shape, NEG, jnp.float32)
        lax.fori_loop(0, n_tiles, max_tile, 0)
        for h in range(ATTN_HEADS):
            m_ref[h:h + 1, :] = jnp.max(mp_ref[h], axis=0, keepdims=True)
    m = [m_ref[h:h + 1, :] for h in range(ATTN_HEADS)]

    def exp_tile(kt, carry):
        set_mask(kt)
        for h in range(ATTN_HEADS):
            p_ref[h] = jnp.exp2(logits(kt, h) - m[h]).astype(jnp.bfloat16)
        for h in range(ATTN_HEADS):
            va = vT_ref[0, h * VROWS:(h + 1) * VROWS, tile_rows(kt)]
            acc_ref[h * VROWS:(h + 1) * VROWS, :] += jnp.dot(va, p_ref[h], preferred_element_type=jnp.float32)
        return carry

    acc_ref[...] = jnp.zeros(acc_ref.shape, jnp.float32)
    lax.fori_loop(0, n_tiles, exp_tile, 0)

    outs = [acc_ref[h * VROWS:h * VROWS + ATTN_HEAD_DIM, :] / acc_ref[h * VROWS + ATTN_HEAD_DIM:h * VROWS + ATTN_HEAD_DIM + 1, :]
            for h in range(ATTN_HEADS)]
    o_ref[0] = jnp.concatenate(outs, axis=0).T


def _dsa_attention(qT, qiT, wT, k, vT, ki2, kn2, rel_bias):
    bsz, _, seq = qT.shape
    topk = min(TOPK_MAX, seq // 4)
    assert seq % QB == 0 and topk <= QB
    assert seq // 16 <= 256
    b2 = rel_bias.astype(jnp.float32) * LOG2E
    bias_stats = jnp.stack([jnp.max(b2, axis=0), jnp.max(b2, axis=0) - jnp.min(b2, axis=0)])
    bias_stats = jnp.broadcast_to(bias_stats[:, :, None], (2, ATTN_HEADS, QB))
    return pl.pallas_call(
        functools.partial(_attn_kernel, topk=topk),
        grid=(bsz, seq // QB),
        in_specs=[
            pl.BlockSpec((1, ATTN_WIDTH, QB), lambda b, i: (b, 0, i)),
            pl.BlockSpec((1, IDX_HEADS * IDX_DIM, QB), lambda b, i: (b, 0, i)),
            pl.BlockSpec((1, IDX_HEADS, QB), lambda b, i: (b, 0, i)),
            pl.BlockSpec((1, seq, ATTN_WIDTH), lambda b, i: (b, 0, 0)),
            pl.BlockSpec((1, ATTN_HEADS * VROWS, seq), lambda b, i: (b, 0, 0)),
            pl.BlockSpec((1, seq, 128), lambda b, i: (b, 0, 0)),
            pl.BlockSpec((1, ATTN_HEADS, seq), lambda b, i: (b, 0, 0)),
            pl.BlockSpec((3, ATTN_HEADS, QB, QB), lambda b, i: (0, 0, 0, 0)),
            pl.BlockSpec((2, ATTN_HEADS, QB), lambda b, i: (0, 0, 0)),
        ],
        out_specs=pl.BlockSpec((1, QB, ATTN_WIDTH), lambda b, i: (b, i, 0)),
        out_shape=jax.ShapeDtypeStruct((bsz, seq, ATTN_WIDTH), jnp.float32),
        scratch_shapes=[
            pltpu.VMEM((seq, QB), jnp.int32),
            pltpu.VMEM((seq, QB), jnp.bfloat16),
            pltpu.VMEM((QB, QB), jnp.float32),
            pltpu.VMEM((ATTN_HEADS, QB, QB), jnp.bfloat16),
            pltpu.VMEM((ATTN_HEADS * VROWS, QB), jnp.float32),
            pltpu.VMEM((ATTN_HEADS, 8, QB), jnp.float32),
            pltpu.VMEM((ATTN_HEADS, QB), jnp.float32),
        ],
        compiler_params=pltpu.CompilerParams(dimension_semantics=("parallel", "arbitrary"),
                                             vmem_limit_bytes=VMEM_LIMIT),
        name="dsa_attention",
    )(qT, qiT, wT, k, vT, ki2, kn2, _bias_tables(rel_bias), bias_stats)


def _ssd_kernel(xbc_ref, z_ref, sm_ref, cw_ref, cb_ref, dtb_ref, a_ref, dsk_ref, nw_ref, y_ref, prev_ref, st_ref):
    q = SSD_CHUNK
    bf = jnp.bfloat16

    @pl.when(pl.program_id(1) == 0)
    def _():
        prev_ref[...] = jnp.zeros(prev_ref.shape, jnp.float32)
        st_ref[...] = jnp.zeros(st_ref.shape, jnp.float32)

    cur = xbc_ref[...]
    prev = prev_ref[...]
    row = lax.broadcasted_iota(jnp.int32, cur.shape, 0)
    acc = cur * cw_ref[CONV_WIDTH - 1:CONV_WIDTH, :] + cb_ref[...]
    for s in range(1, CONV_WIDTH):
        shifted = jnp.where(row >= s, pltpu.roll(cur, s, 0), pltpu.roll(prev, s, 0))
        acc = acc + shifted * cw_ref[CONV_WIDTH - 1 - s:CONV_WIDTH - s, :]
    prev_ref[...] = cur
    u = acc * jax.nn.sigmoid(acc)
    xs = u[:, :SSM_INNER]
    bm = u[:, SSM_INNER:SSM_INNER + SSM_GROUPS * SSM_STATE].astype(bf)
    cm = u[:, SSM_INNER + SSM_GROUPS * SSM_STATE:].astype(bf)

    t = sm_ref[...] + dtb_ref[...]
    dt = jnp.maximum(t, 0.0) + jnp.log1p(jnp.exp(-jnp.abs(t)))
    ii = lax.broadcasted_iota(jnp.int32, (q, q), 0)
    jj = lax.broadcasted_iota(jnp.int32, (q, q), 1)
    causal = ii >= jj
    acum = jnp.dot(causal.astype(jnp.float32), dt * a_ref[...], preferred_element_type=jnp.float32, precision=HIGHEST)
    acum_t = acum.T
    dt_t = dt.T
    ea = jnp.exp(acum)
    last = acum[q - 1:q, :]
    decay = jnp.exp(last - acum) * dt
    ea_last = jnp.exp(last)

    lane_hi = lax.broadcasted_iota(jnp.int32, (q, 128), 1) >= SSM_HEAD_DIM
    row_hi = lax.broadcasted_iota(jnp.int32, (128, SSM_STATE), 0) >= SSM_HEAD_DIM

    def pair_cols(v, e):
        c0, c1 = SMALL_DT + e, SMALL_DT + e + 1
        return jnp.where(lane_hi, v[:, c1:c1 + 1], v[:, c0:c0 + 1])

    for g in range(SSM_GROUPS):
        bg = bm[:, g * SSM_STATE:(g + 1) * SSM_STATE]
        cg = cm[:, g * SSM_STATE:(g + 1) * SSM_STATE]
        cb = lax.dot_general(cg, bg, NT, preferred_element_type=jnp.float32)
        for k in range(g * 4, g * 4 + 4):
            e = 2 * k
            x_pair = xs[:, k * 128:(k + 1) * 128]
            halves = []
            for h in (e, e + 1):
                c = SMALL_DT + h
                seg = acum[:, c:c + 1] - acum_t[c:c + 1, :]
                w = cb * jnp.exp(jnp.where(causal, seg, -jnp.inf)) * dt_t[c:c + 1, :]
                halves.append(jnp.dot(w.astype(bf), x_pair.astype(bf), preferred_element_type=jnp.float32))
            y_pair = jnp.where(lane_hi, halves[1], halves[0])
            state = st_ref[k]
            y_pair = y_pair + lax.dot_general(cg, state.astype(bf), NT, preferred_element_type=jnp.float32) * pair_cols(ea, e)
            y_ref[:, k * 128:(k + 1) * 128] = y_pair
            xd_t = (x_pair * pair_cols(decay, e)).T.astype(bf)
            c0 = SMALL_DT + e
            keep = jnp.where(row_hi, ea_last[:, c0 + 1:c0 + 2], ea_last[:, c0:c0 + 1])
            st_ref[k] = state * keep + jnp.dot(xd_t, bg, preferred_element_type=jnp.float32)

    y = (y_ref[...] + dsk_ref[...] * xs) * (z_ref[...] * jax.nn.sigmoid(z_ref[...]))
    half = SSM_INNER // SSM_GROUPS
    for g in range(SSM_GROUPS):
        yg = y[:, g * half:(g + 1) * half]
        yg = yg * lax.rsqrt(jnp.mean(yg * yg, axis=-1, keepdims=True) + EPS)
        y_ref[:, g * half:(g + 1) * half] = yg * nw_ref[:, g * half:(g + 1) * half]


def _mamba2_ssd(proj, bsz, seq, conv_w, conv_b, dt_bias, a_log, d_skip, norm_w):
    q = SSD_CHUNK
    nc = seq // q
    lane_row = lambda v: jnp.zeros((1, 128), jnp.float32).at[0, SMALL_DT:SMALL_DT + SSM_HEADS].set(v)
    const = lambda shape: pl.BlockSpec(shape, lambda b, c: (0,) * len(shape))
    return pl.pallas_call(
        _ssd_kernel,
        grid=(bsz, nc),
        in_specs=[pl.BlockSpec((q, CONV_CH), lambda b, c: (b * nc + c, COL_XBC // CONV_CH)),
                  pl.BlockSpec((q, SSM_INNER), lambda b, c: (b * nc + c, COL_Z // SSM_INNER)),
                  pl.BlockSpec((q, 128), lambda b, c: (b * nc + c, COL_SMALL // 128)),
                  const((CONV_WIDTH, CONV_CH)), const((1, CONV_CH)), const((1, 128)), const((1, 128)),
                  const((1, SSM_INNER)), const((1, SSM_INNER))],
        out_specs=pl.BlockSpec((q, SSM_INNER), lambda b, c: (b * nc + c, 0)),
        out_shape=jax.ShapeDtypeStruct((bsz * seq, SSM_INNER), jnp.float32),
        scratch_shapes=[pltpu.VMEM((q, CONV_CH), jnp.float32),
                        pltpu.VMEM((SSM_HEADS // 2, 2 * SSM_HEAD_DIM, SSM_STATE), jnp.float32)],
        compiler_params=pltpu.CompilerParams(dimension_semantics=("parallel", "arbitrary"),
                                             vmem_limit_bytes=VMEM_LIMIT),
        name="mamba2_ssd",
    )(proj, proj, proj, conv_w, conv_b.reshape(1, CONV_CH), lane_row(dt_bias), lane_row(-jnp.exp(a_log)),
      jnp.repeat(d_skip, SSM_HEAD_DIM).reshape(1, SSM_INNER), norm_w.reshape(1, SSM_INNER))


def _mix_out_kernel(a_ref, s_ref, gl_ref, x_ref, gm_ref, wo_ref, ws_ref, wout_ref,
                    nf_ref, scf_ref, shf_ref, wr_ref, br_ref, xo_ref, h_ref, lg_ref):
    bf = jnp.bfloat16
    ya = jnp.dot(a_ref[...].astype(bf), wo_ref[...], preferred_element_type=jnp.float32)
    ys = jnp.dot(s_ref[...].astype(bf), ws_ref[...], preferred_element_type=jnp.float32)
    mixed = jax.nn.sigmoid(gl_ref[:, :D_MODEL]) * ya + jax.nn.sigmoid(gl_ref[:, D_MODEL:]) * ys
    x = x_ref[...] + gm_ref[0] * jnp.dot(mixed.astype(bf), wout_ref[...], preferred_element_type=jnp.float32)
    xo_ref[...] = x
    y = x * lax.rsqrt(jnp.mean(x * x, axis=-1, keepdims=True) + EPS) * nf_ref[...]
    h = y * (1.0 + scf_ref[0]) + shf_ref[0]
    h_hi = h.astype(bf)
    hb = pltpu.bitcast(h_hi.astype(jnp.float32), jnp.int32)
    half = D_MODEL // 2
    h_ref[...] = (hb[:, :half] & jnp.int32(-65536)) | lax.shift_right_logical(hb[:, half:], 16)
    h_lo = (h - h_hi.astype(jnp.float32)).astype(bf)
    dot = functools.partial(jnp.dot, preferred_element_type=jnp.float32)
    lg_ref[...] = dot(h_hi, wr_ref[0]) + (dot(h_lo, wr_ref[0]) + dot(h_hi, wr_ref[1])) + br_ref[...]


def _mix_out(attn2, ssd2, proj, x2, g_m, w_attn_o, w_ssm_o, w_out, norm_ffn, sc_f, sh_f, w_router, b_router, seq, tm=512):
    t, d = x2.shape
    per_b = seq // tm
    bf = jnp.bfloat16
    const = lambda shape: pl.BlockSpec(shape, lambda i: (0,) * len(shape))
    perb = pl.BlockSpec((1, 1, d), lambda i: (i // per_b, 0, 0))
    wr = jnp.pad(w_router.astype(jnp.float32), ((0, 0), (0, 128 - N_EXPERTS)))
    wr_hi = wr.astype(bf)
    wr = jnp.stack([wr_hi, (wr - wr_hi.astype(jnp.float32)).astype(bf)])
    br =jnp.pad(b_router, (0, 128 - N_EXPERTS)).reshape(1, 128)
    return pl.pallas_call(
        _mix_out_kernel,
        grid=(t // tm,),
        in_specs=[pl.BlockSpec((tm, ATTN_WIDTH), lambda i: (i, 0)),
                  pl.BlockSpec((tm, SSM_INNER), lambda i: (i, 0)),
                  pl.BlockSpec((tm, 2 * d), lambda i: (i, COL_GATE // (2 * d))),
                  pl.BlockSpec((tm, d), lambda i: (i, 0)),
                  perb,
                  const((ATTN_WIDTH, d)), const((SSM_INNER, d)), const((d, d)),
                  const((1, d)), perb, perb, const((2, d, 128)), const((1, 128))],
        out_specs=[pl.BlockSpec((tm, d), lambda i: (i, 0)),
                   pl.BlockSpec((tm, d // 2), lambda i: (i, 0)),
                   pl.BlockSpec((tm, 128), lambda i: (i, 0))],
        out_shape=[jax.ShapeDtypeStruct((t, d), jnp.float32),
                   jax.ShapeDtypeStruct((t, d // 2), jnp.int32),
                   jax.ShapeDtypeStruct((t, 128), jnp.float32)],
        compiler_params=pltpu.CompilerParams(dimension_semantics=("parallel",), vmem_limit_bytes=VMEM_LIMIT),
        name="mix_out",
    )(attn2, ssd2, proj, x2, g_m[:, None, :], w_attn_o.astype(bf), w_ssm_o.astype(bf), w_out.astype(bf),
      norm_ffn.reshape(1, d), sc_f[:, None, :], sh_f[:, None, :], wr, br)


def _moe_kernel(be_ref, nb_ref, x_ref, g_ref, wgu_ref, bgu_ref, wdn_ref, bdn_ref, o_ref, wgu_bf, wdn_bf):
    i = pl.program_id(0)

    @pl.when((i == 0) | (be_ref[i] != be_ref[jnp.maximum(i - 1, 0)]))
    def _():
        wgu_bf[...] = wgu_ref[0, 0].astype(jnp.bfloat16)
        wdn_bf[...] = wdn_ref[0, 0].astype(jnp.bfloat16)

    @pl.when(i < nb_ref[0])
    def _():
        words = x_ref[...]
        x_hi = pltpu.bitcast(words & jnp.int32(-65536), jnp.float32).astype(jnp.bfloat16)
        x_lo = pltpu.bitcast(words << 16, jnp.float32).astype(jnp.bfloat16)
        x = jnp.concatenate([x_hi, x_lo], axis=1)
        gu = jnp.dot(x, wgu_bf[...], preferred_element_type=jnp.float32) + bgu_ref[0, 0]
        g = jnp.minimum(gu[:, :D_EXPERT], SWIGLU_LIMIT)
        u = jnp.clip(gu[:, D_EXPERT:], -SWIGLU_LIMIT, SWIGLU_LIMIT)
        act = (u + 1.0) * (g * jax.nn.sigmoid(SWIGLU_ALPHA * g))
        out = jnp.dot(act.astype(jnp.bfloat16), wdn_bf[...], preferred_element_type=jnp.float32) + bdn_ref[0, 0]
        o_ref[...] = (out * g_ref[...]).astype(o_ref.dtype)

    @pl.when(i >= nb_ref[0])
    def _():
        o_ref[...] = jnp.zeros_like(o_ref)


def _moe_ffn(xs, row_gate, blk_exp, n_used, w_gu, b_gu, w_dn, b_dn, layer):
    n_rows = xs.shape[0]
    d = D_MODEL
    tm = MOE_TM
    grid_spec = pltpu.PrefetchScalarGridSpec(
        num_scalar_prefetch=2,
        grid=(n_rows // tm,),
        in_specs=[pl.BlockSpec((tm, d // 2), lambda i, be, nb: (i, 0)),
                  pl.BlockSpec((tm, 1), lambda i, be, nb: (i, 0)),
                  pl.BlockSpec((1, 1, d, 2 * D_EXPERT), lambda i, be, nb: (layer, be[i], 0, 0)),
                  pl.BlockSpec((1, 1, 1, 2 * D_EXPERT), lambda i, be, nb: (layer, be[i], 0, 0)),
                  pl.BlockSpec((1, 1, D_EXPERT, d), lambda i, be, nb: (layer, be[i], 0, 0)),
                  pl.BlockSpec((1, 1, 1, d), lambda i, be, nb: (layer, be[i], 0, 0))],
        out_specs=pl.BlockSpec((tm, d), lambda i, be, nb: (i, 0)),
        scratch_shapes=[pltpu.VMEM((d, 2 * D_EXPERT), jnp.bfloat16), pltpu.VMEM((D_EXPERT, d), jnp.bfloat16)],
    )
    return pl.pallas_call(
        _moe_kernel,
        grid_spec=grid_spec,
        out_shape=jax.ShapeDtypeStruct((n_rows, d), jnp.bfloat16),
        compiler_params=pltpu.CompilerParams(dimension_semantics=("arbitrary",), vmem_limit_bytes=VMEM_LIMIT),
        name="moe_ffn",
    )(blk_exp, n_used, xs, row_gate[:, None], w_gu, b_gu[:, :, None, :], w_dn, b_dn[:, :, None, :])


def _moe(h2, logits, w_gu, b_gu, w_dn, b_dn, layer):
    t = h2.shape[0]
    d = D_MODEL
    tm = MOE_TM
    i32 = jnp.int32
    top_val, top_idx = lax.top_k(logits, TOP_K)
    gates = jax.nn.softmax(top_val, axis=-1).reshape(-1)
    n_assign = t * TOP_K
    n_rows = n_assign + N_EXPERTS * tm
    e_flat = top_idx.reshape(n_assign).astype(i32)
    counts = jnp.sum((e_flat[:, None] == jnp.arange(N_EXPERTS, dtype=i32)[None, :]).astype(i32), axis=0)
    padded = (counts + tm - 1) // tm * tm
    filler_exp = jnp.repeat(jnp.arange(N_EXPERTS, dtype=i32), tm)
    filler_key = jnp.where(jnp.tile(jnp.arange(tm, dtype=i32), N_EXPERTS) < jnp.repeat(padded - counts, tm),
                           filler_exp, N_EXPERTS)
    keys = jnp.concatenate([e_flat, filler_key])
    gate_in = jnp.concatenate([gates, jnp.zeros((N_EXPERTS * tm,), jnp.float32)])
    rows = jnp.arange(n_rows, dtype=i32)
    row_key, row_src, row_gate = lax.sort((keys, rows, gate_in), num_keys=1)
    row_tok = jnp.where(row_src < n_assign, row_src // TOP_K, rows % t)
    _, row_of = lax.sort((row_src, rows), num_keys=1)
    dest = row_of[:n_assign].reshape(t, TOP_K).T.reshape(-1)
    blk_exp = jnp.minimum(row_key[::tm], N_EXPERTS - 1)
    n_used = (jnp.sum(padded, keepdims=True) // tm).astype(i32)
    out = _moe_ffn(h2[row_tok], row_gate, blk_exp, n_used, w_gu, b_gu, w_dn, b_dn, layer)
    parts = out[dest].reshape(TOP_K, t, d).astype(jnp.float32)
    return (parts[0] + parts[1]) + (parts[2] + parts[3])


def kernel(x, c, rel_bias, w_ada, b_ada, norm_mix, norm_ffn, w_in, kv_norm, w_kv_up, q_norm, k_norm,
           idx_k_ln_w, idx_k_ln_b, w_attn_o, conv_w, conv_b, dt_bias, a_log, d_skip, ssm_norm, w_ssm_o,
           w_out, w_router, b_router, w_gu, b_gu, w_dn, b_dn):
    bsz, seq, d = x.shape
    t = bsz * seq
    cond = jax.nn.silu(c)
    x2 = x.reshape(t, d)
    for l in range(DEPTH):
        mod = cond @ w_ada[l] + b_ada[l]
        sh_m, sc_m, g_m, sh_f, sc_f, g_f = jnp.split(mod, 6, axis=-1)
        proj = _in_proj(x2, norm_mix[l], sc_m, sh_m, _pack_w_in(w_in[l]), seq)
        qT, k, vT, qiT, ki2, wT, kn2 = _prep(proj, bsz, seq, q_norm[l], kv_norm[l], w_kv_up[l], k_norm[l],
                                             idx_k_ln_w[l], idx_k_ln_b[l])
        attn = _dsa_attention(qT, qiT, wT, k, vT, ki2, kn2, rel_bias)
        y_ssd = _mamba2_ssd(proj, bsz, seq, conv_w[l], conv_b[l], dt_bias[l], a_log[l], d_skip[l], ssm_norm[l])
        x2, h2, logits = _mix_out(attn.reshape(t, ATTN_WIDTH), y_ssd, proj, x2, g_m,
                                  w_attn_o[l], w_ssm_o[l], w_out[l], norm_ffn[l], sc_f, sh_f,
                                  w_router[l], b_router[l], seq)
        y = _moe(h2, logits[:, :N_EXPERTS], w_gu, b_gu, w_dn, b_dn, l)
        x2 = x2 + jnp.repeat(g_f, seq, axis=0) * y
    return x2.reshape(bsz, seq, d)
```

```python
import functools
import math

import jax
import jax.numpy as jnp
import numpy as np
from jax import lax
from jax.experimental import pallas as pl
from jax.experimental.pallas import tpu as pltpu

D_MODEL = 1024
DEPTH = 2
ATTN_HEADS = 8
ATTN_HEAD_DIM = 64
ATTN_WIDTH = ATTN_HEADS * ATTN_HEAD_DIM
KV_RANK = 256
IDX_HEADS = 8
IDX_DIM = 64
TOPK_MAX = 256
N_BUCKETS = 32
MAX_DISTANCE = 128
SSM_HEADS = 16
SSM_HEAD_DIM = 64
SSM_INNER = SSM_HEADS * SSM_HEAD_DIM
SSM_GROUPS = 2
SSM_STATE = 128
CONV_WIDTH = 4
CONV_CH = SSM_INNER + 2 * SSM_GROUPS * SSM_STATE
SSD_CHUNK = 128
N_EXPERTS = 32
TOP_K = 4
D_EXPERT = D_MODEL
SWIGLU_LIMIT = 7.0
SWIGLU_ALPHA = 1.702
EPS = 1e-6

COL_Q = 0
COL_KV = 512
COL_QI = 768
COL_SMALL = 1280
COL_XBC = 1536
COL_Z = 3072
COL_GATE = 4096
PROJ_COLS = 6144
PREP_COLS = 1408
SMALL_KI, SMALL_WI, SMALL_DT = 0, 64, 72

QB = 256
VROWS = 80
INT_MIN = -2 ** 31
KEY_NEG_INF = (0xFF800000 ^ 0x7FFFFFFF) - 2 ** 32
NEG = -1e30
TINY = 2.0 ** -126
LOG2E = math.log2(math.e)
NORM_SLACK = 1.02
MAX_SHIFT_ERROR = 96.0
VMEM_LIMIT = 56 * 1024 * 1024
MOE_TM = 512
HIGHEST = lax.Precision.HIGHEST
NT = (((1,), (1,)), ((), ()))


def _pack_w_in(w):
    o = np.cumsum((0, ATTN_WIDTH, KV_RANK, IDX_HEADS * IDX_DIM, IDX_DIM, IDX_HEADS, SSM_INNER, CONV_CH, SSM_HEADS, 2 * D_MODEL))
    q, kv, qi, ki, wi, z, xbc, dt, gate = (w[:, int(o[n]):int(o[n + 1])] for n in range(9))
    zeros = lambda n: jnp.zeros((w.shape[0], n), w.dtype)
    small = jnp.concatenate([ki, wi, dt, zeros(128 - 88)], axis=1)
    packed = jnp.concatenate([q, kv, qi, small, zeros(COL_XBC - PREP_COLS), xbc, z, gate], axis=1)
    assert packed.shape[1] == PROJ_COLS
    return packed.astype(jnp.bfloat16)


def _in_proj_kernel(x_ref, g_ref, sc_ref, sh_ref, w_ref, o_ref, h_ref):
    @pl.when(pl.program_id(1) == 0)
    def _():
        x = x_ref[...]
        y = x * lax.rsqrt(jnp.mean(x * x, axis=-1, keepdims=True) + EPS) * g_ref[...]
        h_ref[...] = (y * (1.0 + sc_ref[0]) + sh_ref[0]).astype(jnp.bfloat16)
    o_ref[...] = jnp.dot(h_ref[...], w_ref[...], preferred_element_type=jnp.float32)


def _in_proj(x2, gain, sc, sh, w_packed, seq, tm=1024, tn=1024):
    t, d = x2.shape
    per_b = seq // tm
    return pl.pallas_call(
        _in_proj_kernel,
        grid=(t // tm, PROJ_COLS // tn),
        in_specs=[pl.BlockSpec((tm, d), lambda i, j: (i, 0)),
                  pl.BlockSpec((1, d), lambda i, j: (0, 0)),
                  pl.BlockSpec((1, 1, d), lambda i, j: (i // per_b, 0, 0)),
                  pl.BlockSpec((1, 1, d), lambda i, j: (i // per_b, 0, 0)),
                  pl.BlockSpec((d, tn), lambda i, j: (0, j))],
        out_specs=pl.BlockSpec((tm, tn), lambda i, j: (i, j)),
        out_shape=jax.ShapeDtypeStruct((t, PROJ_COLS), jnp.float32),
        scratch_shapes=[pltpu.VMEM((tm, d), jnp.bfloat16)],
        compiler_params=pltpu.CompilerParams(dimension_semantics=("parallel", "arbitrary"),
                                             vmem_limit_bytes=VMEM_LIMIT),
        name="in_proj",
    )(x2, gain.reshape(1, d), sc[:, None, :], sh[:, None, :], w_packed)


def _head_rms_t(xt):
    x3 = xt.reshape(ATTN_HEADS, ATTN_HEAD_DIM, xt.shape[1])
    return lax.rsqrt(jnp.mean(x3 * x3, axis=1, keepdims=True) + EPS)


def _prep_kernel(p_ref, qg_ref, kvg_ref, wkv_ref, kg_ref, lng_ref, lnb_ref,
                 qT_ref, k_ref, vT_ref, qiT_ref, ki_ref, wT_ref, kn2_ref):
    n = p_ref.shape[0]
    q = p_ref[:, COL_Q:COL_Q + ATTN_WIDTH]
    lat = p_ref[:, COL_KV:COL_KV + KV_RANK]
    qi = p_ref[:, COL_QI:COL_QI + IDX_HEADS * IDX_DIM]
    sm = p_ref[:, COL_SMALL:COL_SMALL + 128]

    scale = ATTN_HEAD_DIM ** -0.5 * LOG2E
    qt = q.T
    qn = qt.reshape(ATTN_HEADS, ATTN_HEAD_DIM, n) * _head_rms_t(qt)
    qT_ref[0] = (qn.reshape(ATTN_WIDTH, n) * qg_ref[...] * scale).astype(jnp.bfloat16)

    latn = lat * lax.rsqrt(jnp.mean(lat * lat, axis=-1, keepdims=True) + EPS) * kvg_ref[...]
    kv = jnp.dot(latn.astype(jnp.bfloat16), wkv_ref[...], preferred_element_type=jnp.float32)
    kt = kv[:, :ATTN_WIDTH].T
    kn = (kt.reshape(ATTN_HEADS, ATTN_HEAD_DIM, n) * _head_rms_t(kt)).reshape(ATTN_WIDTH, n) * kg_ref[...]
    k_ref[0] = kn.T.astype(jnp.bfloat16)
    kn3 = kn.reshape(ATTN_HEADS, ATTN_HEAD_DIM, n)
    kn2_ref[0] = jnp.sum(kn3 * kn3, axis=1)
    vt = kv[:, ATTN_WIDTH:].T.reshape(ATTN_HEADS, ATTN_HEAD_DIM, n)
    ones = jnp.ones((ATTN_HEADS, VROWS - ATTN_HEAD_DIM, n), jnp.float32)
    vT_ref[0] = jnp.concatenate([vt, ones], axis=1).reshape(ATTN_HEADS * VROWS, n).astype(jnp.bfloat16)

    qiT_ref[0] = (qi * (IDX_DIM ** -0.5)).T.astype(jnp.bfloat16)

    lane = lax.broadcasted_iota(jnp.int32, sm.shape, 1)
    kid = jnp.where(lane < IDX_DIM, sm, pltpu.roll(sm, IDX_DIM, 1))
    mu = jnp.mean(kid, axis=-1, keepdims=True)
    var = jnp.mean(jnp.square(kid - mu), axis=-1, keepdims=True)
    ki_ref[0] = ((kid - mu) * lax.rsqrt(var + EPS) * lng_ref[...] + lnb_ref[...]).astype(jnp.bfloat16)

    wT_ref[0] = sm.T[SMALL_WI:SMALL_WI + IDX_HEADS, :] * (IDX_HEADS ** -0.5)


def _prep(proj, bsz, seq, q_norm, kv_norm, w_kv_up, k_norm, ln_w, ln_b, tp=512):
    nb = seq // tp
    tile8 = lambda g: jnp.tile(g, ATTN_HEADS).reshape(ATTN_WIDTH, 1)
    const = lambda shape: pl.BlockSpec(shape, lambda b, i: (0,) * len(shape))
    bf = jnp.bfloat16
    return pl.pallas_call(
        _prep_kernel,
        grid=(bsz, nb),
        in_specs=[pl.BlockSpec((tp, PREP_COLS), lambda b, i: (b * nb + i, 0)),
                  const((ATTN_WIDTH, 1)), const((1, KV_RANK)), const((KV_RANK, 2 * ATTN_WIDTH)),
                  const((ATTN_WIDTH, 1)), const((1, 128)), const((1, 128))],
        out_specs=[pl.BlockSpec((1, ATTN_WIDTH, tp), lambda b, i: (b, 0, i)),
                   pl.BlockSpec((1, tp, ATTN_WIDTH), lambda b, i: (b, i, 0)),
                   pl.BlockSpec((1, ATTN_HEADS * VROWS, tp), lambda b, i: (b, 0, i)),
                   pl.BlockSpec((1, ATTN_WIDTH, tp), lambda b, i: (b, 0, i)),
                   pl.BlockSpec((1, tp, 128), lambda b, i: (b, i, 0)),
                   pl.BlockSpec((1, IDX_HEADS, tp), lambda b, i: (b, 0, i)),
                   pl.BlockSpec((1, ATTN_HEADS, tp), lambda b, i: (b, 0, i))],
        out_shape=[jax.ShapeDtypeStruct((bsz, ATTN_WIDTH, seq), bf),
                   jax.ShapeDtypeStruct((bsz, seq, ATTN_WIDTH), bf),
                   jax.ShapeDtypeStruct((bsz, ATTN_HEADS * VROWS, seq), bf),
                   jax.ShapeDtypeStruct((bsz, ATTN_WIDTH, seq), bf),
                   jax.ShapeDtypeStruct((bsz, seq, 128), bf),
                   jax.ShapeDtypeStruct((bsz, IDX_HEADS, seq), jnp.float32),
                   jax.ShapeDtypeStruct((bsz, ATTN_HEADS, seq), jnp.float32)],
        compiler_params=pltpu.CompilerParams(dimension_semantics=("parallel", "parallel"),
                                             vmem_limit_bytes=VMEM_LIMIT),
        name="attn_prep",
    )(proj, tile8(q_norm), kv_norm.reshape(1, KV_RANK), w_kv_up.astype(bf), tile8(k_norm),
      jnp.tile(ln_w, 2).reshape(1, 128), jnp.tile(ln_b, 2).reshape(1, 128))


def _t5_bucket(dist):
    n = jnp.maximum(dist, 0)
    max_exact = N_BUCKETS // 2
    nf = jnp.maximum(n, 1).astype(jnp.float32)
    large = max_exact + (jnp.log(nf / max_exact) / math.log(MAX_DISTANCE / max_exact) * (N_BUCKETS - max_exact)).astype(jnp.int32)
    large = jnp.minimum(large, N_BUCKETS - 1)
    return jnp.where(n < max_exact, n, large)


def _bias_tables(rel_bias):
    s = jnp.arange(QB, dtype=jnp.int32)[None, :, None]
    q = jnp.arange(QB, dtype=jnp.int32)[None, None, :]
    dist = q - s + jnp.array([2 * QB, QB, 0], jnp.int32)[:, None, None]
    onehot = (_t5_bucket(dist)[..., None] == jnp.arange(N_BUCKETS, dtype=jnp.int32)).astype(jnp.float32)
    b = jnp.einsum('tsqb,bh->thsq', onehot, rel_bias.astype(jnp.float32) * LOG2E, precision=HIGHEST)
    return jnp.where((dist >= 0)[:, None], b, NEG)


def _attn_kernel(qT_ref, qiT_ref, wT_ref, k_ref, vT_ref, ki_ref, kn_ref, tab_ref, bst_ref, o_ref,
                 keys_ref, hi_ref, lo_ref, msk_ref, p_ref, acc_ref, mp_ref, m_ref, *, topk):
    i = pl.program_id(1)
    n_tiles = i + 1
    row_hi = lax.broadcasted_iota(jnp.int32, (128, QB), 0) >= 64

    def head_rows(ref, h):
        pair = ref[0, (h // 2) * 128:(h // 2) * 128 + 128, :]
        return jnp.where(row_hi == bool(h % 2), pair, jnp.zeros_like(pair))

    def tile_rows(kt):
        return pl.ds(pl.multiple_of(kt * QB, QB), QB)

    def score_tile(kt, carry):
        ki = ki_ref[0, tile_rows(kt), :]
        sc = jnp.zeros((QB, QB), jnp.float32)
        for h in range(IDX_HEADS):
            d = jnp.dot(ki, head_rows(qiT_ref, h), preferred_element_type=jnp.float32)
            sc = sc + wT_ref[0, h:h + 1, :] * jnp.maximum(d, 0.0)
        srow = lax.broadcasted_iota(jnp.int32, (QB, QB), 0)
        qcol = lax.broadcasted_iota(jnp.int32, (QB, QB), 1)
        sc = jnp.where(jnp.abs(sc) < TINY, 0.0, sc)
        sc = jnp.where((kt == i) & (srow > qcol), -jnp.inf, sc)
        bits = pltpu.bitcast(sc, jnp.int32)
        keys_ref[tile_rows(kt), :] = bits ^ ((bits >> 31) & 0x7FFFFFFF)
        hi_ref[tile_rows(kt), :] = pltpu.bitcast(bits & jnp.int32(-65536), jnp.float32).astype(jnp.bfloat16)
        return carry
    lax.fori_loop(0, n_tiles, score_tile, 0)

    def count_packed_ge(ref, cb):
        one, zero = jnp.ones((), jnp.bfloat16), jnp.zeros((), jnp.bfloat16)

        def body(kt, acc):
            hit = jnp.where(ref[tile_rows(kt), :] >= cb, one, zero)
            parts = [hit[r:r + 16, :] for r in range(0, QB, 16)]
            while len(parts) > 1:
                parts = [a + b for a, b in zip(parts[::2], parts[1::2])]
            return acc + parts[0]
        acc = lax.fori_loop(0, n_tiles, body, jnp.zeros((16, QB), jnp.bfloat16))
        return jnp.sum(acc.astype(jnp.float32), axis=0, keepdims=True)

    def count_hi_ge(cand16):
        b = cand16 ^ ((cand16 >> 15) & 0x7FFF)
        snap = jnp.where(((b & 0x8000) != 0) | ((b & 0x7F) == 0), 0, 0x0080)
        b = jnp.where((b & 0x7F80) == 0, snap, b)
        return count_packed_ge(hi_ref, pltpu.bitcast(b << 16, jnp.float32).astype(jnp.bfloat16))

    def mid_code(v):
        pat = jnp.where(v >= 16384, v - 16256, 0x8000 | (16511 - v))
        return pltpu.bitcast(pat << 16, jnp.float32)

    def count(hit_of_tile):
        def body(kt, acc):
            return acc + jnp.sum(hit_of_tile(kt).reshape(QB // 8, 8, QB), axis=0)
        acc = lax.fori_loop(0, n_tiles, body, jnp.zeros((8, QB), jnp.int32))
        return jnp.sum(acc, axis=0, keepdims=True)

    def count_ge(cand):
        return count(lambda kt: jnp.where(keys_ref[tile_rows(kt), :] >= cand, 1, 0))

    def hi_step(it, r):
        cand = jnp.where(it == 0, jnp.zeros_like(r), r | (1 << (15 - it)))
        return jnp.where(count_hi_ge(cand) >= topk, cand, r)
    r16 = lax.fori_loop(0, 16, hi_step, jnp.full((1, QB), -32768, jnp.int32))

    above = count_hi_ge(r16 + 1)

    def code_tile(kt, carry):
        key = keys_ref[tile_rows(kt), :]
        code = jnp.where((key >> 16) == r16, mid_code((key >> 1) & 0x7FFF), -jnp.inf)
        lo_ref[tile_rows(kt), :] = code.astype(jnp.bfloat16)
        return carry
    lax.fori_loop(0, n_tiles, code_tile, 0)

    def mid_step(it, v):
        cand = v | (1 << (14 - it))
        cnt = above + count_packed_ge(lo_ref, mid_code(cand).astype(jnp.bfloat16))
        return jnp.where(cnt >= topk, cand, v)
    v15 = lax.fori_loop(0, 15, mid_step, jnp.zeros((1, QB), jnp.int32))
    thr = (r16 << 16) | (v15 << 1)
    thr = jnp.where(count_ge(thr | 1) >= topk, thr | 1, thr)

    cnt_gt = count_ge(thr + 1)
    cnt_ge = count_ge(thr)
    need = topk - cnt_gt
    tie = (cnt_ge - cnt_gt > need) & (thr > KEY_NEG_INF)

    @pl.when(jnp.max(tie.astype(jnp.int32)) > 0)
    def _():
        def count_eq_below(cand):
            def ind(kt):
                idx = lax.broadcasted_iota(jnp.int32, (QB, QB), 0) + kt * QB
                return jnp.where((keys_ref[tile_rows(kt), :] == thr) & (idx < cand), 1, 0)
            return count(ind)

        def idx_step(it, r):
            cand = r | (1 << (15 - it))
            return jnp.where(count_eq_below(cand) < need, cand, r)
        last = lax.fori_loop(0, 16, idx_step, jnp.zeros((1, QB), jnp.int32))

        def drop(kt, carry):
            blk = keys_ref[tile_rows(kt), :]
            idx = lax.broadcasted_iota(jnp.int32, (QB, QB), 0) + kt * QB
            keys_ref[tile_rows(kt), :] = jnp.where(tie & (blk == thr) & (idx > last), INT_MIN, blk)
            return carry
        lax.fori_loop(0, n_tiles, drop, 0)

    def logits(kt, h):
        band = jnp.clip(kt - (i - 2), 0, 2)
        kp = k_ref[0, tile_rows(kt), (h // 2) * 128:(h // 2) * 128 + 128]
        s = jnp.dot(kp, head_rows(qT_ref, h), preferred_element_type=jnp.float32)
        return s + msk_ref[...] + tab_ref[band, h]

    def set_mask(kt):
        msk_ref[...] = jnp.where(keys_ref[tile_rows(kt), :] >= thr, 0.0, NEG)

    def max_tile(kt, carry):
        set_mask(kt)
        for h in range(ATTN_HEADS):
            s = logits(kt, h)
            mp_ref[h] = jnp.maximum(mp_ref[h], jnp.max(s.reshape(QB // 8, 8, QB), axis=0))
        return carry

    seq = kn_ref.shape[2]
    in_extent = lax.broadcasted_iota(jnp.int32, (ATTN_HEADS, seq), 1) < n_tiles * QB
    k_max = jnp.max(jnp.where(in_extent, kn_ref[0], 0.0), axis=1, keepdims=True)
    spread = jnp.zeros((1, QB), jnp.float32)
    for h in range(ATTN_HEADS):
        qh = qT_ref[0, h * ATTN_HEAD_DIM:(h + 1) * ATTN_HEAD_DIM, :].astype(jnp.float32)
        reach = jnp.sqrt(jnp.sum(qh * qh, axis=0, keepdims=True) * k_max[h:h + 1, :]) * NORM_SLACK
        m_ref[h:h + 1, :] = reach + bst_ref[0, h:h + 1, :]
        spread = jnp.maximum(spread, 2.0 * reach + bst_ref[1, h:h + 1, :])
    bound_ok = jnp.max(spread) <= MAX_SHIFT_ERROR

    @pl.when(jnp.logical_not(bound_ok))
    def _():
        mp_ref[...] = jnp.full(mp_ref.shape, NEG, jnp.float32)
        lax.fori_loop(0, n_tiles, max_tile, 0)
        for h in range(ATTN_HEADS):
            m_ref[h:h + 1, :] = jnp.max(mp_ref[h], axis=0, keepdims=True)
    m = [m_ref[h:h + 1, :] for h in range(ATTN_HEADS)]

    def exp_tile(kt, carry):
        set_mask(kt)
        for h in range(ATTN_HEADS):
            p_ref[h] = jnp.exp2(logits(kt, h) - m[h]).astype(jnp.bfloat16)
        for h in range(ATTN_HEADS):
            va = vT_ref[0, h * VROWS:(h + 1) * VROWS, tile_rows(kt)]
            acc_ref[h * VROWS:(h + 1) * VROWS, :] += jnp.dot(va, p_ref[h], preferred_element_type=jnp.float32)
        return carry

    acc_ref[...] = jnp.zeros(acc_ref.shape, jnp.float32)
    lax.fori_loop(0, n_tiles, exp_tile, 0)

    outs = [acc_ref[h * VROWS:h * VROWS + ATTN_HEAD_DIM, :] / acc_ref[h * VROWS + ATTN_HEAD_DIM:h * VROWS + ATTN_HEAD_DIM + 1, :]
            for h in range(ATTN_HEADS)]
    o_ref[0] = jnp.concatenate(outs, axis=0).T


def _dsa_attention(qT, qiT, wT, k, vT, ki2, kn2, rel_bias):
    bsz, _, seq = qT.shape
    topk = min(TOPK_MAX, seq // 4)
    assert seq % QB == 0 and topk <= QB
    assert seq // 16 <= 256
    b2 = rel_bias.astype(jnp.float32) * LOG2E
    bias_stats = jnp.stack([jnp.max(b2, axis=0), jnp.max(b2, axis=0) - jnp.min(b2, axis=0)])
    bias_stats = jnp.broadcast_to(bias_stats[:, :, None], (2, ATTN_HEADS, QB))
    return pl.pallas_call(
        functools.partial(_attn_kernel, topk=topk),
        grid=(bsz, seq // QB),
        in_specs=[
            pl.BlockSpec((1, ATTN_WIDTH, QB), lambda b, i: (b, 0, i)),
            pl.BlockSpec((1, IDX_HEADS * IDX_DIM, QB), lambda b, i: (b, 0, i)),
            pl.BlockSpec((1, IDX_HEADS, QB), lambda b, i: (b, 0, i)),
            pl.BlockSpec((1, seq, ATTN_WIDTH), lambda b, i: (b, 0, 0)),
            pl.BlockSpec((1, ATTN_HEADS * VROWS, seq), lambda b, i: (b, 0, 0)),
            pl.BlockSpec((1, seq, 128), lambda b, i: (b, 0, 0)),
            pl.BlockSpec((1, ATTN_HEADS, seq), lambda b, i: (b, 0, 0)),
            pl.BlockSpec((3, ATTN_HEADS, QB, QB), lambda b, i: (0, 0, 0, 0)),
            pl.BlockSpec((2, ATTN_HEADS, QB), lambda b, i: (0, 0, 0)),
        ],
        out_specs=pl.BlockSpec((1, QB, ATTN_WIDTH), lambda b, i: (b, i, 0)),
        out_shape=jax.ShapeDtypeStruct((bsz, seq, ATTN_WIDTH), jnp.float32),
        scratch_shapes=[
            pltpu.VMEM((seq, QB), jnp.int32),
            pltpu.VMEM((seq, QB), jnp.bfloat16),
            pltpu.VMEM((seq, QB), jnp.bfloat16),
            pltpu.VMEM((QB, QB), jnp.float32),
            pltpu.VMEM((ATTN_HEADS, QB, QB), jnp.bfloat16),
            pltpu.VMEM((ATTN_HEADS * VROWS, QB), jnp.float32),
            pltpu.VMEM((ATTN_HEADS, 8, QB), jnp.float32),
            pltpu.VMEM((ATTN_HEADS, QB), jnp.float32),
        ],
        compiler_params=pltpu.CompilerParams(dimension_semantics=("parallel", "arbitrary"),
                                             vmem_limit_bytes=VMEM_LIMIT),
        name="dsa_attention",
    )(qT, qiT, wT, k, vT, ki2, kn2, _bias_tables(rel_bias), bias_stats)


def _ssd_kernel(xbc_ref, z_ref, sm_ref, cw_ref, cb_ref, dtb_ref, a_ref, dsk_ref, nw_ref, y_ref, prev_ref, st_ref):
    q = SSD_CHUNK
    bf = jnp.bfloat16

    @pl.when(pl.program_id(1) == 0)
    def _():
        prev_ref[...] = jnp.zeros(prev_ref.shape, jnp.float32)
        st_ref[...] = jnp.zeros(st_ref.shape, jnp.float32)

    cur = xbc_ref[...]
    prev = prev_ref[...]
    row = lax.broadcasted_iota(jnp.int32, cur.shape, 0)
    acc = cur * cw_ref[CONV_WIDTH - 1:CONV_WIDTH, :] + cb_ref[...]
    for s in range(1, CONV_WIDTH):
        shifted = jnp.where(row >= s, pltpu.roll(cur, s, 0), pltpu.roll(prev, s, 0))
        acc = acc + shifted * cw_ref[CONV_WIDTH - 1 - s:CONV_WIDTH - s, :]
    prev_ref[...] = cur
    u = acc * jax.nn.sigmoid(acc)
    xs = u[:, :SSM_INNER]
    bm = u[:, SSM_INNER:SSM_INNER + SSM_GROUPS * SSM_STATE].astype(bf)
    cm = u[:, SSM_INNER + SSM_GROUPS * SSM_STATE:].astype(bf)

    t = sm_ref[...] + dtb_ref[...]
    dt = jnp.maximum(t, 0.0) + jnp.log1p(jnp.exp(-jnp.abs(t)))
    ii = lax.broadcasted_iota(jnp.int32, (q, q), 0)
    jj = lax.broadcasted_iota(jnp.int32, (q, q), 1)
    causal = ii >= jj
    acum = jnp.dot(causal.astype(jnp.float32), dt * a_ref[...], preferred_element_type=jnp.float32, precision=HIGHEST)
    acum_t = acum.T
    dt_t = dt.T
    ea = jnp.exp(acum)
    last = acum[q - 1:q, :]
    decay = jnp.exp(last - acum) * dt
    ea_last = jnp.exp(last)

    lane_hi = lax.broadcasted_iota(jnp.int32, (q, 128), 1) >= SSM_HEAD_DIM
    row_hi = lax.broadcasted_iota(jnp.int32, (128, SSM_STATE), 0) >= SSM_HEAD_DIM

    def pair_cols(v, e):
        c0, c1 = SMALL_DT + e, SMALL_DT + e + 1
        return jnp.where(lane_hi, v[:, c1:c1 + 1], v[:, c0:c0 + 1])

    for g in range(SSM_GROUPS):
        bg = bm[:, g * SSM_STATE:(g + 1) * SSM_STATE]
        cg = cm[:, g * SSM_STATE:(g + 1) * SSM_STATE]
        cb = lax.dot_general(cg, bg, NT, preferred_element_type=jnp.float32)
        for k in range(g * 4, g * 4 + 4):
            e = 2 * k
            x_pair = xs[:, k * 128:(k + 1) * 128]
            halves = []
            for h in (e, e + 1):
                c = SMALL_DT + h
                seg = acum[:, c:c + 1] - acum_t[c:c + 1, :]
                w = cb * jnp.exp(jnp.where(causal, seg, -jnp.inf)) * dt_t[c:c + 1, :]
                halves.append(jnp.dot(w.astype(bf), x_pair.astype(bf), preferred_element_type=jnp.float32))
            y_pair = jnp.where(lane_hi, halves[1], halves[0])
            state = st_ref[k]
            y_pair = y_pair + lax.dot_general(cg, state.astype(bf), NT, preferred_element_type=jnp.float32) * pair_cols(ea, e)
            y_ref[:, k * 128:(k + 1) * 128] = y_pair
            xd_t = (x_pair * pair_cols(decay, e)).T.astype(bf)
            c0 = SMALL_DT + e
            keep = jnp.where(row_hi, ea_last[:, c0 + 1:c0 + 2], ea_last[:, c0:c0 + 1])
            st_ref[k] = state * keep + jnp.dot(xd_t, bg, preferred_element_type=jnp.float32)

    y = (y_ref[...] + dsk_ref[...] * xs) * (z_ref[...] * jax.nn.sigmoid(z_ref[...]))
    half = SSM_INNER // SSM_GROUPS
    for g in range(SSM_GROUPS):
        yg = y[:, g * half:(g + 1) * half]
        yg = yg * lax.rsqrt(jnp.mean(yg * yg, axis=-1, keepdims=True) + EPS)
        y_ref[:, g * half:(g + 1) * half] = yg * nw_ref[:, g * half:(g + 1) * half]


def _mamba2_ssd(proj, bsz, seq, conv_w, conv_b, dt_bias, a_log, d_skip, norm_w):
    q = SSD_CHUNK
    nc = seq // q
    lane_row = lambda v: jnp.zeros((1, 128), jnp.float32).at[0, SMALL_DT:SMALL_DT + SSM_HEADS].set(v)
    const = lambda shape: pl.BlockSpec(shape, lambda b, c: (0,) * len(shape))
    return pl.pallas_call(
        _ssd_kernel,
        grid=(bsz, nc),
        in_specs=[pl.BlockSpec((q, CONV_CH), lambda b, c: (b * nc + c, COL_XBC // CONV_CH)),
                  pl.BlockSpec((q, SSM_INNER), lambda b, c: (b * nc + c, COL_Z // SSM_INNER)),
                  pl.BlockSpec((q, 128), lambda b, c: (b * nc + c, COL_SMALL // 128)),
                  const((CONV_WIDTH, CONV_CH)), const((1, CONV_CH)), const((1, 128)), const((1, 128)),
                  const((1, SSM_INNER)), const((1, SSM_INNER))],
        out_specs=pl.BlockSpec((q, SSM_INNER), lambda b, c: (b * nc + c, 0)),
        out_shape=jax.ShapeDtypeStruct((bsz * seq, SSM_INNER), jnp.float32),
        scratch_shapes=[pltpu.VMEM((q, CONV_CH), jnp.float32),
                        pltpu.VMEM((SSM_HEADS // 2, 2 * SSM_HEAD_DIM, SSM_STATE), jnp.float32)],
        compiler_params=pltpu.CompilerParams(dimension_semantics=("parallel", "arbitrary"),
                                             vmem_limit_bytes=VMEM_LIMIT),
        name="mamba2_ssd",
    )(proj, proj, proj, conv_w, conv_b.reshape(1, CONV_CH), lane_row(dt_bias), lane_row(-jnp.exp(a_log)),
      jnp.repeat(d_skip, SSM_HEAD_DIM).reshape(1, SSM_INNER), norm_w.reshape(1, SSM_INNER))


def _mix_out_kernel(a_ref, s_ref, gl_ref, x_ref, gm_ref, wo_ref, ws_ref, wout_ref,
                    nf_ref, scf_ref, shf_ref, wr_ref, br_ref, xo_ref, h_ref, lg_ref):
    bf = jnp.bfloat16
    ya = jnp.dot(a_ref[...].astype(bf), wo_ref[...], preferred_element_type=jnp.float32)
    ys = jnp.dot(s_ref[...].astype(bf), ws_ref[...], preferred_element_type=jnp.float32)
    mixed = jax.nn.sigmoid(gl_ref[:, :D_MODEL]) * ya + jax.nn.sigmoid(gl_ref[:, D_MODEL:]) * ys
    x = x_ref[...] + gm_ref[0] * jnp.dot(mixed.astype(bf), wout_ref[...], preferred_element_type=jnp.float32)
    xo_ref[...] = x
    y = x * lax.rsqrt(jnp.mean(x * x, axis=-1, keepdims=True) + EPS) * nf_ref[...]
    h = y * (1.0 + scf_ref[0]) + shf_ref[0]
    h_hi = h.astype(bf)
    hb = pltpu.bitcast(h_hi.astype(jnp.float32), jnp.int32)
    half = D_MODEL // 2
    h_ref[...] = (hb[:, :half] & jnp.int32(-65536)) | lax.shift_right_logical(hb[:, half:], 16)
    h_lo = (h - h_hi.astype(jnp.float32)).astype(bf)
    dot = functools.partial(jnp.dot, preferred_element_type=jnp.float32)
    lg_ref[...] = dot(h_hi, wr_ref[0]) + (dot(h_lo, wr_ref[0]) + dot(h_hi, wr_ref[1])) + br_ref[...]


def _mix_out(attn2, ssd2, proj, x2, g_m, w_attn_o, w_ssm_o, w_out, norm_ffn, sc_f, sh_f, w_router, b_router, seq, tm=512):
    t, d = x2.shape
    per_b = seq // tm
    bf = jnp.bfloat16
    const = lambda shape: pl.BlockSpec(shape, lambda i: (0,) * len(shape))
    perb = pl.BlockSpec((1, 1, d), lambda i: (i // per_b, 0, 0))
    wr = jnp.pad(w_router.astype(jnp.float32), ((0, 0), (0, 128 - N_EXPERTS)))
    wr_hi = wr.astype(bf)
    wr = jnp.stack([wr_hi, (wr - wr_hi.astype(jnp.float32)).astype(bf)])
    br =jnp.pad(b_router, (0, 128 - N_EXPERTS)).reshape(1, 128)
    return pl.pallas_call(
        _mix_out_kernel,
        grid=(t // tm,),
        in_specs=[pl.BlockSpec((tm, ATTN_WIDTH), lambda i: (i, 0)),
                  pl.BlockSpec((tm, SSM_INNER), lambda i: (i, 0)),
                  pl.BlockSpec((tm, 2 * d), lambda i: (i, COL_GATE // (2 * d))),
                  pl.BlockSpec((tm, d), lambda i: (i, 0)),
                  perb,
                  const((ATTN_WIDTH, d)), const((SSM_INNER, d)), const((d, d)),
                  const((1, d)), perb, perb, const((2, d, 128)), const((1, 128))],
        out_specs=[pl.BlockSpec((tm, d), lambda i: (i, 0)),
                   pl.BlockSpec((tm, d // 2), lambda i: (i, 0)),
                   pl.BlockSpec((tm, 128), lambda i: (i, 0))],
        out_shape=[jax.ShapeDtypeStruct((t, d), jnp.float32),
                   jax.ShapeDtypeStruct((t, d // 2), jnp.int32),
                   jax.ShapeDtypeStruct((t, 128), jnp.float32)],
        compiler_params=pltpu.CompilerParams(dimension_semantics=("parallel",), vmem_limit_bytes=VMEM_LIMIT),
        name="mix_out",
    )(attn2, ssd2, proj, x2, g_m[:, None, :], w_attn_o.astype(bf), w_ssm_o.astype(bf), w_out.astype(bf),
      norm_ffn.reshape(1, d), sc_f[:, None, :], sh_f[:, None, :], wr, br)


def _moe_kernel(be_ref, nb_ref, x_ref, g_ref, wgu_ref, bgu_ref, wdn_ref, bdn_ref, o_ref, wgu_bf, wdn_bf):
    i = pl.program_id(0)

    @pl.when((i == 0) | (be_ref[i] != be_ref[jnp.maximum(i - 1, 0)]))
    def _():
        wgu_bf[...] = wgu_ref[0, 0].astype(jnp.bfloat16)
        wdn_bf[...] = wdn_ref[0, 0].astype(jnp.bfloat16)

    @pl.when(i < nb_ref[0])
    def _():
        words = x_ref[...]
        x_hi = pltpu.bitcast(words & jnp.int32(-65536), jnp.float32).astype(jnp.bfloat16)
        x_lo = pltpu.bitcast(words << 16, jnp.float32).astype(jnp.bfloat16)
        x = jnp.concatenate([x_hi, x_lo], axis=1)
        gu = jnp.dot(x, wgu_bf[...], preferred_element_type=jnp.float32) + bgu_ref[0, 0]
        g = jnp.minimum(gu[:, :D_EXPERT], SWIGLU_LIMIT)
        u = jnp.clip(gu[:, D_EXPERT:], -SWIGLU_LIMIT, SWIGLU_LIMIT)
        act = (u + 1.0) * (g * jax.nn.sigmoid(SWIGLU_ALPHA * g))
        out = jnp.dot(act.astype(jnp.bfloat16), wdn_bf[...], preferred_element_type=jnp.float32) + bdn_ref[0, 0]
        o_ref[...] = (out * g_ref[...]).astype(o_ref.dtype)

    @pl.when(i >= nb_ref[0])
    def _():
        o_ref[...] = jnp.zeros_like(o_ref)


def _moe_ffn(xs, row_gate, blk_exp, n_used, w_gu, b_gu, w_dn, b_dn, layer):
    n_rows = xs.shape[0]
    d = D_MODEL
    tm = MOE_TM
    grid_spec = pltpu.PrefetchScalarGridSpec(
        num_scalar_prefetch=2,
        grid=(n_rows // tm,),
        in_specs=[pl.BlockSpec((tm, d // 2), lambda i, be, nb: (i, 0)),
                  pl.BlockSpec((tm, 1), lambda i, be, nb: (i, 0)),
                  pl.BlockSpec((1, 1, d, 2 * D_EXPERT), lambda i, be, nb: (layer, be[i], 0, 0)),
                  pl.BlockSpec((1, 1, 1, 2 * D_EXPERT), lambda i, be, nb: (layer, be[i], 0, 0)),
                  pl.BlockSpec((1, 1, D_EXPERT, d), lambda i, be, nb: (layer, be[i], 0, 0)),
                  pl.BlockSpec((1, 1, 1, d), lambda i, be, nb: (layer, be[i], 0, 0))],
        out_specs=pl.BlockSpec((tm, d), lambda i, be, nb: (i, 0)),
        scratch_shapes=[pltpu.VMEM((d, 2 * D_EXPERT), jnp.bfloat16), pltpu.VMEM((D_EXPERT, d), jnp.bfloat16)],
    )
    return pl.pallas_call(
        _moe_kernel,
        grid_spec=grid_spec,
        out_shape=jax.ShapeDtypeStruct((n_rows, d), jnp.bfloat16),
        compiler_params=pltpu.CompilerParams(dimension_semantics=("arbitrary",), vmem_limit_bytes=VMEM_LIMIT),
        name="moe_ffn",
    )(blk_exp, n_used, xs, row_gate[:, None], w_gu, b_gu[:, :, None, :], w_dn, b_dn[:, :, None, :])


def _moe(h2, logits, w_gu, b_gu, w_dn, b_dn, layer):
    t = h2.shape[0]
    d = D_MODEL
    tm = MOE_TM
    i32 = jnp.int32
    top_val, top_idx = lax.top_k(logits, TOP_K)
    gates = jax.nn.softmax(top_val, axis=-1).reshape(-1)
    n_assign = t * TOP_K
    n_rows = n_assign + N_EXPERTS * tm
    e_flat = top_idx.reshape(n_assign).astype(i32)
    counts = jnp.sum((e_flat[:, None] == jnp.arange(N_EXPERTS, dtype=i32)[None, :]).astype(i32), axis=0)
    padded = (counts + tm - 1) // tm * tm
    filler_exp = jnp.repeat(jnp.arange(N_EXPERTS, dtype=i32), tm)
    filler_key = jnp.where(jnp.tile(jnp.arange(tm, dtype=i32), N_EXPERTS) < jnp.repeat(padded - counts, tm),
                           filler_exp, N_EXPERTS)
    keys = jnp.concatenate([e_flat, filler_key])
    gate_in = jnp.concatenate([gates, jnp.zeros((N_EXPERTS * tm,), jnp.float32)])
    rows = jnp.arange(n_rows, dtype=i32)
    row_key, row_src, row_gate = lax.sort((keys, rows, gate_in), num_keys=1)
    row_tok = jnp.where(row_src < n_assign, row_src // TOP_K, rows % t)
    _, row_of = lax.sort((row_src, rows), num_keys=1)
    dest = row_of[:n_assign].reshape(t, TOP_K).T.reshape(-1)
    blk_exp = jnp.minimum(row_key[::tm], N_EXPERTS - 1)
    n_used = (jnp.sum(padded, keepdims=True) // tm).astype(i32)
    out = _moe_ffn(h2[row_tok], row_gate, blk_exp, n_used, w_gu, b_gu, w_dn, b_dn, layer)
    return out[dest].reshape(TOP_K, t, d)


def _combine_kernel(p_ref, x_ref, g_ref, o_ref):
    f32 = jnp.float32
    y = (p_ref[0].astype(f32) + p_ref[1].astype(f32)) + (p_ref[2].astype(f32) + p_ref[3].astype(f32))
    o_ref[...] = x_ref[...] + g_ref[0] * y


def _combine(parts, x2, g_f, seq, tm=512):
    t, d = x2.shape
    per_b = seq // tm
    return pl.pallas_call(
        _combine_kernel,
        grid=(t // tm,),
        in_specs=[pl.BlockSpec((TOP_K, tm, d), lambda i: (0, i, 0)),
                  pl.BlockSpec((tm, d), lambda i: (i, 0)),
                  pl.BlockSpec((1, 1, d), lambda i: (i // per_b, 0, 0))],
        out_specs=pl.BlockSpec((tm, d), lambda i: (i, 0)),
        out_shape=jax.ShapeDtypeStruct((t, d), jnp.float32),
        compiler_params=pltpu.CompilerParams(dimension_semantics=("parallel",), vmem_limit_bytes=VMEM_LIMIT),
        name="moe_combine",
    )(parts, x2, g_f[:, None, :])


def kernel(x, c, rel_bias, w_ada, b_ada, norm_mix, norm_ffn, w_in, kv_norm, w_kv_up, q_norm, k_norm,
           idx_k_ln_w, idx_k_ln_b, w_attn_o, conv_w, conv_b, dt_bias, a_log, d_skip, ssm_norm, w_ssm_o,
           w_out, w_router, b_router, w_gu, b_gu, w_dn, b_dn):
    bsz, seq, d = x.shape
    t = bsz * seq
    cond = jax.nn.silu(c)
    x2 = x.reshape(t, d)
    for l in range(DEPTH):
        mod = cond @ w_ada[l] + b_ada[l]
        sh_m, sc_m, g_m, sh_f, sc_f, g_f = jnp.split(mod, 6, axis=-1)
        proj = _in_proj(x2, norm_mix[l], sc_m, sh_m, _pack_w_in(w_in[l]), seq)
        qT, k, vT, qiT, ki2, wT, kn2 = _prep(proj, bsz, seq, q_norm[l], kv_norm[l], w_kv_up[l], k_norm[l],
                                             idx_k_ln_w[l], idx_k_ln_b[l])
        attn = _dsa_attention(qT, qiT, wT, k, vT, ki2, kn2, rel_bias)
        y_ssd = _mamba2_ssd(proj, bsz, seq, conv_w[l], conv_b[l], dt_bias[l], a_log[l], d_skip[l], ssm_norm[l])
        x2, h2, logits = _mix_out(attn.reshape(t, ATTN_WIDTH), y_ssd, proj, x2, g_m,
                                  w_attn_o[l], w_ssm_o[l], w_out[l], norm_ffn[l], sc_f, sh_f,
                                  w_router[l], b_router[l], seq)
        parts = _moe(h2, logits[:, :N_EXPERTS], w_gu, b_gu, w_dn, b_dn, l)
        x2 = _combine(parts, x2, g_f, seq)
    return x2.reshape(bsz, seq, d)
```

```python
import functools
import math

import jax
import jax.numpy as jnp
import numpy as np
from jax import lax
from jax.experimental import pallas as pl
from jax.experimental.pallas import tpu as pltpu

D_MODEL = 1024
DEPTH = 2
ATTN_HEADS = 8
ATTN_HEAD_DIM = 64
ATTN_WIDTH = ATTN_HEADS * ATTN_HEAD_DIM
KV_RANK = 256
IDX_HEADS = 8
IDX_DIM = 64
TOPK_MAX = 256
N_BUCKETS = 32
MAX_DISTANCE = 128
SSM_HEADS = 16
SSM_HEAD_DIM = 64
SSM_INNER = SSM_HEADS * SSM_HEAD_DIM
SSM_GROUPS = 2
SSM_STATE = 128
CONV_WIDTH = 4
CONV_CH = SSM_INNER + 2 * SSM_GROUPS * SSM_STATE
SSD_CHUNK = 128
N_EXPERTS = 32
TOP_K = 4
D_EXPERT = D_MODEL
SWIGLU_LIMIT = 7.0
SWIGLU_ALPHA = 1.702
EPS = 1e-6

COL_Q = 0
COL_KV = 512
COL_QI = 768
COL_SMALL = 1280
COL_XBC = 1536
COL_Z = 3072
COL_GATE = 4096
PROJ_COLS = 6144
PREP_COLS = 1408
SMALL_KI, SMALL_WI, SMALL_DT = 0, 64, 72

QB = 256
VROWS = 80
INT_MIN = -2 ** 31
KEY_NEG_INF = (0xFF800000 ^ 0x7FFFFFFF) - 2 ** 32
NEG = -1e30
TINY = 2.0 ** -126
LOG2E = math.log2(math.e)
NORM_SLACK = 1.02
MAX_SHIFT_ERROR = 96.0
VMEM_LIMIT = 56 * 1024 * 1024
MOE_TM = 512
HIGHEST = lax.Precision.HIGHEST
NT = (((1,), (1,)), ((), ()))


def _pack_w_in(w):
    o = np.cumsum((0, ATTN_WIDTH, KV_RANK, IDX_HEADS * IDX_DIM, IDX_DIM, IDX_HEADS, SSM_INNER, CONV_CH, SSM_HEADS, 2 * D_MODEL))
    q, kv, qi, ki, wi, z, xbc, dt, gate = (w[:, int(o[n]):int(o[n + 1])] for n in range(9))
    zeros = lambda n: jnp.zeros((w.shape[0], n), w.dtype)
    small = jnp.concatenate([ki, wi, dt, zeros(128 - 88)], axis=1)
    packed = jnp.concatenate([q, kv, qi, small, zeros(COL_XBC - PREP_COLS), xbc, z, gate], axis=1)
    assert packed.shape[1] == PROJ_COLS
    return packed.astype(jnp.bfloat16)


def _in_proj_kernel(x_ref, g_ref, sc_ref, sh_ref, w_ref, o_ref, h_ref):
    @pl.when(pl.program_id(1) == 0)
    def _():
        x = x_ref[...]
        y = x * lax.rsqrt(jnp.mean(x * x, axis=-1, keepdims=True) + EPS) * g_ref[...]
        h_ref[...] = (y * (1.0 + sc_ref[0]) + sh_ref[0]).astype(jnp.bfloat16)
    o_ref[...] = jnp.dot(h_ref[...], w_ref[...], preferred_element_type=jnp.float32)


def _in_proj(x2, gain, sc, sh, w_packed, seq, tm=1024, tn=1024):
    t, d = x2.shape
    per_b = seq // tm
    return pl.pallas_call(
        _in_proj_kernel,
        grid=(t // tm, PROJ_COLS // tn),
        in_specs=[pl.BlockSpec((tm, d), lambda i, j: (i, 0)),
                  pl.BlockSpec((1, d), lambda i, j: (0, 0)),
                  pl.BlockSpec((1, 1, d), lambda i, j: (i // per_b, 0, 0)),
                  pl.BlockSpec((1, 1, d), lambda i, j: (i // per_b, 0, 0)),
                  pl.BlockSpec((d, tn), lambda i, j: (0, j))],
        out_specs=pl.BlockSpec((tm, tn), lambda i, j: (i, j)),
        out_shape=jax.ShapeDtypeStruct((t, PROJ_COLS), jnp.float32),
        scratch_shapes=[pltpu.VMEM((tm, d), jnp.bfloat16)],
        compiler_params=pltpu.CompilerParams(dimension_semantics=("parallel", "arbitrary"),
                                             vmem_limit_bytes=VMEM_LIMIT),
        name="in_proj",
    )(x2, gain.reshape(1, d), sc[:, None, :], sh[:, None, :], w_packed)


def _head_rms_t(xt):
    x3 = xt.reshape(ATTN_HEADS, ATTN_HEAD_DIM, xt.shape[1])
    return lax.rsqrt(jnp.mean(x3 * x3, axis=1, keepdims=True) + EPS)


def _prep_kernel(p_ref, qg_ref, kvg_ref, wkv_ref, kg_ref, lng_ref, lnb_ref,
                 qT_ref, k_ref, vT_ref, qiT_ref, ki_ref, wT_ref, kn2_ref):
    n = p_ref.shape[0]
    q = p_ref[:, COL_Q:COL_Q + ATTN_WIDTH]
    lat = p_ref[:, COL_KV:COL_KV + KV_RANK]
    qi = p_ref[:, COL_QI:COL_QI + IDX_HEADS * IDX_DIM]
    sm = p_ref[:, COL_SMALL:COL_SMALL + 128]

    scale = ATTN_HEAD_DIM ** -0.5 * LOG2E
    qt = q.T
    qn = qt.reshape(ATTN_HEADS, ATTN_HEAD_DIM, n) * _head_rms_t(qt)
    qT_ref[0] = (qn.reshape(ATTN_WIDTH, n) * qg_ref[...] * scale).astype(jnp.bfloat16)

    latn = lat * lax.rsqrt(jnp.mean(lat * lat, axis=-1, keepdims=True) + EPS) * kvg_ref[...]
    kv = jnp.dot(latn.astype(jnp.bfloat16), wkv_ref[...], preferred_element_type=jnp.float32)
    kt = kv[:, :ATTN_WIDTH].T
    kn = (kt.reshape(ATTN_HEADS, ATTN_HEAD_DIM, n) * _head_rms_t(kt)).reshape(ATTN_WIDTH, n) * kg_ref[...]
    k_ref[0] = kn.T.astype(jnp.bfloat16)
    kn3 = kn.reshape(ATTN_HEADS, ATTN_HEAD_DIM, n)
    kn2_ref[0] = jnp.sum(kn3 * kn3, axis=1)
    vt = kv[:, ATTN_WIDTH:].T.reshape(ATTN_HEADS, ATTN_HEAD_DIM, n)
    ones = jnp.ones((ATTN_HEADS, VROWS - ATTN_HEAD_DIM, n), jnp.float32)
    vT_ref[0] = jnp.concatenate([vt, ones], axis=1).reshape(ATTN_HEADS * VROWS, n).astype(jnp.bfloat16)

    qiT_ref[0] = (qi * (IDX_DIM ** -0.5)).T.astype(jnp.bfloat16)

    lane = lax.broadcasted_iota(jnp.int32, sm.shape, 1)
    kid = jnp.where(lane < IDX_DIM, sm, pltpu.roll(sm, IDX_DIM, 1))
    mu = jnp.mean(kid, axis=-1, keepdims=True)
    var = jnp.mean(jnp.square(kid - mu), axis=-1, keepdims=True)
    ki_ref[0] = ((kid - mu) * lax.rsqrt(var + EPS) * lng_ref[...] + lnb_ref[...]).astype(jnp.bfloat16)

    wT_ref[0] = sm.T[SMALL_WI:SMALL_WI + IDX_HEADS, :] * (IDX_HEADS ** -0.5)


def _prep(proj, bsz, seq, q_norm, kv_norm, w_kv_up, k_norm, ln_w, ln_b, tp=512):
    nb = seq // tp
    tile8 = lambda g: jnp.tile(g, ATTN_HEADS).reshape(ATTN_WIDTH, 1)
    const = lambda shape: pl.BlockSpec(shape, lambda b, i: (0,) * len(shape))
    bf = jnp.bfloat16
    return pl.pallas_call(
        _prep_kernel,
        grid=(bsz, nb),
        in_specs=[pl.BlockSpec((tp, PREP_COLS), lambda b, i: (b * nb + i, 0)),
                  const((ATTN_WIDTH, 1)), const((1, KV_RANK)), const((KV_RANK, 2 * ATTN_WIDTH)),
                  const((ATTN_WIDTH, 1)), const((1, 128)), const((1, 128))],
        out_specs=[pl.BlockSpec((1, ATTN_WIDTH, tp), lambda b, i: (b, 0, i)),
                   pl.BlockSpec((1, tp, ATTN_WIDTH), lambda b, i: (b, i, 0)),
                   pl.BlockSpec((1, ATTN_HEADS * VROWS, tp), lambda b, i: (b, 0, i)),
                   pl.BlockSpec((1, ATTN_WIDTH, tp), lambda b, i: (b, 0, i)),
                   pl.BlockSpec((1, tp, 128), lambda b, i: (b, i, 0)),
                   pl.BlockSpec((1, IDX_HEADS, tp), lambda b, i: (b, 0, i)),
                   pl.BlockSpec((1, ATTN_HEADS, tp), lambda b, i: (b, 0, i))],
        out_shape=[jax.ShapeDtypeStruct((bsz, ATTN_WIDTH, seq), bf),
                   jax.ShapeDtypeStruct((bsz, seq, ATTN_WIDTH), bf),
                   jax.ShapeDtypeStruct((bsz, ATTN_HEADS * VROWS, seq), bf),
                   jax.ShapeDtypeStruct((bsz, ATTN_WIDTH, seq), bf),
                   jax.ShapeDtypeStruct((bsz, seq, 128), bf),
                   jax.ShapeDtypeStruct((bsz, IDX_HEADS, seq), jnp.float32),
                   jax.ShapeDtypeStruct((bsz, ATTN_HEADS, seq), jnp.float32)],
        compiler_params=pltpu.CompilerParams(dimension_semantics=("parallel", "parallel"),
                                             vmem_limit_bytes=VMEM_LIMIT),
        name="attn_prep",
    )(proj, tile8(q_norm), kv_norm.reshape(1, KV_RANK), w_kv_up.astype(bf), tile8(k_norm),
      jnp.tile(ln_w, 2).reshape(1, 128), jnp.tile(ln_b, 2).reshape(1, 128))


def _t5_bucket(dist):
    n = jnp.maximum(dist, 0)
    max_exact = N_BUCKETS // 2
    nf = jnp.maximum(n, 1).astype(jnp.float32)
    large = max_exact + (jnp.log(nf / max_exact) / math.log(MAX_DISTANCE / max_exact) * (N_BUCKETS - max_exact)).astype(jnp.int32)
    large = jnp.minimum(large, N_BUCKETS - 1)
    return jnp.where(n < max_exact, n, large)


def _bias_tables(rel_bias):
    s = jnp.arange(QB, dtype=jnp.int32)[None, :, None]
    q = jnp.arange(QB, dtype=jnp.int32)[None, None, :]
    dist = q - s + jnp.array([2 * QB, QB, 0], jnp.int32)[:, None, None]
    onehot = (_t5_bucket(dist)[..., None] == jnp.arange(N_BUCKETS, dtype=jnp.int32)).astype(jnp.float32)
    b = jnp.einsum('tsqb,bh->thsq', onehot, rel_bias.astype(jnp.float32) * LOG2E, precision=HIGHEST)
    return jnp.where((dist >= 0)[:, None], b, NEG)


def _attn_kernel(qT_ref, qiT_ref, wT_ref, k_ref, vT_ref, ki_ref, kn_ref, tab_ref, bst_ref, o_ref,
                 keys_ref, hi_ref, lo_ref, msk_ref, p_ref, acc_ref, mp_ref, m_ref, *, topk):
    i = pl.program_id(1)
    n_tiles = i + 1
    row_hi = lax.broadcasted_iota(jnp.int32, (128, QB), 0) >= 64

    def head_rows(ref, h):
        pair = ref[0, (h // 2) * 128:(h // 2) * 128 + 128, :]
        return jnp.where(row_hi == bool(h % 2), pair, jnp.zeros_like(pair))

    def tile_rows(kt):
        return pl.ds(pl.multiple_of(kt * QB, QB), QB)

    def score_tile(kt, carry):
        ki = ki_ref[0, tile_rows(kt), :]
        sc = jnp.zeros((QB, QB), jnp.float32)
        for h in range(IDX_HEADS):
            d = jnp.dot(ki, head_rows(qiT_ref, h), preferred_element_type=jnp.float32)
            sc = sc + wT_ref[0, h:h + 1, :] * jnp.maximum(d, 0.0)
        srow = lax.broadcasted_iota(jnp.int32, (QB, QB), 0)
        qcol = lax.broadcasted_iota(jnp.int32, (QB, QB), 1)
        sc = jnp.where(jnp.abs(sc) < TINY, 0.0, sc)
        sc = jnp.where((kt == i) & (srow > qcol), -jnp.inf, sc)
        bits = pltpu.bitcast(sc, jnp.int32)
        keys_ref[tile_rows(kt), :] = bits ^ ((bits >> 31) & 0x7FFFFFFF)
        hi_ref[tile_rows(kt), :] = pltpu.bitcast(bits & jnp.int32(-65536), jnp.float32).astype(jnp.bfloat16)
        return carry
    lax.fori_loop(0, n_tiles, score_tile, 0)

    def count_packed_ge(ref, cb):
        one, zero = jnp.ones((), jnp.bfloat16), jnp.zeros((), jnp.bfloat16)

        def body(kt, acc):
            hit = jnp.where(ref[tile_rows(kt), :] >= cb, one, zero)
            parts = [hit[r:r + 16, :] for r in range(0, QB, 16)]
            while len(parts) > 1:
                parts = [a + b for a, b in zip(parts[::2], parts[1::2])]
            return acc + parts[0]
        acc = lax.fori_loop(0, n_tiles, body, jnp.zeros((16, QB), jnp.bfloat16))
        return jnp.sum(acc.astype(jnp.float32), axis=0, keepdims=True)

    def count_hi_ge(cand16):
        b = cand16 ^ ((cand16 >> 15) & 0x7FFF)
        snap = jnp.where(((b & 0x8000) != 0) | ((b & 0x7F) == 0), 0, 0x0080)
        b = jnp.where((b & 0x7F80) == 0, snap, b)
        return count_packed_ge(hi_ref, pltpu.bitcast(b << 16, jnp.float32).astype(jnp.bfloat16))

    def mid_code(v):
        pat = jnp.where(v >= 16384, v - 16256, 0x8000 | (16511 - v))
        return pltpu.bitcast(pat << 16, jnp.float32)

    def count(hit_of_tile):
        def body(kt, acc):
            return acc + jnp.sum(hit_of_tile(kt).reshape(QB // 8, 8, QB), axis=0)
        acc = lax.fori_loop(0, n_tiles, body, jnp.zeros((8, QB), jnp.int32))
        return jnp.sum(acc, axis=0, keepdims=True)

    def count_ge(cand):
        return count(lambda kt: jnp.where(keys_ref[tile_rows(kt), :] >= cand, 1, 0))

    def hi_step(it, r):
        cand = jnp.where(it == 0, jnp.zeros_like(r), r | (1 << (15 - it)))
        return jnp.where(count_hi_ge(cand) >= topk, cand, r)
    r16 = lax.fori_loop(0, 16, hi_step, jnp.full((1, QB), -32768, jnp.int32))

    above = count_hi_ge(r16 + 1)

    def code_tile(kt, carry):
        key = keys_ref[tile_rows(kt), :]
        code = jnp.where((key >> 16) == r16, mid_code((key >> 1) & 0x7FFF), -jnp.inf)
        lo_ref[tile_rows(kt), :] = code.astype(jnp.bfloat16)
        return carry
    lax.fori_loop(0, n_tiles, code_tile, 0)

    def mid_step(it, v):
        cand = v | (1 << (14 - it))
        cnt = above + count_packed_ge(lo_ref, mid_code(cand).astype(jnp.bfloat16))
        return jnp.where(cnt >= topk, cand, v)
    v15 = lax.fori_loop(0, 15, mid_step, jnp.zeros((1, QB), jnp.int32))
    thr = (r16 << 16) | (v15 << 1)
    thr = jnp.where(count_ge(thr | 1) >= topk, thr | 1, thr)

    cnt_gt = count_ge(thr + 1)
    cnt_ge = count_ge(thr)
    need = topk - cnt_gt
    tie = (cnt_ge - cnt_gt > need) & (thr > KEY_NEG_INF)

    @pl.when(jnp.max(tie.astype(jnp.int32)) > 0)
    def _():
        def count_eq_below(cand):
            def ind(kt):
                idx = lax.broadcasted_iota(jnp.int32, (QB, QB), 0) + kt * QB
                return jnp.where((keys_ref[tile_rows(kt), :] == thr) & (idx < cand), 1, 0)
            return count(ind)

        def idx_step(it, r):
            cand = r | (1 << (15 - it))
            return jnp.where(count_eq_below(cand) < need, cand, r)
        last = lax.fori_loop(0, 16, idx_step, jnp.zeros((1, QB), jnp.int32))

        def drop(kt, carry):
            blk = keys_ref[tile_rows(kt), :]
            idx = lax.broadcasted_iota(jnp.int32, (QB, QB), 0) + kt * QB
            keys_ref[tile_rows(kt), :] = jnp.where(tie & (blk == thr) & (idx > last), INT_MIN, blk)
            return carry
        lax.fori_loop(0, n_tiles, drop, 0)

    def logits(kt, h):
        band = jnp.clip(kt - (i - 2), 0, 2)
        kp = k_ref[0, tile_rows(kt), (h // 2) * 128:(h // 2) * 128 + 128]
        s = jnp.dot(kp, head_rows(qT_ref, h), preferred_element_type=jnp.float32)
        return s + msk_ref[...] + tab_ref[band, h]

    def set_mask(kt):
        msk_ref[...] = jnp.where(keys_ref[tile_rows(kt), :] >= thr, 0.0, NEG)

    def max_tile(kt, carry):
        set_mask(kt)
        for h in range(ATTN_HEADS):
            s = logits(kt, h)
            mp_ref[h] = jnp.maximum(mp_ref[h], jnp.max(s.reshape(QB // 8, 8, QB), axis=0))
        return carry

    seq = kn_ref.shape[2]
    in_extent = lax.broadcasted_iota(jnp.int32, (ATTN_HEADS, seq), 1) < n_tiles * QB
    k_max = jnp.max(jnp.where(in_extent, kn_ref[0], 0.0), axis=1, keepdims=True)
    spread = jnp.zeros((1, QB), jnp.float32)
    for h in range(ATTN_HEADS):
        qh = qT_ref[0, h * ATTN_HEAD_DIM:(h + 1) * ATTN_HEAD_DIM, :].astype(jnp.float32)
        reach = jnp.sqrt(jnp.sum(qh * qh, axis=0, keepdims=True) * k_max[h:h + 1, :]) * NORM_SLACK
        m_ref[h:h + 1, :] = reach + bst_ref[0, h:h + 1, :]
        spread = jnp.maximum(spread, 2.0 * reach + bst_ref[1, h:h + 1, :])
    bound_ok = jnp.max(spread) <= MAX_SHIFT_ERROR

    @pl.when(jnp.logical_not(bound_ok))
    def _():
        mp_ref[...] = jnp.full(mp_ref.shape, NEG, jnp.float32)
        lax.fori_loop(0, n_tiles, max_tile, 0)
        for h in range(ATTN_HEADS):
            m_ref[h:h + 1, :] = jnp.max(mp_ref[h], axis=0, keepdims=True)
    m = [m_ref[h:h + 1, :] for h in range(ATTN_HEADS)]

    def exp_tile(kt, carry):
        set_mask(kt)
        for h in range(ATTN_HEADS):
            p_ref[h] = jnp.exp2(logits(kt, h) - m[h]).astype(jnp.bfloat16)
        for h in range(ATTN_HEADS):
            va = vT_ref[0, h * VROWS:(h + 1) * VROWS, tile_rows(kt)]
            acc_ref[h * VROWS:(h + 1) * VROWS, :] += jnp.dot(va, p_ref[h], preferred_element_type=jnp.float32)
        return carry

    acc_ref[...] = jnp.zeros(acc_ref.shape, jnp.float32)
    lax.fori_loop(0, n_tiles, exp_tile, 0)

    outs = [acc_ref[h * VROWS:h * VROWS + ATTN_HEAD_DIM, :] / acc_ref[h * VROWS + ATTN_HEAD_DIM:h * VROWS + ATTN_HEAD_DIM + 1, :]
            for h in range(ATTN_HEADS)]
    o_ref[0] = jnp.concatenate(outs, axis=0).T


def _dsa_attention(qT, qiT, wT, k, vT, ki2, kn2, rel_bias):
    bsz, _, seq = qT.shape
    topk = min(TOPK_MAX, seq // 4)
    assert seq % QB == 0 and topk <= QB
    assert seq // 16 <= 256
    b2 = rel_bias.astype(jnp.float32) * LOG2E
    bias_stats = jnp.stack([jnp.max(b2, axis=0), jnp.max(b2, axis=0) - jnp.min(b2, axis=0)])
    bias_stats = jnp.broadcast_to(bias_stats[:, :, None], (2, ATTN_HEADS, QB))
    return pl.pallas_call(
        functools.partial(_attn_kernel, topk=topk),
        grid=(bsz, seq // QB),
        in_specs=[
            pl.BlockSpec((1, ATTN_WIDTH, QB), lambda b, i: (b, 0, i)),
            pl.BlockSpec((1, IDX_HEADS * IDX_DIM, QB), lambda b, i: (b, 0, i)),
            pl.BlockSpec((1, IDX_HEADS, QB), lambda b, i: (b, 0, i)),
            pl.BlockSpec((1, seq, ATTN_WIDTH), lambda b, i: (b, 0, 0)),
            pl.BlockSpec((1, ATTN_HEADS * VROWS, seq), lambda b, i: (b, 0, 0)),
            pl.BlockSpec((1, seq, 128), lambda b, i: (b, 0, 0)),
            pl.BlockSpec((1, ATTN_HEADS, seq), lambda b, i: (b, 0, 0)),
            pl.BlockSpec((3, ATTN_HEADS, QB, QB), lambda b, i: (0, 0, 0, 0)),
            pl.BlockSpec((2, ATTN_HEADS, QB), lambda b, i: (0, 0, 0)),
        ],
        out_specs=pl.BlockSpec((1, QB, ATTN_WIDTH), lambda b, i: (b, i, 0)),
        out_shape=jax.ShapeDtypeStruct((bsz, seq, ATTN_WIDTH), jnp.float32),
        scratch_shapes=[
            pltpu.VMEM((seq, QB), jnp.int32),
            pltpu.VMEM((seq, QB), jnp.bfloat16),
            pltpu.VMEM((seq, QB), jnp.bfloat16),
            pltpu.VMEM((QB, QB), jnp.float32),
            pltpu.VMEM((ATTN_HEADS, QB, QB), jnp.bfloat16),
            pltpu.VMEM((ATTN_HEADS * VROWS, QB), jnp.float32),
            pltpu.VMEM((ATTN_HEADS, 8, QB), jnp.float32),
            pltpu.VMEM((ATTN_HEADS, QB), jnp.float32),
        ],
        compiler_params=pltpu.CompilerParams(dimension_semantics=("parallel", "arbitrary"),
                                             vmem_limit_bytes=VMEM_LIMIT),
        name="dsa_attention",
    )(qT, qiT, wT, k, vT, ki2, kn2, _bias_tables(rel_bias), bias_stats)


def _ssd_kernel(xbc_ref, z_ref, sm_ref, cw_ref, cb_ref, dtb_ref, a_ref, dsk_ref, nw_ref, y_ref, prev_ref, st_ref):
    q = SSD_CHUNK
    bf = jnp.bfloat16

    @pl.when(pl.program_id(1) == 0)
    def _():
        prev_ref[...] = jnp.zeros(prev_ref.shape, jnp.float32)
        st_ref[...] = jnp.zeros(st_ref.shape, jnp.float32)

    cur = xbc_ref[...]
    prev = prev_ref[...]
    row = lax.broadcasted_iota(jnp.int32, cur.shape, 0)
    acc = cur * cw_ref[CONV_WIDTH - 1:CONV_WIDTH, :] + cb_ref[...]
    for s in range(1, CONV_WIDTH):
        shifted = jnp.where(row >= s, pltpu.roll(cur, s, 0), pltpu.roll(prev, s, 0))
        acc = acc + shifted * cw_ref[CONV_WIDTH - 1 - s:CONV_WIDTH - s, :]
    prev_ref[...] = cur
    u = acc * jax.nn.sigmoid(acc)
    xs = u[:, :SSM_INNER]
    bm = u[:, SSM_INNER:SSM_INNER + SSM_GROUPS * SSM_STATE].astype(bf)
    cm = u[:, SSM_INNER + SSM_GROUPS * SSM_STATE:].astype(bf)

    t = sm_ref[...] + dtb_ref[...]
    dt = jnp.maximum(t, 0.0) + jnp.log1p(jnp.exp(-jnp.abs(t)))
    ii = lax.broadcasted_iota(jnp.int32, (q, q), 0)
    jj = lax.broadcasted_iota(jnp.int32, (q, q), 1)
    causal = ii >= jj
    acum = jnp.dot(causal.astype(jnp.float32), dt * a_ref[...], preferred_element_type=jnp.float32, precision=HIGHEST)
    acum_t = acum.T
    dt_t = dt.T
    ea = jnp.exp(acum)
    last = acum[q - 1:q, :]
    decay = jnp.exp(last - acum) * dt
    ea_last = jnp.exp(last)

    lane_hi = lax.broadcasted_iota(jnp.int32, (q, 128), 1) >= SSM_HEAD_DIM
    row_hi = lax.broadcasted_iota(jnp.int32, (128, SSM_STATE), 0) >= SSM_HEAD_DIM

    def pair_cols(v, e):
        c0, c1 = SMALL_DT + e, SMALL_DT + e + 1
        return jnp.where(lane_hi, v[:, c1:c1 + 1], v[:, c0:c0 + 1])

    for g in range(SSM_GROUPS):
        bg = bm[:, g * SSM_STATE:(g + 1) * SSM_STATE]
        cg = cm[:, g * SSM_STATE:(g + 1) * SSM_STATE]
        cb = lax.dot_general(cg, bg, NT, preferred_element_type=jnp.float32)
        for k in range(g * 4, g * 4 + 4):
            e = 2 * k
            x_pair = xs[:, k * 128:(k + 1) * 128]
            halves = []
            for h in (e, e + 1):
                c = SMALL_DT + h
                seg = acum[:, c:c + 1] - acum_t[c:c + 1, :]
                w = cb * jnp.exp(jnp.where(causal, seg, -jnp.inf)) * dt_t[c:c + 1, :]
                halves.append(jnp.dot(w.astype(bf), x_pair.astype(bf), preferred_element_type=jnp.float32))
            y_pair = jnp.where(lane_hi, halves[1], halves[0])
            state = st_ref[k]
            y_pair = y_pair + lax.dot_general(cg, state.astype(bf), NT, preferred_element_type=jnp.float32) * pair_cols(ea, e)
            y_ref[:, k * 128:(k + 1) * 128] = y_pair
            xd_t = (x_pair * pair_cols(decay, e)).T.astype(bf)
            c0 = SMALL_DT + e
            keep = jnp.where(row_hi, ea_last[:, c0 + 1:c0 + 2], ea_last[:, c0:c0 + 1])
            st_ref[k] = state * keep + jnp.dot(xd_t, bg, preferred_element_type=jnp.float32)

    y = (y_ref[...] + dsk_ref[...] * xs) * (z_ref[...] * jax.nn.sigmoid(z_ref[...]))
    half = SSM_INNER // SSM_GROUPS
    for g in range(SSM_GROUPS):
        yg = y[:, g * half:(g + 1) * half]
        yg = yg * lax.rsqrt(jnp.mean(yg * yg, axis=-1, keepdims=True) + EPS)
        y_ref[:, g * half:(g + 1) * half] = yg * nw_ref[:, g * half:(g + 1) * half]


def _mamba2_ssd(proj, bsz, seq, conv_w, conv_b, dt_bias, a_log, d_skip, norm_w):
    q = SSD_CHUNK
    nc = seq // q
    lane_row = lambda v: jnp.zeros((1, 128), jnp.float32).at[0, SMALL_DT:SMALL_DT + SSM_HEADS].set(v)
    const = lambda shape: pl.BlockSpec(shape, lambda b, c: (0,) * len(shape))
    return pl.pallas_call(
        _ssd_kernel,
        grid=(bsz, nc),
        in_specs=[pl.BlockSpec((q, CONV_CH), lambda b, c: (b * nc + c, COL_XBC // CONV_CH)),
                  pl.BlockSpec((q, SSM_INNER), lambda b, c: (b * nc + c, COL_Z // SSM_INNER)),
                  pl.BlockSpec((q, 128), lambda b, c: (b * nc + c, COL_SMALL // 128)),
                  const((CONV_WIDTH, CONV_CH)), const((1, CONV_CH)), const((1, 128)), const((1, 128)),
                  const((1, SSM_INNER)), const((1, SSM_INNER))],
        out_specs=pl.BlockSpec((q, SSM_INNER), lambda b, c: (b * nc + c, 0)),
        out_shape=jax.ShapeDtypeStruct((bsz * seq, SSM_INNER), jnp.float32),
        scratch_shapes=[pltpu.VMEM((q, CONV_CH), jnp.float32),
                        pltpu.VMEM((SSM_HEADS // 2, 2 * SSM_HEAD_DIM, SSM_STATE), jnp.float32)],
        compiler_params=pltpu.CompilerParams(dimension_semantics=("parallel", "arbitrary"),
                                             vmem_limit_bytes=VMEM_LIMIT),
        name="mamba2_ssd",
    )(proj, proj, proj, conv_w, conv_b.reshape(1, CONV_CH), lane_row(dt_bias), lane_row(-jnp.exp(a_log)),
      jnp.repeat(d_skip, SSM_HEAD_DIM).reshape(1, SSM_INNER), norm_w.reshape(1, SSM_INNER))


def _mix_out_kernel(a_ref, s_ref, gl_ref, x_ref, gm_ref, wo_ref, ws_ref, wout_ref,
                    nf_ref, scf_ref, shf_ref, wr_ref, br_ref, xo_ref, h_ref, rt_ref, cnt_ref):
    bf = jnp.bfloat16
    ya = jnp.dot(a_ref[...].astype(bf), wo_ref[...], preferred_element_type=jnp.float32)
    ys = jnp.dot(s_ref[...].astype(bf), ws_ref[...], preferred_element_type=jnp.float32)
    mixed = jax.nn.sigmoid(gl_ref[:, :D_MODEL]) * ya + jax.nn.sigmoid(gl_ref[:, D_MODEL:]) * ys
    x = x_ref[...] + gm_ref[0] * jnp.dot(mixed.astype(bf), wout_ref[...], preferred_element_type=jnp.float32)
    xo_ref[...] = x
    y = x * lax.rsqrt(jnp.mean(x * x, axis=-1, keepdims=True) + EPS) * nf_ref[...]
    h = y * (1.0 + scf_ref[0]) + shf_ref[0]
    h_hi = h.astype(bf)
    hb = pltpu.bitcast(h_hi.astype(jnp.float32), jnp.int32)
    half = D_MODEL // 2
    h_ref[...] = (hb[:, :half] & jnp.int32(-65536)) | lax.shift_right_logical(hb[:, half:], 16)
    h_lo = (h - h_hi.astype(jnp.float32)).astype(bf)
    dot = functools.partial(jnp.dot, preferred_element_type=jnp.float32)
    lg = dot(h_hi, wr_ref[0]) + (dot(h_lo, wr_ref[0]) + dot(h_hi, wr_ref[1])) + br_ref[...]

    lane = lax.broadcasted_iota(jnp.int32, lg.shape, 1)
    work = jnp.where(lane < N_EXPERTS, lg, -jnp.inf)
    lane_f = lane.astype(jnp.float32)
    vals, eids, hits = [], [], []
    for _ in range(TOP_K):
        mx = jnp.max(work, axis=-1, keepdims=True)
        ix = jnp.min(jnp.where(work == mx, lane_f, 128.0), axis=-1, keepdims=True)
        vals.append(mx)
        eids.append(ix)
        hits.append(lane_f == ix)
        work = jnp.where(hits[-1], -jnp.inf, work)
    ex = [jnp.exp(v - vals[0]) for v in vals]
    den = (ex[0] + ex[1]) + (ex[2] + ex[3])

    @pl.when(pl.program_id(0) == 0)
    def _():
        cnt_ref[...] = jnp.zeros(cnt_ref.shape, jnp.float32)
    tm = lg.shape[0]
    chosen = jnp.zeros(lg.shape, jnp.float32)
    for hit in hits:
        chosen = jnp.where(hit, 1.0, chosen)
    earlier = lax.broadcasted_iota(jnp.int32, (tm, tm), 0) > lax.broadcasted_iota(jnp.int32, (tm, tm), 1)
    before = dot(earlier.astype(bf), chosen.astype(bf)) + cnt_ref[...]
    cnt_ref[...] = cnt_ref[...] + jnp.sum(chosen, axis=0, keepdims=True)

    route = jnp.zeros(lg.shape, jnp.float32)
    for k, hit in enumerate(hits):
        rank = jnp.sum(jnp.where(hit, before, 0.0), axis=-1, keepdims=True)
        route = jnp.where(lane == k, eids[k], route)
        route = jnp.where(lane == TOP_K + k, ex[k] / den, route)
        route = jnp.where(lane == 2 * TOP_K + k, rank, route)
    rt_ref[...] = route


def _mix_out(attn2, ssd2, proj, x2, g_m, w_attn_o, w_ssm_o, w_out, norm_ffn, sc_f, sh_f, w_router, b_router, seq, tm=512):
    t, d = x2.shape
    per_b = seq // tm
    bf = jnp.bfloat16
    const = lambda shape: pl.BlockSpec(shape, lambda i: (0,) * len(shape))
    perb = pl.BlockSpec((1, 1, d), lambda i: (i // per_b, 0, 0))
    wr = jnp.pad(w_router.astype(jnp.float32), ((0, 0), (0, 128 - N_EXPERTS)))
    wr_hi = wr.astype(bf)
    wr = jnp.stack([wr_hi, (wr - wr_hi.astype(jnp.float32)).astype(bf)])
    br =jnp.pad(b_router, (0, 128 - N_EXPERTS)).reshape(1, 128)
    return pl.pallas_call(
        _mix_out_kernel,
        grid=(t // tm,),
        in_specs=[pl.BlockSpec((tm, ATTN_WIDTH), lambda i: (i, 0)),
                  pl.BlockSpec((tm, SSM_INNER), lambda i: (i, 0)),
                  pl.BlockSpec((tm, 2 * d), lambda i: (i, COL_GATE // (2 * d))),
                  pl.BlockSpec((tm, d), lambda i: (i, 0)),
                  perb,
                  const((ATTN_WIDTH, d)), const((SSM_INNER, d)), const((d, d)),
                  const((1, d)), perb, perb, const((2, d, 128)), const((1, 128))],
        out_specs=[pl.BlockSpec((tm, d), lambda i: (i, 0)),
                   pl.BlockSpec((tm, d // 2), lambda i: (i, 0)),
                   pl.BlockSpec((tm, 128), lambda i: (i, 0)),
                   pl.BlockSpec((1, 128), lambda i: (0, 0))],
        out_shape=[jax.ShapeDtypeStruct((t, d), jnp.float32),
                   jax.ShapeDtypeStruct((t, d // 2), jnp.int32),
                   jax.ShapeDtypeStruct((t, 128), jnp.float32),
                   jax.ShapeDtypeStruct((1, 128), jnp.float32)],
        compiler_params=pltpu.CompilerParams(dimension_semantics=("arbitrary",), vmem_limit_bytes=VMEM_LIMIT),
        name="mix_out",
    )(attn2, ssd2, proj, x2, g_m[:, None, :], w_attn_o.astype(bf), w_ssm_o.astype(bf), w_out.astype(bf),
      norm_ffn.reshape(1, d), sc_f[:, None, :], sh_f[:, None, :], wr, br)


def _moe_kernel(be_ref, nb_ref, x_ref, g_ref, wgu_ref, bgu_ref, wdn_ref, bdn_ref, o_ref, wgu_bf, wdn_bf):
    i = pl.program_id(0)

    @pl.when((i == 0) | (be_ref[i] != be_ref[jnp.maximum(i - 1, 0)]))
    def _():
        wgu_bf[...] = wgu_ref[0, 0].astype(jnp.bfloat16)
        wdn_bf[...] = wdn_ref[0, 0].astype(jnp.bfloat16)

    @pl.when(i < nb_ref[0])
    def _():
        words = x_ref[...]
        x_hi = pltpu.bitcast(words & jnp.int32(-65536), jnp.float32).astype(jnp.bfloat16)
        x_lo = pltpu.bitcast(words << 16, jnp.float32).astype(jnp.bfloat16)
        x = jnp.concatenate([x_hi, x_lo], axis=1)
        gu = jnp.dot(x, wgu_bf[...], preferred_element_type=jnp.float32) + bgu_ref[0, 0]
        g = jnp.minimum(gu[:, :D_EXPERT], SWIGLU_LIMIT)
        u = jnp.clip(gu[:, D_EXPERT:], -SWIGLU_LIMIT, SWIGLU_LIMIT)
        act = (u + 1.0) * (g * jax.nn.sigmoid(SWIGLU_ALPHA * g))
        out = jnp.dot(act.astype(jnp.bfloat16), wdn_bf[...], preferred_element_type=jnp.float32) + bdn_ref[0, 0]
        o_ref[...] = (out * g_ref[...]).astype(o_ref.dtype)

    @pl.when(i >= nb_ref[0])
    def _():
        o_ref[...] = jnp.zeros_like(o_ref)


def _moe_ffn(xs, row_gate, blk_exp, n_used, w_gu, b_gu, w_dn, b_dn, layer):
    n_rows = xs.shape[0]
    d = D_MODEL
    tm = MOE_TM
    grid_spec = pltpu.PrefetchScalarGridSpec(
        num_scalar_prefetch=2,
        grid=(n_rows // tm,),
        in_specs=[pl.BlockSpec((tm, d // 2), lambda i, be, nb: (i, 0)),
                  pl.BlockSpec((tm, 1), lambda i, be, nb: (i, 0)),
                  pl.BlockSpec((1, 1, d, 2 * D_EXPERT), lambda i, be, nb: (layer, be[i], 0, 0)),
                  pl.BlockSpec((1, 1, 1, 2 * D_EXPERT), lambda i, be, nb: (layer, be[i], 0, 0)),
                  pl.BlockSpec((1, 1, D_EXPERT, d), lambda i, be, nb: (layer, be[i], 0, 0)),
                  pl.BlockSpec((1, 1, 1, d), lambda i, be, nb: (layer, be[i], 0, 0))],
        out_specs=pl.BlockSpec((tm, d), lambda i, be, nb: (i, 0)),
        scratch_shapes=[pltpu.VMEM((d, 2 * D_EXPERT), jnp.bfloat16), pltpu.VMEM((D_EXPERT, d), jnp.bfloat16)],
    )
    return pl.pallas_call(
        _moe_kernel,
        grid_spec=grid_spec,
        out_shape=jax.ShapeDtypeStruct((n_rows, d), jnp.bfloat16),
        compiler_params=pltpu.CompilerParams(dimension_semantics=("arbitrary",), vmem_limit_bytes=VMEM_LIMIT),
        name="moe_ffn",
    )(blk_exp, n_used, xs, row_gate[:, None], w_gu, b_gu[:, :, None, :], w_dn, b_dn[:, :, None, :])


def _moe(h2, route, expert_counts, w_gu, b_gu, w_dn, b_dn, layer):
    t = h2.shape[0]
    d = D_MODEL
    tm = MOE_TM
    i32 = jnp.int32
    experts = jnp.arange(N_EXPERTS, dtype=i32)
    top_idx = route[:, :TOP_K].astype(i32)
    gates = route[:, TOP_K:2 * TOP_K].reshape(-1)
    rank = route[:, 2 * TOP_K:3 * TOP_K].astype(i32)
    n_assign = t * TOP_K
    n_rows = n_assign + N_EXPERTS * tm
    e_flat = top_idx.reshape(n_assign)
    counts = expert_counts[0, :N_EXPERTS].astype(i32)
    padded = (counts + tm - 1) // tm * tm
    pad_start = jnp.cumsum(padded) - padded
    dest = rank + jnp.sum(jnp.where(top_idx[..., None] == experts, pad_start, 0), axis=-1)
    dest = dest.T.reshape(-1)
    filler_exp = jnp.repeat(experts, tm)
    filler_key = jnp.where(jnp.tile(jnp.arange(tm, dtype=i32), N_EXPERTS) < jnp.repeat(padded - counts, tm),
                           filler_exp, N_EXPERTS)
    keys = jnp.concatenate([e_flat, filler_key])
    gate_in = jnp.concatenate([gates, jnp.zeros((N_EXPERTS * tm,), jnp.float32)])
    rows = jnp.arange(n_rows, dtype=i32)
    row_key, row_src, row_gate = lax.sort((keys, rows, gate_in), num_keys=1)
    row_tok = jnp.where(row_src < n_assign, row_src // TOP_K, rows % t)
    blk_exp = jnp.minimum(row_key[::tm], N_EXPERTS - 1)
    n_used = (jnp.sum(padded, keepdims=True) // tm).astype(i32)
    out = _moe_ffn(h2[row_tok], row_gate, blk_exp, n_used, w_gu, b_gu, w_dn, b_dn, layer)
    return out[dest].reshape(TOP_K, t, d)


def _combine_kernel(p_ref, x_ref, g_ref, o_ref):
    f32 = jnp.float32
    y = (p_ref[0].astype(f32) + p_ref[1].astype(f32)) + (p_ref[2].astype(f32) + p_ref[3].astype(f32))
    o_ref[...] = x_ref[...] + g_ref[0] * y


def _combine(parts, x2, g_f, seq, tm=512):
    t, d = x2.shape
    per_b = seq // tm
    return pl.pallas_call(
        _combine_kernel,
        grid=(t // tm,),
        in_specs=[pl.BlockSpec((TOP_K, tm, d), lambda i: (0, i, 0)),
                  pl.BlockSpec((tm, d), lambda i: (i, 0)),
                  pl.BlockSpec((1, 1, d), lambda i: (i // per_b, 0, 0))],
        out_specs=pl.BlockSpec((tm, d), lambda i: (i, 0)),
        out_shape=jax.ShapeDtypeStruct((t, d), jnp.float32),
        compiler_params=pltpu.CompilerParams(dimension_semantics=("parallel",), vmem_limit_bytes=VMEM_LIMIT),
        name="moe_combine",
    )(parts, x2, g_f[:, None, :])


def kernel(x, c, rel_bias, w_ada, b_ada, norm_mix, norm_ffn, w_in, kv_norm, w_kv_up, q_norm, k_norm,
           idx_k_ln_w, idx_k_ln_b, w_attn_o, conv_w, conv_b, dt_bias, a_log, d_skip, ssm_norm, w_ssm_o,
           w_out, w_router, b_router, w_gu, b_gu, w_dn, b_dn):
    bsz, seq, d = x.shape
    t = bsz * seq
    cond = jax.nn.silu(c)
    x2 = x.reshape(t, d)
    for l in range(DEPTH):
        mod = cond @ w_ada[l] + b_ada[l]
        sh_m, sc_m, g_m, sh_f, sc_f, g_f = jnp.split(mod, 6, axis=-1)
        proj = _in_proj(x2, norm_mix[l], sc_m, sh_m, _pack_w_in(w_in[l]), seq)
        qT, k, vT, qiT, ki2, wT, kn2 = _prep(proj, bsz, seq, q_norm[l], kv_norm[l], w_kv_up[l], k_norm[l],
                                             idx_k_ln_w[l], idx_k_ln_b[l])
        attn = _dsa_attention(qT, qiT, wT, k, vT, ki2, kn2, rel_bias)
        y_ssd = _mamba2_ssd(proj, bsz, seq, conv_w[l], conv_b[l], dt_bias[l], a_log[l], d_skip[l], ssm_norm[l])
        x2, h2, route, expert_counts = _mix_out(attn.reshape(t, ATTN_WIDTH), y_ssd, proj, x2, g_m,
                                                w_attn_o[l], w_ssm_o[l], w_out[l], norm_ffn[l], sc_f, sh_f,
                                                w_router[l], b_router[l], seq)
        parts = _moe(h2, route, expert_counts, w_gu, b_gu, w_dn, b_dn, l)
        x2 = _combine(parts, x2, g_f, seq)
    return x2.reshape(bsz, seq, d)
```

```python
import functools
import math

import jax
import jax.numpy as jnp
import numpy as np
from jax import lax
from jax.experimental import pallas as pl
from jax.experimental.pallas import tpu as pltpu

D_MODEL = 1024
DEPTH = 2
ATTN_HEADS = 8
ATTN_HEAD_DIM = 64
ATTN_WIDTH = ATTN_HEADS * ATTN_HEAD_DIM
KV_RANK = 256
IDX_HEADS = 8
IDX_DIM = 64
TOPK_MAX = 256
N_BUCKETS = 32
MAX_DISTANCE = 128
SSM_HEADS = 16
SSM_HEAD_DIM = 64
SSM_INNER = SSM_HEADS * SSM_HEAD_DIM
SSM_GROUPS = 2
SSM_STATE = 128
CONV_WIDTH = 4
CONV_CH = SSM_INNER + 2 * SSM_GROUPS * SSM_STATE
SSD_CHUNK = 128
N_EXPERTS = 32
TOP_K = 4
D_EXPERT = D_MODEL
SWIGLU_LIMIT = 7.0
SWIGLU_ALPHA = 1.702
EPS = 1e-6

COL_Q = 0
COL_KV = 512
COL_QI = 768
COL_SMALL = 1280
COL_XBC = 1536
COL_Z = 3072
COL_GATE = 4096
PROJ_COLS = 6144
PREP_COLS = 1408
SMALL_KI, SMALL_WI, SMALL_DT = 0, 64, 72

QB = 256
VROWS = 80
INT_MIN = -2 ** 31
KEY_NEG_INF = (0xFF800000 ^ 0x7FFFFFFF) - 2 ** 32
NEG = -1e30
TINY = 2.0 ** -126
LOG2E = math.log2(math.e)
NORM_SLACK = 1.02
MAX_SHIFT_ERROR = 96.0
VMEM_LIMIT = 56 * 1024 * 1024
MOE_TM = 512
ROUTE_ROWS = 16
HIGHEST = lax.Precision.HIGHEST
NT = (((1,), (1,)), ((), ()))


def _pack_w_in(w):
    o = np.cumsum((0, ATTN_WIDTH, KV_RANK, IDX_HEADS * IDX_DIM, IDX_DIM, IDX_HEADS, SSM_INNER, CONV_CH, SSM_HEADS, 2 * D_MODEL))
    q, kv, qi, ki, wi, z, xbc, dt, gate = (w[:, int(o[n]):int(o[n + 1])] for n in range(9))
    zeros = lambda n: jnp.zeros((w.shape[0], n), w.dtype)
    small = jnp.concatenate([ki, wi, dt, zeros(128 - 88)], axis=1)
    packed = jnp.concatenate([q, kv, qi, small, zeros(COL_XBC - PREP_COLS), xbc, z, gate], axis=1)
    assert packed.shape[1] == PROJ_COLS
    return packed.astype(jnp.bfloat16)


def _in_proj_kernel(x_ref, g_ref, sc_ref, sh_ref, w_ref, o_ref, h_ref):
    @pl.when(pl.program_id(1) == 0)
    def _():
        x = x_ref[...]
        y = x * lax.rsqrt(jnp.mean(x * x, axis=-1, keepdims=True) + EPS) * g_ref[...]
        h_ref[...] = (y * (1.0 + sc_ref[0]) + sh_ref[0]).astype(jnp.bfloat16)
    o_ref[...] = jnp.dot(h_ref[...], w_ref[...], preferred_element_type=jnp.float32)


def _in_proj(x2, gain, sc, sh, w_packed, seq, tm=1024, tn=1024):
    t, d = x2.shape
    per_b = seq // tm
    return pl.pallas_call(
        _in_proj_kernel,
        grid=(t // tm, PROJ_COLS // tn),
        in_specs=[pl.BlockSpec((tm, d), lambda i, j: (i, 0)),
                  pl.BlockSpec((1, d), lambda i, j: (0, 0)),
                  pl.BlockSpec((1, 1, d), lambda i, j: (i // per_b, 0, 0)),
                  pl.BlockSpec((1, 1, d), lambda i, j: (i // per_b, 0, 0)),
                  pl.BlockSpec((d, tn), lambda i, j: (0, j))],
        out_specs=pl.BlockSpec((tm, tn), lambda i, j: (i, j)),
        out_shape=jax.ShapeDtypeStruct((t, PROJ_COLS), jnp.float32),
        scratch_shapes=[pltpu.VMEM((tm, d), jnp.bfloat16)],
        compiler_params=pltpu.CompilerParams(dimension_semantics=("parallel", "arbitrary"),
                                             vmem_limit_bytes=VMEM_LIMIT),
        name="in_proj",
    )(x2, gain.reshape(1, d), sc[:, None, :], sh[:, None, :], w_packed)


def _head_rms_t(xt):
    x3 = xt.reshape(ATTN_HEADS, ATTN_HEAD_DIM, xt.shape[1])
    return lax.rsqrt(jnp.mean(x3 * x3, axis=1, keepdims=True) + EPS)


def _prep_kernel(p_ref, qg_ref, kvg_ref, wkv_ref, kg_ref, lng_ref, lnb_ref,
                 qT_ref, k_ref, vT_ref, qiT_ref, ki_ref, wT_ref, kn2_ref):
    n = p_ref.shape[0]
    q = p_ref[:, COL_Q:COL_Q + ATTN_WIDTH]
    lat = p_ref[:, COL_KV:COL_KV + KV_RANK]
    qi = p_ref[:, COL_QI:COL_QI + IDX_HEADS * IDX_DIM]
    sm = p_ref[:, COL_SMALL:COL_SMALL + 128]

    scale = ATTN_HEAD_DIM ** -0.5 * LOG2E
    qt = q.T
    qn = qt.reshape(ATTN_HEADS, ATTN_HEAD_DIM, n) * _head_rms_t(qt)
    qT_ref[0] = (qn.reshape(ATTN_WIDTH, n) * qg_ref[...] * scale).astype(jnp.bfloat16)

    latn = lat * lax.rsqrt(jnp.mean(lat * lat, axis=-1, keepdims=True) + EPS) * kvg_ref[...]
    kv = jnp.dot(latn.astype(jnp.bfloat16), wkv_ref[...], preferred_element_type=jnp.float32)
    kt = kv[:, :ATTN_WIDTH].T
    kn = (kt.reshape(ATTN_HEADS, ATTN_HEAD_DIM, n) * _head_rms_t(kt)).reshape(ATTN_WIDTH, n) * kg_ref[...]
    k_ref[0] = kn.T.astype(jnp.bfloat16)
    kn3 = kn.reshape(ATTN_HEADS, ATTN_HEAD_DIM, n)
    kn2_ref[0] = jnp.sum(kn3 * kn3, axis=1)
    vt = kv[:, ATTN_WIDTH:].T.reshape(ATTN_HEADS, ATTN_HEAD_DIM, n)
    ones = jnp.ones((ATTN_HEADS, VROWS - ATTN_HEAD_DIM, n), jnp.float32)
    vT_ref[0] = jnp.concatenate([vt, ones], axis=1).reshape(ATTN_HEADS * VROWS, n).astype(jnp.bfloat16)

    qiT_ref[0] = (qi * (IDX_DIM ** -0.5)).T.astype(jnp.bfloat16)

    lane = lax.broadcasted_iota(jnp.int32, sm.shape, 1)
    kid = jnp.where(lane < IDX_DIM, sm, pltpu.roll(sm, IDX_DIM, 1))
    mu = jnp.mean(kid, axis=-1, keepdims=True)
    var = jnp.mean(jnp.square(kid - mu), axis=-1, keepdims=True)
    ki_ref[0] = ((kid - mu) * lax.rsqrt(var + EPS) * lng_ref[...] + lnb_ref[...]).astype(jnp.bfloat16)

    wT_ref[0] = sm.T[SMALL_WI:SMALL_WI + IDX_HEADS, :] * (IDX_HEADS ** -0.5)


def _prep(proj, bsz, seq, q_norm, kv_norm, w_kv_up, k_norm, ln_w, ln_b, tp=512):
    nb = seq // tp
    tile8 = lambda g: jnp.tile(g, ATTN_HEADS).reshape(ATTN_WIDTH, 1)
    const = lambda shape: pl.BlockSpec(shape, lambda b, i: (0,) * len(shape))
    bf = jnp.bfloat16
    return pl.pallas_call(
        _prep_kernel,
        grid=(bsz, nb),
        in_specs=[pl.BlockSpec((tp, PREP_COLS), lambda b, i: (b * nb + i, 0)),
                  const((ATTN_WIDTH, 1)), const((1, KV_RANK)), const((KV_RANK, 2 * ATTN_WIDTH)),
                  const((ATTN_WIDTH, 1)), const((1, 128)), const((1, 128))],
        out_specs=[pl.BlockSpec((1, ATTN_WIDTH, tp), lambda b, i: (b, 0, i)),
                   pl.BlockSpec((1, tp, ATTN_WIDTH), lambda b, i: (b, i, 0)),
                   pl.BlockSpec((1, ATTN_HEADS * VROWS, tp), lambda b, i: (b, 0, i)),
                   pl.BlockSpec((1, ATTN_WIDTH, tp), lambda b, i: (b, 0, i)),
                   pl.BlockSpec((1, tp, 128), lambda b, i: (b, i, 0)),
                   pl.BlockSpec((1, IDX_HEADS, tp), lambda b, i: (b, 0, i)),
                   pl.BlockSpec((1, ATTN_HEADS, tp), lambda b, i: (b, 0, i))],
        out_shape=[jax.ShapeDtypeStruct((bsz, ATTN_WIDTH, seq), bf),
                   jax.ShapeDtypeStruct((bsz, seq, ATTN_WIDTH), bf),
                   jax.ShapeDtypeStruct((bsz, ATTN_HEADS * VROWS, seq), bf),
                   jax.ShapeDtypeStruct((bsz, ATTN_WIDTH, seq), bf),
                   jax.ShapeDtypeStruct((bsz, seq, 128), bf),
                   jax.ShapeDtypeStruct((bsz, IDX_HEADS, seq), jnp.float32),
                   jax.ShapeDtypeStruct((bsz, ATTN_HEADS, seq), jnp.float32)],
        compiler_params=pltpu.CompilerParams(dimension_semantics=("parallel", "parallel"),
                                             vmem_limit_bytes=VMEM_LIMIT),
        name="attn_prep",
    )(proj, tile8(q_norm), kv_norm.reshape(1, KV_RANK), w_kv_up.astype(bf), tile8(k_norm),
      jnp.tile(ln_w, 2).reshape(1, 128), jnp.tile(ln_b, 2).reshape(1, 128))


def _t5_bucket(dist):
    n = jnp.maximum(dist, 0)
    max_exact = N_BUCKETS // 2
    nf = jnp.maximum(n, 1).astype(jnp.float32)
    large = max_exact + (jnp.log(nf / max_exact) / math.log(MAX_DISTANCE / max_exact) * (N_BUCKETS - max_exact)).astype(jnp.int32)
    large = jnp.minimum(large, N_BUCKETS - 1)
    return jnp.where(n < max_exact, n, large)


def _bias_tables(rel_bias):
    s = jnp.arange(QB, dtype=jnp.int32)[None, :, None]
    q = jnp.arange(QB, dtype=jnp.int32)[None, None, :]
    dist = q - s + jnp.array([2 * QB, QB, 0], jnp.int32)[:, None, None]
    onehot = (_t5_bucket(dist)[..., None] == jnp.arange(N_BUCKETS, dtype=jnp.int32)).astype(jnp.float32)
    b = jnp.einsum('tsqb,bh->thsq', onehot, rel_bias.astype(jnp.float32) * LOG2E, precision=HIGHEST)
    return jnp.where((dist >= 0)[:, None], b, NEG)


def _attn_kernel(qT_ref, qiT_ref, wT_ref, k_ref, vT_ref, ki_ref, kn_ref, tab_ref, bst_ref, o_ref,
                 keys_ref, hi_ref, lo_ref, msk_ref, p_ref, acc_ref, mp_ref, m_ref, *, topk):
    i = pl.program_id(1)
    n_tiles = i + 1
    row_hi = lax.broadcasted_iota(jnp.int32, (128, QB), 0) >= 64

    def head_rows(ref, h):
        pair = ref[0, (h // 2) * 128:(h // 2) * 128 + 128, :]
        return jnp.where(row_hi == bool(h % 2), pair, jnp.zeros_like(pair))

    def tile_rows(kt):
        return pl.ds(pl.multiple_of(kt * QB, QB), QB)

    def score_tile(kt, carry):
        ki = ki_ref[0, tile_rows(kt), :]
        sc = jnp.zeros((QB, QB), jnp.float32)
        for h in range(IDX_HEADS):
            d = jnp.dot(ki, head_rows(qiT_ref, h), preferred_element_type=jnp.float32)
            sc = sc + wT_ref[0, h:h + 1, :] * jnp.maximum(d, 0.0)
        srow = lax.broadcasted_iota(jnp.int32, (QB, QB), 0)
        qcol = lax.broadcasted_iota(jnp.int32, (QB, QB), 1)
        sc = jnp.where(jnp.abs(sc) < TINY, 0.0, sc)
        sc = jnp.where((kt == i) & (srow > qcol), -jnp.inf, sc)
        bits = pltpu.bitcast(sc, jnp.int32)
        keys_ref[tile_rows(kt), :] = bits ^ ((bits >> 31) & 0x7FFFFFFF)
        hi_ref[tile_rows(kt), :] = pltpu.bitcast(bits & jnp.int32(-65536), jnp.float32).astype(jnp.bfloat16)
        return carry
    lax.fori_loop(0, n_tiles, score_tile, 0)

    def count_packed_ge(ref, cb):
        one, zero = jnp.ones((), jnp.bfloat16), jnp.zeros((), jnp.bfloat16)

        def body(kt, acc):
            hit = jnp.where(ref[tile_rows(kt), :] >= cb, one, zero)
            parts = [hit[r:r + 16, :] for r in range(0, QB, 16)]
            while len(parts) > 1:
                parts = [a + b for a, b in zip(parts[::2], parts[1::2])]
            return acc + parts[0]
        acc = lax.fori_loop(0, n_tiles, body, jnp.zeros((16, QB), jnp.bfloat16))
        return jnp.sum(acc.astype(jnp.float32), axis=0, keepdims=True)

    def count_hi_ge(cand16):
        b = cand16 ^ ((cand16 >> 15) & 0x7FFF)
        snap = jnp.where(((b & 0x8000) != 0) | ((b & 0x7F) == 0), 0, 0x0080)
        b = jnp.where((b & 0x7F80) == 0, snap, b)
        return count_packed_ge(hi_ref, pltpu.bitcast(b << 16, jnp.float32).astype(jnp.bfloat16))

    def mid_code(v):
        pat = jnp.where(v >= 16384, v - 16256, 0x8000 | (16511 - v))
        return pltpu.bitcast(pat << 16, jnp.float32)

    def count(hit_of_tile):
        def body(kt, acc):
            return acc + jnp.sum(hit_of_tile(kt).reshape(QB // 8, 8, QB), axis=0)
        acc = lax.fori_loop(0, n_tiles, body, jnp.zeros((8, QB), jnp.int32))
        return jnp.sum(acc, axis=0, keepdims=True)

    def count_ge(cand):
        return count(lambda kt: jnp.where(keys_ref[tile_rows(kt), :] >= cand, 1, 0))

    def hi_step(it, r):
        cand = jnp.where(it == 0, jnp.zeros_like(r), r | (1 << (15 - it)))
        return jnp.where(count_hi_ge(cand) >= topk, cand, r)
    r16 = lax.fori_loop(0, 16, hi_step, jnp.full((1, QB), -32768, jnp.int32))

    above = count_hi_ge(r16 + 1)

    def code_tile(kt, carry):
        key = keys_ref[tile_rows(kt), :]
        code = jnp.where((key >> 16) == r16, mid_code((key >> 1) & 0x7FFF), -jnp.inf)
        lo_ref[tile_rows(kt), :] = code.astype(jnp.bfloat16)
        return carry
    lax.fori_loop(0, n_tiles, code_tile, 0)

    def mid_step(it, v):
        cand = v | (1 << (14 - it))
        cnt = above + count_packed_ge(lo_ref, mid_code(cand).astype(jnp.bfloat16))
        return jnp.where(cnt >= topk, cand, v)
    v15 = lax.fori_loop(0, 15, mid_step, jnp.zeros((1, QB), jnp.int32))
    thr = (r16 << 16) | (v15 << 1)
    thr = jnp.where(count_ge(thr | 1) >= topk, thr | 1, thr)

    cnt_gt = count_ge(thr + 1)
    cnt_ge = count_ge(thr)
    need = topk - cnt_gt
    tie = (cnt_ge - cnt_gt > need) & (thr > KEY_NEG_INF)

    @pl.when(jnp.max(tie.astype(jnp.int32)) > 0)
    def _():
        def count_eq_below(cand):
            def ind(kt):
                idx = lax.broadcasted_iota(jnp.int32, (QB, QB), 0) + kt * QB
                return jnp.where((keys_ref[tile_rows(kt), :] == thr) & (idx < cand), 1, 0)
            return count(ind)

        def idx_step(it, r):
            cand = r | (1 << (15 - it))
            return jnp.where(count_eq_below(cand) < need, cand, r)
        last = lax.fori_loop(0, 16, idx_step, jnp.zeros((1, QB), jnp.int32))

        def drop(kt, carry):
            blk = keys_ref[tile_rows(kt), :]
            idx = lax.broadcasted_iota(jnp.int32, (QB, QB), 0) + kt * QB
            keys_ref[tile_rows(kt), :] = jnp.where(tie & (blk == thr) & (idx > last), INT_MIN, blk)
            return carry
        lax.fori_loop(0, n_tiles, drop, 0)

    def logits(kt, h):
        band = jnp.clip(kt - (i - 2), 0, 2)
        kp = k_ref[0, tile_rows(kt), (h // 2) * 128:(h // 2) * 128 + 128]
        s = jnp.dot(kp, head_rows(qT_ref, h), preferred_element_type=jnp.float32)
        return s + msk_ref[...] + tab_ref[band, h]

    def set_mask(kt):
        msk_ref[...] = jnp.where(keys_ref[tile_rows(kt), :] >= thr, 0.0, NEG)

    def max_tile(kt, carry):
        set_mask(kt)
        for h in range(ATTN_HEADS):
            s = logits(kt, h)
            mp_ref[h] = jnp.maximum(mp_ref[h], jnp.max(s.reshape(QB // 8, 8, QB), axis=0))
        return carry

    seq = kn_ref.shape[2]
    in_extent = lax.broadcasted_iota(jnp.int32, (ATTN_HEADS, seq), 1) < n_tiles * QB
    k_max = jnp.max(jnp.where(in_extent, kn_ref[0], 0.0), axis=1, keepdims=True)
    spread = jnp.zeros((1, QB), jnp.float32)
    for h in range(ATTN_HEADS):
        qh = qT_ref[0, h * ATTN_HEAD_DIM:(h + 1) * ATTN_HEAD_DIM, :].astype(jnp.float32)
        reach = jnp.sqrt(jnp.sum(qh * qh, axis=0, keepdims=True) * k_max[h:h + 1, :]) * NORM_SLACK
        m_ref[h:h + 1, :] = reach + bst_ref[0, h:h + 1, :]
        spread = jnp.maximum(spread, 2.0 * reach + bst_ref[1, h:h + 1, :])
    bound_ok = jnp.max(spread) <= MAX_SHIFT_ERROR

    @pl.when(jnp.logical_not(bound_ok))
    def _():
        mp_ref[...] = jnp.full(mp_ref.shape, NEG, jnp.float32)
        lax.fori_loop(0, n_tiles, max_tile, 0)
        for h in range(ATTN_HEADS):
            m_ref[h:h + 1, :] = jnp.max(mp_ref[h], axis=0, keepdims=True)
    m = [m_ref[h:h + 1, :] for h in range(ATTN_HEADS)]

    def exp_tile(kt, carry):
        set_mask(kt)
        for h in range(ATTN_HEADS):
            p_ref[h] = jnp.exp2(logits(kt, h) - m[h]).astype(jnp.bfloat16)
        for h in range(ATTN_HEADS):
            va = vT_ref[0, h * VROWS:(h + 1) * VROWS, tile_rows(kt)]
            acc_ref[h * VROWS:(h + 1) * VROWS, :] += jnp.dot(va, p_ref[h], preferred_element_type=jnp.float32)
        return carry

    acc_ref[...] = jnp.zeros(acc_ref.shape, jnp.float32)
    lax.fori_loop(0, n_tiles, exp_tile, 0)

    outs = [acc_ref[h * VROWS:h * VROWS + ATTN_HEAD_DIM, :] / acc_ref[h * VROWS + ATTN_HEAD_DIM:h * VROWS + ATTN_HEAD_DIM + 1, :]
            for h in range(ATTN_HEADS)]
    o_ref[0] = jnp.concatenate(outs, axis=0).T


def _dsa_attention(qT, qiT, wT, k, vT, ki2, kn2, rel_bias):
    bsz, _, seq = qT.shape
    topk = min(TOPK_MAX, seq // 4)
    assert seq % QB == 0 and topk <= QB
    assert seq // 16 <= 256
    b2 = rel_bias.astype(jnp.float32) * LOG2E
    bias_stats = jnp.stack([jnp.max(b2, axis=0), jnp.max(b2, axis=0) - jnp.min(b2, axis=0)])
    bias_stats = jnp.broadcast_to(bias_stats[:, :, None], (2, ATTN_HEADS, QB))
    return pl.pallas_call(
        functools.partial(_attn_kernel, topk=topk),
        grid=(bsz, seq // QB),
        in_specs=[
            pl.BlockSpec((1, ATTN_WIDTH, QB), lambda b, i: (b, 0, i)),
            pl.BlockSpec((1, IDX_HEADS * IDX_DIM, QB), lambda b, i: (b, 0, i)),
            pl.BlockSpec((1, IDX_HEADS, QB), lambda b, i: (b, 0, i)),
            pl.BlockSpec((1, seq, ATTN_WIDTH), lambda b, i: (b, 0, 0)),
            pl.BlockSpec((1, ATTN_HEADS * VROWS, seq), lambda b, i: (b, 0, 0)),
            pl.BlockSpec((1, seq, 128), lambda b, i: (b, 0, 0)),
            pl.BlockSpec((1, ATTN_HEADS, seq), lambda b, i: (b, 0, 0)),
            pl.BlockSpec((3, ATTN_HEADS, QB, QB), lambda b, i: (0, 0, 0, 0)),
            pl.BlockSpec((2, ATTN_HEADS, QB), lambda b, i: (0, 0, 0)),
        ],
        out_specs=pl.BlockSpec((1, QB, ATTN_WIDTH), lambda b, i: (b, i, 0)),
        out_shape=jax.ShapeDtypeStruct((bsz, seq, ATTN_WIDTH), jnp.float32),
        scratch_shapes=[
            pltpu.VMEM((seq, QB), jnp.int32),
            pltpu.VMEM((seq, QB), jnp.bfloat16),
            pltpu.VMEM((seq, QB), jnp.bfloat16),
            pltpu.VMEM((QB, QB), jnp.float32),
            pltpu.VMEM((ATTN_HEADS, QB, QB), jnp.bfloat16),
            pltpu.VMEM((ATTN_HEADS * VROWS, QB), jnp.float32),
            pltpu.VMEM((ATTN_HEADS, 8, QB), jnp.float32),
            pltpu.VMEM((ATTN_HEADS, QB), jnp.float32),
        ],
        compiler_params=pltpu.CompilerParams(dimension_semantics=("parallel", "arbitrary"),
                                             vmem_limit_bytes=VMEM_LIMIT),
        name="dsa_attention",
    )(qT, qiT, wT, k, vT, ki2, kn2, _bias_tables(rel_bias), bias_stats)


def _ssd_kernel(xbc_ref, z_ref, sm_ref, cw_ref, cb_ref, dtb_ref, a_ref, dsk_ref, nw_ref, y_ref, prev_ref, st_ref):
    q = SSD_CHUNK
    bf = jnp.bfloat16

    @pl.when(pl.program_id(1) == 0)
    def _():
        prev_ref[...] = jnp.zeros(prev_ref.shape, jnp.float32)
        st_ref[...] = jnp.zeros(st_ref.shape, jnp.float32)

    cur = xbc_ref[...]
    prev = prev_ref[...]
    row = lax.broadcasted_iota(jnp.int32, cur.shape, 0)
    acc = cur * cw_ref[CONV_WIDTH - 1:CONV_WIDTH, :] + cb_ref[...]
    for s in range(1, CONV_WIDTH):
        shifted = jnp.where(row >= s, pltpu.roll(cur, s, 0), pltpu.roll(prev, s, 0))
        acc = acc + shifted * cw_ref[CONV_WIDTH - 1 - s:CONV_WIDTH - s, :]
    prev_ref[...] = cur
    u = acc * jax.nn.sigmoid(acc)
    xs = u[:, :SSM_INNER]
    bm = u[:, SSM_INNER:SSM_INNER + SSM_GROUPS * SSM_STATE].astype(bf)
    cm = u[:, SSM_INNER + SSM_GROUPS * SSM_STATE:].astype(bf)

    t = sm_ref[...] + dtb_ref[...]
    dt = jnp.maximum(t, 0.0) + jnp.log1p(jnp.exp(-jnp.abs(t)))
    ii = lax.broadcasted_iota(jnp.int32, (q, q), 0)
    jj = lax.broadcasted_iota(jnp.int32, (q, q), 1)
    causal = ii >= jj
    acum = jnp.dot(causal.astype(jnp.float32), dt * a_ref[...], preferred_element_type=jnp.float32, precision=HIGHEST)
    acum_t = acum.T
    dt_t = dt.T
    ea = jnp.exp(acum)
    last = acum[q - 1:q, :]
    decay = jnp.exp(last - acum) * dt
    ea_last = jnp.exp(last)

    lane_hi = lax.broadcasted_iota(jnp.int32, (q, 128), 1) >= SSM_HEAD_DIM
    row_hi = lax.broadcasted_iota(jnp.int32, (128, SSM_STATE), 0) >= SSM_HEAD_DIM

    def pair_cols(v, e):
        c0, c1 = SMALL_DT + e, SMALL_DT + e + 1
        return jnp.where(lane_hi, v[:, c1:c1 + 1], v[:, c0:c0 + 1])

    for g in range(SSM_GROUPS):
        bg = bm[:, g * SSM_STATE:(g + 1) * SSM_STATE]
        cg = cm[:, g * SSM_STATE:(g + 1) * SSM_STATE]
        cb = lax.dot_general(cg, bg, NT, preferred_element_type=jnp.float32)
        for k in range(g * 4, g * 4 + 4):
            e = 2 * k
            x_pair = xs[:, k * 128:(k + 1) * 128]
            halves = []
            for h in (e, e + 1):
                c = SMALL_DT + h
                seg = acum[:, c:c + 1] - acum_t[c:c + 1, :]
                w = cb * jnp.exp(jnp.where(causal, seg, -jnp.inf)) * dt_t[c:c + 1, :]
                halves.append(jnp.dot(w.astype(bf), x_pair.astype(bf), preferred_element_type=jnp.float32))
            y_pair = jnp.where(lane_hi, halves[1], halves[0])
            state = st_ref[k]
            y_pair = y_pair + lax.dot_general(cg, state.astype(bf), NT, preferred_element_type=jnp.float32) * pair_cols(ea, e)
            y_ref[:, k * 128:(k + 1) * 128] = y_pair
            xd_t = (x_pair * pair_cols(decay, e)).T.astype(bf)
            c0 = SMALL_DT + e
            keep = jnp.where(row_hi, ea_last[:, c0 + 1:c0 + 2], ea_last[:, c0:c0 + 1])
            st_ref[k] = state * keep + jnp.dot(xd_t, bg, preferred_element_type=jnp.float32)

    y = (y_ref[...] + dsk_ref[...] * xs) * (z_ref[...] * jax.nn.sigmoid(z_ref[...]))
    half = SSM_INNER // SSM_GROUPS
    for g in range(SSM_GROUPS):
        yg = y[:, g * half:(g + 1) * half]
        yg = yg * lax.rsqrt(jnp.mean(yg * yg, axis=-1, keepdims=True) + EPS)
        y_ref[:, g * half:(g + 1) * half] = yg * nw_ref[:, g * half:(g + 1) * half]


def _mamba2_ssd(proj, bsz, seq, conv_w, conv_b, dt_bias, a_log, d_skip, norm_w):
    q = SSD_CHUNK
    nc = seq // q
    lane_row = lambda v: jnp.zeros((1, 128), jnp.float32).at[0, SMALL_DT:SMALL_DT + SSM_HEADS].set(v)
    const = lambda shape: pl.BlockSpec(shape, lambda b, c: (0,) * len(shape))
    return pl.pallas_call(
        _ssd_kernel,
        grid=(bsz, nc),
        in_specs=[pl.BlockSpec((q, CONV_CH), lambda b, c: (b * nc + c, COL_XBC // CONV_CH)),
                  pl.BlockSpec((q, SSM_INNER), lambda b, c: (b * nc + c, COL_Z // SSM_INNER)),
                  pl.BlockSpec((q, 128), lambda b, c: (b * nc + c, COL_SMALL // 128)),
                  const((CONV_WIDTH, CONV_CH)), const((1, CONV_CH)), const((1, 128)), const((1, 128)),
                  const((1, SSM_INNER)), const((1, SSM_INNER))],
        out_specs=pl.BlockSpec((q, SSM_INNER), lambda b, c: (b * nc + c, 0)),
        out_shape=jax.ShapeDtypeStruct((bsz * seq, SSM_INNER), jnp.float32),
        scratch_shapes=[pltpu.VMEM((q, CONV_CH), jnp.float32),
                        pltpu.VMEM((SSM_HEADS // 2, 2 * SSM_HEAD_DIM, SSM_STATE), jnp.float32)],
        compiler_params=pltpu.CompilerParams(dimension_semantics=("parallel", "arbitrary"),
                                             vmem_limit_bytes=VMEM_LIMIT),
        name="mamba2_ssd",
    )(proj, proj, proj, conv_w, conv_b.reshape(1, CONV_CH), lane_row(dt_bias), lane_row(-jnp.exp(a_log)),
      jnp.repeat(d_skip, SSM_HEAD_DIM).reshape(1, SSM_INNER), norm_w.reshape(1, SSM_INNER))


def _mix_out_kernel(a_ref, s_ref, gl_ref, x_ref, gm_ref, wo_ref, ws_ref, wout_ref,
                    nf_ref, scf_ref, shf_ref, wr_ref, br_ref, xo_ref, h_ref, rt_ref, cnt_ref):
    bf = jnp.bfloat16
    ya = jnp.dot(a_ref[...].astype(bf), wo_ref[...], preferred_element_type=jnp.float32)
    ys = jnp.dot(s_ref[...].astype(bf), ws_ref[...], preferred_element_type=jnp.float32)
    mixed = jax.nn.sigmoid(gl_ref[:, :D_MODEL]) * ya + jax.nn.sigmoid(gl_ref[:, D_MODEL:]) * ys
    x = x_ref[...] + gm_ref[0] * jnp.dot(mixed.astype(bf), wout_ref[...], preferred_element_type=jnp.float32)
    xo_ref[...] = x
    y = x * lax.rsqrt(jnp.mean(x * x, axis=-1, keepdims=True) + EPS) * nf_ref[...]
    h = y * (1.0 + scf_ref[0]) + shf_ref[0]
    h_hi = h.astype(bf)
    hb = pltpu.bitcast(h_hi.astype(jnp.float32), jnp.int32)
    half = D_MODEL // 2
    h_ref[...] = (hb[:, :half] & jnp.int32(-65536)) | lax.shift_right_logical(hb[:, half:], 16)
    h_lo = (h - h_hi.astype(jnp.float32)).astype(bf)
    dot = functools.partial(jnp.dot, preferred_element_type=jnp.float32)
    lg = dot(h_hi, wr_ref[0]) + (dot(h_lo, wr_ref[0]) + dot(h_hi, wr_ref[1])) + br_ref[...]

    tm = lg.shape[0]
    work = lg.T[:N_EXPERTS, :]
    row = lax.broadcasted_iota(jnp.int32, work.shape, 0).astype(jnp.float32)
    vals, eids, hits = [], [], []
    for _ in range(TOP_K):
        mx = jnp.max(work, axis=0, keepdims=True)
        ix = jnp.min(jnp.where(work == mx, row, float(N_EXPERTS)), axis=0, keepdims=True)
        vals.append(mx)
        eids.append(ix)
        hits.append(row == ix)
        work = jnp.where(hits[-1], -jnp.inf, work)
    ex = [jnp.exp(v - vals[0]) for v in vals]
    den = (ex[0] + ex[1]) + (ex[2] + ex[3])

    @pl.when(pl.program_id(0) == 0)
    def _():
        cnt_ref[...] = jnp.zeros(cnt_ref.shape, jnp.float32)
    chosen = jnp.zeros(work.shape, jnp.float32)
    for hit in hits:
        chosen = jnp.where(hit, 1.0, chosen)
    earlier = lax.broadcasted_iota(jnp.int32, (tm, tm), 0) < lax.broadcasted_iota(jnp.int32, (tm, tm), 1)
    before = dot(chosen.astype(bf), earlier.astype(bf)) + cnt_ref[:, 0:1]
    cnt_ref[...] = cnt_ref[...] + jnp.sum(chosen, axis=1, keepdims=True)

    slot = lax.broadcasted_iota(jnp.int32, rt_ref.shape, 0)
    route = jnp.zeros(rt_ref.shape, jnp.float32)
    for k, hit in enumerate(hits):
        rank = jnp.sum(jnp.where(hit, before, 0.0), axis=0, keepdims=True)
        route = jnp.where(slot == k, eids[k], route)
        route = jnp.where(slot == TOP_K + k, ex[k] / den, route)
        route = jnp.where(slot == 2 * TOP_K + k, rank, route)
    rt_ref[...] = route


def _mix_out(attn2, ssd2, proj, x2, g_m, w_attn_o, w_ssm_o, w_out, norm_ffn, sc_f, sh_f, w_router, b_router, seq, tm=512):
    t, d = x2.shape
    per_b = seq // tm
    bf = jnp.bfloat16
    const = lambda shape: pl.BlockSpec(shape, lambda i: (0,) * len(shape))
    perb = pl.BlockSpec((1, 1, d), lambda i: (i // per_b, 0, 0))
    wr = jnp.pad(w_router.astype(jnp.float32), ((0, 0), (0, 128 - N_EXPERTS)))
    wr_hi = wr.astype(bf)
    wr = jnp.stack([wr_hi, (wr - wr_hi.astype(jnp.float32)).astype(bf)])
    br =jnp.pad(b_router, (0, 128 - N_EXPERTS)).reshape(1, 128)
    return pl.pallas_call(
        _mix_out_kernel,
        grid=(t // tm,),
        in_specs=[pl.BlockSpec((tm, ATTN_WIDTH), lambda i: (i, 0)),
                  pl.BlockSpec((tm, SSM_INNER), lambda i: (i, 0)),
                  pl.BlockSpec((tm, 2 * d), lambda i: (i, COL_GATE // (2 * d))),
                  pl.BlockSpec((tm, d), lambda i: (i, 0)),
                  perb,
                  const((ATTN_WIDTH, d)), const((SSM_INNER, d)), const((d, d)),
                  const((1, d)), perb, perb, const((2, d, 128)), const((1, 128))],
        out_specs=[pl.BlockSpec((tm, d), lambda i: (i, 0)),
                   pl.BlockSpec((tm, d // 2), lambda i: (i, 0)),
                   pl.BlockSpec((ROUTE_ROWS, tm), lambda i: (0, i)),
                   pl.BlockSpec((N_EXPERTS, 128), lambda i: (0, 0))],
        out_shape=[jax.ShapeDtypeStruct((t, d), jnp.float32),
                   jax.ShapeDtypeStruct((t, d // 2), jnp.int32),
                   jax.ShapeDtypeStruct((ROUTE_ROWS, t), jnp.float32),
                   jax.ShapeDtypeStruct((N_EXPERTS, 128), jnp.float32)],
        compiler_params=pltpu.CompilerParams(dimension_semantics=("arbitrary",), vmem_limit_bytes=VMEM_LIMIT),
        name="mix_out",
    )(attn2, ssd2, proj, x2, g_m[:, None, :], w_attn_o.astype(bf), w_ssm_o.astype(bf), w_out.astype(bf),
      norm_ffn.reshape(1, d), sc_f[:, None, :], sh_f[:, None, :], wr, br)


def _moe_kernel(be_ref, nb_ref, x_ref, g_ref, wgu_ref, bgu_ref, wdn_ref, bdn_ref, o_ref, wgu_bf, wdn_bf):
    i = pl.program_id(0)

    @pl.when((i == 0) | (be_ref[i] != be_ref[jnp.maximum(i - 1, 0)]))
    def _():
        wgu_bf[...] = wgu_ref[0, 0].astype(jnp.bfloat16)
        wdn_bf[...] = wdn_ref[0, 0].astype(jnp.bfloat16)

    @pl.when(i < nb_ref[0])
    def _():
        words = x_ref[...]
        x_hi = pltpu.bitcast(words & jnp.int32(-65536), jnp.float32).astype(jnp.bfloat16)
        x_lo = pltpu.bitcast(words << 16, jnp.float32).astype(jnp.bfloat16)
        x = jnp.concatenate([x_hi, x_lo], axis=1)
        gu = jnp.dot(x, wgu_bf[...], preferred_element_type=jnp.float32) + bgu_ref[0, 0]
        g = jnp.minimum(gu[:, :D_EXPERT], SWIGLU_LIMIT)
        u = jnp.clip(gu[:, D_EXPERT:], -SWIGLU_LIMIT, SWIGLU_LIMIT)
        act = (u + 1.0) * (g * jax.nn.sigmoid(SWIGLU_ALPHA * g))
        out = jnp.dot(act.astype(jnp.bfloat16), wdn_bf[...], preferred_element_type=jnp.float32) + bdn_ref[0, 0]
        o_ref[...] = (out * g_ref[...]).astype(o_ref.dtype)

    @pl.when(i >= nb_ref[0])
    def _():
        o_ref[...] = jnp.zeros_like(o_ref)


def _moe_ffn(xs, row_gate, blk_exp, n_used, w_gu, b_gu, w_dn, b_dn, layer):
    n_rows = xs.shape[0]
    d = D_MODEL
    tm = MOE_TM
    grid_spec = pltpu.PrefetchScalarGridSpec(
        num_scalar_prefetch=2,
        grid=(n_rows // tm,),
        in_specs=[pl.BlockSpec((tm, d // 2), lambda i, be, nb: (i, 0)),
                  pl.BlockSpec((tm, 1), lambda i, be, nb: (i, 0)),
                  pl.BlockSpec((1, 1, d, 2 * D_EXPERT), lambda i, be, nb: (layer, be[i], 0, 0)),
                  pl.BlockSpec((1, 1, 1, 2 * D_EXPERT), lambda i, be, nb: (layer, be[i], 0, 0)),
                  pl.BlockSpec((1, 1, D_EXPERT, d), lambda i, be, nb: (layer, be[i], 0, 0)),
                  pl.BlockSpec((1, 1, 1, d), lambda i, be, nb: (layer, be[i], 0, 0))],
        out_specs=pl.BlockSpec((tm, d), lambda i, be, nb: (i, 0)),
        scratch_shapes=[pltpu.VMEM((d, 2 * D_EXPERT), jnp.bfloat16), pltpu.VMEM((D_EXPERT, d), jnp.bfloat16)],
    )
    return pl.pallas_call(
        _moe_kernel,
        grid_spec=grid_spec,
        out_shape=jax.ShapeDtypeStruct((n_rows, d), jnp.bfloat16),
        compiler_params=pltpu.CompilerParams(dimension_semantics=("arbitrary",), vmem_limit_bytes=VMEM_LIMIT),
        name="moe_ffn",
    )(blk_exp, n_used, xs, row_gate[:, None], w_gu, b_gu[:, :, None, :], w_dn, b_dn[:, :, None, :])


def _moe(h2, route, expert_counts, w_gu, b_gu, w_dn, b_dn, layer):
    t = h2.shape[0]
    d = D_MODEL
    tm = MOE_TM
    i32 = jnp.int32
    experts = jnp.arange(N_EXPERTS, dtype=i32)
    top_idx = route[:TOP_K].astype(i32)
    rank = route[2 * TOP_K:3 * TOP_K].astype(i32)
    n_assign = t * TOP_K
    n_rows = n_assign + N_EXPERTS * tm
    e_flat = top_idx.T.reshape(n_assign)
    gates = route[TOP_K:2 * TOP_K].T.reshape(n_assign)
    counts = expert_counts[:, 0].astype(i32)
    padded = (counts + tm - 1) // tm * tm
    pad_start = jnp.cumsum(padded) - padded
    dest = rank + jnp.sum(jnp.where(top_idx[..., None] == experts, pad_start, 0), axis=-1)
    dest = dest.reshape(-1)
    filler_exp = jnp.repeat(experts, tm)
    filler_key = jnp.where(jnp.tile(jnp.arange(tm, dtype=i32), N_EXPERTS) < jnp.repeat(padded - counts, tm),
                           filler_exp, N_EXPERTS)
    keys = jnp.concatenate([e_flat, filler_key])
    gate_in = jnp.concatenate([gates, jnp.zeros((N_EXPERTS * tm,), jnp.float32)])
    rows = jnp.arange(n_rows, dtype=i32)
    row_key, row_src, row_gate = lax.sort((keys, rows, gate_in), num_keys=1)
    row_tok = jnp.where(row_src < n_assign, row_src // TOP_K, rows % t)
    blk_exp = jnp.minimum(row_key[::tm], N_EXPERTS - 1)
    n_used = (jnp.sum(padded, keepdims=True) // tm).astype(i32)
    out = _moe_ffn(h2[row_tok], row_gate, blk_exp, n_used, w_gu, b_gu, w_dn, b_dn, layer)
    return out[dest].reshape(TOP_K, t, d)


def _combine_kernel(p_ref, x_ref, g_ref, o_ref):
    f32 = jnp.float32
    y = (p_ref[0].astype(f32) + p_ref[1].astype(f32)) + (p_ref[2].astype(f32) + p_ref[3].astype(f32))
    o_ref[...] = x_ref[...] + g_ref[0] * y


def _combine(parts, x2, g_f, seq, tm=512):
    t, d = x2.shape
    per_b = seq // tm
    return pl.pallas_call(
        _combine_kernel,
        grid=(t // tm,),
        in_specs=[pl.BlockSpec((TOP_K, tm, d), lambda i: (0, i, 0)),
                  pl.BlockSpec((tm, d), lambda i: (i, 0)),
                  pl.BlockSpec((1, 1, d), lambda i: (i // per_b, 0, 0))],
        out_specs=pl.BlockSpec((tm, d), lambda i: (i, 0)),
        out_shape=jax.ShapeDtypeStruct((t, d), jnp.float32),
        compiler_params=pltpu.CompilerParams(dimension_semantics=("parallel",), vmem_limit_bytes=VMEM_LIMIT),
        name="moe_combine",
    )(parts, x2, g_f[:, None, :])


def kernel(x, c, rel_bias, w_ada, b_ada, norm_mix, norm_ffn, w_in, kv_norm, w_kv_up, q_norm, k_norm,
           idx_k_ln_w, idx_k_ln_b, w_attn_o, conv_w, conv_b, dt_bias, a_log, d_skip, ssm_norm, w_ssm_o,
           w_out, w_router, b_router, w_gu, b_gu, w_dn, b_dn):
    bsz, seq, d = x.shape
    t = bsz * seq
    cond = jax.nn.silu(c)
    x2 = x.reshape(t, d)
    for l in range(DEPTH):
        mod = cond @ w_ada[l] + b_ada[l]
        sh_m, sc_m, g_m, sh_f, sc_f, g_f = jnp.split(mod, 6, axis=-1)
        proj = _in_proj(x2, norm_mix[l], sc_m, sh_m, _pack_w_in(w_in[l]), seq)
        qT, k, vT, qiT, ki2, wT, kn2 = _prep(proj, bsz, seq, q_norm[l], kv_norm[l], w_kv_up[l], k_norm[l],
                                             idx_k_ln_w[l], idx_k_ln_b[l])
        attn = _dsa_attention(qT, qiT, wT, k, vT, ki2, kn2, rel_bias)
        y_ssd = _mamba2_ssd(proj, bsz, seq, conv_w[l], conv_b[l], dt_bias[l], a_log[l], d_skip[l], ssm_norm[l])
        x2, h2, route, expert_counts = _mix_out(attn.reshape(t, ATTN_WIDTH), y_ssd, proj, x2, g_m,
                                                w_attn_o[l], w_ssm_o[l], w_out[l], norm_ffn[l], sc_f, sh_f,
                                                w_router[l], b_router[l], seq)
        parts = _moe(h2, route, expert_counts, w_gu, b_gu, w_dn, b_dn, l)
        x2 = _combine(parts, x2, g_f, seq)
    return x2.reshape(bsz, seq, d)
```

```python
import functools
import math

import jax
import jax.numpy as jnp
import numpy as np
from jax import lax
from jax.experimental import pallas as pl
from jax.experimental.pallas import tpu as pltpu

D_MODEL = 1024
DEPTH = 2
ATTN_HEADS = 8
ATTN_HEAD_DIM = 64
ATTN_WIDTH = ATTN_HEADS * ATTN_HEAD_DIM
KV_RANK = 256
IDX_HEADS = 8
IDX_DIM = 64
TOPK_MAX = 256
N_BUCKETS = 32
MAX_DISTANCE = 128
SSM_HEADS = 16
SSM_HEAD_DIM = 64
SSM_INNER = SSM_HEADS * SSM_HEAD_DIM
SSM_GROUPS = 2
SSM_STATE = 128
CONV_WIDTH = 4
CONV_CH = SSM_INNER + 2 * SSM_GROUPS * SSM_STATE
SSD_CHUNK = 128
N_EXPERTS = 32
TOP_K = 4
D_EXPERT = D_MODEL
SWIGLU_LIMIT = 7.0
SWIGLU_ALPHA = 1.702
EPS = 1e-6

COL_Q = 0
COL_KV = 512
COL_QI = 768
COL_SMALL = 1280
COL_XBC = 1536
COL_Z = 3072
COL_GATE = 4096
PROJ_COLS = 6144
PREP_COLS = 1408
SMALL_KI, SMALL_WI, SMALL_DT = 0, 64, 72

QB = 256
VROWS = 80
INT_MIN = -2 ** 31
KEY_NEG_INF = (0xFF800000 ^ 0x7FFFFFFF) - 2 ** 32
NEG = -1e30
TINY = 2.0 ** -126
LOG2E = math.log2(math.e)
NORM_SLACK = 1.02
MAX_SHIFT_ERROR = 96.0
VMEM_LIMIT = 56 * 1024 * 1024
MOE_TM = 512
ROUTE_ROWS = 16
HIGHEST = lax.Precision.HIGHEST
NT = (((1,), (1,)), ((), ()))


def _pack_w_in(w):
    o = np.cumsum((0, ATTN_WIDTH, KV_RANK, IDX_HEADS * IDX_DIM, IDX_DIM, IDX_HEADS, SSM_INNER, CONV_CH, SSM_HEADS, 2 * D_MODEL))
    q, kv, qi, ki, wi, z, xbc, dt, gate = (w[:, int(o[n]):int(o[n + 1])] for n in range(9))
    zeros = lambda n: jnp.zeros((w.shape[0], n), w.dtype)
    small = jnp.concatenate([ki, wi, dt, zeros(128 - 88)], axis=1)
    packed = jnp.concatenate([q, kv, qi, small, zeros(COL_XBC - PREP_COLS), xbc, z, gate], axis=1)
    assert packed.shape[1] == PROJ_COLS
    return packed.astype(jnp.bfloat16)


def _in_proj_kernel(x_ref, g_ref, sc_ref, sh_ref, w_ref, o_ref, h_ref):
    @pl.when(pl.program_id(1) == 0)
    def _():
        x = x_ref[...]
        y = x * lax.rsqrt(jnp.mean(x * x, axis=-1, keepdims=True) + EPS) * g_ref[...]
        h_ref[...] = (y * (1.0 + sc_ref[0]) + sh_ref[0]).astype(jnp.bfloat16)
    o_ref[...] = jnp.dot(h_ref[...], w_ref[...], preferred_element_type=jnp.float32)


def _in_proj(x2, gain, sc, sh, w_packed, seq, tm=1024, tn=1024):
    t, d = x2.shape
    per_b = seq // tm
    return pl.pallas_call(
        _in_proj_kernel,
        grid=(t // tm, PROJ_COLS // tn),
        in_specs=[pl.BlockSpec((tm, d), lambda i, j: (i, 0)),
                  pl.BlockSpec((1, d), lambda i, j: (0, 0)),
                  pl.BlockSpec((1, 1, d), lambda i, j: (i // per_b, 0, 0)),
                  pl.BlockSpec((1, 1, d), lambda i, j: (i // per_b, 0, 0)),
                  pl.BlockSpec((d, tn), lambda i, j: (0, j))],
        out_specs=pl.BlockSpec((tm, tn), lambda i, j: (i, j)),
        out_shape=jax.ShapeDtypeStruct((t, PROJ_COLS), jnp.float32),
        scratch_shapes=[pltpu.VMEM((tm, d), jnp.bfloat16)],
        compiler_params=pltpu.CompilerParams(dimension_semantics=("parallel", "arbitrary"),
                                             vmem_limit_bytes=VMEM_LIMIT),
        name="in_proj",
    )(x2, gain.reshape(1, d), sc[:, None, :], sh[:, None, :], w_packed)


def _head_rms_t(xt):
    x3 = xt.reshape(ATTN_HEADS, ATTN_HEAD_DIM, xt.shape[1])
    return lax.rsqrt(jnp.mean(x3 * x3, axis=1, keepdims=True) + EPS)


def _prep_kernel(p_ref, qg_ref, kvg_ref, wkv_ref, kg_ref, lng_ref, lnb_ref,
                 qT_ref, k_ref, vT_ref, qiT_ref, ki_ref, wT_ref, kn2_ref):
    n = p_ref.shape[0]
    q = p_ref[:, COL_Q:COL_Q + ATTN_WIDTH]
    lat = p_ref[:, COL_KV:COL_KV + KV_RANK]
    qi = p_ref[:, COL_QI:COL_QI + IDX_HEADS * IDX_DIM]
    sm = p_ref[:, COL_SMALL:COL_SMALL + 128]

    scale = ATTN_HEAD_DIM ** -0.5 * LOG2E
    qt = q.T
    qn = qt.reshape(ATTN_HEADS, ATTN_HEAD_DIM, n) * _head_rms_t(qt)
    qT_ref[0] = (qn.reshape(ATTN_WIDTH, n) * qg_ref[...] * scale).astype(jnp.bfloat16)

    latn = lat * lax.rsqrt(jnp.mean(lat * lat, axis=-1, keepdims=True) + EPS) * kvg_ref[...]
    kv = jnp.dot(latn.astype(jnp.bfloat16), wkv_ref[...], preferred_element_type=jnp.float32)
    kt = kv[:, :ATTN_WIDTH].T
    kn = (kt.reshape(ATTN_HEADS, ATTN_HEAD_DIM, n) * _head_rms_t(kt)).reshape(ATTN_WIDTH, n) * kg_ref[...]
    k_ref[0] = kn.T.astype(jnp.bfloat16)
    kn3 = kn.reshape(ATTN_HEADS, ATTN_HEAD_DIM, n)
    kn2_ref[0] = jnp.sum(kn3 * kn3, axis=1)
    vt = kv[:, ATTN_WIDTH:].T.reshape(ATTN_HEADS, ATTN_HEAD_DIM, n)
    ones = jnp.ones((ATTN_HEADS, VROWS - ATTN_HEAD_DIM, n), jnp.float32)
    vT_ref[0] = jnp.concatenate([vt, ones], axis=1).reshape(ATTN_HEADS * VROWS, n).astype(jnp.bfloat16)

    qiT_ref[0] = (qi * (IDX_DIM ** -0.5)).T.astype(jnp.bfloat16)

    lane = lax.broadcasted_iota(jnp.int32, sm.shape, 1)
    kid = jnp.where(lane < IDX_DIM, sm, pltpu.roll(sm, IDX_DIM, 1))
    mu = jnp.mean(kid, axis=-1, keepdims=True)
    var = jnp.mean(jnp.square(kid - mu), axis=-1, keepdims=True)
    ki_ref[0] = ((kid - mu) * lax.rsqrt(var + EPS) * lng_ref[...] + lnb_ref[...]).astype(jnp.bfloat16)

    wT_ref[0] = sm.T[SMALL_WI:SMALL_WI + IDX_HEADS, :] * (IDX_HEADS ** -0.5)


def _prep(proj, bsz, seq, q_norm, kv_norm, w_kv_up, k_norm, ln_w, ln_b, tp=512):
    nb = seq // tp
    tile8 = lambda g: jnp.tile(g, ATTN_HEADS).reshape(ATTN_WIDTH, 1)
    const = lambda shape: pl.BlockSpec(shape, lambda b, i: (0,) * len(shape))
    bf = jnp.bfloat16
    return pl.pallas_call(
        _prep_kernel,
        grid=(bsz, nb),
        in_specs=[pl.BlockSpec((tp, PREP_COLS), lambda b, i: (b * nb + i, 0)),
                  const((ATTN_WIDTH, 1)), const((1, KV_RANK)), const((KV_RANK, 2 * ATTN_WIDTH)),
                  const((ATTN_WIDTH, 1)), const((1, 128)), const((1, 128))],
        out_specs=[pl.BlockSpec((1, ATTN_WIDTH, tp), lambda b, i: (b, 0, i)),
                   pl.BlockSpec((1, tp, ATTN_WIDTH), lambda b, i: (b, i, 0)),
                   pl.BlockSpec((1, ATTN_HEADS * VROWS, tp), lambda b, i: (b, 0, i)),
                   pl.BlockSpec((1, ATTN_WIDTH, tp), lambda b, i: (b, 0, i)),
                   pl.BlockSpec((1, tp, 128), lambda b, i: (b, i, 0)),
                   pl.BlockSpec((1, IDX_HEADS, tp), lambda b, i: (b, 0, i)),
                   pl.BlockSpec((1, ATTN_HEADS, tp), lambda b, i: (b, 0, i))],
        out_shape=[jax.ShapeDtypeStruct((bsz, ATTN_WIDTH, seq), bf),
                   jax.ShapeDtypeStruct((bsz, seq, ATTN_WIDTH), bf),
                   jax.ShapeDtypeStruct((bsz, ATTN_HEADS * VROWS, seq), bf),
                   jax.ShapeDtypeStruct((bsz, ATTN_WIDTH, seq), bf),
                   jax.ShapeDtypeStruct((bsz, seq, 128), bf),
                   jax.ShapeDtypeStruct((bsz, IDX_HEADS, seq), jnp.float32),
                   jax.ShapeDtypeStruct((bsz, ATTN_HEADS, seq), jnp.float32)],
        compiler_params=pltpu.CompilerParams(dimension_semantics=("parallel", "parallel"),
                                             vmem_limit_bytes=VMEM_LIMIT),
        name="attn_prep",
    )(proj, tile8(q_norm), kv_norm.reshape(1, KV_RANK), w_kv_up.astype(bf), tile8(k_norm),
      jnp.tile(ln_w, 2).reshape(1, 128), jnp.tile(ln_b, 2).reshape(1, 128))


def _t5_bucket(dist):
    n = jnp.maximum(dist, 0)
    max_exact = N_BUCKETS // 2
    nf = jnp.maximum(n, 1).astype(jnp.float32)
    large = max_exact + (jnp.log(nf / max_exact) / math.log(MAX_DISTANCE / max_exact) * (N_BUCKETS - max_exact)).astype(jnp.int32)
    large = jnp.minimum(large, N_BUCKETS - 1)
    return jnp.where(n < max_exact, n, large)


def _bias_tables(rel_bias):
    s = jnp.arange(QB, dtype=jnp.int32)[None, :, None]
    q = jnp.arange(QB, dtype=jnp.int32)[None, None, :]
    dist = q - s + jnp.array([2 * QB, QB, 0], jnp.int32)[:, None, None]
    onehot = (_t5_bucket(dist)[..., None] == jnp.arange(N_BUCKETS, dtype=jnp.int32)).astype(jnp.float32)
    b = jnp.einsum('tsqb,bh->thsq', onehot, rel_bias.astype(jnp.float32) * LOG2E, precision=HIGHEST)
    return jnp.where((dist >= 0)[:, None], b, NEG)


def _attn_kernel(qT_ref, qiT_ref, wT_ref, k_ref, vT_ref, ki_ref, kn_ref, tab_ref, bst_ref, o_ref,
                 keys_ref, hi_ref, lo_ref, msk_ref, p_ref, acc_ref, mp_ref, m_ref, *, topk):
    i = pl.program_id(1)
    n_tiles = i + 1
    row_hi = lax.broadcasted_iota(jnp.int32, (128, QB), 0) >= 64

    def head_rows(ref, h):
        pair = ref[0, (h // 2) * 128:(h // 2) * 128 + 128, :]
        return jnp.where(row_hi == bool(h % 2), pair, jnp.zeros_like(pair))

    def tile_rows(kt):
        return pl.ds(pl.multiple_of(kt * QB, QB), QB)

    def score_tile(kt, carry):
        ki = ki_ref[0, tile_rows(kt), :]
        sc = jnp.zeros((QB, QB), jnp.float32)
        for h in range(IDX_HEADS):
            d = jnp.dot(ki, head_rows(qiT_ref, h), preferred_element_type=jnp.float32)
            sc = sc + wT_ref[0, h:h + 1, :] * jnp.maximum(d, 0.0)
        srow = lax.broadcasted_iota(jnp.int32, (QB, QB), 0)
        qcol = lax.broadcasted_iota(jnp.int32, (QB, QB), 1)
        sc = jnp.where(jnp.abs(sc) < TINY, 0.0, sc)
        sc = jnp.where((kt == i) & (srow > qcol), -jnp.inf, sc)
        bits = pltpu.bitcast(sc, jnp.int32)
        keys_ref[tile_rows(kt), :] = bits ^ ((bits >> 31) & 0x7FFFFFFF)
        hi_ref[tile_rows(kt), :] = pltpu.bitcast(bits & jnp.int32(-65536), jnp.float32).astype(jnp.bfloat16)
        return carry
    lax.fori_loop(0, n_tiles, score_tile, 0)

    def count_packed_ge(ref, cb):
        one, zero = jnp.ones((), jnp.bfloat16), jnp.zeros((), jnp.bfloat16)

        def body(kt, acc):
            hit = jnp.where(ref[tile_rows(kt), :] >= cb, one, zero)
            parts = [hit[r:r + 16, :] for r in range(0, QB, 16)]
            while len(parts) > 1:
                parts = [a + b for a, b in zip(parts[::2], parts[1::2])]
            return acc + parts[0]
        acc = lax.fori_loop(0, n_tiles, body, jnp.zeros((16, QB), jnp.bfloat16))
        return jnp.sum(acc.astype(jnp.float32), axis=0, keepdims=True)

    def count_hi_ge(cand16):
        b = cand16 ^ ((cand16 >> 15) & 0x7FFF)
        snap = jnp.where(((b & 0x8000) != 0) | ((b & 0x7F) == 0), 0, 0x0080)
        b = jnp.where((b & 0x7F80) == 0, snap, b)
        return count_packed_ge(hi_ref, pltpu.bitcast(b << 16, jnp.float32).astype(jnp.bfloat16))

    def mid_code(v):
        pat = jnp.where(v >= 16384, v - 16256, 0x8000 | (16511 - v))
        return pltpu.bitcast(pat << 16, jnp.float32)

    def count(hit_of_tile):
        def body(kt, acc):
            return acc + jnp.sum(hit_of_tile(kt).reshape(QB // 8, 8, QB), axis=0)
        acc = lax.fori_loop(0, n_tiles, body, jnp.zeros((8, QB), jnp.int32))
        return jnp.sum(acc, axis=0, keepdims=True)

    def count_ge(cand):
        return count(lambda kt: jnp.where(keys_ref[tile_rows(kt), :] >= cand, 1, 0))

    def hi_step(it, r):
        cand = jnp.where(it == 0, jnp.zeros_like(r), r | (1 << (15 - it)))
        return jnp.where(count_hi_ge(cand) >= topk, cand, r)
    r16 = lax.fori_loop(0, 16, hi_step, jnp.full((1, QB), -32768, jnp.int32))

    above = count_hi_ge(r16 + 1)

    def code_tile(kt, carry):
        key = keys_ref[tile_rows(kt), :]
        code = jnp.where((key >> 16) == r16, mid_code((key >> 1) & 0x7FFF), -jnp.inf)
        lo_ref[tile_rows(kt), :] = code.astype(jnp.bfloat16)
        return carry
    lax.fori_loop(0, n_tiles, code_tile, 0)

    def mid_step(it, v):
        cand = v | (1 << (14 - it))
        cnt = above + count_packed_ge(lo_ref, mid_code(cand).astype(jnp.bfloat16))
        return jnp.where(cnt >= topk, cand, v)
    v15 = lax.fori_loop(0, 15, mid_step, jnp.zeros((1, QB), jnp.int32))
    thr = (r16 << 16) | (v15 << 1)
    thr = jnp.where(count_ge(thr | 1) >= topk, thr | 1, thr)

    cnt_gt = count_ge(thr + 1)
    cnt_ge = count_ge(thr)
    need = topk - cnt_gt
    tie = (cnt_ge - cnt_gt > need) & (thr > KEY_NEG_INF)

    @pl.when(jnp.max(tie.astype(jnp.int32)) > 0)
    def _():
        def count_eq_below(cand):
            def ind(kt):
                idx = lax.broadcasted_iota(jnp.int32, (QB, QB), 0) + kt * QB
                return jnp.where((keys_ref[tile_rows(kt), :] == thr) & (idx < cand), 1, 0)
            return count(ind)

        def idx_step(it, r):
            cand = r | (1 << (15 - it))
            return jnp.where(count_eq_below(cand) < need, cand, r)
        last = lax.fori_loop(0, 16, idx_step, jnp.zeros((1, QB), jnp.int32))

        def drop(kt, carry):
            blk = keys_ref[tile_rows(kt), :]
            idx = lax.broadcasted_iota(jnp.int32, (QB, QB), 0) + kt * QB
            keys_ref[tile_rows(kt), :] = jnp.where(tie & (blk == thr) & (idx > last), INT_MIN, blk)
            return carry
        lax.fori_loop(0, n_tiles, drop, 0)

    def logits(kt, h):
        band = jnp.clip(kt - (i - 2), 0, 2)
        kp = k_ref[0, tile_rows(kt), (h // 2) * 128:(h // 2) * 128 + 128]
        s = jnp.dot(kp, head_rows(qT_ref, h), preferred_element_type=jnp.float32)
        return s + msk_ref[...] + tab_ref[band, h]

    def set_mask(kt):
        msk_ref[...] = jnp.where(keys_ref[tile_rows(kt), :] >= thr, 0.0, NEG)

    def max_tile(kt, carry):
        set_mask(kt)
        for h in range(ATTN_HEADS):
            s = logits(kt, h)
            mp_ref[h] = jnp.maximum(mp_ref[h], jnp.max(s.reshape(QB // 8, 8, QB), axis=0))
        return carry

    seq = kn_ref.shape[2]
    in_extent = lax.broadcasted_iota(jnp.int32, (ATTN_HEADS, seq), 1) < n_tiles * QB
    k_max = jnp.max(jnp.where(in_extent, kn_ref[0], 0.0), axis=1, keepdims=True)
    spread = jnp.zeros((1, QB), jnp.float32)
    for h in range(ATTN_HEADS):
        qh = qT_ref[0, h * ATTN_HEAD_DIM:(h + 1) * ATTN_HEAD_DIM, :].astype(jnp.float32)
        reach = jnp.sqrt(jnp.sum(qh * qh, axis=0, keepdims=True) * k_max[h:h + 1, :]) * NORM_SLACK
        m_ref[h:h + 1, :] = reach + bst_ref[0, h:h + 1, :]
        spread = jnp.maximum(spread, 2.0 * reach + bst_ref[1, h:h + 1, :])
    bound_ok = jnp.max(spread) <= MAX_SHIFT_ERROR

    @pl.when(jnp.logical_not(bound_ok))
    def _():
        mp_ref[...] = jnp.full(mp_ref.shape, NEG, jnp.float32)
        lax.fori_loop(0, n_tiles, max_tile, 0)
        for h in range(ATTN_HEADS):
            m_ref[h:h + 1, :] = jnp.max(mp_ref[h], axis=0, keepdims=True)
    m = [m_ref[h:h + 1, :] for h in range(ATTN_HEADS)]

    def exp_tile(kt, carry):
        set_mask(kt)
        for h in range(ATTN_HEADS):
            p_ref[h] = jnp.exp2(logits(kt, h) - m[h]).astype(jnp.bfloat16)
        for h in range(ATTN_HEADS):
            va = vT_ref[0, h * VROWS:(h + 1) * VROWS, tile_rows(kt)]
            acc_ref[h * VROWS:(h + 1) * VROWS, :] += jnp.dot(va, p_ref[h], preferred_element_type=jnp.float32)
        return carry

    acc_ref[...] = jnp.zeros(acc_ref.shape, jnp.float32)
    lax.fori_loop(0, n_tiles, exp_tile, 0)

    outs = [acc_ref[h * VROWS:h * VROWS + ATTN_HEAD_DIM, :] / acc_ref[h * VROWS + ATTN_HEAD_DIM:h * VROWS + ATTN_HEAD_DIM + 1, :]
            for h in range(ATTN_HEADS)]
    o_ref[0] = jnp.concatenate(outs, axis=0).T


def _dsa_attention(qT, qiT, wT, k, vT, ki2, kn2, rel_bias):
    bsz, _, seq = qT.shape
    topk = min(TOPK_MAX, seq // 4)
    assert seq % QB == 0 and topk <= QB
    assert seq // 16 <= 256
    b2 = rel_bias.astype(jnp.float32) * LOG2E
    bias_stats = jnp.stack([jnp.max(b2, axis=0), jnp.max(b2, axis=0) - jnp.min(b2, axis=0)])
    bias_stats = jnp.broadcast_to(bias_stats[:, :, None], (2, ATTN_HEADS, QB))
    return pl.pallas_call(
        functools.partial(_attn_kernel, topk=topk),
        grid=(bsz, seq // QB),
        in_specs=[
            pl.BlockSpec((1, ATTN_WIDTH, QB), lambda b, i: (b, 0, i)),
            pl.BlockSpec((1, IDX_HEADS * IDX_DIM, QB), lambda b, i: (b, 0, i)),
            pl.BlockSpec((1, IDX_HEADS, QB), lambda b, i: (b, 0, i)),
            pl.BlockSpec((1, seq, ATTN_WIDTH), lambda b, i: (b, 0, 0)),
            pl.BlockSpec((1, ATTN_HEADS * VROWS, seq), lambda b, i: (b, 0, 0)),
            pl.BlockSpec((1, seq, 128), lambda b, i: (b, 0, 0)),
            pl.BlockSpec((1, ATTN_HEADS, seq), lambda b, i: (b, 0, 0)),
            pl.BlockSpec((3, ATTN_HEADS, QB, QB), lambda b, i: (0, 0, 0, 0)),
            pl.BlockSpec((2, ATTN_HEADS, QB), lambda b, i: (0, 0, 0)),
        ],
        out_specs=pl.BlockSpec((1, QB, ATTN_WIDTH), lambda b, i: (b, i, 0)),
        out_shape=jax.ShapeDtypeStruct((bsz, seq, ATTN_WIDTH), jnp.float32),
        scratch_shapes=[
            pltpu.VMEM((seq, QB), jnp.int32),
            pltpu.VMEM((seq, QB), jnp.bfloat16),
            pltpu.VMEM((seq, QB), jnp.bfloat16),
            pltpu.VMEM((QB, QB), jnp.float32),
            pltpu.VMEM((ATTN_HEADS, QB, QB), jnp.bfloat16),
            pltpu.VMEM((ATTN_HEADS * VROWS, QB), jnp.float32),
            pltpu.VMEM((ATTN_HEADS, 8, QB), jnp.float32),
            pltpu.VMEM((ATTN_HEADS, QB), jnp.float32),
        ],
        compiler_params=pltpu.CompilerParams(dimension_semantics=("parallel", "arbitrary"),
                                             vmem_limit_bytes=VMEM_LIMIT),
        name="dsa_attention",
    )(qT, qiT, wT, k, vT, ki2, kn2, _bias_tables(rel_bias), bias_stats)


def _ssd_kernel(xbc_ref, z_ref, sm_ref, cw_ref, cb_ref, dtb_ref, a_ref, dsk_ref, nw_ref, y_ref, prev_ref, st_ref):
    q = SSD_CHUNK
    bf = jnp.bfloat16

    @pl.when(pl.program_id(1) == 0)
    def _():
        prev_ref[...] = jnp.zeros(prev_ref.shape, jnp.float32)
        st_ref[...] = jnp.zeros(st_ref.shape, jnp.float32)

    cur = xbc_ref[...]
    prev = prev_ref[...]
    row = lax.broadcasted_iota(jnp.int32, cur.shape, 0)
    acc = cur * cw_ref[CONV_WIDTH - 1:CONV_WIDTH, :] + cb_ref[...]
    for s in range(1, CONV_WIDTH):
        shifted = jnp.where(row >= s, pltpu.roll(cur, s, 0), pltpu.roll(prev, s, 0))
        acc = acc + shifted * cw_ref[CONV_WIDTH - 1 - s:CONV_WIDTH - s, :]
    prev_ref[...] = cur
    u = acc * jax.nn.sigmoid(acc)
    xs = u[:, :SSM_INNER]
    bm = u[:, SSM_INNER:SSM_INNER + SSM_GROUPS * SSM_STATE].astype(bf)
    cm = u[:, SSM_INNER + SSM_GROUPS * SSM_STATE:].astype(bf)

    t = sm_ref[...] + dtb_ref[...]
    dt = jnp.maximum(t, 0.0) + jnp.log1p(jnp.exp(-jnp.abs(t)))
    ii = lax.broadcasted_iota(jnp.int32, (q, q), 0)
    jj = lax.broadcasted_iota(jnp.int32, (q, q), 1)
    causal = ii >= jj
    acum = jnp.dot(causal.astype(jnp.float32), dt * a_ref[...], preferred_element_type=jnp.float32, precision=HIGHEST)
    acum_t = acum.T
    dt_t = dt.T
    ea = jnp.exp(acum)
    last = acum[q - 1:q, :]
    decay = jnp.exp(last - acum) * dt
    ea_last = jnp.exp(last)

    lane_hi = lax.broadcasted_iota(jnp.int32, (q, 128), 1) >= SSM_HEAD_DIM
    row_hi = lax.broadcasted_iota(jnp.int32, (128, SSM_STATE), 0) >= SSM_HEAD_DIM

    def pair_cols(v, e):
        c0, c1 = SMALL_DT + e, SMALL_DT + e + 1
        return jnp.where(lane_hi, v[:, c1:c1 + 1], v[:, c0:c0 + 1])

    for g in range(SSM_GROUPS):
        bg = bm[:, g * SSM_STATE:(g + 1) * SSM_STATE]
        cg = cm[:, g * SSM_STATE:(g + 1) * SSM_STATE]
        cb = lax.dot_general(cg, bg, NT, preferred_element_type=jnp.float32)
        for k in range(g * 4, g * 4 + 4):
            e = 2 * k
            x_pair = xs[:, k * 128:(k + 1) * 128]
            halves = []
            for h in (e, e + 1):
                c = SMALL_DT + h
                seg = acum[:, c:c + 1] - acum_t[c:c + 1, :]
                w = cb * jnp.exp(jnp.where(causal, seg, -jnp.inf)) * dt_t[c:c + 1, :]
                halves.append(jnp.dot(w.astype(bf), x_pair.astype(bf), preferred_element_type=jnp.float32))
            y_pair = jnp.where(lane_hi, halves[1], halves[0])
            state = st_ref[k]
            y_pair = y_pair + lax.dot_general(cg, state.astype(bf), NT, preferred_element_type=jnp.float32) * pair_cols(ea, e)
            y_ref[:, k * 128:(k + 1) * 128] = y_pair
            xd_t = (x_pair * pair_cols(decay, e)).T.astype(bf)
            c0 = SMALL_DT + e
            keep = jnp.where(row_hi, ea_last[:, c0 + 1:c0 + 2], ea_last[:, c0:c0 + 1])
            st_ref[k] = state * keep + jnp.dot(xd_t, bg, preferred_element_type=jnp.float32)

    y = (y_ref[...] + dsk_ref[...] * xs) * (z_ref[...] * jax.nn.sigmoid(z_ref[...]))
    half = SSM_INNER // SSM_GROUPS
    for g in range(SSM_GROUPS):
        yg = y[:, g * half:(g + 1) * half]
        yg = yg * lax.rsqrt(jnp.mean(yg * yg, axis=-1, keepdims=True) + EPS)
        y_ref[:, g * half:(g + 1) * half] = yg * nw_ref[:, g * half:(g + 1) * half]


def _mamba2_ssd(proj, bsz, seq, conv_w, conv_b, dt_bias, a_log, d_skip, norm_w):
    q = SSD_CHUNK
    nc = seq // q
    lane_row = lambda v: jnp.zeros((1, 128), jnp.float32).at[0, SMALL_DT:SMALL_DT + SSM_HEADS].set(v)
    const = lambda shape: pl.BlockSpec(shape, lambda b, c: (0,) * len(shape))
    return pl.pallas_call(
        _ssd_kernel,
        grid=(bsz, nc),
        in_specs=[pl.BlockSpec((q, CONV_CH), lambda b, c: (b * nc + c, COL_XBC // CONV_CH)),
                  pl.BlockSpec((q, SSM_INNER), lambda b, c: (b * nc + c, COL_Z // SSM_INNER)),
                  pl.BlockSpec((q, 128), lambda b, c: (b * nc + c, COL_SMALL // 128)),
                  const((CONV_WIDTH, CONV_CH)), const((1, CONV_CH)), const((1, 128)), const((1, 128)),
                  const((1, SSM_INNER)), const((1, SSM_INNER))],
        out_specs=pl.BlockSpec((q, SSM_INNER), lambda b, c: (b * nc + c, 0)),
        out_shape=jax.ShapeDtypeStruct((bsz * seq, SSM_INNER), jnp.float32),
        scratch_shapes=[pltpu.VMEM((q, CONV_CH), jnp.float32),
                        pltpu.VMEM((SSM_HEADS // 2, 2 * SSM_HEAD_DIM, SSM_STATE), jnp.float32)],
        compiler_params=pltpu.CompilerParams(dimension_semantics=("parallel", "arbitrary"),
                                             vmem_limit_bytes=VMEM_LIMIT),
        name="mamba2_ssd",
    )(proj, proj, proj, conv_w, conv_b.reshape(1, CONV_CH), lane_row(dt_bias), lane_row(-jnp.exp(a_log)),
      jnp.repeat(d_skip, SSM_HEAD_DIM).reshape(1, SSM_INNER), norm_w.reshape(1, SSM_INNER))


def _mix_out_kernel(a_ref, s_ref, gl_ref, x_ref, gm_ref, wo_ref, ws_ref, wout_ref,
                    nf_ref, scf_ref, shf_ref, wr_ref, br_ref, xo_ref, h_ref, rt_ref, gt_ref, cnt_ref):
    bf = jnp.bfloat16
    ya = jnp.dot(a_ref[...].astype(bf), wo_ref[...], preferred_element_type=jnp.float32)
    ys = jnp.dot(s_ref[...].astype(bf), ws_ref[...], preferred_element_type=jnp.float32)
    mixed = jax.nn.sigmoid(gl_ref[:, :D_MODEL]) * ya + jax.nn.sigmoid(gl_ref[:, D_MODEL:]) * ys
    x = x_ref[...] + gm_ref[0] * jnp.dot(mixed.astype(bf), wout_ref[...], preferred_element_type=jnp.float32)
    xo_ref[...] = x
    y = x * lax.rsqrt(jnp.mean(x * x, axis=-1, keepdims=True) + EPS) * nf_ref[...]
    h = y * (1.0 + scf_ref[0]) + shf_ref[0]
    h_hi = h.astype(bf)
    hb = pltpu.bitcast(h_hi.astype(jnp.float32), jnp.int32)
    half = D_MODEL // 2
    h_ref[...] = (hb[:, :half] & jnp.int32(-65536)) | lax.shift_right_logical(hb[:, half:], 16)
    h_lo = (h - h_hi.astype(jnp.float32)).astype(bf)
    dot = functools.partial(jnp.dot, preferred_element_type=jnp.float32)
    lg = dot(h_hi, wr_ref[0]) + (dot(h_lo, wr_ref[0]) + dot(h_hi, wr_ref[1])) + br_ref[...]

    tm = lg.shape[0]
    work = lg.T[:N_EXPERTS, :]
    row = lax.broadcasted_iota(jnp.int32, work.shape, 0).astype(jnp.float32)
    vals, eids, hits = [], [], []
    for _ in range(TOP_K):
        mx = jnp.max(work, axis=0, keepdims=True)
        ix = jnp.min(jnp.where(work == mx, row, float(N_EXPERTS)), axis=0, keepdims=True)
        vals.append(mx)
        eids.append(ix)
        hits.append(row == ix)
        work = jnp.where(hits[-1], -jnp.inf, work)
    ex = [jnp.exp(v - vals[0]) for v in vals]
    den = (ex[0] + ex[1]) + (ex[2] + ex[3])

    @pl.when(pl.program_id(0) == 0)
    def _():
        cnt_ref[...] = jnp.zeros(cnt_ref.shape, jnp.float32)
    chosen = jnp.zeros(work.shape, jnp.float32)
    for hit in hits:
        chosen = jnp.where(hit, 1.0, chosen)
    earlier = lax.broadcasted_iota(jnp.int32, (tm, tm), 0) < lax.broadcasted_iota(jnp.int32, (tm, tm), 1)
    before = dot(chosen.astype(bf), earlier.astype(bf)) + cnt_ref[:, 0:1]
    cnt_ref[...] = cnt_ref[...] + jnp.sum(chosen, axis=1, keepdims=True)

    slot = lax.broadcasted_iota(jnp.int32, (128, tm), 0)
    route = jnp.zeros((128, tm), jnp.float32)
    for k, hit in enumerate(hits):
        rank = jnp.sum(jnp.where(hit, before, 0.0), axis=0, keepdims=True)
        route = jnp.where(slot == k, eids[k], route)
        route = jnp.where(slot == TOP_K + k, ex[k] / den, route)
        route = jnp.where(slot == 2 * TOP_K + k, rank, route)
    rt_ref[...] = route[:ROUTE_ROWS, :]
    gt_ref[...] = route.T


def _mix_out(attn2, ssd2, proj, x2, g_m, w_attn_o, w_ssm_o, w_out, norm_ffn, sc_f, sh_f, w_router, b_router, seq, tm=512):
    t, d = x2.shape
    per_b = seq // tm
    bf = jnp.bfloat16
    const = lambda shape: pl.BlockSpec(shape, lambda i: (0,) * len(shape))
    perb = pl.BlockSpec((1, 1, d), lambda i: (i // per_b, 0, 0))
    wr = jnp.pad(w_router.astype(jnp.float32), ((0, 0), (0, 128 - N_EXPERTS)))
    wr_hi = wr.astype(bf)
    wr = jnp.stack([wr_hi, (wr - wr_hi.astype(jnp.float32)).astype(bf)])
    br =jnp.pad(b_router, (0, 128 - N_EXPERTS)).reshape(1, 128)
    return pl.pallas_call(
        _mix_out_kernel,
        grid=(t // tm,),
        in_specs=[pl.BlockSpec((tm, ATTN_WIDTH), lambda i: (i, 0)),
                  pl.BlockSpec((tm, SSM_INNER), lambda i: (i, 0)),
                  pl.BlockSpec((tm, 2 * d), lambda i: (i, COL_GATE // (2 * d))),
                  pl.BlockSpec((tm, d), lambda i: (i, 0)),
                  perb,
                  const((ATTN_WIDTH, d)), const((SSM_INNER, d)), const((d, d)),
                  const((1, d)), perb, perb, const((2, d, 128)), const((1, 128))],
        out_specs=[pl.BlockSpec((tm, d), lambda i: (i, 0)),
                   pl.BlockSpec((tm, d // 2), lambda i: (i, 0)),
                   pl.BlockSpec((ROUTE_ROWS, tm), lambda i: (0, i)),
                   pl.BlockSpec((tm, 128), lambda i: (i, 0)),
                   pl.BlockSpec((N_EXPERTS, 128), lambda i: (0, 0))],
        out_shape=[jax.ShapeDtypeStruct((t, d), jnp.float32),
                   jax.ShapeDtypeStruct((t, d // 2), jnp.int32),
                   jax.ShapeDtypeStruct((ROUTE_ROWS, t), jnp.float32),
                   jax.ShapeDtypeStruct((t, 128), jnp.float32),
                   jax.ShapeDtypeStruct((N_EXPERTS, 128), jnp.float32)],
        compiler_params=pltpu.CompilerParams(dimension_semantics=("arbitrary",), vmem_limit_bytes=VMEM_LIMIT),
        name="mix_out",
    )(attn2, ssd2, proj, x2, g_m[:, None, :], w_attn_o.astype(bf), w_ssm_o.astype(bf), w_out.astype(bf),
      norm_ffn.reshape(1, d), sc_f[:, None, :], sh_f[:, None, :], wr, br)


def _moe_kernel(be_ref, nb_ref, x_ref, wgu_ref, bgu_ref, wdn_ref, bdn_ref, o_ref, wgu_bf, wdn_bf):
    i = pl.program_id(0)

    @pl.when((i == 0) | (be_ref[i] != be_ref[jnp.maximum(i - 1, 0)]))
    def _():
        wgu_bf[...] = wgu_ref[0, 0].astype(jnp.bfloat16)
        wdn_bf[...] = wdn_ref[0, 0].astype(jnp.bfloat16)

    @pl.when(i < nb_ref[0])
    def _():
        words = x_ref[...]
        x_hi = pltpu.bitcast(words & jnp.int32(-65536), jnp.float32).astype(jnp.bfloat16)
        x_lo = pltpu.bitcast(words << 16, jnp.float32).astype(jnp.bfloat16)
        x = jnp.concatenate([x_hi, x_lo], axis=1)
        gu = jnp.dot(x, wgu_bf[...], preferred_element_type=jnp.float32) + bgu_ref[0, 0]
        g = jnp.minimum(gu[:, :D_EXPERT], SWIGLU_LIMIT)
        u = jnp.clip(gu[:, D_EXPERT:], -SWIGLU_LIMIT, SWIGLU_LIMIT)
        act = (u + 1.0) * (g * jax.nn.sigmoid(SWIGLU_ALPHA * g))
        out = jnp.dot(act.astype(jnp.bfloat16), wdn_bf[...], preferred_element_type=jnp.float32) + bdn_ref[0, 0]
        o_ref[...] = out.astype(o_ref.dtype)

    @pl.when(i >= nb_ref[0])
    def _():
        o_ref[...] = jnp.zeros_like(o_ref)


def _moe_ffn(xs, blk_exp, n_used, w_gu, b_gu, w_dn, b_dn, layer):
    n_rows = xs.shape[0]
    d = D_MODEL
    tm = MOE_TM
    grid_spec = pltpu.PrefetchScalarGridSpec(
        num_scalar_prefetch=2,
        grid=(n_rows // tm,),
        in_specs=[pl.BlockSpec((tm, d // 2), lambda i, be, nb: (i, 0)),
                  pl.BlockSpec((1, 1, d, 2 * D_EXPERT), lambda i, be, nb: (layer, be[i], 0, 0)),
                  pl.BlockSpec((1, 1, 1, 2 * D_EXPERT), lambda i, be, nb: (layer, be[i], 0, 0)),
                  pl.BlockSpec((1, 1, D_EXPERT, d), lambda i, be, nb: (layer, be[i], 0, 0)),
                  pl.BlockSpec((1, 1, 1, d), lambda i, be, nb: (layer, be[i], 0, 0))],
        out_specs=pl.BlockSpec((tm, d), lambda i, be, nb: (i, 0)),
        scratch_shapes=[pltpu.VMEM((d, 2 * D_EXPERT), jnp.bfloat16), pltpu.VMEM((D_EXPERT, d), jnp.bfloat16)],
    )
    return pl.pallas_call(
        _moe_kernel,
        grid_spec=grid_spec,
        out_shape=jax.ShapeDtypeStruct((n_rows, d), jnp.bfloat16),
        compiler_params=pltpu.CompilerParams(dimension_semantics=("arbitrary",), vmem_limit_bytes=VMEM_LIMIT),
        name="moe_ffn",
    )(blk_exp, n_used, xs, w_gu, b_gu[:, :, None, :], w_dn, b_dn[:, :, None, :])


def _moe(h2, route, expert_counts, w_gu, b_gu, w_dn, b_dn, layer):
    t = h2.shape[0]
    d = D_MODEL
    tm = MOE_TM
    i32 = jnp.int32
    experts = jnp.arange(N_EXPERTS, dtype=i32)
    top_idx = route[:TOP_K].astype(i32)
    rank = route[2 * TOP_K:3 * TOP_K].astype(i32)
    n_assign = t * TOP_K
    n_rows = n_assign + N_EXPERTS * tm
    e_flat = top_idx.T.reshape(n_assign)
    counts = expert_counts[:, 0].astype(i32)
    padded = (counts + tm - 1) // tm * tm
    pad_start = jnp.cumsum(padded) - padded
    dest = rank + jnp.sum(jnp.where(top_idx[..., None] == experts, pad_start, 0), axis=-1)
    dest = dest.reshape(-1)
    filler_exp = jnp.repeat(experts, tm)
    filler_key = jnp.where(jnp.tile(jnp.arange(tm, dtype=i32), N_EXPERTS) < jnp.repeat(padded - counts, tm),
                           filler_exp, N_EXPERTS)
    keys = jnp.concatenate([e_flat, filler_key])
    rows = jnp.arange(n_rows, dtype=i32)
    row_key, row_src = lax.sort((keys, rows), num_keys=1)
    row_tok = jnp.where(row_src < n_assign, row_src // TOP_K, rows % t)
    blk_exp = jnp.minimum(row_key[::tm], N_EXPERTS - 1)
    n_used = (jnp.sum(padded, keepdims=True) // tm).astype(i32)
    out = _moe_ffn(h2[row_tok], blk_exp, n_used, w_gu, b_gu, w_dn, b_dn, layer)
    return out[dest].reshape(TOP_K, t, d)


def _combine_kernel(p_ref, r_ref, x_ref, g_ref, o_ref):
    f32 = jnp.float32
    w = [r_ref[:, TOP_K + k:TOP_K + k + 1] for k in range(TOP_K)]
    y = (w[0] * p_ref[0].astype(f32) + w[1] * p_ref[1].astype(f32)) + (w[2] * p_ref[2].astype(f32) + w[3] * p_ref[3].astype(f32))
    o_ref[...] = x_ref[...] + g_ref[0] * y


def _combine(parts, route_tok, x2, g_f, seq, tm=512):
    t, d = x2.shape
    per_b = seq // tm
    return pl.pallas_call(
        _combine_kernel,
        grid=(t // tm,),
        in_specs=[pl.BlockSpec((TOP_K, tm, d), lambda i: (0, i, 0)),
                  pl.BlockSpec((tm, 128), lambda i: (i, 0)),
                  pl.BlockSpec((tm, d), lambda i: (i, 0)),
                  pl.BlockSpec((1, 1, d), lambda i: (i // per_b, 0, 0))],
        out_specs=pl.BlockSpec((tm, d), lambda i: (i, 0)),
        out_shape=jax.ShapeDtypeStruct((t, d), jnp.float32),
        compiler_params=pltpu.CompilerParams(dimension_semantics=("parallel",), vmem_limit_bytes=VMEM_LIMIT),
        name="moe_combine",
    )(parts, route_tok, x2, g_f[:, None, :])


def kernel(x, c, rel_bias, w_ada, b_ada, norm_mix, norm_ffn, w_in, kv_norm, w_kv_up, q_norm, k_norm,
           idx_k_ln_w, idx_k_ln_b, w_attn_o, conv_w, conv_b, dt_bias, a_log, d_skip, ssm_norm, w_ssm_o,
           w_out, w_router, b_router, w_gu, b_gu, w_dn, b_dn):
    bsz, seq, d = x.shape
    t = bsz * seq
    cond = jax.nn.silu(c)
    x2 = x.reshape(t, d)
    for l in range(DEPTH):
        mod = cond @ w_ada[l] + b_ada[l]
        sh_m, sc_m, g_m, sh_f, sc_f, g_f = jnp.split(mod, 6, axis=-1)
        proj = _in_proj(x2, norm_mix[l], sc_m, sh_m, _pack_w_in(w_in[l]), seq)
        qT, k, vT, qiT, ki2, wT, kn2 = _prep(proj, bsz, seq, q_norm[l], kv_norm[l], w_kv_up[l], k_norm[l],
                                             idx_k_ln_w[l], idx_k_ln_b[l])
        attn = _dsa_attention(qT, qiT, wT, k, vT, ki2, kn2, rel_bias)
        y_ssd = _mamba2_ssd(proj, bsz, seq, conv_w[l], conv_b[l], dt_bias[l], a_log[l], d_skip[l], ssm_norm[l])
        x2, h2, route, route_tok, expert_counts = _mix_out(
            attn.reshape(t, ATTN_WIDTH), y_ssd, proj, x2, g_m, w_attn_o[l], w_ssm_o[l], w_out[l], norm_ffn[l],
            sc_f, sh_f, w_router[l], b_router[l], seq)
        parts = _moe(h2, route, expert_counts, w_gu, b_gu, w_dn, b_dn, l)
        x2 = _combine(parts, route_tok, x2, g_f, seq)
    return x2.reshape(bsz, seq, d)
```

```python
import functools
import math

import jax
import jax.numpy as jnp
import numpy as np
from jax import lax
from jax.experimental import pallas as pl
from jax.experimental.pallas import tpu as pltpu

D_MODEL = 1024
DEPTH = 2
ATTN_HEADS = 8
ATTN_HEAD_DIM = 64
ATTN_WIDTH = ATTN_HEADS * ATTN_HEAD_DIM
KV_RANK = 256
IDX_HEADS = 8
IDX_DIM = 64
TOPK_MAX = 256
N_BUCKETS = 32
MAX_DISTANCE = 128
SSM_HEADS = 16
SSM_HEAD_DIM = 64
SSM_INNER = SSM_HEADS * SSM_HEAD_DIM
SSM_GROUPS = 2
SSM_STATE = 128
CONV_WIDTH = 4
CONV_CH = SSM_INNER + 2 * SSM_GROUPS * SSM_STATE
SSD_CHUNK = 128
N_EXPERTS = 32
TOP_K = 4
D_EXPERT = D_MODEL
SWIGLU_LIMIT = 7.0
SWIGLU_ALPHA = 1.702
EPS = 1e-6

COL_Q = 0
COL_KV = 512
COL_QI = 768
COL_SMALL = 1280
COL_XBC = 1536
COL_Z = 3072
COL_GATE = 4096
PROJ_COLS = 6144
PREP_COLS = 1408
SMALL_KI, SMALL_WI, SMALL_DT = 0, 64, 72

QB = 256
VROWS = 80
INT_MIN = -2 ** 31
KEY_NEG_INF = (0xFF800000 ^ 0x7FFFFFFF) - 2 ** 32
NEG = -1e30
TINY = 2.0 ** -126
LOG2E = math.log2(math.e)
NORM_SLACK = 1.02
MAX_SHIFT_ERROR = 96.0
VMEM_LIMIT = 56 * 1024 * 1024
MOE_TM = 512
MOE_SLABS = 2
ROUTE_ROWS = 16
HIGHEST = lax.Precision.HIGHEST
NT = (((1,), (1,)), ((), ()))


def _pack_w_in(w):
    o = np.cumsum((0, ATTN_WIDTH, KV_RANK, IDX_HEADS * IDX_DIM, IDX_DIM, IDX_HEADS, SSM_INNER, CONV_CH, SSM_HEADS, 2 * D_MODEL))
    q, kv, qi, ki, wi, z, xbc, dt, gate = (w[:, int(o[n]):int(o[n + 1])] for n in range(9))
    zeros = lambda n: jnp.zeros((w.shape[0], n), w.dtype)
    small = jnp.concatenate([ki, wi, dt, zeros(128 - 88)], axis=1)
    packed = jnp.concatenate([q, kv, qi, small, zeros(COL_XBC - PREP_COLS), xbc, z, gate], axis=1)
    assert packed.shape[1] == PROJ_COLS
    return packed.astype(jnp.bfloat16)


def _in_proj_kernel(x_ref, g_ref, sc_ref, sh_ref, w_ref, o_ref, h_ref):
    @pl.when(pl.program_id(1) == 0)
    def _():
        x = x_ref[...]
        y = x * lax.rsqrt(jnp.mean(x * x, axis=-1, keepdims=True) + EPS) * g_ref[...]
        h_ref[...] = (y * (1.0 + sc_ref[0]) + sh_ref[0]).astype(jnp.bfloat16)
    o_ref[...] = jnp.dot(h_ref[...], w_ref[...], preferred_element_type=jnp.float32)


def _in_proj(x2, gain, sc, sh, w_packed, seq, tm=1024, tn=1024):
    t, d = x2.shape
    per_b = seq // tm
    return pl.pallas_call(
        _in_proj_kernel,
        grid=(t // tm, PROJ_COLS // tn),
        in_specs=[pl.BlockSpec((tm, d), lambda i, j: (i, 0)),
                  pl.BlockSpec((1, d), lambda i, j: (0, 0)),
                  pl.BlockSpec((1, 1, d), lambda i, j: (i // per_b, 0, 0)),
                  pl.BlockSpec((1, 1, d), lambda i, j: (i // per_b, 0, 0)),
                  pl.BlockSpec((d, tn), lambda i, j: (0, j))],
        out_specs=pl.BlockSpec((tm, tn), lambda i, j: (i, j)),
        out_shape=jax.ShapeDtypeStruct((t, PROJ_COLS), jnp.float32),
        scratch_shapes=[pltpu.VMEM((tm, d), jnp.bfloat16)],
        compiler_params=pltpu.CompilerParams(dimension_semantics=("parallel", "arbitrary"),
                                             vmem_limit_bytes=VMEM_LIMIT),
        name="in_proj",
    )(x2, gain.reshape(1, d), sc[:, None, :], sh[:, None, :], w_packed)


def _head_rms_t(xt):
    x3 = xt.reshape(ATTN_HEADS, ATTN_HEAD_DIM, xt.shape[1])
    return lax.rsqrt(jnp.mean(x3 * x3, axis=1, keepdims=True) + EPS)


def _prep_kernel(p_ref, qg_ref, kvg_ref, wkv_ref, kg_ref, lng_ref, lnb_ref,
                 qT_ref, k_ref, vT_ref, qiT_ref, ki_ref, wT_ref, kn2_ref):
    n = p_ref.shape[0]
    q = p_ref[:, COL_Q:COL_Q + ATTN_WIDTH]
    lat = p_ref[:, COL_KV:COL_KV + KV_RANK]
    qi = p_ref[:, COL_QI:COL_QI + IDX_HEADS * IDX_DIM]
    sm = p_ref[:, COL_SMALL:COL_SMALL + 128]

    scale = ATTN_HEAD_DIM ** -0.5 * LOG2E
    qt = q.T
    qn = qt.reshape(ATTN_HEADS, ATTN_HEAD_DIM, n) * _head_rms_t(qt)
    qT_ref[0] = (qn.reshape(ATTN_WIDTH, n) * qg_ref[...] * scale).astype(jnp.bfloat16)

    latn = lat * lax.rsqrt(jnp.mean(lat * lat, axis=-1, keepdims=True) + EPS) * kvg_ref[...]
    kv = jnp.dot(latn.astype(jnp.bfloat16), wkv_ref[...], preferred_element_type=jnp.float32)
    kt = kv[:, :ATTN_WIDTH].T
    kn = (kt.reshape(ATTN_HEADS, ATTN_HEAD_DIM, n) * _head_rms_t(kt)).reshape(ATTN_WIDTH, n) * kg_ref[...]
    k_ref[0] = kn.T.astype(jnp.bfloat16)
    kn3 = kn.reshape(ATTN_HEADS, ATTN_HEAD_DIM, n)
    kn2_ref[0] = jnp.sum(kn3 * kn3, axis=1)
    vt = kv[:, ATTN_WIDTH:].T.reshape(ATTN_HEADS, ATTN_HEAD_DIM, n)
    ones = jnp.ones((ATTN_HEADS, VROWS - ATTN_HEAD_DIM, n), jnp.float32)
    vT_ref[0] = jnp.concatenate([vt, ones], axis=1).reshape(ATTN_HEADS * VROWS, n).astype(jnp.bfloat16)

    qiT_ref[0] = (qi * (IDX_DIM ** -0.5)).T.astype(jnp.bfloat16)

    lane = lax.broadcasted_iota(jnp.int32, sm.shape, 1)
    kid = jnp.where(lane < IDX_DIM, sm, pltpu.roll(sm, IDX_DIM, 1))
    mu = jnp.mean(kid, axis=-1, keepdims=True)
    var = jnp.mean(jnp.square(kid - mu), axis=-1, keepdims=True)
    ki_ref[0] = ((kid - mu) * lax.rsqrt(var + EPS) * lng_ref[...] + lnb_ref[...]).astype(jnp.bfloat16)

    wT_ref[0] = sm.T[SMALL_WI:SMALL_WI + IDX_HEADS, :] * (IDX_HEADS ** -0.5)


def _prep(proj, bsz, seq, q_norm, kv_norm, w_kv_up, k_norm, ln_w, ln_b, tp=512):
    nb = seq // tp
    tile8 = lambda g: jnp.tile(g, ATTN_HEADS).reshape(ATTN_WIDTH, 1)
    const = lambda shape: pl.BlockSpec(shape, lambda b, i: (0,) * len(shape))
    bf = jnp.bfloat16
    return pl.pallas_call(
        _prep_kernel,
        grid=(bsz, nb),
        in_specs=[pl.BlockSpec((tp, PREP_COLS), lambda b, i: (b * nb + i, 0)),
                  const((ATTN_WIDTH, 1)), const((1, KV_RANK)), const((KV_RANK, 2 * ATTN_WIDTH)),
                  const((ATTN_WIDTH, 1)), const((1, 128)), const((1, 128))],
        out_specs=[pl.BlockSpec((1, ATTN_WIDTH, tp), lambda b, i: (b, 0, i)),
                   pl.BlockSpec((1, tp, ATTN_WIDTH), lambda b, i: (b, i, 0)),
                   pl.BlockSpec((1, ATTN_HEADS * VROWS, tp), lambda b, i: (b, 0, i)),
                   pl.BlockSpec((1, ATTN_WIDTH, tp), lambda b, i: (b, 0, i)),
                   pl.BlockSpec((1, tp, 128), lambda b, i: (b, i, 0)),
                   pl.BlockSpec((1, IDX_HEADS, tp), lambda b, i: (b, 0, i)),
                   pl.BlockSpec((1, ATTN_HEADS, tp), lambda b, i: (b, 0, i))],
        out_shape=[jax.ShapeDtypeStruct((bsz, ATTN_WIDTH, seq), bf),
                   jax.ShapeDtypeStruct((bsz, seq, ATTN_WIDTH), bf),
                   jax.ShapeDtypeStruct((bsz, ATTN_HEADS * VROWS, seq), bf),
                   jax.ShapeDtypeStruct((bsz, ATTN_WIDTH, seq), bf),
                   jax.ShapeDtypeStruct((bsz, seq, 128), bf),
                   jax.ShapeDtypeStruct((bsz, IDX_HEADS, seq), jnp.float32),
                   jax.ShapeDtypeStruct((bsz, ATTN_HEADS, seq), jnp.float32)],
        compiler_params=pltpu.CompilerParams(dimension_semantics=("parallel", "parallel"),
                                             vmem_limit_bytes=VMEM_LIMIT),
        name="attn_prep",
    )(proj, tile8(q_norm), kv_norm.reshape(1, KV_RANK), w_kv_up.astype(bf), tile8(k_norm),
      jnp.tile(ln_w, 2).reshape(1, 128), jnp.tile(ln_b, 2).reshape(1, 128))


def _t5_bucket(dist):
    n = jnp.maximum(dist, 0)
    max_exact = N_BUCKETS // 2
    nf = jnp.maximum(n, 1).astype(jnp.float32)
    large = max_exact + (jnp.log(nf / max_exact) / math.log(MAX_DISTANCE / max_exact) * (N_BUCKETS - max_exact)).astype(jnp.int32)
    large = jnp.minimum(large, N_BUCKETS - 1)
    return jnp.where(n < max_exact, n, large)


def _bias_tables(rel_bias):
    s = jnp.arange(QB, dtype=jnp.int32)[None, :, None]
    q = jnp.arange(QB, dtype=jnp.int32)[None, None, :]
    dist = q - s + jnp.array([2 * QB, QB, 0], jnp.int32)[:, None, None]
    onehot = (_t5_bucket(dist)[..., None] == jnp.arange(N_BUCKETS, dtype=jnp.int32)).astype(jnp.float32)
    b = jnp.einsum('tsqb,bh->thsq', onehot, rel_bias.astype(jnp.float32) * LOG2E, precision=HIGHEST)
    return jnp.where((dist >= 0)[:, None], b, NEG)


def _attn_kernel(qT_ref, qiT_ref, wT_ref, k_ref, vT_ref, ki_ref, kn_ref, tab_ref, bst_ref, o_ref,
                 keys_ref, hi_ref, lo_ref, msk_ref, p_ref, acc_ref, mp_ref, m_ref, *, topk):
    i = pl.program_id(1)
    n_tiles = i + 1
    row_hi = lax.broadcasted_iota(jnp.int32, (128, QB), 0) >= 64

    def head_rows(ref, h):
        pair = ref[0, (h // 2) * 128:(h // 2) * 128 + 128, :]
        return jnp.where(row_hi == bool(h % 2), pair, jnp.zeros_like(pair))

    def tile_rows(kt):
        return pl.ds(pl.multiple_of(kt * QB, QB), QB)

    def score_tile(kt, carry):
        ki = ki_ref[0, tile_rows(kt), :]
        sc = jnp.zeros((QB, QB), jnp.float32)
        for h in range(IDX_HEADS):
            d = jnp.dot(ki, head_rows(qiT_ref, h), preferred_element_type=jnp.float32)
            sc = sc + wT_ref[0, h:h + 1, :] * jnp.maximum(d, 0.0)
        srow = lax.broadcasted_iota(jnp.int32, (QB, QB), 0)
        qcol = lax.broadcasted_iota(jnp.int32, (QB, QB), 1)
        sc = jnp.where(jnp.abs(sc) < TINY, 0.0, sc)
        sc = jnp.where((kt == i) & (srow > qcol), -jnp.inf, sc)
        bits = pltpu.bitcast(sc, jnp.int32)
        keys_ref[tile_rows(kt), :] = bits ^ ((bits >> 31) & 0x7FFFFFFF)
        hi_ref[tile_rows(kt), :] = pltpu.bitcast(bits & jnp.int32(-65536), jnp.float32).astype(jnp.bfloat16)
        return carry
    lax.fori_loop(0, n_tiles, score_tile, 0)

    def count_packed_ge(ref, cb):
        one, zero = jnp.ones((), jnp.bfloat16), jnp.zeros((), jnp.bfloat16)

        def body(kt, acc):
            hit = jnp.where(ref[tile_rows(kt), :] >= cb, one, zero)
            parts = [hit[r:r + 16, :] for r in range(0, QB, 16)]
            while len(parts) > 1:
                parts = [a + b for a, b in zip(parts[::2], parts[1::2])]
            return acc + parts[0]
        acc = lax.fori_loop(0, n_tiles, body, jnp.zeros((16, QB), jnp.bfloat16))
        return jnp.sum(acc.astype(jnp.float32), axis=0, keepdims=True)

    def count_hi_ge(cand16):
        b = cand16 ^ ((cand16 >> 15) & 0x7FFF)
        snap = jnp.where(((b & 0x8000) != 0) | ((b & 0x7F) == 0), 0, 0x0080)
        b = jnp.where((b & 0x7F80) == 0, snap, b)
        return count_packed_ge(hi_ref, pltpu.bitcast(b << 16, jnp.float32).astype(jnp.bfloat16))

    def mid_code(v):
        pat = jnp.where(v >= 16384, v - 16256, 0x8000 | (16511 - v))
        return pltpu.bitcast(pat << 16, jnp.float32)

    def count(hit_of_tile):
        def body(kt, acc):
            return acc + jnp.sum(hit_of_tile(kt).reshape(QB // 8, 8, QB), axis=0)
        acc = lax.fori_loop(0, n_tiles, body, jnp.zeros((8, QB), jnp.int32))
        return jnp.sum(acc, axis=0, keepdims=True)

    def count_ge(cand):
        return count(lambda kt: jnp.where(keys_ref[tile_rows(kt), :] >= cand, 1, 0))

    def hi_step(it, r):
        cand = jnp.where(it == 0, jnp.zeros_like(r), r | (1 << (15 - it)))
        return jnp.where(count_hi_ge(cand) >= topk, cand, r)
    r16 = lax.fori_loop(0, 16, hi_step, jnp.full((1, QB), -32768, jnp.int32))

    above = count_hi_ge(r16 + 1)

    def code_tile(kt, carry):
        key = keys_ref[tile_rows(kt), :]
        code = jnp.where((key >> 16) == r16, mid_code((key >> 1) & 0x7FFF), -jnp.inf)
        lo_ref[tile_rows(kt), :] = code.astype(jnp.bfloat16)
        return carry
    lax.fori_loop(0, n_tiles, code_tile, 0)

    def mid_step(it, v):
        cand = v | (1 << (14 - it))
        cnt = above + count_packed_ge(lo_ref, mid_code(cand).astype(jnp.bfloat16))
        return jnp.where(cnt >= topk, cand, v)
    v15 = lax.fori_loop(0, 15, mid_step, jnp.zeros((1, QB), jnp.int32))
    thr = (r16 << 16) | (v15 << 1)
    thr = jnp.where(count_ge(thr | 1) >= topk, thr | 1, thr)

    cnt_gt = count_ge(thr + 1)
    cnt_ge = count_ge(thr)
    need = topk - cnt_gt
    tie = (cnt_ge - cnt_gt > need) & (thr > KEY_NEG_INF)

    @pl.when(jnp.max(tie.astype(jnp.int32)) > 0)
    def _():
        def count_eq_below(cand):
            def ind(kt):
                idx = lax.broadcasted_iota(jnp.int32, (QB, QB), 0) + kt * QB
                return jnp.where((keys_ref[tile_rows(kt), :] == thr) & (idx < cand), 1, 0)
            return count(ind)

        def idx_step(it, r):
            cand = r | (1 << (15 - it))
            return jnp.where(count_eq_below(cand) < need, cand, r)
        last = lax.fori_loop(0, 16, idx_step, jnp.zeros((1, QB), jnp.int32))

        def drop(kt, carry):
            blk = keys_ref[tile_rows(kt), :]
            idx = lax.broadcasted_iota(jnp.int32, (QB, QB), 0) + kt * QB
            keys_ref[tile_rows(kt), :] = jnp.where(tie & (blk == thr) & (idx > last), INT_MIN, blk)
            return carry
        lax.fori_loop(0, n_tiles, drop, 0)

    def logits(kt, h):
        band = jnp.clip(kt - (i - 2), 0, 2)
        kp = k_ref[0, tile_rows(kt), (h // 2) * 128:(h // 2) * 128 + 128]
        s = jnp.dot(kp, head_rows(qT_ref, h), preferred_element_type=jnp.float32)
        return s + msk_ref[...] + tab_ref[band, h]

    def set_mask(kt):
        msk_ref[...] = jnp.where(keys_ref[tile_rows(kt), :] >= thr, 0.0, NEG)

    def max_tile(kt, carry):
        set_mask(kt)
        for h in range(ATTN_HEADS):
            s = logits(kt, h)
            mp_ref[h] = jnp.maximum(mp_ref[h], jnp.max(s.reshape(QB // 8, 8, QB), axis=0))
        return carry

    seq = kn_ref.shape[2]
    in_extent = lax.broadcasted_iota(jnp.int32, (ATTN_HEADS, seq), 1) < n_tiles * QB
    k_max = jnp.max(jnp.where(in_extent, kn_ref[0], 0.0), axis=1, keepdims=True)
    spread = jnp.zeros((1, QB), jnp.float32)
    for h in range(ATTN_HEADS):
        qh = qT_ref[0, h * ATTN_HEAD_DIM:(h + 1) * ATTN_HEAD_DIM, :].astype(jnp.float32)
        reach = jnp.sqrt(jnp.sum(qh * qh, axis=0, keepdims=True) * k_max[h:h + 1, :]) * NORM_SLACK
        m_ref[h:h + 1, :] = reach + bst_ref[0, h:h + 1, :]
        spread = jnp.maximum(spread, 2.0 * reach + bst_ref[1, h:h + 1, :])
    bound_ok = jnp.max(spread) <= MAX_SHIFT_ERROR

    @pl.when(jnp.logical_not(bound_ok))
    def _():
        mp_ref[...] = jnp.full(mp_ref.shape, NEG, jnp.float32)
        lax.fori_loop(0, n_tiles, max_tile, 0)
        for h in range(ATTN_HEADS):
            m_ref[h:h + 1, :] = jnp.max(mp_ref[h], axis=0, keepdims=True)
    m = [m_ref[h:h + 1, :] for h in range(ATTN_HEADS)]

    def exp_tile(kt, carry):
        set_mask(kt)
        for h in range(ATTN_HEADS):
            p_ref[h] = jnp.exp2(logits(kt, h) - m[h]).astype(jnp.bfloat16)
        for h in range(ATTN_HEADS):
            va = vT_ref[0, h * VROWS:(h + 1) * VROWS, tile_rows(kt)]
            acc_ref[h * VROWS:(h + 1) * VROWS, :] += jnp.dot(va, p_ref[h], preferred_element_type=jnp.float32)
        return carry

    acc_ref[...] = jnp.zeros(acc_ref.shape, jnp.float32)
    lax.fori_loop(0, n_tiles, exp_tile, 0)

    outs = [acc_ref[h * VROWS:h * VROWS + ATTN_HEAD_DIM, :] / acc_ref[h * VROWS + ATTN_HEAD_DIM:h * VROWS + ATTN_HEAD_DIM + 1, :]
            for h in range(ATTN_HEADS)]
    o_ref[0] = jnp.concatenate(outs, axis=0).T


def _dsa_attention(qT, qiT, wT, k, vT, ki2, kn2, rel_bias):
    bsz, _, seq = qT.shape
    topk = min(TOPK_MAX, seq // 4)
    assert seq % QB == 0 and topk <= QB
    assert seq // 16 <= 256
    b2 = rel_bias.astype(jnp.float32) * LOG2E
    bias_stats = jnp.stack([jnp.max(b2, axis=0), jnp.max(b2, axis=0) - jnp.min(b2, axis=0)])
    bias_stats = jnp.broadcast_to(bias_stats[:, :, None], (2, ATTN_HEADS, QB))
    return pl.pallas_call(
        functools.partial(_attn_kernel, topk=topk),
        grid=(bsz, seq // QB),
        in_specs=[
            pl.BlockSpec((1, ATTN_WIDTH, QB), lambda b, i: (b, 0, i)),
            pl.BlockSpec((1, IDX_HEADS * IDX_DIM, QB), lambda b, i: (b, 0, i)),
            pl.BlockSpec((1, IDX_HEADS, QB), lambda b, i: (b, 0, i)),
            pl.BlockSpec((1, seq, ATTN_WIDTH), lambda b, i: (b, 0, 0)),
            pl.BlockSpec((1, ATTN_HEADS * VROWS, seq), lambda b, i: (b, 0, 0)),
            pl.BlockSpec((1, seq, 128), lambda b, i: (b, 0, 0)),
            pl.BlockSpec((1, ATTN_HEADS, seq), lambda b, i: (b, 0, 0)),
            pl.BlockSpec((3, ATTN_HEADS, QB, QB), lambda b, i: (0, 0, 0, 0)),
            pl.BlockSpec((2, ATTN_HEADS, QB), lambda b, i: (0, 0, 0)),
        ],
        out_specs=pl.BlockSpec((1, QB, ATTN_WIDTH), lambda b, i: (b, i, 0)),
        out_shape=jax.ShapeDtypeStruct((bsz, seq, ATTN_WIDTH), jnp.float32),
        scratch_shapes=[
            pltpu.VMEM((seq, QB), jnp.int32),
            pltpu.VMEM((seq, QB), jnp.bfloat16),
            pltpu.VMEM((seq, QB), jnp.bfloat16),
            pltpu.VMEM((QB, QB), jnp.float32),
            pltpu.VMEM((ATTN_HEADS, QB, QB), jnp.bfloat16),
            pltpu.VMEM((ATTN_HEADS * VROWS, QB), jnp.float32),
            pltpu.VMEM((ATTN_HEADS, 8, QB), jnp.float32),
            pltpu.VMEM((ATTN_HEADS, QB), jnp.float32),
        ],
        compiler_params=pltpu.CompilerParams(dimension_semantics=("parallel", "arbitrary"),
                                             vmem_limit_bytes=VMEM_LIMIT),
        name="dsa_attention",
    )(qT, qiT, wT, k, vT, ki2, kn2, _bias_tables(rel_bias), bias_stats)


def _ssd_kernel(xbc_ref, z_ref, sm_ref, cw_ref, cb_ref, dtb_ref, a_ref, dsk_ref, nw_ref, y_ref, prev_ref, st_ref):
    q = SSD_CHUNK
    bf = jnp.bfloat16

    @pl.when(pl.program_id(1) == 0)
    def _():
        prev_ref[...] = jnp.zeros(prev_ref.shape, jnp.float32)
        st_ref[...] = jnp.zeros(st_ref.shape, jnp.float32)

    cur = xbc_ref[...]
    prev = prev_ref[...]
    row = lax.broadcasted_iota(jnp.int32, cur.shape, 0)
    acc = cur * cw_ref[CONV_WIDTH - 1:CONV_WIDTH, :] + cb_ref[...]
    for s in range(1, CONV_WIDTH):
        shifted = jnp.where(row >= s, pltpu.roll(cur, s, 0), pltpu.roll(prev, s, 0))
        acc = acc + shifted * cw_ref[CONV_WIDTH - 1 - s:CONV_WIDTH - s, :]
    prev_ref[...] = cur
    u = acc * jax.nn.sigmoid(acc)
    xs = u[:, :SSM_INNER]
    bm = u[:, SSM_INNER:SSM_INNER + SSM_GROUPS * SSM_STATE].astype(bf)
    cm = u[:, SSM_INNER + SSM_GROUPS * SSM_STATE:].astype(bf)

    t = sm_ref[...] + dtb_ref[...]
    dt = jnp.maximum(t, 0.0) + jnp.log1p(jnp.exp(-jnp.abs(t)))
    ii = lax.broadcasted_iota(jnp.int32, (q, q), 0)
    jj = lax.broadcasted_iota(jnp.int32, (q, q), 1)
    causal = ii >= jj
    acum = jnp.dot(causal.astype(jnp.float32), dt * a_ref[...], preferred_element_type=jnp.float32, precision=HIGHEST)
    acum_t = acum.T
    dt_t = dt.T
    ea = jnp.exp(acum)
    last = acum[q - 1:q, :]
    decay = jnp.exp(last - acum) * dt
    ea_last = jnp.exp(last)

    lane_hi = lax.broadcasted_iota(jnp.int32, (q, 128), 1) >= SSM_HEAD_DIM
    row_hi = lax.broadcasted_iota(jnp.int32, (128, SSM_STATE), 0) >= SSM_HEAD_DIM

    def pair_cols(v, e):
        c0, c1 = SMALL_DT + e, SMALL_DT + e + 1
        return jnp.where(lane_hi, v[:, c1:c1 + 1], v[:, c0:c0 + 1])

    for g in range(SSM_GROUPS):
        bg = bm[:, g * SSM_STATE:(g + 1) * SSM_STATE]
        cg = cm[:, g * SSM_STATE:(g + 1) * SSM_STATE]
        cb = lax.dot_general(cg, bg, NT, preferred_element_type=jnp.float32)
        for k in range(g * 4, g * 4 + 4):
            e = 2 * k
            x_pair = xs[:, k * 128:(k + 1) * 128]
            halves = []
            for h in (e, e + 1):
                c = SMALL_DT + h
                seg = acum[:, c:c + 1] - acum_t[c:c + 1, :]
                w = cb * jnp.exp(jnp.where(causal, seg, -jnp.inf)) * dt_t[c:c + 1, :]
                halves.append(jnp.dot(w.astype(bf), x_pair.astype(bf), preferred_element_type=jnp.float32))
            y_pair = jnp.where(lane_hi, halves[1], halves[0])
            state = st_ref[k]
            y_pair = y_pair + lax.dot_general(cg, state.astype(bf), NT, preferred_element_type=jnp.float32) * pair_cols(ea, e)
            y_ref[:, k * 128:(k + 1) * 128] = y_pair
            xd_t = (x_pair * pair_cols(decay, e)).T.astype(bf)
            c0 = SMALL_DT + e
            keep = jnp.where(row_hi, ea_last[:, c0 + 1:c0 + 2], ea_last[:, c0:c0 + 1])
            st_ref[k] = state * keep + jnp.dot(xd_t, bg, preferred_element_type=jnp.float32)

    y = (y_ref[...] + dsk_ref[...] * xs) * (z_ref[...] * jax.nn.sigmoid(z_ref[...]))
    half = SSM_INNER // SSM_GROUPS
    for g in range(SSM_GROUPS):
        yg = y[:, g * half:(g + 1) * half]
        yg = yg * lax.rsqrt(jnp.mean(yg * yg, axis=-1, keepdims=True) + EPS)
        y_ref[:, g * half:(g + 1) * half] = yg * nw_ref[:, g * half:(g + 1) * half]


def _mamba2_ssd(proj, bsz, seq, conv_w, conv_b, dt_bias, a_log, d_skip, norm_w):
    q = SSD_CHUNK
    nc = seq // q
    lane_row = lambda v: jnp.zeros((1, 128), jnp.float32).at[0, SMALL_DT:SMALL_DT + SSM_HEADS].set(v)
    const = lambda shape: pl.BlockSpec(shape, lambda b, c: (0,) * len(shape))
    return pl.pallas_call(
        _ssd_kernel,
        grid=(bsz, nc),
        in_specs=[pl.BlockSpec((q, CONV_CH), lambda b, c: (b * nc + c, COL_XBC // CONV_CH)),
                  pl.BlockSpec((q, SSM_INNER), lambda b, c: (b * nc + c, COL_Z // SSM_INNER)),
                  pl.BlockSpec((q, 128), lambda b, c: (b * nc + c, COL_SMALL // 128)),
                  const((CONV_WIDTH, CONV_CH)), const((1, CONV_CH)), const((1, 128)), const((1, 128)),
                  const((1, SSM_INNER)), const((1, SSM_INNER))],
        out_specs=pl.BlockSpec((q, SSM_INNER), lambda b, c: (b * nc + c, 0)),
        out_shape=jax.ShapeDtypeStruct((bsz * seq, SSM_INNER), jnp.float32),
        scratch_shapes=[pltpu.VMEM((q, CONV_CH), jnp.float32),
                        pltpu.VMEM((SSM_HEADS // 2, 2 * SSM_HEAD_DIM, SSM_STATE), jnp.float32)],
        compiler_params=pltpu.CompilerParams(dimension_semantics=("parallel", "arbitrary"),
                                             vmem_limit_bytes=VMEM_LIMIT),
        name="mamba2_ssd",
    )(proj, proj, proj, conv_w, conv_b.reshape(1, CONV_CH), lane_row(dt_bias), lane_row(-jnp.exp(a_log)),
      jnp.repeat(d_skip, SSM_HEAD_DIM).reshape(1, SSM_INNER), norm_w.reshape(1, SSM_INNER))


def _mix_out_kernel(a_ref, s_ref, gl_ref, x_ref, gm_ref, wo_ref, ws_ref, wout_ref,
                    nf_ref, scf_ref, shf_ref, wr_ref, br_ref, xo_ref, h_ref, rt_ref, gt_ref, cnt_ref):
    bf = jnp.bfloat16
    ya = jnp.dot(a_ref[...].astype(bf), wo_ref[...], preferred_element_type=jnp.float32)
    ys = jnp.dot(s_ref[...].astype(bf), ws_ref[...], preferred_element_type=jnp.float32)
    mixed = jax.nn.sigmoid(gl_ref[:, :D_MODEL]) * ya + jax.nn.sigmoid(gl_ref[:, D_MODEL:]) * ys
    x = x_ref[...] + gm_ref[0] * jnp.dot(mixed.astype(bf), wout_ref[...], preferred_element_type=jnp.float32)
    xo_ref[...] = x
    y = x * lax.rsqrt(jnp.mean(x * x, axis=-1, keepdims=True) + EPS) * nf_ref[...]
    h = y * (1.0 + scf_ref[0]) + shf_ref[0]
    h_hi = h.astype(bf)
    hb = pltpu.bitcast(h_hi.astype(jnp.float32), jnp.int32)
    half = D_MODEL // 2
    h_ref[...] = (hb[:, :half] & jnp.int32(-65536)) | lax.shift_right_logical(hb[:, half:], 16)
    h_lo = (h - h_hi.astype(jnp.float32)).astype(bf)
    dot = functools.partial(jnp.dot, preferred_element_type=jnp.float32)
    lg = dot(h_hi, wr_ref[0]) + (dot(h_lo, wr_ref[0]) + dot(h_hi, wr_ref[1])) + br_ref[...]

    tm = lg.shape[0]
    work = lg.T[:N_EXPERTS, :]
    row = lax.broadcasted_iota(jnp.int32, work.shape, 0).astype(jnp.float32)
    vals, eids, hits = [], [], []
    for _ in range(TOP_K):
        mx = jnp.max(work, axis=0, keepdims=True)
        ix = jnp.min(jnp.where(work == mx, row, float(N_EXPERTS)), axis=0, keepdims=True)
        vals.append(mx)
        eids.append(ix)
        hits.append(row == ix)
        work = jnp.where(hits[-1], -jnp.inf, work)
    ex = [jnp.exp(v - vals[0]) for v in vals]
    den = (ex[0] + ex[1]) + (ex[2] + ex[3])

    @pl.when(pl.program_id(0) == 0)
    def _():
        cnt_ref[...] = jnp.zeros(cnt_ref.shape, jnp.float32)
    chosen = jnp.zeros(work.shape, jnp.float32)
    for hit in hits:
        chosen = jnp.where(hit, 1.0, chosen)
    earlier = lax.broadcasted_iota(jnp.int32, (tm, tm), 0) < lax.broadcasted_iota(jnp.int32, (tm, tm), 1)
    before = dot(chosen.astype(bf), earlier.astype(bf)) + cnt_ref[:, 0:1]
    cnt_ref[...] = cnt_ref[...] + jnp.sum(chosen, axis=1, keepdims=True)

    slot = lax.broadcasted_iota(jnp.int32, (128, tm), 0)
    route = jnp.zeros((128, tm), jnp.float32)
    for k, hit in enumerate(hits):
        rank = jnp.sum(jnp.where(hit, before, 0.0), axis=0, keepdims=True)
        route = jnp.where(slot == k, eids[k], route)
        route = jnp.where(slot == TOP_K + k, ex[k] / den, route)
        route = jnp.where(slot == 2 * TOP_K + k, rank, route)
    rt_ref[...] = route[:ROUTE_ROWS, :]
    gt_ref[...] = route.T


def _mix_out(attn2, ssd2, proj, x2, g_m, w_attn_o, w_ssm_o, w_out, norm_ffn, sc_f, sh_f, w_router, b_router, seq, tm=512):
    t, d = x2.shape
    per_b = seq // tm
    bf = jnp.bfloat16
    const = lambda shape: pl.BlockSpec(shape, lambda i: (0,) * len(shape))
    perb = pl.BlockSpec((1, 1, d), lambda i: (i // per_b, 0, 0))
    wr = jnp.pad(w_router.astype(jnp.float32), ((0, 0), (0, 128 - N_EXPERTS)))
    wr_hi = wr.astype(bf)
    wr = jnp.stack([wr_hi, (wr - wr_hi.astype(jnp.float32)).astype(bf)])
    br =jnp.pad(b_router, (0, 128 - N_EXPERTS)).reshape(1, 128)
    return pl.pallas_call(
        _mix_out_kernel,
        grid=(t // tm,),
        in_specs=[pl.BlockSpec((tm, ATTN_WIDTH), lambda i: (i, 0)),
                  pl.BlockSpec((tm, SSM_INNER), lambda i: (i, 0)),
                  pl.BlockSpec((tm, 2 * d), lambda i: (i, COL_GATE // (2 * d))),
                  pl.BlockSpec((tm, d), lambda i: (i, 0)),
                  perb,
                  const((ATTN_WIDTH, d)), const((SSM_INNER, d)), const((d, d)),
                  const((1, d)), perb, perb, const((2, d, 128)), const((1, 128))],
        out_specs=[pl.BlockSpec((tm, d), lambda i: (i, 0)),
                   pl.BlockSpec((tm, d // 2), lambda i: (i, 0)),
                   pl.BlockSpec((ROUTE_ROWS, tm), lambda i: (0, i)),
                   pl.BlockSpec((tm, 128), lambda i: (i, 0)),
                   pl.BlockSpec((N_EXPERTS, 128), lambda i: (0, 0))],
        out_shape=[jax.ShapeDtypeStruct((t, d), jnp.float32),
                   jax.ShapeDtypeStruct((t, d // 2), jnp.int32),
                   jax.ShapeDtypeStruct((ROUTE_ROWS, t), jnp.float32),
                   jax.ShapeDtypeStruct((t, 128), jnp.float32),
                   jax.ShapeDtypeStruct((N_EXPERTS, 128), jnp.float32)],
        compiler_params=pltpu.CompilerParams(dimension_semantics=("arbitrary",), vmem_limit_bytes=VMEM_LIMIT),
        name="mix_out",
    )(attn2, ssd2, proj, x2, g_m[:, None, :], w_attn_o.astype(bf), w_ssm_o.astype(bf), w_out.astype(bf),
      norm_ffn.reshape(1, d), sc_f[:, None, :], sh_f[:, None, :], wr, br)


def _moe_kernel(be_ref, nb_ref, x_ref, wgu_ref, bgu_ref, wdn_ref, bdn_ref, *rest, first_block):
    o_ref, wgu_bf, wdn_bf = rest[-3:]
    i = pl.program_id(0)
    j = i + first_block

    @pl.when((i == 0) | (be_ref[j] != be_ref[jnp.maximum(j - 1, 0)]))
    def _():
        wgu_bf[...] = wgu_ref[0, 0].astype(jnp.bfloat16)
        wdn_bf[...] = wdn_ref[0, 0].astype(jnp.bfloat16)

    @pl.when(j < nb_ref[0])
    def _():
        words = x_ref[...]
        x_hi = pltpu.bitcast(words & jnp.int32(-65536), jnp.float32).astype(jnp.bfloat16)
        x_lo = pltpu.bitcast(words << 16, jnp.float32).astype(jnp.bfloat16)
        x = jnp.concatenate([x_hi, x_lo], axis=1)
        gu = jnp.dot(x, wgu_bf[...], preferred_element_type=jnp.float32) + bgu_ref[0, 0]
        g = jnp.minimum(gu[:, :D_EXPERT], SWIGLU_LIMIT)
        u = jnp.clip(gu[:, D_EXPERT:], -SWIGLU_LIMIT, SWIGLU_LIMIT)
        act = (u + 1.0) * (g * jax.nn.sigmoid(SWIGLU_ALPHA * g))
        out = jnp.dot(act.astype(jnp.bfloat16), wdn_bf[...], preferred_element_type=jnp.float32) + bdn_ref[0, 0]
        o_ref[...] = out.astype(o_ref.dtype)

    @pl.when(j >= nb_ref[0])
    def _():
        o_ref[...] = jnp.zeros_like(o_ref)


def _moe_ffn(xs_parts, blk_exp, n_used, w_gu, b_gu, w_dn, b_dn, layer):
    d = D_MODEL
    tm = MOE_TM
    n_rows = sum(xs.shape[0] for xs in xs_parts)
    out, first = None, 0
    for xs in xs_parts:
        nblk = xs.shape[0] // tm
        wmap = lambda i, be, nb, first=first: (layer, be[i + first], 0, 0)
        in_specs = [pl.BlockSpec((tm, d // 2), lambda i, be, nb: (i, 0)),
                    pl.BlockSpec((1, 1, d, 2 * D_EXPERT), wmap),
                    pl.BlockSpec((1, 1, 1, 2 * D_EXPERT), wmap),
                    pl.BlockSpec((1, 1, D_EXPERT, d), wmap),
                    pl.BlockSpec((1, 1, 1, d), wmap)]
        args = [blk_exp, n_used, xs, w_gu, b_gu[:, :, None, :], w_dn, b_dn[:, :, None, :]]
        aliases = {}
        if out is not None:
            in_specs.append(pl.BlockSpec(memory_space=pl.ANY))
            args.append(out)
            aliases = {len(args) - 1: 0}
        grid_spec = pltpu.PrefetchScalarGridSpec(
            num_scalar_prefetch=2,
            grid=(nblk,),
            in_specs=in_specs,
            out_specs=pl.BlockSpec((tm, d), lambda i, be, nb, first=first: (i + first, 0)),
            scratch_shapes=[pltpu.VMEM((d, 2 * D_EXPERT), jnp.bfloat16), pltpu.VMEM((D_EXPERT, d), jnp.bfloat16)],
        )
        out = pl.pallas_call(
            functools.partial(_moe_kernel, first_block=first),
            grid_spec=grid_spec,
            out_shape=jax.ShapeDtypeStruct((n_rows, d), jnp.bfloat16),
            input_output_aliases=aliases,
            compiler_params=pltpu.CompilerParams(dimension_semantics=("arbitrary",), vmem_limit_bytes=VMEM_LIMIT),
            name="moe_ffn",
        )(*args)
        first += nblk
    return out


def _moe(h2, route, expert_counts, w_gu, b_gu, w_dn, b_dn, layer):
    t = h2.shape[0]
    d = D_MODEL
    tm = MOE_TM
    i32 = jnp.int32
    experts = jnp.arange(N_EXPERTS, dtype=i32)
    top_idx = route[:TOP_K].astype(i32)
    rank = route[2 * TOP_K:3 * TOP_K].astype(i32)
    n_assign = t * TOP_K
    n_rows = n_assign + N_EXPERTS * tm
    e_flat = top_idx.T.reshape(n_assign)
    counts = expert_counts[:, 0].astype(i32)
    padded = (counts + tm - 1) // tm * tm
    pad_start = jnp.cumsum(padded) - padded
    dest = rank + jnp.sum(jnp.where(top_idx[..., None] == experts, pad_start, 0), axis=-1)
    dest = dest.reshape(-1)
    filler_exp = jnp.repeat(experts, tm)
    filler_key = jnp.where(jnp.tile(jnp.arange(tm, dtype=i32), N_EXPERTS) < jnp.repeat(padded - counts, tm),
                           filler_exp, N_EXPERTS)
    keys = jnp.concatenate([e_flat, filler_key])
    rows = jnp.arange(n_rows, dtype=i32)
    row_key, row_src = lax.sort((keys, rows), num_keys=1)
    row_tok = jnp.where(row_src < n_assign, row_src // TOP_K, rows % t)
    blk_exp = jnp.minimum(row_key[::tm], N_EXPERTS - 1)
    n_used = (jnp.sum(padded, keepdims=True) // tm).astype(i32)
    slab = n_rows // MOE_SLABS
    xs_parts = [h2[row_tok[s * slab:(s + 1) * slab]] for s in range(MOE_SLABS)]
    out = _moe_ffn(xs_parts, blk_exp, n_used, w_gu, b_gu, w_dn, b_dn, layer)
    return out[dest].reshape(TOP_K, t, d)


def _combine_kernel(p_ref, r_ref, x_ref, g_ref, o_ref):
    f32 = jnp.float32
    w = [r_ref[:, TOP_K + k:TOP_K + k + 1] for k in range(TOP_K)]
    y = (w[0] * p_ref[0].astype(f32) + w[1] * p_ref[1].astype(f32)) + (w[2] * p_ref[2].astype(f32) + w[3] * p_ref[3].astype(f32))
    o_ref[...] = x_ref[...] + g_ref[0] * y


def _combine(parts, route_tok, x2, g_f, seq, tm=512):
    t, d = x2.shape
    per_b = seq // tm
    return pl.pallas_call(
        _combine_kernel,
        grid=(t // tm,),
        in_specs=[pl.BlockSpec((TOP_K, tm, d), lambda i: (0, i, 0)),
                  pl.BlockSpec((tm, 128), lambda i: (i, 0)),
                  pl.BlockSpec((tm, d), lambda i: (i, 0)),
                  pl.BlockSpec((1, 1, d), lambda i: (i // per_b, 0, 0))],
        out_specs=pl.BlockSpec((tm, d), lambda i: (i, 0)),
        out_shape=jax.ShapeDtypeStruct((t, d), jnp.float32),
        compiler_params=pltpu.CompilerParams(dimension_semantics=("parallel",), vmem_limit_bytes=VMEM_LIMIT),
        name="moe_combine",
    )(parts, route_tok, x2, g_f[:, None, :])


def kernel(x, c, rel_bias, w_ada, b_ada, norm_mix, norm_ffn, w_in, kv_norm, w_kv_up, q_norm, k_norm,
           idx_k_ln_w, idx_k_ln_b, w_attn_o, conv_w, conv_b, dt_bias, a_log, d_skip, ssm_norm, w_ssm_o,
           w_out, w_router, b_router, w_gu, b_gu, w_dn, b_dn):
    bsz, seq, d = x.shape
    t = bsz * seq
    cond = jax.nn.silu(c)
    x2 = x.reshape(t, d)
    for l in range(DEPTH):
        mod = cond @ w_ada[l] + b_ada[l]
        sh_m, sc_m, g_m, sh_f, sc_f, g_f = jnp.split(mod, 6, axis=-1)
        proj = _in_proj(x2, norm_mix[l], sc_m, sh_m, _pack_w_in(w_in[l]), seq)
        qT, k, vT, qiT, ki2, wT, kn2 = _prep(proj, bsz, seq, q_norm[l], kv_norm[l], w_kv_up[l], k_norm[l],
                                             idx_k_ln_w[l], idx_k_ln_b[l])
        attn = _dsa_attention(qT, qiT, wT, k, vT, ki2, kn2, rel_bias)
        y_ssd = _mamba2_ssd(proj, bsz, seq, conv_w[l], conv_b[l], dt_bias[l], a_log[l], d_skip[l], ssm_norm[l])
        x2, h2, route, route_tok, expert_counts = _mix_out(
            attn.reshape(t, ATTN_WIDTH), y_ssd, proj, x2, g_m, w_attn_o[l], w_ssm_o[l], w_out[l], norm_ffn[l],
            sc_f, sh_f, w_router[l], b_router[l], seq)
        parts = _moe(h2, route, expert_counts, w_gu, b_gu, w_dn, b_dn, l)
        x2 = _combine(parts, route_tok, x2, g_f, seq)
    return x2.reshape(bsz, seq, d)
```

```python
import functools
import math

import jax
import jax.numpy as jnp
import numpy as np
from jax import lax
from jax.experimental import pallas as pl
from jax.experimental.pallas import tpu as pltpu

D_MODEL = 1024
DEPTH = 2
ATTN_HEADS = 8
ATTN_HEAD_DIM = 64
ATTN_WIDTH = ATTN_HEADS * ATTN_HEAD_DIM
KV_RANK = 256
IDX_HEADS = 8
IDX_DIM = 64
TOPK_MAX = 256
N_BUCKETS = 32
MAX_DISTANCE = 128
SSM_HEADS = 16
SSM_HEAD_DIM = 64
SSM_INNER = SSM_HEADS * SSM_HEAD_DIM
SSM_GROUPS = 2
SSM_STATE = 128
CONV_WIDTH = 4
CONV_CH = SSM_INNER + 2 * SSM_GROUPS * SSM_STATE
SSD_CHUNK = 128
N_EXPERTS = 32
TOP_K = 4
D_EXPERT = D_MODEL
SWIGLU_LIMIT = 7.0
SWIGLU_ALPHA = 1.702
EPS = 1e-6

COL_Q = 0
COL_KV = 512
COL_QI = 768
COL_SMALL = 1280
COL_XBC = 1536
COL_Z = 3072
COL_GATE = 4096
PROJ_COLS = 6144
PREP_COLS = 1408
SMALL_KI, SMALL_WI, SMALL_DT = 0, 64, 72

QB = 256
VROWS = 80
INT_MIN = -2 ** 31
KEY_NEG_INF = (0xFF800000 ^ 0x7FFFFFFF) - 2 ** 32
NEG = -1e30
TINY = 2.0 ** -126
LOG2E = math.log2(math.e)
NORM_SLACK = 1.02
MAX_SHIFT_ERROR = 96.0
VMEM_LIMIT = 56 * 1024 * 1024
MOE_TM = 512
MOE_SLABS = 4
ROUTE_ROWS = 16
HIGHEST = lax.Precision.HIGHEST
NT = (((1,), (1,)), ((), ()))


def _pack_w_in(w):
    o = np.cumsum((0, ATTN_WIDTH, KV_RANK, IDX_HEADS * IDX_DIM, IDX_DIM, IDX_HEADS, SSM_INNER, CONV_CH, SSM_HEADS, 2 * D_MODEL))
    q, kv, qi, ki, wi, z, xbc, dt, gate = (w[:, int(o[n]):int(o[n + 1])] for n in range(9))
    zeros = lambda n: jnp.zeros((w.shape[0], n), w.dtype)
    small = jnp.concatenate([ki, wi, dt, zeros(128 - 88)], axis=1)
    packed = jnp.concatenate([q, kv, qi, small, zeros(COL_XBC - PREP_COLS), xbc, z, gate], axis=1)
    assert packed.shape[1] == PROJ_COLS
    return packed.astype(jnp.bfloat16)


def _in_proj_kernel(x_ref, g_ref, sc_ref, sh_ref, w_ref, o_ref, h_ref):
    @pl.when(pl.program_id(1) == 0)
    def _():
        x = x_ref[...]
        y = x * lax.rsqrt(jnp.mean(x * x, axis=-1, keepdims=True) + EPS) * g_ref[...]
        h_ref[...] = (y * (1.0 + sc_ref[0]) + sh_ref[0]).astype(jnp.bfloat16)
    o_ref[...] = jnp.dot(h_ref[...], w_ref[...], preferred_element_type=jnp.float32)


def _in_proj(x2, gain, sc, sh, w_packed, seq, tm=1024, tn=2048):
    t, d = x2.shape
    per_b = seq // tm
    return pl.pallas_call(
        _in_proj_kernel,
        grid=(t // tm, PROJ_COLS // tn),
        in_specs=[pl.BlockSpec((tm, d), lambda i, j: (i, 0)),
                  pl.BlockSpec((1, d), lambda i, j: (0, 0)),
                  pl.BlockSpec((1, 1, d), lambda i, j: (i // per_b, 0, 0)),
                  pl.BlockSpec((1, 1, d), lambda i, j: (i // per_b, 0, 0)),
                  pl.BlockSpec((d, tn), lambda i, j: (0, j))],
        out_specs=pl.BlockSpec((tm, tn), lambda i, j: (i, j)),
        out_shape=jax.ShapeDtypeStruct((t, PROJ_COLS), jnp.float32),
        scratch_shapes=[pltpu.VMEM((tm, d), jnp.bfloat16)],
        compiler_params=pltpu.CompilerParams(dimension_semantics=("parallel", "arbitrary"),
                                             vmem_limit_bytes=VMEM_LIMIT),
        name="in_proj",
    )(x2, gain.reshape(1, d), sc[:, None, :], sh[:, None, :], w_packed)


def _head_rms_t(xt):
    x3 = xt.reshape(ATTN_HEADS, ATTN_HEAD_DIM, xt.shape[1])
    return lax.rsqrt(jnp.mean(x3 * x3, axis=1, keepdims=True) + EPS)


def _prep_kernel(p_ref, qg_ref, kvg_ref, wkv_ref, kg_ref, lng_ref, lnb_ref,
                 qT_ref, k_ref, vT_ref, qiT_ref, ki_ref, wT_ref, kn2_ref):
    n = p_ref.shape[0]
    q = p_ref[:, COL_Q:COL_Q + ATTN_WIDTH]
    lat = p_ref[:, COL_KV:COL_KV + KV_RANK]
    qi = p_ref[:, COL_QI:COL_QI + IDX_HEADS * IDX_DIM]
    sm = p_ref[:, COL_SMALL:COL_SMALL + 128]

    scale = ATTN_HEAD_DIM ** -0.5 * LOG2E
    qt = q.T
    qn = qt.reshape(ATTN_HEADS, ATTN_HEAD_DIM, n) * _head_rms_t(qt)
    qT_ref[0] = (qn.reshape(ATTN_WIDTH, n) * qg_ref[...] * scale).astype(jnp.bfloat16)

    latn = lat * lax.rsqrt(jnp.mean(lat * lat, axis=-1, keepdims=True) + EPS) * kvg_ref[...]
    kv = jnp.dot(latn.astype(jnp.bfloat16), wkv_ref[...], preferred_element_type=jnp.float32)
    kt = kv[:, :ATTN_WIDTH].T
    kn = (kt.reshape(ATTN_HEADS, ATTN_HEAD_DIM, n) * _head_rms_t(kt)).reshape(ATTN_WIDTH, n) * kg_ref[...]
    k_ref[0] = kn.T.astype(jnp.bfloat16)
    kn3 = kn.reshape(ATTN_HEADS, ATTN_HEAD_DIM, n)
    kn2_ref[0] = jnp.sum(kn3 * kn3, axis=1)
    vt = kv[:, ATTN_WIDTH:].T.reshape(ATTN_HEADS, ATTN_HEAD_DIM, n)
    ones = jnp.ones((ATTN_HEADS, VROWS - ATTN_HEAD_DIM, n), jnp.float32)
    vT_ref[0] = jnp.concatenate([vt, ones], axis=1).reshape(ATTN_HEADS * VROWS, n).astype(jnp.bfloat16)

    qiT_ref[0] = (qi * (IDX_DIM ** -0.5)).T.astype(jnp.bfloat16)

    lane = lax.broadcasted_iota(jnp.int32, sm.shape, 1)
    kid = jnp.where(lane < IDX_DIM, sm, pltpu.roll(sm, IDX_DIM, 1))
    mu = jnp.mean(kid, axis=-1, keepdims=True)
    var = jnp.mean(jnp.square(kid - mu), axis=-1, keepdims=True)
    ki_ref[0] = ((kid - mu) * lax.rsqrt(var + EPS) * lng_ref[...] + lnb_ref[...]).astype(jnp.bfloat16)

    wT_ref[0] = sm.T[SMALL_WI:SMALL_WI + IDX_HEADS, :] * (IDX_HEADS ** -0.5)


def _prep(proj, bsz, seq, q_norm, kv_norm, w_kv_up, k_norm, ln_w, ln_b, tp=512):
    nb = seq // tp
    tile8 = lambda g: jnp.tile(g, ATTN_HEADS).reshape(ATTN_WIDTH, 1)
    const = lambda shape: pl.BlockSpec(shape, lambda b, i: (0,) * len(shape))
    bf = jnp.bfloat16
    return pl.pallas_call(
        _prep_kernel,
        grid=(bsz, nb),
        in_specs=[pl.BlockSpec((tp, PREP_COLS), lambda b, i: (b * nb + i, 0)),
                  const((ATTN_WIDTH, 1)), const((1, KV_RANK)), const((KV_RANK, 2 * ATTN_WIDTH)),
                  const((ATTN_WIDTH, 1)), const((1, 128)), const((1, 128))],
        out_specs=[pl.BlockSpec((1, ATTN_WIDTH, tp), lambda b, i: (b, 0, i)),
                   pl.BlockSpec((1, tp, ATTN_WIDTH), lambda b, i: (b, i, 0)),
                   pl.BlockSpec((1, ATTN_HEADS * VROWS, tp), lambda b, i: (b, 0, i)),
                   pl.BlockSpec((1, ATTN_WIDTH, tp), lambda b, i: (b, 0, i)),
                   pl.BlockSpec((1, tp, 128), lambda b, i: (b, i, 0)),
                   pl.BlockSpec((1, IDX_HEADS, tp), lambda b, i: (b, 0, i)),
                   pl.BlockSpec((1, ATTN_HEADS, tp), lambda b, i: (b, 0, i))],
        out_shape=[jax.ShapeDtypeStruct((bsz, ATTN_WIDTH, seq), bf),
                   jax.ShapeDtypeStruct((bsz, seq, ATTN_WIDTH), bf),
                   jax.ShapeDtypeStruct((bsz, ATTN_HEADS * VROWS, seq), bf),
                   jax.ShapeDtypeStruct((bsz, ATTN_WIDTH, seq), bf),
                   jax.ShapeDtypeStruct((bsz, seq, 128), bf),
                   jax.ShapeDtypeStruct((bsz, IDX_HEADS, seq), jnp.float32),
                   jax.ShapeDtypeStruct((bsz, ATTN_HEADS, seq), jnp.float32)],
        compiler_params=pltpu.CompilerParams(dimension_semantics=("parallel", "parallel"),
                                             vmem_limit_bytes=VMEM_LIMIT),
        name="attn_prep",
    )(proj, tile8(q_norm), kv_norm.reshape(1, KV_RANK), w_kv_up.astype(bf), tile8(k_norm),
      jnp.tile(ln_w, 2).reshape(1, 128), jnp.tile(ln_b, 2).reshape(1, 128))


def _t5_bucket(dist):
    n = jnp.maximum(dist, 0)
    max_exact = N_BUCKETS // 2
    nf = jnp.maximum(n, 1).astype(jnp.float32)
    large = max_exact + (jnp.log(nf / max_exact) / math.log(MAX_DISTANCE / max_exact) * (N_BUCKETS - max_exact)).astype(jnp.int32)
    large = jnp.minimum(large, N_BUCKETS - 1)
    return jnp.where(n < max_exact, n, large)


def _bias_tables(rel_bias):
    s = jnp.arange(QB, dtype=jnp.int32)[None, :, None]
    q = jnp.arange(QB, dtype=jnp.int32)[None, None, :]
    dist = q - s + jnp.array([2 * QB, QB, 0], jnp.int32)[:, None, None]
    onehot = (_t5_bucket(dist)[..., None] == jnp.arange(N_BUCKETS, dtype=jnp.int32)).astype(jnp.float32)
    b = jnp.einsum('tsqb,bh->thsq', onehot, rel_bias.astype(jnp.float32) * LOG2E, precision=HIGHEST)
    return jnp.where((dist >= 0)[:, None], b, NEG)


def _attn_kernel(qT_ref, qiT_ref, wT_ref, k_ref, vT_ref, ki_ref, kn_ref, tab_ref, bst_ref, o_ref,
                 keys_ref, hi_ref, lo_ref, msk_ref, p_ref, acc_ref, mp_ref, m_ref, *, topk):
    i = pl.program_id(1)
    n_tiles = i + 1
    row_hi = lax.broadcasted_iota(jnp.int32, (128, QB), 0) >= 64

    def head_rows(ref, h):
        pair = ref[0, (h // 2) * 128:(h // 2) * 128 + 128, :]
        return jnp.where(row_hi == bool(h % 2), pair, jnp.zeros_like(pair))

    def tile_rows(kt):
        return pl.ds(pl.multiple_of(kt * QB, QB), QB)

    def score_tile(kt, carry):
        ki = ki_ref[0, tile_rows(kt), :]
        sc = jnp.zeros((QB, QB), jnp.float32)
        for h in range(IDX_HEADS):
            d = jnp.dot(ki, head_rows(qiT_ref, h), preferred_element_type=jnp.float32)
            sc = sc + wT_ref[0, h:h + 1, :] * jnp.maximum(d, 0.0)
        srow = lax.broadcasted_iota(jnp.int32, (QB, QB), 0)
        qcol = lax.broadcasted_iota(jnp.int32, (QB, QB), 1)
        sc = jnp.where(jnp.abs(sc) < TINY, 0.0, sc)
        sc = jnp.where((kt == i) & (srow > qcol), -jnp.inf, sc)
        bits = pltpu.bitcast(sc, jnp.int32)
        keys_ref[tile_rows(kt), :] = bits ^ ((bits >> 31) & 0x7FFFFFFF)
        hi_ref[tile_rows(kt), :] = pltpu.bitcast(bits & jnp.int32(-65536), jnp.float32).astype(jnp.bfloat16)
        return carry
    lax.fori_loop(0, n_tiles, score_tile, 0)

    def count_packed_ge(ref, cb):
        one, zero = jnp.ones((), jnp.bfloat16), jnp.zeros((), jnp.bfloat16)

        def body(kt, acc):
            hit = jnp.where(ref[tile_rows(kt), :] >= cb, one, zero)
            parts = [hit[r:r + 16, :] for r in range(0, QB, 16)]
            while len(parts) > 1:
                parts = [a + b for a, b in zip(parts[::2], parts[1::2])]
            return acc + parts[0]
        acc = lax.fori_loop(0, n_tiles, body, jnp.zeros((16, QB), jnp.bfloat16))
        return jnp.sum(acc.astype(jnp.float32), axis=0, keepdims=True)

    def count_hi_ge(cand16):
        b = cand16 ^ ((cand16 >> 15) & 0x7FFF)
        snap = jnp.where(((b & 0x8000) != 0) | ((b & 0x7F) == 0), 0, 0x0080)
        b = jnp.where((b & 0x7F80) == 0, snap, b)
        return count_packed_ge(hi_ref, pltpu.bitcast(b << 16, jnp.float32).astype(jnp.bfloat16))

    def mid_code(v):
        pat = jnp.where(v >= 16384, v - 16256, 0x8000 | (16511 - v))
        return pltpu.bitcast(pat << 16, jnp.float32)

    def count(hit_of_tile):
        def body(kt, acc):
            return acc + jnp.sum(hit_of_tile(kt).reshape(QB // 8, 8, QB), axis=0)
        acc = lax.fori_loop(0, n_tiles, body, jnp.zeros((8, QB), jnp.int32))
        return jnp.sum(acc, axis=0, keepdims=True)

    def count_ge(cand):
        return count(lambda kt: jnp.where(keys_ref[tile_rows(kt), :] >= cand, 1, 0))

    def hi_step(it, r):
        cand = jnp.where(it == 0, jnp.zeros_like(r), r | (1 << (15 - it)))
        return jnp.where(count_hi_ge(cand) >= topk, cand, r)
    r16 = lax.fori_loop(0, 16, hi_step, jnp.full((1, QB), -32768, jnp.int32))

    above = count_hi_ge(r16 + 1)

    def code_tile(kt, carry):
        key = keys_ref[tile_rows(kt), :]
        code = jnp.where((key >> 16) == r16, mid_code((key >> 1) & 0x7FFF), -jnp.inf)
        lo_ref[tile_rows(kt), :] = code.astype(jnp.bfloat16)
        return carry
    lax.fori_loop(0, n_tiles, code_tile, 0)

    def mid_step(it, v):
        cand = v | (1 << (14 - it))
        cnt = above + count_packed_ge(lo_ref, mid_code(cand).astype(jnp.bfloat16))
        return jnp.where(cnt >= topk, cand, v)
    v15 = lax.fori_loop(0, 15, mid_step, jnp.zeros((1, QB), jnp.int32))
    thr = (r16 << 16) | (v15 << 1)
    thr = jnp.where(count_ge(thr | 1) >= topk, thr | 1, thr)

    cnt_gt = count_ge(thr + 1)
    cnt_ge = count_ge(thr)
    need = topk - cnt_gt
    tie = (cnt_ge - cnt_gt > need) & (thr > KEY_NEG_INF)

    @pl.when(jnp.max(tie.astype(jnp.int32)) > 0)
    def _():
        def count_eq_below(cand):
            def ind(kt):
                idx = lax.broadcasted_iota(jnp.int32, (QB, QB), 0) + kt * QB
                return jnp.where((keys_ref[tile_rows(kt), :] == thr) & (idx < cand), 1, 0)
            return count(ind)

        def idx_step(it, r):
            cand = r | (1 << (15 - it))
            return jnp.where(count_eq_below(cand) < need, cand, r)
        last = lax.fori_loop(0, 16, idx_step, jnp.zeros((1, QB), jnp.int32))

        def drop(kt, carry):
            blk = keys_ref[tile_rows(kt), :]
            idx = lax.broadcasted_iota(jnp.int32, (QB, QB), 0) + kt * QB
            keys_ref[tile_rows(kt), :] = jnp.where(tie & (blk == thr) & (idx > last), INT_MIN, blk)
            return carry
        lax.fori_loop(0, n_tiles, drop, 0)

    def logits(kt, h):
        band = jnp.clip(kt - (i - 2), 0, 2)
        kp = k_ref[0, tile_rows(kt), (h // 2) * 128:(h // 2) * 128 + 128]
        s = jnp.dot(kp, head_rows(qT_ref, h), preferred_element_type=jnp.float32)
        return s + msk_ref[...] + tab_ref[band, h]

    def set_mask(kt):
        msk_ref[...] = jnp.where(keys_ref[tile_rows(kt), :] >= thr, 0.0, NEG)

    def max_tile(kt, carry):
        set_mask(kt)
        for h in range(ATTN_HEADS):
            s = logits(kt, h)
            mp_ref[h] = jnp.maximum(mp_ref[h], jnp.max(s.reshape(QB // 8, 8, QB), axis=0))
        return carry

    seq = kn_ref.shape[2]
    in_extent = lax.broadcasted_iota(jnp.int32, (ATTN_HEADS, seq), 1) < n_tiles * QB
    k_max = jnp.max(jnp.where(in_extent, kn_ref[0], 0.0), axis=1, keepdims=True)
    spread = jnp.zeros((1, QB), jnp.float32)
    for h in range(ATTN_HEADS):
        qh = qT_ref[0, h * ATTN_HEAD_DIM:(h + 1) * ATTN_HEAD_DIM, :].astype(jnp.float32)
        reach = jnp.sqrt(jnp.sum(qh * qh, axis=0, keepdims=True) * k_max[h:h + 1, :]) * NORM_SLACK
        m_ref[h:h + 1, :] = reach + bst_ref[0, h:h + 1, :]
        spread = jnp.maximum(spread, 2.0 * reach + bst_ref[1, h:h + 1, :])
    bound_ok = jnp.max(spread) <= MAX_SHIFT_ERROR

    @pl.when(jnp.logical_not(bound_ok))
    def _():
        mp_ref[...] = jnp.full(mp_ref.shape, NEG, jnp.float32)
        lax.fori_loop(0, n_tiles, max_tile, 0)
        for h in range(ATTN_HEADS):
            m_ref[h:h + 1, :] = jnp.max(mp_ref[h], axis=0, keepdims=True)
    m = [m_ref[h:h + 1, :] for h in range(ATTN_HEADS)]

    def exp_tile(kt, carry):
        set_mask(kt)
        for h in range(ATTN_HEADS):
            p_ref[h] = jnp.exp2(logits(kt, h) - m[h]).astype(jnp.bfloat16)
        for h in range(ATTN_HEADS):
            va = vT_ref[0, h * VROWS:(h + 1) * VROWS, tile_rows(kt)]
            acc_ref[h * VROWS:(h + 1) * VROWS, :] += jnp.dot(va, p_ref[h], preferred_element_type=jnp.float32)
        return carry

    acc_ref[...] = jnp.zeros(acc_ref.shape, jnp.float32)
    lax.fori_loop(0, n_tiles, exp_tile, 0)

    outs = [acc_ref[h * VROWS:h * VROWS + ATTN_HEAD_DIM, :] / acc_ref[h * VROWS + ATTN_HEAD_DIM:h * VROWS + ATTN_HEAD_DIM + 1, :]
            for h in range(ATTN_HEADS)]
    o_ref[0] = jnp.concatenate(outs, axis=0).T


def _dsa_attention(qT, qiT, wT, k, vT, ki2, kn2, rel_bias):
    bsz, _, seq = qT.shape
    topk = min(TOPK_MAX, seq // 4)
    assert seq % QB == 0 and topk <= QB
    assert seq // 16 <= 256
    b2 = rel_bias.astype(jnp.float32) * LOG2E
    bias_stats = jnp.stack([jnp.max(b2, axis=0), jnp.max(b2, axis=0) - jnp.min(b2, axis=0)])
    bias_stats = jnp.broadcast_to(bias_stats[:, :, None], (2, ATTN_HEADS, QB))
    return pl.pallas_call(
        functools.partial(_attn_kernel, topk=topk),
        grid=(bsz, seq // QB),
        in_specs=[
            pl.BlockSpec((1, ATTN_WIDTH, QB), lambda b, i: (b, 0, i)),
            pl.BlockSpec((1, IDX_HEADS * IDX_DIM, QB), lambda b, i: (b, 0, i)),
            pl.BlockSpec((1, IDX_HEADS, QB), lambda b, i: (b, 0, i)),
            pl.BlockSpec((1, seq, ATTN_WIDTH), lambda b, i: (b, 0, 0)),
            pl.BlockSpec((1, ATTN_HEADS * VROWS, seq), lambda b, i: (b, 0, 0)),
            pl.BlockSpec((1, seq, 128), lambda b, i: (b, 0, 0)),
            pl.BlockSpec((1, ATTN_HEADS, seq), lambda b, i: (b, 0, 0)),
            pl.BlockSpec((3, ATTN_HEADS, QB, QB), lambda b, i: (0, 0, 0, 0)),
            pl.BlockSpec((2, ATTN_HEADS, QB), lambda b, i: (0, 0, 0)),
        ],
        out_specs=pl.BlockSpec((1, QB, ATTN_WIDTH), lambda b, i: (b, i, 0)),
        out_shape=jax.ShapeDtypeStruct((bsz, seq, ATTN_WIDTH), jnp.float32),
        scratch_shapes=[
            pltpu.VMEM((seq, QB), jnp.int32),
            pltpu.VMEM((seq, QB), jnp.bfloat16),
            pltpu.VMEM((seq, QB), jnp.bfloat16),
            pltpu.VMEM((QB, QB), jnp.float32),
            pltpu.VMEM((ATTN_HEADS, QB, QB), jnp.bfloat16),
            pltpu.VMEM((ATTN_HEADS * VROWS, QB), jnp.float32),
            pltpu.VMEM((ATTN_HEADS, 8, QB), jnp.float32),
            pltpu.VMEM((ATTN_HEADS, QB), jnp.float32),
        ],
        compiler_params=pltpu.CompilerParams(dimension_semantics=("parallel", "arbitrary"),
                                             vmem_limit_bytes=VMEM_LIMIT),
        name="dsa_attention",
    )(qT, qiT, wT, k, vT, ki2, kn2, _bias_tables(rel_bias), bias_stats)


def _ssd_kernel(xbc_ref, z_ref, sm_ref, cw_ref, cb_ref, dtb_ref, a_ref, dsk_ref, nw_ref, y_ref, prev_ref, st_ref):
    q = SSD_CHUNK
    bf = jnp.bfloat16

    @pl.when(pl.program_id(1) == 0)
    def _():
        prev_ref[...] = jnp.zeros(prev_ref.shape, jnp.float32)
        st_ref[...] = jnp.zeros(st_ref.shape, jnp.float32)

    cur = xbc_ref[...]
    prev = prev_ref[...]
    row = lax.broadcasted_iota(jnp.int32, cur.shape, 0)
    acc = cur * cw_ref[CONV_WIDTH - 1:CONV_WIDTH, :] + cb_ref[...]
    for s in range(1, CONV_WIDTH):
        shifted = jnp.where(row >= s, pltpu.roll(cur, s, 0), pltpu.roll(prev, s, 0))
        acc = acc + shifted * cw_ref[CONV_WIDTH - 1 - s:CONV_WIDTH - s, :]
    prev_ref[...] = cur
    u = acc * jax.nn.sigmoid(acc)
    xs = u[:, :SSM_INNER]
    bm = u[:, SSM_INNER:SSM_INNER + SSM_GROUPS * SSM_STATE].astype(bf)
    cm = u[:, SSM_INNER + SSM_GROUPS * SSM_STATE:].astype(bf)

    t = sm_ref[...] + dtb_ref[...]
    dt = jnp.maximum(t, 0.0) + jnp.log1p(jnp.exp(-jnp.abs(t)))
    ii = lax.broadcasted_iota(jnp.int32, (q, q), 0)
    jj = lax.broadcasted_iota(jnp.int32, (q, q), 1)
    causal = ii >= jj
    acum = jnp.dot(causal.astype(jnp.float32), dt * a_ref[...], preferred_element_type=jnp.float32, precision=HIGHEST)
    acum_t = acum.T
    dt_t = dt.T
    ea = jnp.exp(acum)
    last = acum[q - 1:q, :]
    decay = jnp.exp(last - acum) * dt
    ea_last = jnp.exp(last)

    lane_hi = lax.broadcasted_iota(jnp.int32, (q, 128), 1) >= SSM_HEAD_DIM
    row_hi = lax.broadcasted_iota(jnp.int32, (128, SSM_STATE), 0) >= SSM_HEAD_DIM

    def pair_cols(v, e):
        c0, c1 = SMALL_DT + e, SMALL_DT + e + 1
        return jnp.where(lane_hi, v[:, c1:c1 + 1], v[:, c0:c0 + 1])

    for g in range(SSM_GROUPS):
        bg = bm[:, g * SSM_STATE:(g + 1) * SSM_STATE]
        cg = cm[:, g * SSM_STATE:(g + 1) * SSM_STATE]
        cb = lax.dot_general(cg, bg, NT, preferred_element_type=jnp.float32)
        for k in range(g * 4, g * 4 + 4):
            e = 2 * k
            x_pair = xs[:, k * 128:(k + 1) * 128]
            halves = []
            for h in (e, e + 1):
                c = SMALL_DT + h
                seg = acum[:, c:c + 1] - acum_t[c:c + 1, :]
                w = cb * jnp.exp(jnp.where(causal, seg, -jnp.inf)) * dt_t[c:c + 1, :]
                halves.append(jnp.dot(w.astype(bf), x_pair.astype(bf), preferred_element_type=jnp.float32))
            y_pair = jnp.where(lane_hi, halves[1], halves[0])
            state = st_ref[k]
            y_pair = y_pair + lax.dot_general(cg, state.astype(bf), NT, preferred_element_type=jnp.float32) * pair_cols(ea, e)
            y_ref[:, k * 128:(k + 1) * 128] = y_pair
            xd_t = (x_pair * pair_cols(decay, e)).T.astype(bf)
            c0 = SMALL_DT + e
            keep = jnp.where(row_hi, ea_last[:, c0 + 1:c0 + 2], ea_last[:, c0:c0 + 1])
            st_ref[k] = state * keep + jnp.dot(xd_t, bg, preferred_element_type=jnp.float32)

    y = (y_ref[...] + dsk_ref[...] * xs) * (z_ref[...] * jax.nn.sigmoid(z_ref[...]))
    half = SSM_INNER // SSM_GROUPS
    for g in range(SSM_GROUPS):
        yg = y[:, g * half:(g + 1) * half]
        yg = yg * lax.rsqrt(jnp.mean(yg * yg, axis=-1, keepdims=True) + EPS)
        y_ref[:, g * half:(g + 1) * half] = yg * nw_ref[:, g * half:(g + 1) * half]


def _mamba2_ssd(proj, bsz, seq, conv_w, conv_b, dt_bias, a_log, d_skip, norm_w):
    q = SSD_CHUNK
    nc = seq // q
    lane_row = lambda v: jnp.zeros((1, 128), jnp.float32).at[0, SMALL_DT:SMALL_DT + SSM_HEADS].set(v)
    const = lambda shape: pl.BlockSpec(shape, lambda b, c: (0,) * len(shape))
    return pl.pallas_call(
        _ssd_kernel,
        grid=(bsz, nc),
        in_specs=[pl.BlockSpec((q, CONV_CH), lambda b, c: (b * nc + c, COL_XBC // CONV_CH)),
                  pl.BlockSpec((q, SSM_INNER), lambda b, c: (b * nc + c, COL_Z // SSM_INNER)),
                  pl.BlockSpec((q, 128), lambda b, c: (b * nc + c, COL_SMALL // 128)),
                  const((CONV_WIDTH, CONV_CH)), const((1, CONV_CH)), const((1, 128)), const((1, 128)),
                  const((1, SSM_INNER)), const((1, SSM_INNER))],
        out_specs=pl.BlockSpec((q, SSM_INNER), lambda b, c: (b * nc + c, 0)),
        out_shape=jax.ShapeDtypeStruct((bsz * seq, SSM_INNER), jnp.float32),
        scratch_shapes=[pltpu.VMEM((q, CONV_CH), jnp.float32),
                        pltpu.VMEM((SSM_HEADS // 2, 2 * SSM_HEAD_DIM, SSM_STATE), jnp.float32)],
        compiler_params=pltpu.CompilerParams(dimension_semantics=("parallel", "arbitrary"),
                                             vmem_limit_bytes=VMEM_LIMIT),
        name="mamba2_ssd",
    )(proj, proj, proj, conv_w, conv_b.reshape(1, CONV_CH), lane_row(dt_bias), lane_row(-jnp.exp(a_log)),
      jnp.repeat(d_skip, SSM_HEAD_DIM).reshape(1, SSM_INNER), norm_w.reshape(1, SSM_INNER))


def _mix_out_kernel(a_ref, s_ref, gl_ref, x_ref, gm_ref, wo_ref, ws_ref, wout_ref,
                    nf_ref, scf_ref, shf_ref, wr_ref, br_ref, xo_ref, h_ref, rt_ref, gt_ref, cnt_ref):
    bf = jnp.bfloat16
    ya = jnp.dot(a_ref[...].astype(bf), wo_ref[...], preferred_element_type=jnp.float32)
    ys = jnp.dot(s_ref[...].astype(bf), ws_ref[...], preferred_element_type=jnp.float32)
    mixed = jax.nn.sigmoid(gl_ref[:, :D_MODEL]) * ya + jax.nn.sigmoid(gl_ref[:, D_MODEL:]) * ys
    x = x_ref[...] + gm_ref[0] * jnp.dot(mixed.astype(bf), wout_ref[...], preferred_element_type=jnp.float32)
    xo_ref[...] = x
    y = x * lax.rsqrt(jnp.mean(x * x, axis=-1, keepdims=True) + EPS) * nf_ref[...]
    h = y * (1.0 + scf_ref[0]) + shf_ref[0]
    h_hi = h.astype(bf)
    hb = pltpu.bitcast(h_hi.astype(jnp.float32), jnp.int32)
    half = D_MODEL // 2
    h_ref[...] = (hb[:, :half] & jnp.int32(-65536)) | lax.shift_right_logical(hb[:, half:], 16)
    h_lo = (h - h_hi.astype(jnp.float32)).astype(bf)
    dot = functools.partial(jnp.dot, preferred_element_type=jnp.float32)
    lg = dot(h_hi, wr_ref[0]) + (dot(h_lo, wr_ref[0]) + dot(h_hi, wr_ref[1])) + br_ref[...]

    tm = lg.shape[0]
    work = lg.T[:N_EXPERTS, :]
    row = lax.broadcasted_iota(jnp.int32, work.shape, 0).astype(jnp.float32)
    vals, eids, hits = [], [], []
    for _ in range(TOP_K):
        mx = jnp.max(work, axis=0, keepdims=True)
        ix = jnp.min(jnp.where(work == mx, row, float(N_EXPERTS)), axis=0, keepdims=True)
        vals.append(mx)
        eids.append(ix)
        hits.append(row == ix)
        work = jnp.where(hits[-1], -jnp.inf, work)
    ex = [jnp.exp(v - vals[0]) for v in vals]
    den = (ex[0] + ex[1]) + (ex[2] + ex[3])

    @pl.when(pl.program_id(0) == 0)
    def _():
        cnt_ref[...] = jnp.zeros(cnt_ref.shape, jnp.float32)
    chosen = jnp.zeros(work.shape, jnp.float32)
    for hit in hits:
        chosen = jnp.where(hit, 1.0, chosen)
    earlier = lax.broadcasted_iota(jnp.int32, (tm, tm), 0) < lax.broadcasted_iota(jnp.int32, (tm, tm), 1)
    before = dot(chosen.astype(bf), earlier.astype(bf)) + cnt_ref[:, 0:1]
    cnt_ref[...] = cnt_ref[...] + jnp.sum(chosen, axis=1, keepdims=True)

    slot = lax.broadcasted_iota(jnp.int32, (128, tm), 0)
    route = jnp.zeros((128, tm), jnp.float32)
    for k, hit in enumerate(hits):
        rank = jnp.sum(jnp.where(hit, before, 0.0), axis=0, keepdims=True)
        route = jnp.where(slot == k, eids[k], route)
        route = jnp.where(slot == TOP_K + k, ex[k] / den, route)
        route = jnp.where(slot == 2 * TOP_K + k, rank, route)
    rt_ref[...] = route[:ROUTE_ROWS, :]
    gt_ref[...] = route.T


def _mix_out(attn2, ssd2, proj, x2, g_m, w_attn_o, w_ssm_o, w_out, norm_ffn, sc_f, sh_f, w_router, b_router, seq, tm=512):
    t, d = x2.shape
    per_b = seq // tm
    bf = jnp.bfloat16
    const = lambda shape: pl.BlockSpec(shape, lambda i: (0,) * len(shape))
    perb = pl.BlockSpec((1, 1, d), lambda i: (i // per_b, 0, 0))
    wr = jnp.pad(w_router.astype(jnp.float32), ((0, 0), (0, 128 - N_EXPERTS)))
    wr_hi = wr.astype(bf)
    wr = jnp.stack([wr_hi, (wr - wr_hi.astype(jnp.float32)).astype(bf)])
    br =jnp.pad(b_router, (0, 128 - N_EXPERTS)).reshape(1, 128)
    return pl.pallas_call(
        _mix_out_kernel,
        grid=(t // tm,),
        in_specs=[pl.BlockSpec((tm, ATTN_WIDTH), lambda i: (i, 0)),
                  pl.BlockSpec((tm, SSM_INNER), lambda i: (i, 0)),
                  pl.BlockSpec((tm, 2 * d), lambda i: (i, COL_GATE // (2 * d))),
                  pl.BlockSpec((tm, d), lambda i: (i, 0)),
                  perb,
                  const((ATTN_WIDTH, d)), const((SSM_INNER, d)), const((d, d)),
                  const((1, d)), perb, perb, const((2, d, 128)), const((1, 128))],
        out_specs=[pl.BlockSpec((tm, d), lambda i: (i, 0)),
                   pl.BlockSpec((tm, d // 2), lambda i: (i, 0)),
                   pl.BlockSpec((ROUTE_ROWS, tm), lambda i: (0, i)),
                   pl.BlockSpec((tm, 128), lambda i: (i, 0)),
                   pl.BlockSpec((N_EXPERTS, 128), lambda i: (0, 0))],
        out_shape=[jax.ShapeDtypeStruct((t, d), jnp.float32),
                   jax.ShapeDtypeStruct((t, d // 2), jnp.int32),
                   jax.ShapeDtypeStruct((ROUTE_ROWS, t), jnp.float32),
                   jax.ShapeDtypeStruct((t, 128), jnp.float32),
                   jax.ShapeDtypeStruct((N_EXPERTS, 128), jnp.float32)],
        compiler_params=pltpu.CompilerParams(dimension_semantics=("arbitrary",), vmem_limit_bytes=VMEM_LIMIT),
        name="mix_out",
    )(attn2, ssd2, proj, x2, g_m[:, None, :], w_attn_o.astype(bf), w_ssm_o.astype(bf), w_out.astype(bf),
      norm_ffn.reshape(1, d), sc_f[:, None, :], sh_f[:, None, :], wr, br)


def _moe_kernel(be_ref, nb_ref, x_ref, wgu_ref, bgu_ref, wdn_ref, bdn_ref, *rest, first_block):
    o_ref, wgu_bf, wdn_bf = rest[-3:]
    i = pl.program_id(0)
    j = i + first_block

    @pl.when((i == 0) | (be_ref[j] != be_ref[jnp.maximum(j - 1, 0)]))
    def _():
        wgu_bf[...] = wgu_ref[0, 0].astype(jnp.bfloat16)
        wdn_bf[...] = wdn_ref[0, 0].astype(jnp.bfloat16)

    @pl.when(j < nb_ref[0])
    def _():
        words = x_ref[...]
        x_hi = pltpu.bitcast(words & jnp.int32(-65536), jnp.float32).astype(jnp.bfloat16)
        x_lo = pltpu.bitcast(words << 16, jnp.float32).astype(jnp.bfloat16)
        x = jnp.concatenate([x_hi, x_lo], axis=1)
        gu = jnp.dot(x, wgu_bf[...], preferred_element_type=jnp.float32) + bgu_ref[0, 0]
        g = jnp.minimum(gu[:, :D_EXPERT], SWIGLU_LIMIT)
        u = jnp.clip(gu[:, D_EXPERT:], -SWIGLU_LIMIT, SWIGLU_LIMIT)
        act = (u + 1.0) * (g * jax.nn.sigmoid(SWIGLU_ALPHA * g))
        out = jnp.dot(act.astype(jnp.bfloat16), wdn_bf[...], preferred_element_type=jnp.float32) + bdn_ref[0, 0]
        o_ref[...] = out.astype(o_ref.dtype)

    @pl.when(j >= nb_ref[0])
    def _():
        o_ref[...] = jnp.zeros_like(o_ref)


def _moe_ffn(xs_parts, blk_exp, n_used, w_gu, b_gu, w_dn, b_dn, layer):
    d = D_MODEL
    tm = MOE_TM
    n_rows = sum(xs.shape[0] for xs in xs_parts)
    out, first = None, 0
    for xs in xs_parts:
        nblk = xs.shape[0] // tm
        wmap = lambda i, be, nb, first=first: (layer, be[i + first], 0, 0)
        in_specs = [pl.BlockSpec((tm, d // 2), lambda i, be, nb: (i, 0)),
                    pl.BlockSpec((1, 1, d, 2 * D_EXPERT), wmap),
                    pl.BlockSpec((1, 1, 1, 2 * D_EXPERT), wmap),
                    pl.BlockSpec((1, 1, D_EXPERT, d), wmap),
                    pl.BlockSpec((1, 1, 1, d), wmap)]
        args = [blk_exp, n_used, xs, w_gu, b_gu[:, :, None, :], w_dn, b_dn[:, :, None, :]]
        aliases = {}
        if out is not None:
            in_specs.append(pl.BlockSpec(memory_space=pl.ANY))
            args.append(out)
            aliases = {len(args) - 1: 0}
        grid_spec = pltpu.PrefetchScalarGridSpec(
            num_scalar_prefetch=2,
            grid=(nblk,),
            in_specs=in_specs,
            out_specs=pl.BlockSpec((tm, d), lambda i, be, nb, first=first: (i + first, 0)),
            scratch_shapes=[pltpu.VMEM((d, 2 * D_EXPERT), jnp.bfloat16), pltpu.VMEM((D_EXPERT, d), jnp.bfloat16)],
        )
        out = pl.pallas_call(
            functools.partial(_moe_kernel, first_block=first),
            grid_spec=grid_spec,
            out_shape=jax.ShapeDtypeStruct((n_rows, d), jnp.bfloat16),
            input_output_aliases=aliases,
            compiler_params=pltpu.CompilerParams(dimension_semantics=("arbitrary",), vmem_limit_bytes=VMEM_LIMIT),
            name="moe_ffn",
        )(*args)
        first += nblk
    return out


def _moe(h2, route, expert_counts, w_gu, b_gu, w_dn, b_dn, layer):
    t = h2.shape[0]
    d = D_MODEL
    tm = MOE_TM
    i32 = jnp.int32
    experts = jnp.arange(N_EXPERTS, dtype=i32)
    top_idx = route[:TOP_K].astype(i32)
    rank = route[2 * TOP_K:3 * TOP_K].astype(i32)
    n_assign = t * TOP_K
    n_rows = n_assign + N_EXPERTS * tm
    e_flat = top_idx.T.reshape(n_assign)
    counts = expert_counts[:, 0].astype(i32)
    padded = (counts + tm - 1) // tm * tm
    pad_start = jnp.cumsum(padded) - padded
    dest = rank + jnp.sum(jnp.where(top_idx[..., None] == experts, pad_start, 0), axis=-1)
    dest = dest.reshape(-1)
    filler_exp = jnp.repeat(experts, tm)
    filler_key = jnp.where(jnp.tile(jnp.arange(tm, dtype=i32), N_EXPERTS) < jnp.repeat(padded - counts, tm),
                           filler_exp, N_EXPERTS)
    keys = jnp.concatenate([e_flat, filler_key])
    rows = jnp.arange(n_rows, dtype=i32)
    row_key, row_src = lax.sort((keys, rows), num_keys=1)
    row_tok = jnp.where(row_src < n_assign, row_src // TOP_K, rows % t)
    blk_exp = jnp.minimum(row_key[::tm], N_EXPERTS - 1)
    n_used = (jnp.sum(padded, keepdims=True) // tm).astype(i32)
    slab = n_rows // MOE_SLABS
    xs_parts = [h2[row_tok[s * slab:(s + 1) * slab]] for s in range(MOE_SLABS)]
    out = _moe_ffn(xs_parts, blk_exp, n_used, w_gu, b_gu, w_dn, b_dn, layer)
    return out[dest].reshape(TOP_K, t, d)


def _combine_kernel(p_ref, r_ref, x_ref, g_ref, o_ref):
    f32 = jnp.float32
    w = [r_ref[:, TOP_K + k:TOP_K + k + 1] for k in range(TOP_K)]
    y = (w[0] * p_ref[0].astype(f32) + w[1] * p_ref[1].astype(f32)) + (w[2] * p_ref[2].astype(f32) + w[3] * p_ref[3].astype(f32))
    o_ref[...] = x_ref[...] + g_ref[0] * y


def _combine(parts, route_tok, x2, g_f, seq, tm=512):
    t, d = x2.shape
    per_b = seq // tm
    return pl.pallas_call(
        _combine_kernel,
        grid=(t // tm,),
        in_specs=[pl.BlockSpec((TOP_K, tm, d), lambda i: (0, i, 0)),
                  pl.BlockSpec((tm, 128), lambda i: (i, 0)),
                  pl.BlockSpec((tm, d), lambda i: (i, 0)),
                  pl.BlockSpec((1, 1, d), lambda i: (i // per_b, 0, 0))],
        out_specs=pl.BlockSpec((tm, d), lambda i: (i, 0)),
        out_shape=jax.ShapeDtypeStruct((t, d), jnp.float32),
        compiler_params=pltpu.CompilerParams(dimension_semantics=("parallel",), vmem_limit_bytes=VMEM_LIMIT),
        name="moe_combine",
    )(parts, route_tok, x2, g_f[:, None, :])


def kernel(x, c, rel_bias, w_ada, b_ada, norm_mix, norm_ffn, w_in, kv_norm, w_kv_up, q_norm, k_norm,
           idx_k_ln_w, idx_k_ln_b, w_attn_o, conv_w, conv_b, dt_bias, a_log, d_skip, ssm_norm, w_ssm_o,
           w_out, w_router, b_router, w_gu, b_gu, w_dn, b_dn):
    bsz, seq, d = x.shape
    t = bsz * seq
    cond = jax.nn.silu(c)
    x2 = x.reshape(t, d)
    for l in range(DEPTH):
        mod = cond @ w_ada[l] + b_ada[l]
        sh_m, sc_m, g_m, sh_f, sc_f, g_f = jnp.split(mod, 6, axis=-1)
        proj = _in_proj(x2, norm_mix[l], sc_m, sh_m, _pack_w_in(w_in[l]), seq)
        qT, k, vT, qiT, ki2, wT, kn2 = _prep(proj, bsz, seq, q_norm[l], kv_norm[l], w_kv_up[l], k_norm[l],
                                             idx_k_ln_w[l], idx_k_ln_b[l])
        attn = _dsa_attention(qT, qiT, wT, k, vT, ki2, kn2, rel_bias)
        y_ssd = _mamba2_ssd(proj, bsz, seq, conv_w[l], conv_b[l], dt_bias[l], a_log[l], d_skip[l], ssm_norm[l])
        x2, h2, route, route_tok, expert_counts = _mix_out(
            attn.reshape(t, ATTN_WIDTH), y_ssd, proj, x2, g_m, w_attn_o[l], w_ssm_o[l], w_out[l], norm_ffn[l],
            sc_f, sh_f, w_router[l], b_router[l], seq)
        parts = _moe(h2, route, expert_counts, w_gu, b_gu, w_dn, b_dn, l)
        x2 = _combine(parts, route_tok, x2, g_f, seq)
    return x2.reshape(bsz, seq, d)
```

```python
import functools
import math

import jax
import jax.numpy as jnp
import numpy as np
from jax import lax
from jax.experimental import pallas as pl
from jax.experimental.pallas import tpu as pltpu

D_MODEL = 1024
DEPTH = 2
ATTN_HEADS = 8
ATTN_HEAD_DIM = 64
ATTN_WIDTH = ATTN_HEADS * ATTN_HEAD_DIM
KV_RANK = 256
IDX_HEADS = 8
IDX_DIM = 64
TOPK_MAX = 256
N_BUCKETS = 32
MAX_DISTANCE = 128
SSM_HEADS = 16
SSM_HEAD_DIM = 64
SSM_INNER = SSM_HEADS * SSM_HEAD_DIM
SSM_GROUPS = 2
SSM_STATE = 128
CONV_WIDTH = 4
CONV_CH = SSM_INNER + 2 * SSM_GROUPS * SSM_STATE
SSD_CHUNK = 128
SSD_STEP = 256
N_EXPERTS = 32
TOP_K = 4
D_EXPERT = D_MODEL
SWIGLU_LIMIT = 7.0
SWIGLU_ALPHA = 1.702
EPS = 1e-6

COL_Q = 0
COL_KV = 512
COL_QI = 768
COL_SMALL = 1280
COL_XBC = 1536
COL_Z = 3072
COL_GATE = 4096
PROJ_COLS = 6144
PREP_COLS = 1408
SMALL_KI, SMALL_WI, SMALL_DT = 0, 64, 72

QB = 256
VROWS = 80
INT_MIN = -2 ** 31
KEY_NEG_INF = (0xFF800000 ^ 0x7FFFFFFF) - 2 ** 32
NEG = -1e30
TINY = 2.0 ** -126
LOG2E = math.log2(math.e)
NORM_SLACK = 1.02
MAX_SHIFT_ERROR = 96.0
VMEM_LIMIT = 56 * 1024 * 1024
MOE_TM = 512
MOE_SLABS = 4
ROUTE_ROWS = 16
HIGHEST = lax.Precision.HIGHEST
NT = (((1,), (1,)), ((), ()))


def _pack_w_in(w):
    o = np.cumsum((0, ATTN_WIDTH, KV_RANK, IDX_HEADS * IDX_DIM, IDX_DIM, IDX_HEADS, SSM_INNER, CONV_CH, SSM_HEADS, 2 * D_MODEL))
    q, kv, qi, ki, wi, z, xbc, dt, gate = (w[:, int(o[n]):int(o[n + 1])] for n in range(9))
    zeros = lambda n: jnp.zeros((w.shape[0], n), w.dtype)
    small = jnp.concatenate([ki, wi, dt, zeros(128 - 88)], axis=1)
    packed = jnp.concatenate([q, kv, qi, small, zeros(COL_XBC - PREP_COLS), xbc, z, gate], axis=1)
    assert packed.shape[1] == PROJ_COLS
    return packed.astype(jnp.bfloat16)


def _in_proj_kernel(x_ref, g_ref, sc_ref, sh_ref, w_ref, o_ref, h_ref):
    @pl.when(pl.program_id(1) == 0)
    def _():
        x = x_ref[...]
        y = x * lax.rsqrt(jnp.mean(x * x, axis=-1, keepdims=True) + EPS) * g_ref[...]
        h_ref[...] = (y * (1.0 + sc_ref[0]) + sh_ref[0]).astype(jnp.bfloat16)
    o_ref[...] = jnp.dot(h_ref[...], w_ref[...], preferred_element_type=jnp.float32)


def _in_proj(x2, gain, sc, sh, w_packed, seq, tm=1024, tn=2048):
    t, d = x2.shape
    per_b = seq // tm
    return pl.pallas_call(
        _in_proj_kernel,
        grid=(t // tm, PROJ_COLS // tn),
        in_specs=[pl.BlockSpec((tm, d), lambda i, j: (i, 0)),
                  pl.BlockSpec((1, d), lambda i, j: (0, 0)),
                  pl.BlockSpec((1, 1, d), lambda i, j: (i // per_b, 0, 0)),
                  pl.BlockSpec((1, 1, d), lambda i, j: (i // per_b, 0, 0)),
                  pl.BlockSpec((d, tn), lambda i, j: (0, j))],
        out_specs=pl.BlockSpec((tm, tn), lambda i, j: (i, j)),
        out_shape=jax.ShapeDtypeStruct((t, PROJ_COLS), jnp.float32),
        scratch_shapes=[pltpu.VMEM((tm, d), jnp.bfloat16)],
        compiler_params=pltpu.CompilerParams(dimension_semantics=("parallel", "arbitrary"),
                                             vmem_limit_bytes=VMEM_LIMIT),
        name="in_proj",
    )(x2, gain.reshape(1, d), sc[:, None, :], sh[:, None, :], w_packed)


def _head_rms_t(xt):
    x3 = xt.reshape(ATTN_HEADS, ATTN_HEAD_DIM, xt.shape[1])
    return lax.rsqrt(jnp.mean(x3 * x3, axis=1, keepdims=True) + EPS)


def _prep_kernel(p_ref, qg_ref, kvg_ref, wkv_ref, kg_ref, lng_ref, lnb_ref,
                 qT_ref, k_ref, vT_ref, qiT_ref, ki_ref, wT_ref, kn2_ref):
    n = p_ref.shape[0]
    q = p_ref[:, COL_Q:COL_Q + ATTN_WIDTH]
    lat = p_ref[:, COL_KV:COL_KV + KV_RANK]
    qi = p_ref[:, COL_QI:COL_QI + IDX_HEADS * IDX_DIM]
    sm = p_ref[:, COL_SMALL:COL_SMALL + 128]

    scale = ATTN_HEAD_DIM ** -0.5 * LOG2E
    qt = q.T
    qn = qt.reshape(ATTN_HEADS, ATTN_HEAD_DIM, n) * _head_rms_t(qt)
    qT_ref[0] = (qn.reshape(ATTN_WIDTH, n) * qg_ref[...] * scale).astype(jnp.bfloat16)

    latn = lat * lax.rsqrt(jnp.mean(lat * lat, axis=-1, keepdims=True) + EPS) * kvg_ref[...]
    kv = jnp.dot(latn.astype(jnp.bfloat16), wkv_ref[...], preferred_element_type=jnp.float32)
    kt = kv[:, :ATTN_WIDTH].T
    kn = (kt.reshape(ATTN_HEADS, ATTN_HEAD_DIM, n) * _head_rms_t(kt)).reshape(ATTN_WIDTH, n) * kg_ref[...]
    k_ref[0] = kn.T.astype(jnp.bfloat16)
    kn3 = kn.reshape(ATTN_HEADS, ATTN_HEAD_DIM, n)
    kn2_ref[0] = jnp.sum(kn3 * kn3, axis=1)
    vt = kv[:, ATTN_WIDTH:].T.reshape(ATTN_HEADS, ATTN_HEAD_DIM, n)
    ones = jnp.ones((ATTN_HEADS, VROWS - ATTN_HEAD_DIM, n), jnp.float32)
    vT_ref[0] = jnp.concatenate([vt, ones], axis=1).reshape(ATTN_HEADS * VROWS, n).astype(jnp.bfloat16)

    qiT_ref[0] = (qi * (IDX_DIM ** -0.5)).T.astype(jnp.bfloat16)

    lane = lax.broadcasted_iota(jnp.int32, sm.shape, 1)
    kid = jnp.where(lane < IDX_DIM, sm, pltpu.roll(sm, IDX_DIM, 1))
    mu = jnp.mean(kid, axis=-1, keepdims=True)
    var = jnp.mean(jnp.square(kid - mu), axis=-1, keepdims=True)
    ki_ref[0] = ((kid - mu) * lax.rsqrt(var + EPS) * lng_ref[...] + lnb_ref[...]).astype(jnp.bfloat16)

    wT_ref[0] = sm.T[SMALL_WI:SMALL_WI + IDX_HEADS, :] * (IDX_HEADS ** -0.5)


def _prep(proj, bsz, seq, q_norm, kv_norm, w_kv_up, k_norm, ln_w, ln_b, tp=512):
    nb = seq // tp
    tile8 = lambda g: jnp.tile(g, ATTN_HEADS).reshape(ATTN_WIDTH, 1)
    const = lambda shape: pl.BlockSpec(shape, lambda b, i: (0,) * len(shape))
    bf = jnp.bfloat16
    return pl.pallas_call(
        _prep_kernel,
        grid=(bsz, nb),
        in_specs=[pl.BlockSpec((tp, PREP_COLS), lambda b, i: (b * nb + i, 0)),
                  const((ATTN_WIDTH, 1)), const((1, KV_RANK)), const((KV_RANK, 2 * ATTN_WIDTH)),
                  const((ATTN_WIDTH, 1)), const((1, 128)), const((1, 128))],
        out_specs=[pl.BlockSpec((1, ATTN_WIDTH, tp), lambda b, i: (b, 0, i)),
                   pl.BlockSpec((1, tp, ATTN_WIDTH), lambda b, i: (b, i, 0)),
                   pl.BlockSpec((1, ATTN_HEADS * VROWS, tp), lambda b, i: (b, 0, i)),
                   pl.BlockSpec((1, ATTN_WIDTH, tp), lambda b, i: (b, 0, i)),
                   pl.BlockSpec((1, tp, 128), lambda b, i: (b, i, 0)),
                   pl.BlockSpec((1, IDX_HEADS, tp), lambda b, i: (b, 0, i)),
                   pl.BlockSpec((1, ATTN_HEADS, tp), lambda b, i: (b, 0, i))],
        out_shape=[jax.ShapeDtypeStruct((bsz, ATTN_WIDTH, seq), bf),
                   jax.ShapeDtypeStruct((bsz, seq, ATTN_WIDTH), bf),
                   jax.ShapeDtypeStruct((bsz, ATTN_HEADS * VROWS, seq), bf),
                   jax.ShapeDtypeStruct((bsz, ATTN_WIDTH, seq), bf),
                   jax.ShapeDtypeStruct((bsz, seq, 128), bf),
                   jax.ShapeDtypeStruct((bsz, IDX_HEADS, seq), jnp.float32),
                   jax.ShapeDtypeStruct((bsz, ATTN_HEADS, seq), jnp.float32)],
        compiler_params=pltpu.CompilerParams(dimension_semantics=("parallel", "parallel"),
                                             vmem_limit_bytes=VMEM_LIMIT),
        name="attn_prep",
    )(proj, tile8(q_norm), kv_norm.reshape(1, KV_RANK), w_kv_up.astype(bf), tile8(k_norm),
      jnp.tile(ln_w, 2).reshape(1, 128), jnp.tile(ln_b, 2).reshape(1, 128))


def _t5_bucket(dist):
    n = jnp.maximum(dist, 0)
    max_exact = N_BUCKETS // 2
    nf = jnp.maximum(n, 1).astype(jnp.float32)
    large = max_exact + (jnp.log(nf / max_exact) / math.log(MAX_DISTANCE / max_exact) * (N_BUCKETS - max_exact)).astype(jnp.int32)
    large = jnp.minimum(large, N_BUCKETS - 1)
    return jnp.where(n < max_exact, n, large)


def _bias_tables(rel_bias):
    s = jnp.arange(QB, dtype=jnp.int32)[None, :, None]
    q = jnp.arange(QB, dtype=jnp.int32)[None, None, :]
    dist = q - s + jnp.array([2 * QB, QB, 0], jnp.int32)[:, None, None]
    onehot = (_t5_bucket(dist)[..., None] == jnp.arange(N_BUCKETS, dtype=jnp.int32)).astype(jnp.float32)
    b = jnp.einsum('tsqb,bh->thsq', onehot, rel_bias.astype(jnp.float32) * LOG2E, precision=HIGHEST)
    return jnp.where((dist >= 0)[:, None], b, NEG)


def _attn_kernel(qT_ref, qiT_ref, wT_ref, k_ref, vT_ref, ki_ref, kn_ref, tab_ref, bst_ref, o_ref,
                 keys_ref, hi_ref, lo_ref, msk_ref, p_ref, acc_ref, mp_ref, m_ref, *, topk):
    i = pl.program_id(1)
    n_tiles = i + 1
    row_hi = lax.broadcasted_iota(jnp.int32, (128, QB), 0) >= 64

    def head_rows(ref, h):
        pair = ref[0, (h // 2) * 128:(h // 2) * 128 + 128, :]
        return jnp.where(row_hi == bool(h % 2), pair, jnp.zeros_like(pair))

    def tile_rows(kt):
        return pl.ds(pl.multiple_of(kt * QB, QB), QB)

    def score_tile(kt, carry):
        ki = ki_ref[0, tile_rows(kt), :]
        sc = jnp.zeros((QB, QB), jnp.float32)
        for h in range(IDX_HEADS):
            d = jnp.dot(ki, head_rows(qiT_ref, h), preferred_element_type=jnp.float32)
            sc = sc + wT_ref[0, h:h + 1, :] * jnp.maximum(d, 0.0)
        srow = lax.broadcasted_iota(jnp.int32, (QB, QB), 0)
        qcol = lax.broadcasted_iota(jnp.int32, (QB, QB), 1)
        sc = jnp.where(jnp.abs(sc) < TINY, 0.0, sc)
        sc = jnp.where((kt == i) & (srow > qcol), -jnp.inf, sc)
        bits = pltpu.bitcast(sc, jnp.int32)
        keys_ref[tile_rows(kt), :] = bits ^ ((bits >> 31) & 0x7FFFFFFF)
        hi_ref[tile_rows(kt), :] = pltpu.bitcast(bits & jnp.int32(-65536), jnp.float32).astype(jnp.bfloat16)
        return carry
    lax.fori_loop(0, n_tiles, score_tile, 0)

    def count_packed_ge(ref, cb):
        one, zero = jnp.ones((), jnp.bfloat16), jnp.zeros((), jnp.bfloat16)

        def body(kt, acc):
            hit = jnp.where(ref[tile_rows(kt), :] >= cb, one, zero)
            parts = [hit[r:r + 16, :] for r in range(0, QB, 16)]
            while len(parts) > 1:
                parts = [a + b for a, b in zip(parts[::2], parts[1::2])]
            return acc + parts[0]
        acc = lax.fori_loop(0, n_tiles, body, jnp.zeros((16, QB), jnp.bfloat16))
        return jnp.sum(acc.astype(jnp.float32), axis=0, keepdims=True)

    def count_hi_ge(cand16):
        b = cand16 ^ ((cand16 >> 15) & 0x7FFF)
        snap = jnp.where(((b & 0x8000) != 0) | ((b & 0x7F) == 0), 0, 0x0080)
        b = jnp.where((b & 0x7F80) == 0, snap, b)
        return count_packed_ge(hi_ref, pltpu.bitcast(b << 16, jnp.float32).astype(jnp.bfloat16))

    def mid_code(v):
        pat = jnp.where(v >= 16384, v - 16256, 0x8000 | (16511 - v))
        return pltpu.bitcast(pat << 16, jnp.float32)

    def count(hit_of_tile):
        def body(kt, acc):
            return acc + jnp.sum(hit_of_tile(kt).reshape(QB // 8, 8, QB), axis=0)
        acc = lax.fori_loop(0, n_tiles, body, jnp.zeros((8, QB), jnp.int32))
        return jnp.sum(acc, axis=0, keepdims=True)

    def count_ge(cand):
        return count(lambda kt: jnp.where(keys_ref[tile_rows(kt), :] >= cand, 1, 0))

    def hi_step(it, r):
        cand = jnp.where(it == 0, jnp.zeros_like(r), r | (1 << (15 - it)))
        return jnp.where(count_hi_ge(cand) >= topk, cand, r)
    r16 = lax.fori_loop(0, 16, hi_step, jnp.full((1, QB), -32768, jnp.int32))

    above = count_hi_ge(r16 + 1)

    def code_tile(kt, carry):
        key = keys_ref[tile_rows(kt), :]
        code = jnp.where((key >> 16) == r16, mid_code((key >> 1) & 0x7FFF), -jnp.inf)
        lo_ref[tile_rows(kt), :] = code.astype(jnp.bfloat16)
        return carry
    lax.fori_loop(0, n_tiles, code_tile, 0)

    def mid_step(it, v):
        cand = v | (1 << (14 - it))
        cnt = above + count_packed_ge(lo_ref, mid_code(cand).astype(jnp.bfloat16))
        return jnp.where(cnt >= topk, cand, v)
    v15 = lax.fori_loop(0, 15, mid_step, jnp.zeros((1, QB), jnp.int32))
    thr = (r16 << 16) | (v15 << 1)
    thr = jnp.where(count_ge(thr | 1) >= topk, thr | 1, thr)

    cnt_gt = count_ge(thr + 1)
    cnt_ge = count_ge(thr)
    need = topk - cnt_gt
    tie = (cnt_ge - cnt_gt > need) & (thr > KEY_NEG_INF)

    @pl.when(jnp.max(tie.astype(jnp.int32)) > 0)
    def _():
        def count_eq_below(cand):
            def ind(kt):
                idx = lax.broadcasted_iota(jnp.int32, (QB, QB), 0) + kt * QB
                return jnp.where((keys_ref[tile_rows(kt), :] == thr) & (idx < cand), 1, 0)
            return count(ind)

        def idx_step(it, r):
            cand = r | (1 << (15 - it))
            return jnp.where(count_eq_below(cand) < need, cand, r)
        last = lax.fori_loop(0, 16, idx_step, jnp.zeros((1, QB), jnp.int32))

        def drop(kt, carry):
            blk = keys_ref[tile_rows(kt), :]
            idx = lax.broadcasted_iota(jnp.int32, (QB, QB), 0) + kt * QB
            keys_ref[tile_rows(kt), :] = jnp.where(tie & (blk == thr) & (idx > last), INT_MIN, blk)
            return carry
        lax.fori_loop(0, n_tiles, drop, 0)

    def logits(kt, h):
        band = jnp.clip(kt - (i - 2), 0, 2)
        kp = k_ref[0, tile_rows(kt), (h // 2) * 128:(h // 2) * 128 + 128]
        s = jnp.dot(kp, head_rows(qT_ref, h), preferred_element_type=jnp.float32)
        return s + msk_ref[...] + tab_ref[band, h]

    def set_mask(kt):
        msk_ref[...] = jnp.where(keys_ref[tile_rows(kt), :] >= thr, 0.0, NEG)

    def max_tile(kt, carry):
        set_mask(kt)
        for h in range(ATTN_HEADS):
            s = logits(kt, h)
            mp_ref[h] = jnp.maximum(mp_ref[h], jnp.max(s.reshape(QB // 8, 8, QB), axis=0))
        return carry

    seq = kn_ref.shape[2]
    in_extent = lax.broadcasted_iota(jnp.int32, (ATTN_HEADS, seq), 1) < n_tiles * QB
    k_max = jnp.max(jnp.where(in_extent, kn_ref[0], 0.0), axis=1, keepdims=True)
    spread = jnp.zeros((1, QB), jnp.float32)
    for h in range(ATTN_HEADS):
        qh = qT_ref[0, h * ATTN_HEAD_DIM:(h + 1) * ATTN_HEAD_DIM, :].astype(jnp.float32)
        reach = jnp.sqrt(jnp.sum(qh * qh, axis=0, keepdims=True) * k_max[h:h + 1, :]) * NORM_SLACK
        m_ref[h:h + 1, :] = reach + bst_ref[0, h:h + 1, :]
        spread = jnp.maximum(spread, 2.0 * reach + bst_ref[1, h:h + 1, :])
    bound_ok = jnp.max(spread) <= MAX_SHIFT_ERROR

    @pl.when(jnp.logical_not(bound_ok))
    def _():
        mp_ref[...] = jnp.full(mp_ref.shape, NEG, jnp.float32)
        lax.fori_loop(0, n_tiles, max_tile, 0)
        for h in range(ATTN_HEADS):
            m_ref[h:h + 1, :] = jnp.max(mp_ref[h], axis=0, keepdims=True)
    m = [m_ref[h:h + 1, :] for h in range(ATTN_HEADS)]

    def exp_tile(kt, carry):
        set_mask(kt)
        for h in range(ATTN_HEADS):
            p_ref[h] = jnp.exp2(logits(kt, h) - m[h]).astype(jnp.bfloat16)
        for h in range(ATTN_HEADS):
            va = vT_ref[0, h * VROWS:(h + 1) * VROWS, tile_rows(kt)]
            acc_ref[h * VROWS:(h + 1) * VROWS, :] += jnp.dot(va, p_ref[h], preferred_element_type=jnp.float32)
        return carry

    acc_ref[...] = jnp.zeros(acc_ref.shape, jnp.float32)
    lax.fori_loop(0, n_tiles, exp_tile, 0)

    outs = [acc_ref[h * VROWS:h * VROWS + ATTN_HEAD_DIM, :] / acc_ref[h * VROWS + ATTN_HEAD_DIM:h * VROWS + ATTN_HEAD_DIM + 1, :]
            for h in range(ATTN_HEADS)]
    o_ref[0] = jnp.concatenate(outs, axis=0).T


def _dsa_attention(qT, qiT, wT, k, vT, ki2, kn2, rel_bias):
    bsz, _, seq = qT.shape
    topk = min(TOPK_MAX, seq // 4)
    assert seq % QB == 0 and topk <= QB
    assert seq // 16 <= 256
    b2 = rel_bias.astype(jnp.float32) * LOG2E
    bias_stats = jnp.stack([jnp.max(b2, axis=0), jnp.max(b2, axis=0) - jnp.min(b2, axis=0)])
    bias_stats = jnp.broadcast_to(bias_stats[:, :, None], (2, ATTN_HEADS, QB))
    return pl.pallas_call(
        functools.partial(_attn_kernel, topk=topk),
        grid=(bsz, seq // QB),
        in_specs=[
            pl.BlockSpec((1, ATTN_WIDTH, QB), lambda b, i: (b, 0, i)),
            pl.BlockSpec((1, IDX_HEADS * IDX_DIM, QB), lambda b, i: (b, 0, i)),
            pl.BlockSpec((1, IDX_HEADS, QB), lambda b, i: (b, 0, i)),
            pl.BlockSpec((1, seq, ATTN_WIDTH), lambda b, i: (b, 0, 0)),
            pl.BlockSpec((1, ATTN_HEADS * VROWS, seq), lambda b, i: (b, 0, 0)),
            pl.BlockSpec((1, seq, 128), lambda b, i: (b, 0, 0)),
            pl.BlockSpec((1, ATTN_HEADS, seq), lambda b, i: (b, 0, 0)),
            pl.BlockSpec((3, ATTN_HEADS, QB, QB), lambda b, i: (0, 0, 0, 0)),
            pl.BlockSpec((2, ATTN_HEADS, QB), lambda b, i: (0, 0, 0)),
        ],
        out_specs=pl.BlockSpec((1, QB, ATTN_WIDTH), lambda b, i: (b, i, 0)),
        out_shape=jax.ShapeDtypeStruct((bsz, seq, ATTN_WIDTH), jnp.float32),
        scratch_shapes=[
            pltpu.VMEM((seq, QB), jnp.int32),
            pltpu.VMEM((seq, QB), jnp.bfloat16),
            pltpu.VMEM((seq, QB), jnp.bfloat16),
            pltpu.VMEM((QB, QB), jnp.float32),
            pltpu.VMEM((ATTN_HEADS, QB, QB), jnp.bfloat16),
            pltpu.VMEM((ATTN_HEADS * VROWS, QB), jnp.float32),
            pltpu.VMEM((ATTN_HEADS, 8, QB), jnp.float32),
            pltpu.VMEM((ATTN_HEADS, QB), jnp.float32),
        ],
        compiler_params=pltpu.CompilerParams(dimension_semantics=("parallel", "arbitrary"),
                                             vmem_limit_bytes=VMEM_LIMIT),
        name="dsa_attention",
    )(qT, qiT, wT, k, vT, ki2, kn2, _bias_tables(rel_bias), bias_stats)


def _ssd_kernel(xbc_ref, z_ref, sm_ref, cw_ref, cb_ref, dtb_ref, a_ref, dsk_ref, nw_ref, y_ref, prev_ref, st_ref):
    q = SSD_CHUNK
    bf = jnp.bfloat16

    @pl.when(pl.program_id(1) == 0)
    def _():
        prev_ref[...] = jnp.zeros(prev_ref.shape, jnp.float32)
        st_ref[...] = jnp.zeros(st_ref.shape, jnp.float32)

    prev = prev_ref[...]
    for sub in range(xbc_ref.shape[0] // q):
        rows = slice(sub * q, (sub + 1) * q)
        prev = _ssd_chunk(xbc_ref[rows, :], prev, z_ref[rows, :], sm_ref[rows, :], cw_ref, cb_ref, dtb_ref, a_ref,
                          dsk_ref, nw_ref, y_ref.at[rows, :], st_ref)
    prev_ref[...] = prev


def _ssd_chunk(cur, prev, z, sm, cw_ref, cb_ref, dtb_ref, a_ref, dsk_ref, nw_ref, y_ref, st_ref):
    q = SSD_CHUNK
    bf = jnp.bfloat16
    row = lax.broadcasted_iota(jnp.int32, cur.shape, 0)
    acc = cur * cw_ref[CONV_WIDTH - 1:CONV_WIDTH, :] + cb_ref[...]
    for s in range(1, CONV_WIDTH):
        shifted = jnp.where(row >= s, pltpu.roll(cur, s, 0), pltpu.roll(prev, s, 0))
        acc = acc + shifted * cw_ref[CONV_WIDTH - 1 - s:CONV_WIDTH - s, :]
    u = acc * jax.nn.sigmoid(acc)
    xs = u[:, :SSM_INNER]
    bm = u[:, SSM_INNER:SSM_INNER + SSM_GROUPS * SSM_STATE].astype(bf)
    cm = u[:, SSM_INNER + SSM_GROUPS * SSM_STATE:].astype(bf)

    t = sm + dtb_ref[...]
    dt = jnp.maximum(t, 0.0) + jnp.log1p(jnp.exp(-jnp.abs(t)))
    ii = lax.broadcasted_iota(jnp.int32, (q, q), 0)
    jj = lax.broadcasted_iota(jnp.int32, (q, q), 1)
    causal = ii >= jj
    acum = jnp.dot(causal.astype(jnp.float32), dt * a_ref[...], preferred_element_type=jnp.float32, precision=HIGHEST)
    acum_t = acum.T
    dt_t = dt.T
    ea = jnp.exp(acum)
    last = acum[q - 1:q, :]
    decay = jnp.exp(last - acum) * dt
    ea_last = jnp.exp(last)

    lane_hi = lax.broadcasted_iota(jnp.int32, (q, 128), 1) >= SSM_HEAD_DIM
    row_hi = lax.broadcasted_iota(jnp.int32, (128, SSM_STATE), 0) >= SSM_HEAD_DIM

    def pair_cols(v, e):
        c0, c1 = SMALL_DT + e, SMALL_DT + e + 1
        return jnp.where(lane_hi, v[:, c1:c1 + 1], v[:, c0:c0 + 1])

    for g in range(SSM_GROUPS):
        bg = bm[:, g * SSM_STATE:(g + 1) * SSM_STATE]
        cg = cm[:, g * SSM_STATE:(g + 1) * SSM_STATE]
        cb = lax.dot_general(cg, bg, NT, preferred_element_type=jnp.float32)
        for k in range(g * 4, g * 4 + 4):
            e = 2 * k
            x_pair = xs[:, k * 128:(k + 1) * 128]
            halves = []
            for h in (e, e + 1):
                c = SMALL_DT + h
                seg = acum[:, c:c + 1] - acum_t[c:c + 1, :]
                w = cb * jnp.exp(jnp.where(causal, seg, -jnp.inf)) * dt_t[c:c + 1, :]
                halves.append(jnp.dot(w.astype(bf), x_pair.astype(bf), preferred_element_type=jnp.float32))
            y_pair = jnp.where(lane_hi, halves[1], halves[0])
            state = st_ref[k]
            y_pair = y_pair + lax.dot_general(cg, state.astype(bf), NT, preferred_element_type=jnp.float32) * pair_cols(ea, e)
            y_ref[:, k * 128:(k + 1) * 128] = y_pair
            xd_t = (x_pair * pair_cols(decay, e)).T.astype(bf)
            c0 = SMALL_DT + e
            keep = jnp.where(row_hi, ea_last[:, c0 + 1:c0 + 2], ea_last[:, c0:c0 + 1])
            st_ref[k] = state * keep + jnp.dot(xd_t, bg, preferred_element_type=jnp.float32)

    y = (y_ref[...] + dsk_ref[...] * xs) * (z * jax.nn.sigmoid(z))
    half = SSM_INNER // SSM_GROUPS
    for g in range(SSM_GROUPS):
        yg = y[:, g * half:(g + 1) * half]
        yg = yg * lax.rsqrt(jnp.mean(yg * yg, axis=-1, keepdims=True) + EPS)
        y_ref[:, g * half:(g + 1) * half] = yg * nw_ref[:, g * half:(g + 1) * half]
    return cur


def _mamba2_ssd(proj, bsz, seq, conv_w, conv_b, dt_bias, a_log, d_skip, norm_w):
    q = SSD_STEP
    nc = seq // q
    lane_row = lambda v: jnp.zeros((1, 128), jnp.float32).at[0, SMALL_DT:SMALL_DT + SSM_HEADS].set(v)
    const = lambda shape: pl.BlockSpec(shape, lambda b, c: (0,) * len(shape))
    return pl.pallas_call(
        _ssd_kernel,
        grid=(bsz, nc),
        in_specs=[pl.BlockSpec((q, CONV_CH), lambda b, c: (b * nc + c, COL_XBC // CONV_CH)),
                  pl.BlockSpec((q, SSM_INNER), lambda b, c: (b * nc + c, COL_Z // SSM_INNER)),
                  pl.BlockSpec((q, 128), lambda b, c: (b * nc + c, COL_SMALL // 128)),
                  const((CONV_WIDTH, CONV_CH)), const((1, CONV_CH)), const((1, 128)), const((1, 128)),
                  const((1, SSM_INNER)), const((1, SSM_INNER))],
        out_specs=pl.BlockSpec((q, SSM_INNER), lambda b, c: (b * nc + c, 0)),
        out_shape=jax.ShapeDtypeStruct((bsz * seq, SSM_INNER), jnp.float32),
        scratch_shapes=[pltpu.VMEM((SSD_CHUNK, CONV_CH), jnp.float32),
                        pltpu.VMEM((SSM_HEADS // 2, 2 * SSM_HEAD_DIM, SSM_STATE), jnp.float32)],
        compiler_params=pltpu.CompilerParams(dimension_semantics=("parallel", "arbitrary"),
                                             vmem_limit_bytes=VMEM_LIMIT),
        name="mamba2_ssd",
    )(proj, proj, proj, conv_w, conv_b.reshape(1, CONV_CH), lane_row(dt_bias), lane_row(-jnp.exp(a_log)),
      jnp.repeat(d_skip, SSM_HEAD_DIM).reshape(1, SSM_INNER), norm_w.reshape(1, SSM_INNER))


def _mix_out_kernel(a_ref, s_ref, gl_ref, x_ref, gm_ref, wo_ref, ws_ref, wout_ref,
                    nf_ref, scf_ref, shf_ref, wr_ref, br_ref, xo_ref, h_ref, rt_ref, gt_ref, cnt_ref):
    bf = jnp.bfloat16
    ya = jnp.dot(a_ref[...].astype(bf), wo_ref[...], preferred_element_type=jnp.float32)
    ys = jnp.dot(s_ref[...].astype(bf), ws_ref[...], preferred_element_type=jnp.float32)
    mixed = jax.nn.sigmoid(gl_ref[:, :D_MODEL]) * ya + jax.nn.sigmoid(gl_ref[:, D_MODEL:]) * ys
    x = x_ref[...] + gm_ref[0] * jnp.dot(mixed.astype(bf), wout_ref[...], preferred_element_type=jnp.float32)
    xo_ref[...] = x
    y = x * lax.rsqrt(jnp.mean(x * x, axis=-1, keepdims=True) + EPS) * nf_ref[...]
    h = y * (1.0 + scf_ref[0]) + shf_ref[0]
    h_hi = h.astype(bf)
    hb = pltpu.bitcast(h_hi.astype(jnp.float32), jnp.int32)
    half = D_MODEL // 2
    h_ref[...] = (hb[:, :half] & jnp.int32(-65536)) | lax.shift_right_logical(hb[:, half:], 16)
    h_lo = (h - h_hi.astype(jnp.float32)).astype(bf)
    dot = functools.partial(jnp.dot, preferred_element_type=jnp.float32)
    lg = dot(h_hi, wr_ref[0]) + (dot(h_lo, wr_ref[0]) + dot(h_hi, wr_ref[1])) + br_ref[...]

    tm = lg.shape[0]
    work = lg.T[:N_EXPERTS, :]
    row = lax.broadcasted_iota(jnp.int32, work.shape, 0).astype(jnp.float32)
    vals, eids, hits = [], [], []
    for _ in range(TOP_K):
        mx = jnp.max(work, axis=0, keepdims=True)
        ix = jnp.min(jnp.where(work == mx, row, float(N_EXPERTS)), axis=0, keepdims=True)
        vals.append(mx)
        eids.append(ix)
        hits.append(row == ix)
        work = jnp.where(hits[-1], -jnp.inf, work)
    ex = [jnp.exp(v - vals[0]) for v in vals]
    den = (ex[0] + ex[1]) + (ex[2] + ex[3])

    @pl.when(pl.program_id(0) == 0)
    def _():
        cnt_ref[...] = jnp.zeros(cnt_ref.shape, jnp.float32)
    chosen = jnp.zeros(work.shape, jnp.float32)
    for hit in hits:
        chosen = jnp.where(hit, 1.0, chosen)
    earlier = lax.broadcasted_iota(jnp.int32, (tm, tm), 0) < lax.broadcasted_iota(jnp.int32, (tm, tm), 1)
    before = dot(chosen.astype(bf), earlier.astype(bf)) + cnt_ref[:, 0:1]
    cnt_ref[...] = cnt_ref[...] + jnp.sum(chosen, axis=1, keepdims=True)

    slot = lax.broadcasted_iota(jnp.int32, (128, tm), 0)
    route = jnp.zeros((128, tm), jnp.float32)
    for k, hit in enumerate(hits):
        rank = jnp.sum(jnp.where(hit, before, 0.0), axis=0, keepdims=True)
        route = jnp.where(slot == k, eids[k], route)
        route = jnp.where(slot == TOP_K + k, ex[k] / den, route)
        route = jnp.where(slot == 2 * TOP_K + k, rank, route)
    rt_ref[...] = route[:ROUTE_ROWS, :]
    gt_ref[...] = route.T


def _mix_out(attn2, ssd2, proj, x2, g_m, w_attn_o, w_ssm_o, w_out, norm_ffn, sc_f, sh_f, w_router, b_router, seq, tm=512):
    t, d = x2.shape
    per_b = seq // tm
    bf = jnp.bfloat16
    const = lambda shape: pl.BlockSpec(shape, lambda i: (0,) * len(shape))
    perb = pl.BlockSpec((1, 1, d), lambda i: (i // per_b, 0, 0))
    wr = jnp.pad(w_router.astype(jnp.float32), ((0, 0), (0, 128 - N_EXPERTS)))
    wr_hi = wr.astype(bf)
    wr = jnp.stack([wr_hi, (wr - wr_hi.astype(jnp.float32)).astype(bf)])
    br =jnp.pad(b_router, (0, 128 - N_EXPERTS)).reshape(1, 128)
    return pl.pallas_call(
        _mix_out_kernel,
        grid=(t // tm,),
        in_specs=[pl.BlockSpec((tm, ATTN_WIDTH), lambda i: (i, 0)),
                  pl.BlockSpec((tm, SSM_INNER), lambda i: (i, 0)),
                  pl.BlockSpec((tm, 2 * d), lambda i: (i, COL_GATE // (2 * d))),
                  pl.BlockSpec((tm, d), lambda i: (i, 0)),
                  perb,
                  const((ATTN_WIDTH, d)), const((SSM_INNER, d)), const((d, d)),
                  const((1, d)), perb, perb, const((2, d, 128)), const((1, 128))],
        out_specs=[pl.BlockSpec((tm, d), lambda i: (i, 0)),
                   pl.BlockSpec((tm, d // 2), lambda i: (i, 0)),
                   pl.BlockSpec((ROUTE_ROWS, tm), lambda i: (0, i)),
                   pl.BlockSpec((tm, 128), lambda i: (i, 0)),
                   pl.BlockSpec((N_EXPERTS, 128), lambda i: (0, 0))],
        out_shape=[jax.ShapeDtypeStruct((t, d), jnp.float32),
                   jax.ShapeDtypeStruct((t, d // 2), jnp.int32),
                   jax.ShapeDtypeStruct((ROUTE_ROWS, t), jnp.float32),
                   jax.ShapeDtypeStruct((t, 128), jnp.float32),
                   jax.ShapeDtypeStruct((N_EXPERTS, 128), jnp.float32)],
        compiler_params=pltpu.CompilerParams(dimension_semantics=("arbitrary",), vmem_limit_bytes=VMEM_LIMIT),
        name="mix_out",
    )(attn2, ssd2, proj, x2, g_m[:, None, :], w_attn_o.astype(bf), w_ssm_o.astype(bf), w_out.astype(bf),
      norm_ffn.reshape(1, d), sc_f[:, None, :], sh_f[:, None, :], wr, br)


def _moe_kernel(be_ref, nb_ref, x_ref, wgu_ref, bgu_ref, wdn_ref, bdn_ref, *rest, first_block):
    o_ref, wgu_bf, wdn_bf = rest[-3:]
    i = pl.program_id(0)
    j = i + first_block

    @pl.when((i == 0) | (be_ref[j] != be_ref[jnp.maximum(j - 1, 0)]))
    def _():
        wgu_bf[...] = wgu_ref[0, 0].astype(jnp.bfloat16)
        wdn_bf[...] = wdn_ref[0, 0].astype(jnp.bfloat16)

    @pl.when(j < nb_ref[0])
    def _():
        words = x_ref[...]
        x_hi = pltpu.bitcast(words & jnp.int32(-65536), jnp.float32).astype(jnp.bfloat16)
        x_lo = pltpu.bitcast(words << 16, jnp.float32).astype(jnp.bfloat16)
        x = jnp.concatenate([x_hi, x_lo], axis=1)
        gu = jnp.dot(x, wgu_bf[...], preferred_element_type=jnp.float32) + bgu_ref[0, 0]
        g = jnp.minimum(gu[:, :D_EXPERT], SWIGLU_LIMIT)
        u = jnp.clip(gu[:, D_EXPERT:], -SWIGLU_LIMIT, SWIGLU_LIMIT)
        act = (u + 1.0) * (g * jax.nn.sigmoid(SWIGLU_ALPHA * g))
        out = jnp.dot(act.astype(jnp.bfloat16), wdn_bf[...], preferred_element_type=jnp.float32) + bdn_ref[0, 0]
        o_ref[...] = out.astype(o_ref.dtype)

    @pl.when(j >= nb_ref[0])
    def _():
        o_ref[...] = jnp.zeros_like(o_ref)


def _moe_ffn(xs_parts, blk_exp, n_used, w_gu, b_gu, w_dn, b_dn, layer):
    d = D_MODEL
    tm = MOE_TM
    n_rows = sum(xs.shape[0] for xs in xs_parts)
    out, first = None, 0
    for xs in xs_parts:
        nblk = xs.shape[0] // tm
        wmap = lambda i, be, nb, first=first: (layer, be[i + first], 0, 0)
        in_specs = [pl.BlockSpec((tm, d // 2), lambda i, be, nb: (i, 0)),
                    pl.BlockSpec((1, 1, d, 2 * D_EXPERT), wmap),
                    pl.BlockSpec((1, 1, 1, 2 * D_EXPERT), wmap),
                    pl.BlockSpec((1, 1, D_EXPERT, d), wmap),
                    pl.BlockSpec((1, 1, 1, d), wmap)]
        args = [blk_exp, n_used, xs, w_gu, b_gu[:, :, None, :], w_dn, b_dn[:, :, None, :]]
        aliases = {}
        if out is not None:
            in_specs.append(pl.BlockSpec(memory_space=pl.ANY))
            args.append(out)
            aliases = {len(args) - 1: 0}
        grid_spec = pltpu.PrefetchScalarGridSpec(
            num_scalar_prefetch=2,
            grid=(nblk,),
            in_specs=in_specs,
            out_specs=pl.BlockSpec((tm, d), lambda i, be, nb, first=first: (i + first, 0)),
            scratch_shapes=[pltpu.VMEM((d, 2 * D_EXPERT), jnp.bfloat16), pltpu.VMEM((D_EXPERT, d), jnp.bfloat16)],
        )
        out = pl.pallas_call(
            functools.partial(_moe_kernel, first_block=first),
            grid_spec=grid_spec,
            out_shape=jax.ShapeDtypeStruct((n_rows, d), jnp.bfloat16),
            input_output_aliases=aliases,
            compiler_params=pltpu.CompilerParams(dimension_semantics=("arbitrary",), vmem_limit_bytes=VMEM_LIMIT),
            name="moe_ffn",
        )(*args)
        first += nblk
    return out


def _moe(h2, route, expert_counts, w_gu, b_gu, w_dn, b_dn, layer):
    t = h2.shape[0]
    d = D_MODEL
    tm = MOE_TM
    i32 = jnp.int32
    experts = jnp.arange(N_EXPERTS, dtype=i32)
    top_idx = route[:TOP_K].astype(i32)
    rank = route[2 * TOP_K:3 * TOP_K].astype(i32)
    n_assign = t * TOP_K
    n_rows = n_assign + N_EXPERTS * tm
    e_flat = top_idx.T.reshape(n_assign)
    counts = expert_counts[:, 0].astype(i32)
    padded = (counts + tm - 1) // tm * tm
    pad_start = jnp.cumsum(padded) - padded
    dest = rank + jnp.sum(jnp.where(top_idx[..., None] == experts, pad_start, 0), axis=-1)
    dest = dest.reshape(-1)
    filler_exp = jnp.repeat(experts, tm)
    filler_key = jnp.where(jnp.tile(jnp.arange(tm, dtype=i32), N_EXPERTS) < jnp.repeat(padded - counts, tm),
                           filler_exp, N_EXPERTS)
    keys = jnp.concatenate([e_flat, filler_key])
    rows = jnp.arange(n_rows, dtype=i32)
    row_key, row_src = lax.sort((keys, rows), num_keys=1)
    row_tok = jnp.where(row_src < n_assign, row_src // TOP_K, rows % t)
    blk_exp = jnp.minimum(row_key[::tm], N_EXPERTS - 1)
    n_used = (jnp.sum(padded, keepdims=True) // tm).astype(i32)
    slab = n_rows // MOE_SLABS
    xs_parts = [h2[row_tok[s * slab:(s + 1) * slab]] for s in range(MOE_SLABS)]
    out = _moe_ffn(xs_parts, blk_exp, n_used, w_gu, b_gu, w_dn, b_dn, layer)
    return out[dest].reshape(TOP_K, t, d)


def _combine_kernel(p_ref, r_ref, x_ref, g_ref, o_ref):
    f32 = jnp.float32
    w = [r_ref[:, TOP_K + k:TOP_K + k + 1] for k in range(TOP_K)]
    y = (w[0] * p_ref[0].astype(f32) + w[1] * p_ref[1].astype(f32)) + (w[2] * p_ref[2].astype(f32) + w[3] * p_ref[3].astype(f32))
    o_ref[...] = x_ref[...] + g_ref[0] * y


def _combine(parts, route_tok, x2, g_f, seq, tm=512):
    t, d = x2.shape
    per_b = seq // tm
    return pl.pallas_call(
        _combine_kernel,
        grid=(t // tm,),
        in_specs=[pl.BlockSpec((TOP_K, tm, d), lambda i: (0, i, 0)),
                  pl.BlockSpec((tm, 128), lambda i: (i, 0)),
                  pl.BlockSpec((tm, d), lambda i: (i, 0)),
                  pl.BlockSpec((1, 1, d), lambda i: (i // per_b, 0, 0))],
        out_specs=pl.BlockSpec((tm, d), lambda i: (i, 0)),
        out_shape=jax.ShapeDtypeStruct((t, d), jnp.float32),
        compiler_params=pltpu.CompilerParams(dimension_semantics=("parallel",), vmem_limit_bytes=VMEM_LIMIT),
        name="moe_combine",
    )(parts, route_tok, x2, g_f[:, None, :])


def kernel(x, c, rel_bias, w_ada, b_ada, norm_mix, norm_ffn, w_in, kv_norm, w_kv_up, q_norm, k_norm,
           idx_k_ln_w, idx_k_ln_b, w_attn_o, conv_w, conv_b, dt_bias, a_log, d_skip, ssm_norm, w_ssm_o,
           w_out, w_router, b_router, w_gu, b_gu, w_dn, b_dn):
    bsz, seq, d = x.shape
    t = bsz * seq
    cond = jax.nn.silu(c)
    x2 = x.reshape(t, d)
    for l in range(DEPTH):
        mod = cond @ w_ada[l] + b_ada[l]
        sh_m, sc_m, g_m, sh_f, sc_f, g_f = jnp.split(mod, 6, axis=-1)
        proj = _in_proj(x2, norm_mix[l], sc_m, sh_m, _pack_w_in(w_in[l]), seq)
        qT, k, vT, qiT, ki2, wT, kn2 = _prep(proj, bsz, seq, q_norm[l], kv_norm[l], w_kv_up[l], k_norm[l],
                                             idx_k_ln_w[l], idx_k_ln_b[l])
        attn = _dsa_attention(qT, qiT, wT, k, vT, ki2, kn2, rel_bias)
        y_ssd = _mamba2_ssd(proj, bsz, seq, conv_w[l], conv_b[l], dt_bias[l], a_log[l], d_skip[l], ssm_norm[l])
        x2, h2, route, route_tok, expert_counts = _mix_out(
            attn.reshape(t, ATTN_WIDTH), y_ssd, proj, x2, g_m, w_attn_o[l], w_ssm_o[l], w_out[l], norm_ffn[l],
            sc_f, sh_f, w_router[l], b_router[l], seq)
        parts = _moe(h2, route, expert_counts, w_gu, b_gu, w_dn, b_dn, l)
        x2 = _combine(parts, route_tok, x2, g_f, seq)
    return x2.reshape(bsz, seq, d)
```

```python
import functools
import math

import jax
import jax.numpy as jnp
import numpy as np
from jax import lax
from jax.experimental import pallas as pl
from jax.experimental.pallas import tpu as pltpu

D_MODEL = 1024
DEPTH = 2
ATTN_HEADS = 8
ATTN_HEAD_DIM = 64
ATTN_WIDTH = ATTN_HEADS * ATTN_HEAD_DIM
KV_RANK = 256
IDX_HEADS = 8
IDX_DIM = 64
TOPK_MAX = 256
N_BUCKETS = 32
MAX_DISTANCE = 128
SSM_HEADS = 16
SSM_HEAD_DIM = 64
SSM_INNER = SSM_HEADS * SSM_HEAD_DIM
SSM_GROUPS = 2
SSM_STATE = 128
CONV_WIDTH = 4
CONV_CH = SSM_INNER + 2 * SSM_GROUPS * SSM_STATE
SSD_CHUNK = 128
SSD_STEP = 256
N_EXPERTS = 32
TOP_K = 4
D_EXPERT = D_MODEL
SWIGLU_LIMIT = 7.0
SWIGLU_ALPHA = 1.702
EPS = 1e-6

COL_Q = 0
COL_KV = 512
COL_QI = 768
COL_SMALL = 1280
COL_XBC = 1536
COL_Z = 3072
COL_GATE = 4096
PROJ_COLS = 6144
PREP_COLS = 1408
SMALL_KI, SMALL_WI, SMALL_DT = 0, 64, 72

QB = 256
VROWS = 80
INT_MIN = -2 ** 31
KEY_NEG_INF = (0xFF800000 ^ 0x7FFFFFFF) - 2 ** 32
NEG = -1e30
TINY = 2.0 ** -126
LOG2E = math.log2(math.e)
NORM_SLACK = 1.02
MAX_SHIFT_ERROR = 96.0
VMEM_LIMIT = 56 * 1024 * 1024
MOE_TM = 512
MOE_SLABS = 4
ROW_BITS = 18
ROUTE_ROWS = 16
HIGHEST = lax.Precision.HIGHEST
NT = (((1,), (1,)), ((), ()))


def _pack_w_in(w):
    o = np.cumsum((0, ATTN_WIDTH, KV_RANK, IDX_HEADS * IDX_DIM, IDX_DIM, IDX_HEADS, SSM_INNER, CONV_CH, SSM_HEADS, 2 * D_MODEL))
    q, kv, qi, ki, wi, z, xbc, dt, gate = (w[:, int(o[n]):int(o[n + 1])] for n in range(9))
    zeros = lambda n: jnp.zeros((w.shape[0], n), w.dtype)
    small = jnp.concatenate([ki, wi, dt, zeros(128 - 88)], axis=1)
    packed = jnp.concatenate([q, kv, qi, small, zeros(COL_XBC - PREP_COLS), xbc, z, gate], axis=1)
    assert packed.shape[1] == PROJ_COLS
    return packed.astype(jnp.bfloat16)


def _in_proj_kernel(x_ref, g_ref, sc_ref, sh_ref, w_ref, o_ref, h_ref):
    @pl.when(pl.program_id(1) == 0)
    def _():
        x = x_ref[...]
        y = x * lax.rsqrt(jnp.mean(x * x, axis=-1, keepdims=True) + EPS) * g_ref[...]
        h_ref[...] = (y * (1.0 + sc_ref[0]) + sh_ref[0]).astype(jnp.bfloat16)
    o_ref[...] = jnp.dot(h_ref[...], w_ref[...], preferred_element_type=jnp.float32)


def _in_proj(x2, gain, sc, sh, w_packed, seq, tm=1024, tn=2048):
    t, d = x2.shape
    per_b = seq // tm
    return pl.pallas_call(
        _in_proj_kernel,
        grid=(t // tm, PROJ_COLS // tn),
        in_specs=[pl.BlockSpec((tm, d), lambda i, j: (i, 0)),
                  pl.BlockSpec((1, d), lambda i, j: (0, 0)),
                  pl.BlockSpec((1, 1, d), lambda i, j: (i // per_b, 0, 0)),
                  pl.BlockSpec((1, 1, d), lambda i, j: (i // per_b, 0, 0)),
                  pl.BlockSpec((d, tn), lambda i, j: (0, j))],
        out_specs=pl.BlockSpec((tm, tn), lambda i, j: (i, j)),
        out_shape=jax.ShapeDtypeStruct((t, PROJ_COLS), jnp.float32),
        scratch_shapes=[pltpu.VMEM((tm, d), jnp.bfloat16)],
        compiler_params=pltpu.CompilerParams(dimension_semantics=("parallel", "arbitrary"),
                                             vmem_limit_bytes=VMEM_LIMIT),
        name="in_proj",
    )(x2, gain.reshape(1, d), sc[:, None, :], sh[:, None, :], w_packed)


def _head_rms_t(xt):
    x3 = xt.reshape(ATTN_HEADS, ATTN_HEAD_DIM, xt.shape[1])
    return lax.rsqrt(jnp.mean(x3 * x3, axis=1, keepdims=True) + EPS)


def _prep_kernel(p_ref, qg_ref, kvg_ref, wkv_ref, kg_ref, lng_ref, lnb_ref,
                 qT_ref, k_ref, vT_ref, qiT_ref, ki_ref, wT_ref, kn2_ref):
    n = p_ref.shape[0]
    q = p_ref[:, COL_Q:COL_Q + ATTN_WIDTH]
    lat = p_ref[:, COL_KV:COL_KV + KV_RANK]
    qi = p_ref[:, COL_QI:COL_QI + IDX_HEADS * IDX_DIM]
    sm = p_ref[:, COL_SMALL:COL_SMALL + 128]

    scale = ATTN_HEAD_DIM ** -0.5 * LOG2E
    qt = q.T
    qn = qt.reshape(ATTN_HEADS, ATTN_HEAD_DIM, n) * _head_rms_t(qt)
    qT_ref[0] = (qn.reshape(ATTN_WIDTH, n) * qg_ref[...] * scale).astype(jnp.bfloat16)

    latn = lat * lax.rsqrt(jnp.mean(lat * lat, axis=-1, keepdims=True) + EPS) * kvg_ref[...]
    kv = jnp.dot(latn.astype(jnp.bfloat16), wkv_ref[...], preferred_element_type=jnp.float32)
    kt = kv[:, :ATTN_WIDTH].T
    kn = (kt.reshape(ATTN_HEADS, ATTN_HEAD_DIM, n) * _head_rms_t(kt)).reshape(ATTN_WIDTH, n) * kg_ref[...]
    k_ref[0] = kn.T.astype(jnp.bfloat16)
    kn3 = kn.reshape(ATTN_HEADS, ATTN_HEAD_DIM, n)
    kn2_ref[0] = jnp.sum(kn3 * kn3, axis=1)
    vt = kv[:, ATTN_WIDTH:].T.reshape(ATTN_HEADS, ATTN_HEAD_DIM, n)
    ones = jnp.ones((ATTN_HEADS, VROWS - ATTN_HEAD_DIM, n), jnp.float32)
    vT_ref[0] = jnp.concatenate([vt, ones], axis=1).reshape(ATTN_HEADS * VROWS, n).astype(jnp.bfloat16)

    qiT_ref[0] = (qi * (IDX_DIM ** -0.5)).T.astype(jnp.bfloat16)

    lane = lax.broadcasted_iota(jnp.int32, sm.shape, 1)
    kid = jnp.where(lane < IDX_DIM, sm, pltpu.roll(sm, IDX_DIM, 1))
    mu = jnp.mean(kid, axis=-1, keepdims=True)
    var = jnp.mean(jnp.square(kid - mu), axis=-1, keepdims=True)
    ki_ref[0] = ((kid - mu) * lax.rsqrt(var + EPS) * lng_ref[...] + lnb_ref[...]).astype(jnp.bfloat16)

    wT_ref[0] = sm.T[SMALL_WI:SMALL_WI + IDX_HEADS, :] * (IDX_HEADS ** -0.5)


def _prep(proj, bsz, seq, q_norm, kv_norm, w_kv_up, k_norm, ln_w, ln_b, tp=512):
    nb = seq // tp
    tile8 = lambda g: jnp.tile(g, ATTN_HEADS).reshape(ATTN_WIDTH, 1)
    const = lambda shape: pl.BlockSpec(shape, lambda b, i: (0,) * len(shape))
    bf = jnp.bfloat16
    return pl.pallas_call(
        _prep_kernel,
        grid=(bsz, nb),
        in_specs=[pl.BlockSpec((tp, PREP_COLS), lambda b, i: (b * nb + i, 0)),
                  const((ATTN_WIDTH, 1)), const((1, KV_RANK)), const((KV_RANK, 2 * ATTN_WIDTH)),
                  const((ATTN_WIDTH, 1)), const((1, 128)), const((1, 128))],
        out_specs=[pl.BlockSpec((1, ATTN_WIDTH, tp), lambda b, i: (b, 0, i)),
                   pl.BlockSpec((1, tp, ATTN_WIDTH), lambda b, i: (b, i, 0)),
                   pl.BlockSpec((1, ATTN_HEADS * VROWS, tp), lambda b, i: (b, 0, i)),
                   pl.BlockSpec((1, ATTN_WIDTH, tp), lambda b, i: (b, 0, i)),
                   pl.BlockSpec((1, tp, 128), lambda b, i: (b, i, 0)),
                   pl.BlockSpec((1, IDX_HEADS, tp), lambda b, i: (b, 0, i)),
                   pl.BlockSpec((1, ATTN_HEADS, tp), lambda b, i: (b, 0, i))],
        out_shape=[jax.ShapeDtypeStruct((bsz, ATTN_WIDTH, seq), bf),
                   jax.ShapeDtypeStruct((bsz, seq, ATTN_WIDTH), bf),
                   jax.ShapeDtypeStruct((bsz, ATTN_HEADS * VROWS, seq), bf),
                   jax.ShapeDtypeStruct((bsz, ATTN_WIDTH, seq), bf),
                   jax.ShapeDtypeStruct((bsz, seq, 128), bf),
                   jax.ShapeDtypeStruct((bsz, IDX_HEADS, seq), jnp.float32),
                   jax.ShapeDtypeStruct((bsz, ATTN_HEADS, seq), jnp.float32)],
        compiler_params=pltpu.CompilerParams(dimension_semantics=("parallel", "parallel"),
                                             vmem_limit_bytes=VMEM_LIMIT),
        name="attn_prep",
    )(proj, tile8(q_norm), kv_norm.reshape(1, KV_RANK), w_kv_up.astype(bf), tile8(k_norm),
      jnp.tile(ln_w, 2).reshape(1, 128), jnp.tile(ln_b, 2).reshape(1, 128))


def _t5_bucket(dist):
    n = jnp.maximum(dist, 0)
    max_exact = N_BUCKETS // 2
    nf = jnp.maximum(n, 1).astype(jnp.float32)
    large = max_exact + (jnp.log(nf / max_exact) / math.log(MAX_DISTANCE / max_exact) * (N_BUCKETS - max_exact)).astype(jnp.int32)
    large = jnp.minimum(large, N_BUCKETS - 1)
    return jnp.where(n < max_exact, n, large)


def _bias_tables(rel_bias):
    s = jnp.arange(QB, dtype=jnp.int32)[None, :, None]
    q = jnp.arange(QB, dtype=jnp.int32)[None, None, :]
    dist = q - s + jnp.array([2 * QB, QB, 0], jnp.int32)[:, None, None]
    onehot = (_t5_bucket(dist)[..., None] == jnp.arange(N_BUCKETS, dtype=jnp.int32)).astype(jnp.float32)
    b = jnp.einsum('tsqb,bh->thsq', onehot, rel_bias.astype(jnp.float32) * LOG2E, precision=HIGHEST)
    return jnp.where((dist >= 0)[:, None], b, NEG)


def _attn_kernel(qT_ref, qiT_ref, wT_ref, k_ref, vT_ref, ki_ref, kn_ref, tab_ref, bst_ref, o_ref,
                 keys_ref, hi_ref, lo_ref, msk_ref, p_ref, acc_ref, mp_ref, m_ref, *, topk):
    i = pl.program_id(1)
    n_tiles = i + 1
    row_hi = lax.broadcasted_iota(jnp.int32, (128, QB), 0) >= 64

    def head_rows(ref, h):
        pair = ref[0, (h // 2) * 128:(h // 2) * 128 + 128, :]
        return jnp.where(row_hi == bool(h % 2), pair, jnp.zeros_like(pair))

    def tile_rows(kt):
        return pl.ds(pl.multiple_of(kt * QB, QB), QB)

    def score_tile(kt, carry):
        ki = ki_ref[0, tile_rows(kt), :]
        sc = jnp.zeros((QB, QB), jnp.float32)
        for h in range(IDX_HEADS):
            d = jnp.dot(ki, head_rows(qiT_ref, h), preferred_element_type=jnp.float32)
            sc = sc + wT_ref[0, h:h + 1, :] * jnp.maximum(d, 0.0)
        srow = lax.broadcasted_iota(jnp.int32, (QB, QB), 0)
        qcol = lax.broadcasted_iota(jnp.int32, (QB, QB), 1)
        sc = jnp.where(jnp.abs(sc) < TINY, 0.0, sc)
        sc = jnp.where((kt == i) & (srow > qcol), -jnp.inf, sc)
        bits = pltpu.bitcast(sc, jnp.int32)
        keys_ref[tile_rows(kt), :] = bits ^ ((bits >> 31) & 0x7FFFFFFF)
        hi_ref[tile_rows(kt), :] = pltpu.bitcast(bits & jnp.int32(-65536), jnp.float32).astype(jnp.bfloat16)
        return carry
    lax.fori_loop(0, n_tiles, score_tile, 0)

    def count_packed_ge(ref, cb):
        one, zero = jnp.ones((), jnp.bfloat16), jnp.zeros((), jnp.bfloat16)

        def body(kt, acc):
            hit = jnp.where(ref[tile_rows(kt), :] >= cb, one, zero)
            parts = [hit[r:r + 16, :] for r in range(0, QB, 16)]
            while len(parts) > 1:
                parts = [a + b for a, b in zip(parts[::2], parts[1::2])]
            return acc + parts[0]
        acc = lax.fori_loop(0, n_tiles, body, jnp.zeros((16, QB), jnp.bfloat16))
        return jnp.sum(acc.astype(jnp.float32), axis=0, keepdims=True)

    def count_hi_ge(cand16):
        b = cand16 ^ ((cand16 >> 15) & 0x7FFF)
        snap = jnp.where(((b & 0x8000) != 0) | ((b & 0x7F) == 0), 0, 0x0080)
        b = jnp.where((b & 0x7F80) == 0, snap, b)
        return count_packed_ge(hi_ref, pltpu.bitcast(b << 16, jnp.float32).astype(jnp.bfloat16))

    def mid_code(v):
        pat = jnp.where(v >= 16384, v - 16256, 0x8000 | (16511 - v))
        return pltpu.bitcast(pat << 16, jnp.float32)

    def count(hit_of_tile):
        def body(kt, acc):
            return acc + jnp.sum(hit_of_tile(kt).reshape(QB // 8, 8, QB), axis=0)
        acc = lax.fori_loop(0, n_tiles, body, jnp.zeros((8, QB), jnp.int32))
        return jnp.sum(acc, axis=0, keepdims=True)

    def count_ge(cand):
        return count(lambda kt: jnp.where(keys_ref[tile_rows(kt), :] >= cand, 1, 0))

    def hi_step(it, r):
        cand = jnp.where(it == 0, jnp.zeros_like(r), r | (1 << (15 - it)))
        return jnp.where(count_hi_ge(cand) >= topk, cand, r)
    r16 = lax.fori_loop(0, 16, hi_step, jnp.full((1, QB), -32768, jnp.int32))

    above = count_hi_ge(r16 + 1)

    def code_tile(kt, carry):
        key = keys_ref[tile_rows(kt), :]
        code = jnp.where((key >> 16) == r16, mid_code((key >> 1) & 0x7FFF), -jnp.inf)
        lo_ref[tile_rows(kt), :] = code.astype(jnp.bfloat16)
        return carry
    lax.fori_loop(0, n_tiles, code_tile, 0)

    def mid_step(it, v):
        cand = v | (1 << (14 - it))
        cnt = above + count_packed_ge(lo_ref, mid_code(cand).astype(jnp.bfloat16))
        return jnp.where(cnt >= topk, cand, v)
    v15 = lax.fori_loop(0, 15, mid_step, jnp.zeros((1, QB), jnp.int32))
    thr = (r16 << 16) | (v15 << 1)
    thr = jnp.where(count_ge(thr | 1) >= topk, thr | 1, thr)

    cnt_gt = count_ge(thr + 1)
    cnt_ge = count_ge(thr)
    need = topk - cnt_gt
    tie = (cnt_ge - cnt_gt > need) & (thr > KEY_NEG_INF)

    @pl.when(jnp.max(tie.astype(jnp.int32)) > 0)
    def _():
        def count_eq_below(cand):
            def ind(kt):
                idx = lax.broadcasted_iota(jnp.int32, (QB, QB), 0) + kt * QB
                return jnp.where((keys_ref[tile_rows(kt), :] == thr) & (idx < cand), 1, 0)
            return count(ind)

        def idx_step(it, r):
            cand = r | (1 << (15 - it))
            return jnp.where(count_eq_below(cand) < need, cand, r)
        last = lax.fori_loop(0, 16, idx_step, jnp.zeros((1, QB), jnp.int32))

        def drop(kt, carry):
            blk = keys_ref[tile_rows(kt), :]
            idx = lax.broadcasted_iota(jnp.int32, (QB, QB), 0) + kt * QB
            keys_ref[tile_rows(kt), :] = jnp.where(tie & (blk == thr) & (idx > last), INT_MIN, blk)
            return carry
        lax.fori_loop(0, n_tiles, drop, 0)

    def logits(kt, h):
        band = jnp.clip(kt - (i - 2), 0, 2)
        kp = k_ref[0, tile_rows(kt), (h // 2) * 128:(h // 2) * 128 + 128]
        s = jnp.dot(kp, head_rows(qT_ref, h), preferred_element_type=jnp.float32)
        return s + msk_ref[...] + tab_ref[band, h]

    def set_mask(kt):
        msk_ref[...] = jnp.where(keys_ref[tile_rows(kt), :] >= thr, 0.0, NEG)

    def max_tile(kt, carry):
        set_mask(kt)
        for h in range(ATTN_HEADS):
            s = logits(kt, h)
            mp_ref[h] = jnp.maximum(mp_ref[h], jnp.max(s.reshape(QB // 8, 8, QB), axis=0))
        return carry

    seq = kn_ref.shape[2]
    in_extent = lax.broadcasted_iota(jnp.int32, (ATTN_HEADS, seq), 1) < n_tiles * QB
    k_max = jnp.max(jnp.where(in_extent, kn_ref[0], 0.0), axis=1, keepdims=True)
    spread = jnp.zeros((1, QB), jnp.float32)
    for h in range(ATTN_HEADS):
        qh = qT_ref[0, h * ATTN_HEAD_DIM:(h + 1) * ATTN_HEAD_DIM, :].astype(jnp.float32)
        reach = jnp.sqrt(jnp.sum(qh * qh, axis=0, keepdims=True) * k_max[h:h + 1, :]) * NORM_SLACK
        m_ref[h:h + 1, :] = reach + bst_ref[0, h:h + 1, :]
        spread = jnp.maximum(spread, 2.0 * reach + bst_ref[1, h:h + 1, :])
    bound_ok = jnp.max(spread) <= MAX_SHIFT_ERROR

    @pl.when(jnp.logical_not(bound_ok))
    def _():
        mp_ref[...] = jnp.full(mp_ref.shape, NEG, jnp.float32)
        lax.fori_loop(0, n_tiles, max_tile, 0)
        for h in range(ATTN_HEADS):
            m_ref[h:h + 1, :] = jnp.max(mp_ref[h], axis=0, keepdims=True)
    m = [m_ref[h:h + 1, :] for h in range(ATTN_HEADS)]

    def exp_tile(kt, carry):
        set_mask(kt)
        for h in range(ATTN_HEADS):
            p_ref[h] = jnp.exp2(logits(kt, h) - m[h]).astype(jnp.bfloat16)
        for h in range(ATTN_HEADS):
            va = vT_ref[0, h * VROWS:(h + 1) * VROWS, tile_rows(kt)]
            acc_ref[h * VROWS:(h + 1) * VROWS, :] += jnp.dot(va, p_ref[h], preferred_element_type=jnp.float32)
        return carry

    acc_ref[...] = jnp.zeros(acc_ref.shape, jnp.float32)
    lax.fori_loop(0, n_tiles, exp_tile, 0)

    outs = [acc_ref[h * VROWS:h * VROWS + ATTN_HEAD_DIM, :] / acc_ref[h * VROWS + ATTN_HEAD_DIM:h * VROWS + ATTN_HEAD_DIM + 1, :]
            for h in range(ATTN_HEADS)]
    o_ref[0] = jnp.concatenate(outs, axis=0).T


def _dsa_attention(qT, qiT, wT, k, vT, ki2, kn2, rel_bias):
    bsz, _, seq = qT.shape
    topk = min(TOPK_MAX, seq // 4)
    assert seq % QB == 0 and topk <= QB
    assert seq // 16 <= 256
    b2 = rel_bias.astype(jnp.float32) * LOG2E
    bias_stats = jnp.stack([jnp.max(b2, axis=0), jnp.max(b2, axis=0) - jnp.min(b2, axis=0)])
    bias_stats = jnp.broadcast_to(bias_stats[:, :, None], (2, ATTN_HEADS, QB))
    return pl.pallas_call(
        functools.partial(_attn_kernel, topk=topk),
        grid=(bsz, seq // QB),
        in_specs=[
            pl.BlockSpec((1, ATTN_WIDTH, QB), lambda b, i: (b, 0, i)),
            pl.BlockSpec((1, IDX_HEADS * IDX_DIM, QB), lambda b, i: (b, 0, i)),
            pl.BlockSpec((1, IDX_HEADS, QB), lambda b, i: (b, 0, i)),
            pl.BlockSpec((1, seq, ATTN_WIDTH), lambda b, i: (b, 0, 0)),
            pl.BlockSpec((1, ATTN_HEADS * VROWS, seq), lambda b, i: (b, 0, 0)),
            pl.BlockSpec((1, seq, 128), lambda b, i: (b, 0, 0)),
            pl.BlockSpec((1, ATTN_HEADS, seq), lambda b, i: (b, 0, 0)),
            pl.BlockSpec((3, ATTN_HEADS, QB, QB), lambda b, i: (0, 0, 0, 0)),
            pl.BlockSpec((2, ATTN_HEADS, QB), lambda b, i: (0, 0, 0)),
        ],
        out_specs=pl.BlockSpec((1, QB, ATTN_WIDTH), lambda b, i: (b, i, 0)),
        out_shape=jax.ShapeDtypeStruct((bsz, seq, ATTN_WIDTH), jnp.float32),
        scratch_shapes=[
            pltpu.VMEM((seq, QB), jnp.int32),
            pltpu.VMEM((seq, QB), jnp.bfloat16),
            pltpu.VMEM((seq, QB), jnp.bfloat16),
            pltpu.VMEM((QB, QB), jnp.float32),
            pltpu.VMEM((ATTN_HEADS, QB, QB), jnp.bfloat16),
            pltpu.VMEM((ATTN_HEADS * VROWS, QB), jnp.float32),
            pltpu.VMEM((ATTN_HEADS, 8, QB), jnp.float32),
            pltpu.VMEM((ATTN_HEADS, QB), jnp.float32),
        ],
        compiler_params=pltpu.CompilerParams(dimension_semantics=("parallel", "arbitrary"),
                                             vmem_limit_bytes=VMEM_LIMIT),
        name="dsa_attention",
    )(qT, qiT, wT, k, vT, ki2, kn2, _bias_tables(rel_bias), bias_stats)


def _ssd_kernel(xbc_ref, z_ref, sm_ref, cw_ref, cb_ref, dtb_ref, a_ref, dsk_ref, nw_ref, y_ref, prev_ref, st_ref):
    q = SSD_CHUNK
    bf = jnp.bfloat16

    @pl.when(pl.program_id(1) == 0)
    def _():
        prev_ref[...] = jnp.zeros(prev_ref.shape, jnp.float32)
        st_ref[...] = jnp.zeros(st_ref.shape, jnp.float32)

    prev = prev_ref[...]
    for sub in range(xbc_ref.shape[0] // q):
        rows = slice(sub * q, (sub + 1) * q)
        prev = _ssd_chunk(xbc_ref[rows, :], prev, z_ref[rows, :], sm_ref[rows, :], cw_ref, cb_ref, dtb_ref, a_ref,
                          dsk_ref, nw_ref, y_ref.at[rows, :], st_ref)
    prev_ref[...] = prev


def _ssd_chunk(cur, prev, z, sm, cw_ref, cb_ref, dtb_ref, a_ref, dsk_ref, nw_ref, y_ref, st_ref):
    q = SSD_CHUNK
    bf = jnp.bfloat16
    row = lax.broadcasted_iota(jnp.int32, cur.shape, 0)
    acc = cur * cw_ref[CONV_WIDTH - 1:CONV_WIDTH, :] + cb_ref[...]
    for s in range(1, CONV_WIDTH):
        shifted = jnp.where(row >= s, pltpu.roll(cur, s, 0), pltpu.roll(prev, s, 0))
        acc = acc + shifted * cw_ref[CONV_WIDTH - 1 - s:CONV_WIDTH - s, :]
    u = acc * jax.nn.sigmoid(acc)
    xs = u[:, :SSM_INNER]
    bm = u[:, SSM_INNER:SSM_INNER + SSM_GROUPS * SSM_STATE].astype(bf)
    cm = u[:, SSM_INNER + SSM_GROUPS * SSM_STATE:].astype(bf)

    t = sm + dtb_ref[...]
    dt = jnp.maximum(t, 0.0) + jnp.log1p(jnp.exp(-jnp.abs(t)))
    ii = lax.broadcasted_iota(jnp.int32, (q, q), 0)
    jj = lax.broadcasted_iota(jnp.int32, (q, q), 1)
    causal = ii >= jj
    acum = jnp.dot(causal.astype(jnp.float32), dt * a_ref[...], preferred_element_type=jnp.float32, precision=HIGHEST)
    acum_t = acum.T
    dt_t = dt.T
    ea = jnp.exp(acum)
    last = acum[q - 1:q, :]
    decay = jnp.exp(last - acum) * dt
    ea_last = jnp.exp(last)

    lane_hi = lax.broadcasted_iota(jnp.int32, (q, 128), 1) >= SSM_HEAD_DIM
    row_hi = lax.broadcasted_iota(jnp.int32, (128, SSM_STATE), 0) >= SSM_HEAD_DIM

    def pair_cols(v, e):
        c0, c1 = SMALL_DT + e, SMALL_DT + e + 1
        return jnp.where(lane_hi, v[:, c1:c1 + 1], v[:, c0:c0 + 1])

    for g in range(SSM_GROUPS):
        bg = bm[:, g * SSM_STATE:(g + 1) * SSM_STATE]
        cg = cm[:, g * SSM_STATE:(g + 1) * SSM_STATE]
        cb = lax.dot_general(cg, bg, NT, preferred_element_type=jnp.float32)
        for k in range(g * 4, g * 4 + 4):
            e = 2 * k
            x_pair = xs[:, k * 128:(k + 1) * 128]
            halves = []
            for h in (e, e + 1):
                c = SMALL_DT + h
                seg = acum[:, c:c + 1] - acum_t[c:c + 1, :]
                w = cb * jnp.exp(jnp.where(causal, seg, -jnp.inf)) * dt_t[c:c + 1, :]
                halves.append(jnp.dot(w.astype(bf), x_pair.astype(bf), preferred_element_type=jnp.float32))
            y_pair = jnp.where(lane_hi, halves[1], halves[0])
            state = st_ref[k]
            y_pair = y_pair + lax.dot_general(cg, state.astype(bf), NT, preferred_element_type=jnp.float32) * pair_cols(ea, e)
            y_ref[:, k * 128:(k + 1) * 128] = y_pair
            xd_t = (x_pair * pair_cols(decay, e)).T.astype(bf)
            c0 = SMALL_DT + e
            keep = jnp.where(row_hi, ea_last[:, c0 + 1:c0 + 2], ea_last[:, c0:c0 + 1])
            st_ref[k] = state * keep + jnp.dot(xd_t, bg, preferred_element_type=jnp.float32)

    y = (y_ref[...] + dsk_ref[...] * xs) * (z * jax.nn.sigmoid(z))
    half = SSM_INNER // SSM_GROUPS
    for g in range(SSM_GROUPS):
        yg = y[:, g * half:(g + 1) * half]
        yg = yg * lax.rsqrt(jnp.mean(yg * yg, axis=-1, keepdims=True) + EPS)
        y_ref[:, g * half:(g + 1) * half] = yg * nw_ref[:, g * half:(g + 1) * half]
    return cur


def _mamba2_ssd(proj, bsz, seq, conv_w, conv_b, dt_bias, a_log, d_skip, norm_w):
    q = SSD_STEP
    nc = seq // q
    lane_row = lambda v: jnp.zeros((1, 128), jnp.float32).at[0, SMALL_DT:SMALL_DT + SSM_HEADS].set(v)
    const = lambda shape: pl.BlockSpec(shape, lambda b, c: (0,) * len(shape))
    return pl.pallas_call(
        _ssd_kernel,
        grid=(bsz, nc),
        in_specs=[pl.BlockSpec((q, CONV_CH), lambda b, c: (b * nc + c, COL_XBC // CONV_CH)),
                  pl.BlockSpec((q, SSM_INNER), lambda b, c: (b * nc + c, COL_Z // SSM_INNER)),
                  pl.BlockSpec((q, 128), lambda b, c: (b * nc + c, COL_SMALL // 128)),
                  const((CONV_WIDTH, CONV_CH)), const((1, CONV_CH)), const((1, 128)), const((1, 128)),
                  const((1, SSM_INNER)), const((1, SSM_INNER))],
        out_specs=pl.BlockSpec((q, SSM_INNER), lambda b, c: (b * nc + c, 0)),
        out_shape=jax.ShapeDtypeStruct((bsz * seq, SSM_INNER), jnp.float32),
        scratch_shapes=[pltpu.VMEM((SSD_CHUNK, CONV_CH), jnp.float32),
                        pltpu.VMEM((SSM_HEADS // 2, 2 * SSM_HEAD_DIM, SSM_STATE), jnp.float32)],
        compiler_params=pltpu.CompilerParams(dimension_semantics=("parallel", "arbitrary"),
                                             vmem_limit_bytes=VMEM_LIMIT),
        name="mamba2_ssd",
    )(proj, proj, proj, conv_w, conv_b.reshape(1, CONV_CH), lane_row(dt_bias), lane_row(-jnp.exp(a_log)),
      jnp.repeat(d_skip, SSM_HEAD_DIM).reshape(1, SSM_INNER), norm_w.reshape(1, SSM_INNER))


def _mix_out_kernel(a_ref, s_ref, gl_ref, x_ref, gm_ref, wo_ref, ws_ref, wout_ref,
                    nf_ref, scf_ref, shf_ref, wr_ref, br_ref, xo_ref, h_ref, rt_ref, gt_ref, cnt_ref):
    bf = jnp.bfloat16
    ya = jnp.dot(a_ref[...].astype(bf), wo_ref[...], preferred_element_type=jnp.float32)
    ys = jnp.dot(s_ref[...].astype(bf), ws_ref[...], preferred_element_type=jnp.float32)
    mixed = jax.nn.sigmoid(gl_ref[:, :D_MODEL]) * ya + jax.nn.sigmoid(gl_ref[:, D_MODEL:]) * ys
    x = x_ref[...] + gm_ref[0] * jnp.dot(mixed.astype(bf), wout_ref[...], preferred_element_type=jnp.float32)
    xo_ref[...] = x
    y = x * lax.rsqrt(jnp.mean(x * x, axis=-1, keepdims=True) + EPS) * nf_ref[...]
    h = y * (1.0 + scf_ref[0]) + shf_ref[0]
    h_hi = h.astype(bf)
    hb = pltpu.bitcast(h_hi.astype(jnp.float32), jnp.int32)
    half = D_MODEL // 2
    h_ref[...] = (hb[:, :half] & jnp.int32(-65536)) | lax.shift_right_logical(hb[:, half:], 16)
    h_lo = (h - h_hi.astype(jnp.float32)).astype(bf)
    dot = functools.partial(jnp.dot, preferred_element_type=jnp.float32)
    lg = dot(h_hi, wr_ref[0]) + (dot(h_lo, wr_ref[0]) + dot(h_hi, wr_ref[1])) + br_ref[...]

    tm = lg.shape[0]
    work = lg.T[:N_EXPERTS, :]
    row = lax.broadcasted_iota(jnp.int32, work.shape, 0).astype(jnp.float32)
    vals, eids, hits = [], [], []
    for _ in range(TOP_K):
        mx = jnp.max(work, axis=0, keepdims=True)
        ix = jnp.min(jnp.where(work == mx, row, float(N_EXPERTS)), axis=0, keepdims=True)
        vals.append(mx)
        eids.append(ix)
        hits.append(row == ix)
        work = jnp.where(hits[-1], -jnp.inf, work)
    ex = [jnp.exp(v - vals[0]) for v in vals]
    den = (ex[0] + ex[1]) + (ex[2] + ex[3])

    @pl.when(pl.program_id(0) == 0)
    def _():
        cnt_ref[...] = jnp.zeros(cnt_ref.shape, jnp.float32)
    chosen = jnp.zeros(work.shape, jnp.float32)
    for hit in hits:
        chosen = jnp.where(hit, 1.0, chosen)
    earlier = lax.broadcasted_iota(jnp.int32, (tm, tm), 0) < lax.broadcasted_iota(jnp.int32, (tm, tm), 1)
    before = dot(chosen.astype(bf), earlier.astype(bf)) + cnt_ref[:, 0:1]
    cnt_ref[...] = cnt_ref[...] + jnp.sum(chosen, axis=1, keepdims=True)

    slot = lax.broadcasted_iota(jnp.int32, (128, tm), 0)
    route = jnp.zeros((128, tm), jnp.float32)
    for k, hit in enumerate(hits):
        rank = jnp.sum(jnp.where(hit, before, 0.0), axis=0, keepdims=True)
        route = jnp.where(slot == k, eids[k], route)
        route = jnp.where(slot == TOP_K + k, ex[k] / den, route)
        route = jnp.where(slot == 2 * TOP_K + k, rank, route)
    rt_ref[...] = route[:ROUTE_ROWS, :]
    gt_ref[...] = route.T


def _mix_out(attn2, ssd2, proj, x2, g_m, w_attn_o, w_ssm_o, w_out, norm_ffn, sc_f, sh_f, w_router, b_router, seq, tm=512):
    t, d = x2.shape
    per_b = seq // tm
    bf = jnp.bfloat16
    const = lambda shape: pl.BlockSpec(shape, lambda i: (0,) * len(shape))
    perb = pl.BlockSpec((1, 1, d), lambda i: (i // per_b, 0, 0))
    wr = jnp.pad(w_router.astype(jnp.float32), ((0, 0), (0, 128 - N_EXPERTS)))
    wr_hi = wr.astype(bf)
    wr = jnp.stack([wr_hi, (wr - wr_hi.astype(jnp.float32)).astype(bf)])
    br =jnp.pad(b_router, (0, 128 - N_EXPERTS)).reshape(1, 128)
    return pl.pallas_call(
        _mix_out_kernel,
        grid=(t // tm,),
        in_specs=[pl.BlockSpec((tm, ATTN_WIDTH), lambda i: (i, 0)),
                  pl.BlockSpec((tm, SSM_INNER), lambda i: (i, 0)),
                  pl.BlockSpec((tm, 2 * d), lambda i: (i, COL_GATE // (2 * d))),
                  pl.BlockSpec((tm, d), lambda i: (i, 0)),
                  perb,
                  const((ATTN_WIDTH, d)), const((SSM_INNER, d)), const((d, d)),
                  const((1, d)), perb, perb, const((2, d, 128)), const((1, 128))],
        out_specs=[pl.BlockSpec((tm, d), lambda i: (i, 0)),
                   pl.BlockSpec((tm, d // 2), lambda i: (i, 0)),
                   pl.BlockSpec((ROUTE_ROWS, tm), lambda i: (0, i)),
                   pl.BlockSpec((tm, 128), lambda i: (i, 0)),
                   pl.BlockSpec((N_EXPERTS, 128), lambda i: (0, 0))],
        out_shape=[jax.ShapeDtypeStruct((t, d), jnp.float32),
                   jax.ShapeDtypeStruct((t, d // 2), jnp.int32),
                   jax.ShapeDtypeStruct((ROUTE_ROWS, t), jnp.float32),
                   jax.ShapeDtypeStruct((t, 128), jnp.float32),
                   jax.ShapeDtypeStruct((N_EXPERTS, 128), jnp.float32)],
        compiler_params=pltpu.CompilerParams(dimension_semantics=("arbitrary",), vmem_limit_bytes=VMEM_LIMIT),
        name="mix_out",
    )(attn2, ssd2, proj, x2, g_m[:, None, :], w_attn_o.astype(bf), w_ssm_o.astype(bf), w_out.astype(bf),
      norm_ffn.reshape(1, d), sc_f[:, None, :], sh_f[:, None, :], wr, br)


def _moe_kernel(be_ref, nb_ref, x_ref, wgu_ref, bgu_ref, wdn_ref, bdn_ref, *rest, first_block):
    o_ref, wgu_bf, wdn_bf = rest[-3:]
    i = pl.program_id(0)
    j = i + first_block

    @pl.when((i == 0) | (be_ref[j] != be_ref[jnp.maximum(j - 1, 0)]))
    def _():
        wgu_bf[...] = wgu_ref[0, 0].astype(jnp.bfloat16)
        wdn_bf[...] = wdn_ref[0, 0].astype(jnp.bfloat16)

    @pl.when(j < nb_ref[0])
    def _():
        words = x_ref[...]
        x_hi = pltpu.bitcast(words & jnp.int32(-65536), jnp.float32).astype(jnp.bfloat16)
        x_lo = pltpu.bitcast(words << 16, jnp.float32).astype(jnp.bfloat16)
        x = jnp.concatenate([x_hi, x_lo], axis=1)
        gu = jnp.dot(x, wgu_bf[...], preferred_element_type=jnp.float32) + bgu_ref[0, 0]
        g = jnp.minimum(gu[:, :D_EXPERT], SWIGLU_LIMIT)
        u = jnp.clip(gu[:, D_EXPERT:], -SWIGLU_LIMIT, SWIGLU_LIMIT)
        act = (u + 1.0) * (g * jax.nn.sigmoid(SWIGLU_ALPHA * g))
        out = jnp.dot(act.astype(jnp.bfloat16), wdn_bf[...], preferred_element_type=jnp.float32) + bdn_ref[0, 0]
        o_ref[...] = out.astype(o_ref.dtype)

    @pl.when(j >= nb_ref[0])
    def _():
        o_ref[...] = jnp.zeros_like(o_ref)


def _moe_ffn(xs_parts, blk_exp, n_used, w_gu, b_gu, w_dn, b_dn, layer):
    d = D_MODEL
    tm = MOE_TM
    n_rows = sum(xs.shape[0] for xs in xs_parts)
    out, first = None, 0
    for xs in xs_parts:
        nblk = xs.shape[0] // tm
        wmap = lambda i, be, nb, first=first: (layer, be[i + first], 0, 0)
        in_specs = [pl.BlockSpec((tm, d // 2), lambda i, be, nb: (i, 0)),
                    pl.BlockSpec((1, 1, d, 2 * D_EXPERT), wmap),
                    pl.BlockSpec((1, 1, 1, 2 * D_EXPERT), wmap),
                    pl.BlockSpec((1, 1, D_EXPERT, d), wmap),
                    pl.BlockSpec((1, 1, 1, d), wmap)]
        args = [blk_exp, n_used, xs, w_gu, b_gu[:, :, None, :], w_dn, b_dn[:, :, None, :]]
        aliases = {}
        if out is not None:
            in_specs.append(pl.BlockSpec(memory_space=pl.ANY))
            args.append(out)
            aliases = {len(args) - 1: 0}
        grid_spec = pltpu.PrefetchScalarGridSpec(
            num_scalar_prefetch=2,
            grid=(nblk,),
            in_specs=in_specs,
            out_specs=pl.BlockSpec((tm, d), lambda i, be, nb, first=first: (i + first, 0)),
            scratch_shapes=[pltpu.VMEM((d, 2 * D_EXPERT), jnp.bfloat16), pltpu.VMEM((D_EXPERT, d), jnp.bfloat16)],
        )
        out = pl.pallas_call(
            functools.partial(_moe_kernel, first_block=first),
            grid_spec=grid_spec,
            out_shape=jax.ShapeDtypeStruct((n_rows, d), jnp.bfloat16),
            input_output_aliases=aliases,
            compiler_params=pltpu.CompilerParams(dimension_semantics=("arbitrary",), vmem_limit_bytes=VMEM_LIMIT),
            name="moe_ffn",
        )(*args)
        first += nblk
    return out


def _moe(h2, route, expert_counts, w_gu, b_gu, w_dn, b_dn, layer):
    t = h2.shape[0]
    d = D_MODEL
    tm = MOE_TM
    i32 = jnp.int32
    experts = jnp.arange(N_EXPERTS, dtype=i32)
    top_idx = route[:TOP_K].astype(i32)
    rank = route[2 * TOP_K:3 * TOP_K].astype(i32)
    n_assign = t * TOP_K
    n_rows = n_assign + N_EXPERTS * tm
    e_flat = top_idx.T.reshape(n_assign)
    counts = expert_counts[:, 0].astype(i32)
    padded = (counts + tm - 1) // tm * tm
    pad_start = jnp.cumsum(padded) - padded
    dest = rank + jnp.sum(jnp.where(top_idx[..., None] == experts, pad_start, 0), axis=-1)
    dest = dest.reshape(-1)
    filler_exp = jnp.repeat(experts, tm)
    filler_key = jnp.where(jnp.tile(jnp.arange(tm, dtype=i32), N_EXPERTS) < jnp.repeat(padded - counts, tm),
                           filler_exp, N_EXPERTS)
    rows = jnp.arange(n_rows, dtype=i32)
    assert n_rows < (1 << ROW_BITS) and (N_EXPERTS + 1) << ROW_BITS < (1 << 31)
    packed = lax.sort((jnp.concatenate([e_flat, filler_key]) << ROW_BITS) | rows)
    row_key, row_src = packed >> ROW_BITS, packed & ((1 << ROW_BITS) - 1)
    row_tok = jnp.where(row_src < n_assign, row_src // TOP_K, rows % t)
    blk_exp = jnp.minimum(row_key[::tm], N_EXPERTS - 1)
    n_used = (jnp.sum(padded, keepdims=True) // tm).astype(i32)
    slab = n_rows // MOE_SLABS
    xs_parts = [h2[row_tok[s * slab:(s + 1) * slab]] for s in range(MOE_SLABS)]
    out = _moe_ffn(xs_parts, blk_exp, n_used, w_gu, b_gu, w_dn, b_dn, layer)
    return out[dest].reshape(TOP_K, t, d)


def _combine_kernel(p_ref, r_ref, x_ref, g_ref, o_ref):
    f32 = jnp.float32
    w = [r_ref[:, TOP_K + k:TOP_K + k + 1] for k in range(TOP_K)]
    y = (w[0] * p_ref[0].astype(f32) + w[1] * p_ref[1].astype(f32)) + (w[2] * p_ref[2].astype(f32) + w[3] * p_ref[3].astype(f32))
    o_ref[...] = x_ref[...] + g_ref[0] * y


def _combine(parts, route_tok, x2, g_f, seq, tm=512):
    t, d = x2.shape
    per_b = seq // tm
    return pl.pallas_call(
        _combine_kernel,
        grid=(t // tm,),
        in_specs=[pl.BlockSpec((TOP_K, tm, d), lambda i: (0, i, 0)),
                  pl.BlockSpec((tm, 128), lambda i: (i, 0)),
                  pl.BlockSpec((tm, d), lambda i: (i, 0)),
                  pl.BlockSpec((1, 1, d), lambda i: (i // per_b, 0, 0))],
        out_specs=pl.BlockSpec((tm, d), lambda i: (i, 0)),
        out_shape=jax.ShapeDtypeStruct((t, d), jnp.float32),
        compiler_params=pltpu.CompilerParams(dimension_semantics=("parallel",), vmem_limit_bytes=VMEM_LIMIT),
        name="moe_combine",
    )(parts, route_tok, x2, g_f[:, None, :])


def kernel(x, c, rel_bias, w_ada, b_ada, norm_mix, norm_ffn, w_in, kv_norm, w_kv_up, q_norm, k_norm,
           idx_k_ln_w, idx_k_ln_b, w_attn_o, conv_w, conv_b, dt_bias, a_log, d_skip, ssm_norm, w_ssm_o,
           w_out, w_router, b_router, w_gu, b_gu, w_dn, b_dn):
    bsz, seq, d = x.shape
    t = bsz * seq
    cond = jax.nn.silu(c)
    x2 = x.reshape(t, d)
    for l in range(DEPTH):
        mod = cond @ w_ada[l] + b_ada[l]
        sh_m, sc_m, g_m, sh_f, sc_f, g_f = jnp.split(mod, 6, axis=-1)
        proj = _in_proj(x2, norm_mix[l], sc_m, sh_m, _pack_w_in(w_in[l]), seq)
        qT, k, vT, qiT, ki2, wT, kn2 = _prep(proj, bsz, seq, q_norm[l], kv_norm[l], w_kv_up[l], k_norm[l],
                                             idx_k_ln_w[l], idx_k_ln_b[l])
        attn = _dsa_attention(qT, qiT, wT, k, vT, ki2, kn2, rel_bias)
        y_ssd = _mamba2_ssd(proj, bsz, seq, conv_w[l], conv_b[l], dt_bias[l], a_log[l], d_skip[l], ssm_norm[l])
        x2, h2, route, route_tok, expert_counts = _mix_out(
            attn.reshape(t, ATTN_WIDTH), y_ssd, proj, x2, g_m, w_attn_o[l], w_ssm_o[l], w_out[l], norm_ffn[l],
            sc_f, sh_f, w_router[l], b_router[l], seq)
        parts = _moe(h2, route, expert_counts, w_gu, b_gu, w_dn, b_dn, l)
        x2 = _combine(parts, route_tok, x2, g_f, seq)
    return x2.reshape(bsz, seq, d)
```

```python
import functools
import math

import jax
import jax.numpy as jnp
import numpy as np
from jax import lax
from jax.experimental import pallas as pl
from jax.experimental.pallas import tpu as pltpu

D_MODEL = 1024
DEPTH = 2
ATTN_HEADS = 8
ATTN_HEAD_DIM = 64
ATTN_WIDTH = ATTN_HEADS * ATTN_HEAD_DIM
KV_RANK = 256
IDX_HEADS = 8
IDX_DIM = 64
TOPK_MAX = 256
N_BUCKETS = 32
MAX_DISTANCE = 128
SSM_HEADS = 16
SSM_HEAD_DIM = 64
SSM_INNER = SSM_HEADS * SSM_HEAD_DIM
SSM_GROUPS = 2
SSM_STATE = 128
CONV_WIDTH = 4
CONV_CH = SSM_INNER + 2 * SSM_GROUPS * SSM_STATE
SSD_CHUNK = 128
SSD_STEP = 256
N_EXPERTS = 32
TOP_K = 4
D_EXPERT = D_MODEL
SWIGLU_LIMIT = 7.0
SWIGLU_ALPHA = 1.702
EPS = 1e-6

COL_Q = 0
COL_KV = 512
COL_QI = 768
COL_SMALL = 1280
COL_XBC = 1536
COL_Z = 3072
COL_GATE = 4096
PROJ_COLS = 6144
PREP_COLS = 1408
SMALL_KI, SMALL_WI, SMALL_DT = 0, 64, 72

QB = 256
VROWS = 80
INT_MIN = -2 ** 31
KEY_NEG_INF = (0xFF800000 ^ 0x7FFFFFFF) - 2 ** 32
NEG = -1e30
TINY = 2.0 ** -126
LOG2E = math.log2(math.e)
NORM_SLACK = 1.02
MAX_SHIFT_ERROR = 96.0
VMEM_LIMIT = 56 * 1024 * 1024
MOE_TM = 512
MOE_SLABS = 4
ROW_BITS = 18
ROUTE_ROWS = 16
HIGHEST = lax.Precision.HIGHEST
NT = (((1,), (1,)), ((), ()))


def _pack_w_in(w):
    o = np.cumsum((0, ATTN_WIDTH, KV_RANK, IDX_HEADS * IDX_DIM, IDX_DIM, IDX_HEADS, SSM_INNER, CONV_CH, SSM_HEADS, 2 * D_MODEL))
    q, kv, qi, ki, wi, z, xbc, dt, gate = (w[:, int(o[n]):int(o[n + 1])] for n in range(9))
    zeros = lambda n: jnp.zeros((w.shape[0], n), w.dtype)
    small = jnp.concatenate([ki, wi, dt, zeros(128 - 88)], axis=1)
    packed = jnp.concatenate([q, kv, qi, small, zeros(COL_XBC - PREP_COLS), xbc, z, gate], axis=1)
    assert packed.shape[1] == PROJ_COLS
    return packed.astype(jnp.bfloat16)


def _in_proj_kernel(x_ref, g_ref, sc_ref, sh_ref, w_ref, o_ref, h_ref):
    @pl.when(pl.program_id(1) == 0)
    def _():
        x = x_ref[...]
        y = x * lax.rsqrt(jnp.mean(x * x, axis=-1, keepdims=True) + EPS) * g_ref[...]
        h_ref[...] = (y * (1.0 + sc_ref[0]) + sh_ref[0]).astype(jnp.bfloat16)
    o_ref[...] = jnp.dot(h_ref[...], w_ref[...], preferred_element_type=jnp.float32)


def _in_proj(x2, gain, sc, sh, w_packed, seq, tm=1024, tn=2048):
    t, d = x2.shape
    per_b = seq // tm
    return pl.pallas_call(
        _in_proj_kernel,
        grid=(t // tm, PROJ_COLS // tn),
        in_specs=[pl.BlockSpec((tm, d), lambda i, j: (i, 0)),
                  pl.BlockSpec((1, d), lambda i, j: (0, 0)),
                  pl.BlockSpec((1, 1, d), lambda i, j: (i // per_b, 0, 0)),
                  pl.BlockSpec((1, 1, d), lambda i, j: (i // per_b, 0, 0)),
                  pl.BlockSpec((d, tn), lambda i, j: (0, j))],
        out_specs=pl.BlockSpec((tm, tn), lambda i, j: (i, j)),
        out_shape=jax.ShapeDtypeStruct((t, PROJ_COLS), jnp.float32),
        scratch_shapes=[pltpu.VMEM((tm, d), jnp.bfloat16)],
        compiler_params=pltpu.CompilerParams(dimension_semantics=("parallel", "arbitrary"),
                                             vmem_limit_bytes=VMEM_LIMIT),
        name="in_proj",
    )(x2, gain.reshape(1, d), sc[:, None, :], sh[:, None, :], w_packed)


def _head_rms_t(xt):
    x3 = xt.reshape(ATTN_HEADS, ATTN_HEAD_DIM, xt.shape[1])
    return lax.rsqrt(jnp.mean(x3 * x3, axis=1, keepdims=True) + EPS)


def _prep_kernel(p_ref, qg_ref, kvg_ref, wkv_ref, kg_ref, lng_ref, lnb_ref,
                 qT_ref, k_ref, vT_ref, qiT_ref, ki_ref, wT_ref, kn2_ref):
    n = p_ref.shape[0]
    q = p_ref[:, COL_Q:COL_Q + ATTN_WIDTH]
    lat = p_ref[:, COL_KV:COL_KV + KV_RANK]
    qi = p_ref[:, COL_QI:COL_QI + IDX_HEADS * IDX_DIM]
    sm = p_ref[:, COL_SMALL:COL_SMALL + 128]

    scale = ATTN_HEAD_DIM ** -0.5 * LOG2E
    qt = q.T
    qn = qt.reshape(ATTN_HEADS, ATTN_HEAD_DIM, n) * _head_rms_t(qt)
    qT_ref[0] = (qn.reshape(ATTN_WIDTH, n) * qg_ref[...] * scale).astype(jnp.bfloat16)

    latn = lat * lax.rsqrt(jnp.mean(lat * lat, axis=-1, keepdims=True) + EPS) * kvg_ref[...]
    kv = jnp.dot(latn.astype(jnp.bfloat16), wkv_ref[...], preferred_element_type=jnp.float32)
    kt = kv[:, :ATTN_WIDTH].T
    kn = (kt.reshape(ATTN_HEADS, ATTN_HEAD_DIM, n) * _head_rms_t(kt)).reshape(ATTN_WIDTH, n) * kg_ref[...]
    k_ref[0] = kn.T.astype(jnp.bfloat16)
    kn3 = kn.reshape(ATTN_HEADS, ATTN_HEAD_DIM, n)
    kn2_ref[0] = jnp.sum(kn3 * kn3, axis=1)
    vt = kv[:, ATTN_WIDTH:].T.reshape(ATTN_HEADS, ATTN_HEAD_DIM, n)
    ones = jnp.ones((ATTN_HEADS, VROWS - ATTN_HEAD_DIM, n), jnp.float32)
    vT_ref[0] = jnp.concatenate([vt, ones], axis=1).reshape(ATTN_HEADS * VROWS, n).astype(jnp.bfloat16)

    qiT_ref[0] = (qi * (IDX_DIM ** -0.5)).T.astype(jnp.bfloat16)

    lane = lax.broadcasted_iota(jnp.int32, sm.shape, 1)
    kid = jnp.where(lane < IDX_DIM, sm, pltpu.roll(sm, IDX_DIM, 1))
    mu = jnp.mean(kid, axis=-1, keepdims=True)
    var = jnp.mean(jnp.square(kid - mu), axis=-1, keepdims=True)
    ki_ref[0] = ((kid - mu) * lax.rsqrt(var + EPS) * lng_ref[...] + lnb_ref[...]).astype(jnp.bfloat16)

    wT_ref[0] = sm.T[SMALL_WI:SMALL_WI + IDX_HEADS, :] * (IDX_HEADS ** -0.5)


def _prep(proj, bsz, seq, q_norm, kv_norm, w_kv_up, k_norm, ln_w, ln_b, tp=512):
    nb = seq // tp
    tile8 = lambda g: jnp.tile(g, ATTN_HEADS).reshape(ATTN_WIDTH, 1)
    const = lambda shape: pl.BlockSpec(shape, lambda b, i: (0,) * len(shape))
    bf = jnp.bfloat16
    return pl.pallas_call(
        _prep_kernel,
        grid=(bsz, nb),
        in_specs=[pl.BlockSpec((tp, PREP_COLS), lambda b, i: (b * nb + i, 0)),
                  const((ATTN_WIDTH, 1)), const((1, KV_RANK)), const((KV_RANK, 2 * ATTN_WIDTH)),
                  const((ATTN_WIDTH, 1)), const((1, 128)), const((1, 128))],
        out_specs=[pl.BlockSpec((1, ATTN_WIDTH, tp), lambda b, i: (b, 0, i)),
                   pl.BlockSpec((1, tp, ATTN_WIDTH), lambda b, i: (b, i, 0)),
                   pl.BlockSpec((1, ATTN_HEADS * VROWS, tp), lambda b, i: (b, 0, i)),
                   pl.BlockSpec((1, ATTN_WIDTH, tp), lambda b, i: (b, 0, i)),
                   pl.BlockSpec((1, tp, 128), lambda b, i: (b, i, 0)),
                   pl.BlockSpec((1, IDX_HEADS, tp), lambda b, i: (b, 0, i)),
                   pl.BlockSpec((1, ATTN_HEADS, tp), lambda b, i: (b, 0, i))],
        out_shape=[jax.ShapeDtypeStruct((bsz, ATTN_WIDTH, seq), bf),
                   jax.ShapeDtypeStruct((bsz, seq, ATTN_WIDTH), bf),
                   jax.ShapeDtypeStruct((bsz, ATTN_HEADS * VROWS, seq), bf),
                   jax.ShapeDtypeStruct((bsz, ATTN_WIDTH, seq), bf),
                   jax.ShapeDtypeStruct((bsz, seq, 128), bf),
                   jax.ShapeDtypeStruct((bsz, IDX_HEADS, seq), jnp.float32),
                   jax.ShapeDtypeStruct((bsz, ATTN_HEADS, seq), jnp.float32)],
        compiler_params=pltpu.CompilerParams(dimension_semantics=("parallel", "parallel"),
                                             vmem_limit_bytes=VMEM_LIMIT),
        name="attn_prep",
    )(proj, tile8(q_norm), kv_norm.reshape(1, KV_RANK), w_kv_up.astype(bf), tile8(k_norm),
      jnp.tile(ln_w, 2).reshape(1, 128), jnp.tile(ln_b, 2).reshape(1, 128))


def _t5_bucket(dist):
    n = jnp.maximum(dist, 0)
    max_exact = N_BUCKETS // 2
    nf = jnp.maximum(n, 1).astype(jnp.float32)
    large = max_exact + (jnp.log(nf / max_exact) / math.log(MAX_DISTANCE / max_exact) * (N_BUCKETS - max_exact)).astype(jnp.int32)
    large = jnp.minimum(large, N_BUCKETS - 1)
    return jnp.where(n < max_exact, n, large)


def _bias_tables(rel_bias):
    s = jnp.arange(QB, dtype=jnp.int32)[None, :, None]
    q = jnp.arange(QB, dtype=jnp.int32)[None, None, :]
    dist = q - s + jnp.array([2 * QB, QB, 0], jnp.int32)[:, None, None]
    onehot = (_t5_bucket(dist)[..., None] == jnp.arange(N_BUCKETS, dtype=jnp.int32)).astype(jnp.float32)
    b = jnp.einsum('tsqb,bh->thsq', onehot, rel_bias.astype(jnp.float32) * LOG2E, precision=HIGHEST)
    return jnp.where((dist >= 0)[:, None], b, NEG)


def _attn_kernel(qT_ref, qiT_ref, wT_ref, k_ref, vT_ref, ki_ref, kn_ref, tab_ref, bst_ref, o_ref,
                 keys_ref, hi_ref, lo_ref, msk_ref, p_ref, acc_ref, mp_ref, m_ref, *, topk):
    i = pl.program_id(1)
    n_tiles = i + 1
    row_hi = lax.broadcasted_iota(jnp.int32, (128, QB), 0) >= 64

    def head_rows(ref, h):
        pair = ref[0, (h // 2) * 128:(h // 2) * 128 + 128, :]
        return jnp.where(row_hi == bool(h % 2), pair, jnp.zeros_like(pair))

    def tile_rows(kt):
        return pl.ds(pl.multiple_of(kt * QB, QB), QB)

    def score_tile(kt, carry, diagonal=False):
        ki = ki_ref[0, tile_rows(kt), :]
        sc = jnp.zeros((QB, QB), jnp.float32)
        for h in range(IDX_HEADS):
            d = jnp.dot(ki, head_rows(qiT_ref, h), preferred_element_type=jnp.float32)
            sc = sc + wT_ref[0, h:h + 1, :] * jnp.maximum(d, 0.0)
        sc = jnp.where(jnp.abs(sc) < TINY, 0.0, sc)
        if diagonal:
            srow = lax.broadcasted_iota(jnp.int32, (QB, QB), 0)
            qcol = lax.broadcasted_iota(jnp.int32, (QB, QB), 1)
            sc = jnp.where(srow > qcol, -jnp.inf, sc)
        bits = pltpu.bitcast(sc, jnp.int32)
        keys_ref[tile_rows(kt), :] = bits ^ ((bits >> 31) & 0x7FFFFFFF)
        hi_ref[tile_rows(kt), :] = pltpu.bitcast(bits & jnp.int32(-65536), jnp.float32).astype(jnp.bfloat16)
        return carry
    lax.fori_loop(0, i, score_tile, 0)
    score_tile(i, 0, diagonal=True)

    def count_packed_ge(ref, cb):
        one, zero = jnp.ones((), jnp.bfloat16), jnp.zeros((), jnp.bfloat16)

        def body(kt, acc):
            hit = jnp.where(ref[tile_rows(kt), :] >= cb, one, zero)
            parts = [hit[r:r + 16, :] for r in range(0, QB, 16)]
            while len(parts) > 1:
                parts = [a + b for a, b in zip(parts[::2], parts[1::2])]
            return acc + parts[0]
        acc = lax.fori_loop(0, n_tiles, body, jnp.zeros((16, QB), jnp.bfloat16))
        return jnp.sum(acc.astype(jnp.float32), axis=0, keepdims=True)

    def count_hi_ge(cand16):
        b = cand16 ^ ((cand16 >> 15) & 0x7FFF)
        snap = jnp.where(((b & 0x8000) != 0) | ((b & 0x7F) == 0), 0, 0x0080)
        b = jnp.where((b & 0x7F80) == 0, snap, b)
        return count_packed_ge(hi_ref, pltpu.bitcast(b << 16, jnp.float32).astype(jnp.bfloat16))

    def mid_code(v):
        pat = jnp.where(v >= 16384, v - 16256, 0x8000 | (16511 - v))
        return pltpu.bitcast(pat << 16, jnp.float32)

    def count(hit_of_tile):
        def body(kt, acc):
            return acc + jnp.sum(hit_of_tile(kt).reshape(QB // 8, 8, QB), axis=0)
        acc = lax.fori_loop(0, n_tiles, body, jnp.zeros((8, QB), jnp.int32))
        return jnp.sum(acc, axis=0, keepdims=True)

    def count_ge(cand):
        return count(lambda kt: jnp.where(keys_ref[tile_rows(kt), :] >= cand, 1, 0))

    def hi_step(it, r):
        cand = jnp.where(it == 0, jnp.zeros_like(r), r | (1 << (15 - it)))
        return jnp.where(count_hi_ge(cand) >= topk, cand, r)
    r16 = lax.fori_loop(0, 16, hi_step, jnp.full((1, QB), -32768, jnp.int32))

    above = count_hi_ge(r16 + 1)

    def code_tile(kt, carry):
        key = keys_ref[tile_rows(kt), :]
        code = jnp.where((key >> 16) == r16, mid_code((key >> 1) & 0x7FFF), -jnp.inf)
        lo_ref[tile_rows(kt), :] = code.astype(jnp.bfloat16)
        return carry
    lax.fori_loop(0, n_tiles, code_tile, 0)

    def mid_step(it, v):
        cand = v | (1 << (14 - it))
        cnt = above + count_packed_ge(lo_ref, mid_code(cand).astype(jnp.bfloat16))
        return jnp.where(cnt >= topk, cand, v)
    v15 = lax.fori_loop(0, 15, mid_step, jnp.zeros((1, QB), jnp.int32))
    thr = (r16 << 16) | (v15 << 1)
    thr = jnp.where(count_ge(thr | 1) >= topk, thr | 1, thr)

    cnt_gt = count_ge(thr + 1)
    cnt_ge = count_ge(thr)
    need = topk - cnt_gt
    tie = (cnt_ge - cnt_gt > need) & (thr > KEY_NEG_INF)

    @pl.when(jnp.max(tie.astype(jnp.int32)) > 0)
    def _():
        def count_eq_below(cand):
            def ind(kt):
                idx = lax.broadcasted_iota(jnp.int32, (QB, QB), 0) + kt * QB
                return jnp.where((keys_ref[tile_rows(kt), :] == thr) & (idx < cand), 1, 0)
            return count(ind)

        def idx_step(it, r):
            cand = r | (1 << (15 - it))
            return jnp.where(count_eq_below(cand) < need, cand, r)
        last = lax.fori_loop(0, 16, idx_step, jnp.zeros((1, QB), jnp.int32))

        def drop(kt, carry):
            blk = keys_ref[tile_rows(kt), :]
            idx = lax.broadcasted_iota(jnp.int32, (QB, QB), 0) + kt * QB
            keys_ref[tile_rows(kt), :] = jnp.where(tie & (blk == thr) & (idx > last), INT_MIN, blk)
            return carry
        lax.fori_loop(0, n_tiles, drop, 0)

    def logits(kt, h):
        band = jnp.clip(kt - (i - 2), 0, 2)
        kp = k_ref[0, tile_rows(kt), (h // 2) * 128:(h // 2) * 128 + 128]
        s = jnp.dot(kp, head_rows(qT_ref, h), preferred_element_type=jnp.float32)
        return s + msk_ref[...] + tab_ref[band, h]

    def set_mask(kt):
        msk_ref[...] = jnp.where(keys_ref[tile_rows(kt), :] >= thr, 0.0, NEG)

    def max_tile(kt, carry):
        set_mask(kt)
        for h in range(ATTN_HEADS):
            s = logits(kt, h)
            mp_ref[h] = jnp.maximum(mp_ref[h], jnp.max(s.reshape(QB // 8, 8, QB), axis=0))
        return carry

    seq = kn_ref.shape[2]
    in_extent = lax.broadcasted_iota(jnp.int32, (ATTN_HEADS, seq), 1) < n_tiles * QB
    k_max = jnp.max(jnp.where(in_extent, kn_ref[0], 0.0), axis=1, keepdims=True)
    spread = jnp.zeros((1, QB), jnp.float32)
    for h in range(ATTN_HEADS):
        qh = qT_ref[0, h * ATTN_HEAD_DIM:(h + 1) * ATTN_HEAD_DIM, :].astype(jnp.float32)
        reach = jnp.sqrt(jnp.sum(qh * qh, axis=0, keepdims=True) * k_max[h:h + 1, :]) * NORM_SLACK
        m_ref[h:h + 1, :] = reach + bst_ref[0, h:h + 1, :]
        spread = jnp.maximum(spread, 2.0 * reach + bst_ref[1, h:h + 1, :])
    bound_ok = jnp.max(spread) <= MAX_SHIFT_ERROR

    @pl.when(jnp.logical_not(bound_ok))
    def _():
        mp_ref[...] = jnp.full(mp_ref.shape, NEG, jnp.float32)
        lax.fori_loop(0, n_tiles, max_tile, 0)
        for h in range(ATTN_HEADS):
            m_ref[h:h + 1, :] = jnp.max(mp_ref[h], axis=0, keepdims=True)
    m = [m_ref[h:h + 1, :] for h in range(ATTN_HEADS)]

    def exp_tile(kt, carry):
        set_mask(kt)
        for h in range(ATTN_HEADS):
            p_ref[h] = jnp.exp2(logits(kt, h) - m[h]).astype(jnp.bfloat16)
        for h in range(ATTN_HEADS):
            va = vT_ref[0, h * VROWS:(h + 1) * VROWS, tile_rows(kt)]
            acc_ref[h * VROWS:(h + 1) * VROWS, :] += jnp.dot(va, p_ref[h], preferred_element_type=jnp.float32)
        return carry

    acc_ref[...] = jnp.zeros(acc_ref.shape, jnp.float32)
    lax.fori_loop(0, n_tiles, exp_tile, 0)

    outs = [acc_ref[h * VROWS:h * VROWS + ATTN_HEAD_DIM, :] / acc_ref[h * VROWS + ATTN_HEAD_DIM:h * VROWS + ATTN_HEAD_DIM + 1, :]
            for h in range(ATTN_HEADS)]
    o_ref[0] = jnp.concatenate(outs, axis=0).T


def _dsa_attention(qT, qiT, wT, k, vT, ki2, kn2, rel_bias):
    bsz, _, seq = qT.shape
    topk = min(TOPK_MAX, seq // 4)
    assert seq % QB == 0 and topk <= QB
    assert seq // 16 <= 256
    b2 = rel_bias.astype(jnp.float32) * LOG2E
    bias_stats = jnp.stack([jnp.max(b2, axis=0), jnp.max(b2, axis=0) - jnp.min(b2, axis=0)])
    bias_stats = jnp.broadcast_to(bias_stats[:, :, None], (2, ATTN_HEADS, QB))
    return pl.pallas_call(
        functools.partial(_attn_kernel, topk=topk),
        grid=(bsz, seq // QB),
        in_specs=[
            pl.BlockSpec((1, ATTN_WIDTH, QB), lambda b, i: (b, 0, i)),
            pl.BlockSpec((1, IDX_HEADS * IDX_DIM, QB), lambda b, i: (b, 0, i)),
            pl.BlockSpec((1, IDX_HEADS, QB), lambda b, i: (b, 0, i)),
            pl.BlockSpec((1, seq, ATTN_WIDTH), lambda b, i: (b, 0, 0)),
            pl.BlockSpec((1, ATTN_HEADS * VROWS, seq), lambda b, i: (b, 0, 0)),
            pl.BlockSpec((1, seq, 128), lambda b, i: (b, 0, 0)),
            pl.BlockSpec((1, ATTN_HEADS, seq), lambda b, i: (b, 0, 0)),
            pl.BlockSpec((3, ATTN_HEADS, QB, QB), lambda b, i: (0, 0, 0, 0)),
            pl.BlockSpec((2, ATTN_HEADS, QB), lambda b, i: (0, 0, 0)),
        ],
        out_specs=pl.BlockSpec((1, QB, ATTN_WIDTH), lambda b, i: (b, i, 0)),
        out_shape=jax.ShapeDtypeStruct((bsz, seq, ATTN_WIDTH), jnp.float32),
        scratch_shapes=[
            pltpu.VMEM((seq, QB), jnp.int32),
            pltpu.VMEM((seq, QB), jnp.bfloat16),
            pltpu.VMEM((seq, QB), jnp.bfloat16),
            pltpu.VMEM((QB, QB), jnp.float32),
            pltpu.VMEM((ATTN_HEADS, QB, QB), jnp.bfloat16),
            pltpu.VMEM((ATTN_HEADS * VROWS, QB), jnp.float32),
            pltpu.VMEM((ATTN_HEADS, 8, QB), jnp.float32),
            pltpu.VMEM((ATTN_HEADS, QB), jnp.float32),
        ],
        compiler_params=pltpu.CompilerParams(dimension_semantics=("parallel", "arbitrary"),
                                             vmem_limit_bytes=VMEM_LIMIT),
        name="dsa_attention",
    )(qT, qiT, wT, k, vT, ki2, kn2, _bias_tables(rel_bias), bias_stats)


def _ssd_kernel(xbc_ref, z_ref, sm_ref, cw_ref, cb_ref, dtb_ref, a_ref, dsk_ref, nw_ref, y_ref, prev_ref, st_ref):
    q = SSD_CHUNK
    bf = jnp.bfloat16

    @pl.when(pl.program_id(1) == 0)
    def _():
        prev_ref[...] = jnp.zeros(prev_ref.shape, jnp.float32)
        st_ref[...] = jnp.zeros(st_ref.shape, jnp.float32)

    prev = prev_ref[...]
    for sub in range(xbc_ref.shape[0] // q):
        rows = slice(sub * q, (sub + 1) * q)
        prev = _ssd_chunk(xbc_ref[rows, :], prev, z_ref[rows, :], sm_ref[rows, :], cw_ref, cb_ref, dtb_ref, a_ref,
                          dsk_ref, nw_ref, y_ref.at[rows, :], st_ref)
    prev_ref[...] = prev


def _ssd_chunk(cur, prev, z, sm, cw_ref, cb_ref, dtb_ref, a_ref, dsk_ref, nw_ref, y_ref, st_ref):
    q = SSD_CHUNK
    bf = jnp.bfloat16
    row = lax.broadcasted_iota(jnp.int32, cur.shape, 0)
    acc = cur * cw_ref[CONV_WIDTH - 1:CONV_WIDTH, :] + cb_ref[...]
    for s in range(1, CONV_WIDTH):
        shifted = jnp.where(row >= s, pltpu.roll(cur, s, 0), pltpu.roll(prev, s, 0))
        acc = acc + shifted * cw_ref[CONV_WIDTH - 1 - s:CONV_WIDTH - s, :]
    u = acc * jax.nn.sigmoid(acc)
    xs = u[:, :SSM_INNER]
    bm = u[:, SSM_INNER:SSM_INNER + SSM_GROUPS * SSM_STATE].astype(bf)
    cm = u[:, SSM_INNER + SSM_GROUPS * SSM_STATE:].astype(bf)

    t = sm + dtb_ref[...]
    dt = jnp.maximum(t, 0.0) + jnp.log1p(jnp.exp(-jnp.abs(t)))
    ii = lax.broadcasted_iota(jnp.int32, (q, q), 0)
    jj = lax.broadcasted_iota(jnp.int32, (q, q), 1)
    causal = ii >= jj
    acum = jnp.dot(causal.astype(jnp.float32), dt * a_ref[...], preferred_element_type=jnp.float32, precision=HIGHEST)
    acum_t = acum.T
    dt_t = dt.T
    ea = jnp.exp(acum)
    last = acum[q - 1:q, :]
    decay = jnp.exp(last - acum) * dt
    ea_last = jnp.exp(last)

    lane_hi = lax.broadcasted_iota(jnp.int32, (q, 128), 1) >= SSM_HEAD_DIM
    row_hi = lax.broadcasted_iota(jnp.int32, (128, SSM_STATE), 0) >= SSM_HEAD_DIM

    def pair_cols(v, e):
        c0, c1 = SMALL_DT + e, SMALL_DT + e + 1
        return jnp.where(lane_hi, v[:, c1:c1 + 1], v[:, c0:c0 + 1])

    for g in range(SSM_GROUPS):
        bg = bm[:, g * SSM_STATE:(g + 1) * SSM_STATE]
        cg = cm[:, g * SSM_STATE:(g + 1) * SSM_STATE]
        cb = lax.dot_general(cg, bg, NT, preferred_element_type=jnp.float32)
        for k in range(g * 4, g * 4 + 4):
            e = 2 * k
            x_pair = xs[:, k * 128:(k + 1) * 128]
            halves = []
            for h in (e, e + 1):
                c = SMALL_DT + h
                seg = acum[:, c:c + 1] - acum_t[c:c + 1, :]
                w = cb * jnp.exp(jnp.where(causal, seg, -jnp.inf)) * dt_t[c:c + 1, :]
                halves.append(jnp.dot(w.astype(bf), x_pair.astype(bf), preferred_element_type=jnp.float32))
            y_pair = jnp.where(lane_hi, halves[1], halves[0])
            state = st_ref[k]
            y_pair = y_pair + lax.dot_general(cg, state.astype(bf), NT, preferred_element_type=jnp.float32) * pair_cols(ea, e)
            y_ref[:, k * 128:(k + 1) * 128] = y_pair
            xd_t = (x_pair * pair_cols(decay, e)).T.astype(bf)
            c0 = SMALL_DT + e
            keep = jnp.where(row_hi, ea_last[:, c0 + 1:c0 + 2], ea_last[:, c0:c0 + 1])
            st_ref[k] = state * keep + jnp.dot(xd_t, bg, preferred_element_type=jnp.float32)

    y = (y_ref[...] + dsk_ref[...] * xs) * (z * jax.nn.sigmoid(z))
    half = SSM_INNER // SSM_GROUPS
    for g in range(SSM_GROUPS):
        yg = y[:, g * half:(g + 1) * half]
        yg = yg * lax.rsqrt(jnp.mean(yg * yg, axis=-1, keepdims=True) + EPS)
        y_ref[:, g * half:(g + 1) * half] = yg * nw_ref[:, g * half:(g + 1) * half]
    return cur


def _mamba2_ssd(proj, bsz, seq, conv_w, conv_b, dt_bias, a_log, d_skip, norm_w):
    q = SSD_STEP
    nc = seq // q
    lane_row = lambda v: jnp.zeros((1, 128), jnp.float32).at[0, SMALL_DT:SMALL_DT + SSM_HEADS].set(v)
    const = lambda shape: pl.BlockSpec(shape, lambda b, c: (0,) * len(shape))
    return pl.pallas_call(
        _ssd_kernel,
        grid=(bsz, nc),
        in_specs=[pl.BlockSpec((q, CONV_CH), lambda b, c: (b * nc + c, COL_XBC // CONV_CH)),
                  pl.BlockSpec((q, SSM_INNER), lambda b, c: (b * nc + c, COL_Z // SSM_INNER)),
                  pl.BlockSpec((q, 128), lambda b, c: (b * nc + c, COL_SMALL // 128)),
                  const((CONV_WIDTH, CONV_CH)), const((1, CONV_CH)), const((1, 128)), const((1, 128)),
                  const((1, SSM_INNER)), const((1, SSM_INNER))],
        out_specs=pl.BlockSpec((q, SSM_INNER), lambda b, c: (b * nc + c, 0)),
        out_shape=jax.ShapeDtypeStruct((bsz * seq, SSM_INNER), jnp.float32),
        scratch_shapes=[pltpu.VMEM((SSD_CHUNK, CONV_CH), jnp.float32),
                        pltpu.VMEM((SSM_HEADS // 2, 2 * SSM_HEAD_DIM, SSM_STATE), jnp.float32)],
        compiler_params=pltpu.CompilerParams(dimension_semantics=("parallel", "arbitrary"),
                                             vmem_limit_bytes=VMEM_LIMIT),
        name="mamba2_ssd",
    )(proj, proj, proj, conv_w, conv_b.reshape(1, CONV_CH), lane_row(dt_bias), lane_row(-jnp.exp(a_log)),
      jnp.repeat(d_skip, SSM_HEAD_DIM).reshape(1, SSM_INNER), norm_w.reshape(1, SSM_INNER))


def _mix_out_kernel(a_ref, s_ref, gl_ref, x_ref, gm_ref, wo_ref, ws_ref, wout_ref,
                    nf_ref, scf_ref, shf_ref, wr_ref, br_ref, xo_ref, h_ref, rt_ref, gt_ref, cnt_ref):
    bf = jnp.bfloat16
    ya = jnp.dot(a_ref[...].astype(bf), wo_ref[...], preferred_element_type=jnp.float32)
    ys = jnp.dot(s_ref[...].astype(bf), ws_ref[...], preferred_element_type=jnp.float32)
    mixed = jax.nn.sigmoid(gl_ref[:, :D_MODEL]) * ya + jax.nn.sigmoid(gl_ref[:, D_MODEL:]) * ys
    x = x_ref[...] + gm_ref[0] * jnp.dot(mixed.astype(bf), wout_ref[...], preferred_element_type=jnp.float32)
    xo_ref[...] = x
    y = x * lax.rsqrt(jnp.mean(x * x, axis=-1, keepdims=True) + EPS) * nf_ref[...]
    h = y * (1.0 + scf_ref[0]) + shf_ref[0]
    h_hi = h.astype(bf)
    hb = pltpu.bitcast(h_hi.astype(jnp.float32), jnp.int32)
    half = D_MODEL // 2
    h_ref[...] = (hb[:, :half] & jnp.int32(-65536)) | lax.shift_right_logical(hb[:, half:], 16)
    h_lo = (h - h_hi.astype(jnp.float32)).astype(bf)
    dot = functools.partial(jnp.dot, preferred_element_type=jnp.float32)
    lg = dot(h_hi, wr_ref[0]) + (dot(h_lo, wr_ref[0]) + dot(h_hi, wr_ref[1])) + br_ref[...]

    tm = lg.shape[0]
    work = lg.T[:N_EXPERTS, :]
    row = lax.broadcasted_iota(jnp.int32, work.shape, 0).astype(jnp.float32)
    vals, eids, hits = [], [], []
    for _ in range(TOP_K):
        mx = jnp.max(work, axis=0, keepdims=True)
        ix = jnp.min(jnp.where(work == mx, row, float(N_EXPERTS)), axis=0, keepdims=True)
        vals.append(mx)
        eids.append(ix)
        hits.append(row == ix)
        work = jnp.where(hits[-1], -jnp.inf, work)
    ex = [jnp.exp(v - vals[0]) for v in vals]
    den = (ex[0] + ex[1]) + (ex[2] + ex[3])

    @pl.when(pl.program_id(0) == 0)
    def _():
        cnt_ref[...] = jnp.zeros(cnt_ref.shape, jnp.float32)
    chosen = jnp.zeros(work.shape, jnp.float32)
    for hit in hits:
        chosen = jnp.where(hit, 1.0, chosen)
    earlier = lax.broadcasted_iota(jnp.int32, (tm, tm), 0) < lax.broadcasted_iota(jnp.int32, (tm, tm), 1)
    before = dot(chosen.astype(bf), earlier.astype(bf)) + cnt_ref[:, 0:1]
    cnt_ref[...] = cnt_ref[...] + jnp.sum(chosen, axis=1, keepdims=True)

    slot = lax.broadcasted_iota(jnp.int32, (128, tm), 0)
    route = jnp.zeros((128, tm), jnp.float32)
    for k, hit in enumerate(hits):
        rank = jnp.sum(jnp.where(hit, before, 0.0), axis=0, keepdims=True)
        route = jnp.where(slot == k, eids[k], route)
        route = jnp.where(slot == TOP_K + k, ex[k] / den, route)
        route = jnp.where(slot == 2 * TOP_K + k, rank, route)
    rt_ref[...] = route[:ROUTE_ROWS, :]
    gt_ref[...] = route.T


def _mix_out(attn2, ssd2, proj, x2, g_m, w_attn_o, w_ssm_o, w_out, norm_ffn, sc_f, sh_f, w_router, b_router, seq, tm=512):
    t, d = x2.shape
    per_b = seq // tm
    bf = jnp.bfloat16
    const = lambda shape: pl.BlockSpec(shape, lambda i: (0,) * len(shape))
    perb = pl.BlockSpec((1, 1, d), lambda i: (i // per_b, 0, 0))
    wr = jnp.pad(w_router.astype(jnp.float32), ((0, 0), (0, 128 - N_EXPERTS)))
    wr_hi = wr.astype(bf)
    wr = jnp.stack([wr_hi, (wr - wr_hi.astype(jnp.float32)).astype(bf)])
    br = jnp.pad(b_router, (0, 128 - N_EXPERTS)).reshape(1, 128)
    return pl.pallas_call(
        _mix_out_kernel,
        grid=(t // tm,),
        in_specs=[pl.BlockSpec((tm, ATTN_WIDTH), lambda i: (i, 0)),
                  pl.BlockSpec((tm, SSM_INNER), lambda i: (i, 0)),
                  pl.BlockSpec((tm, 2 * d), lambda i: (i, COL_GATE // (2 * d))),
                  pl.BlockSpec((tm, d), lambda i: (i, 0)),
                  perb,
                  const((ATTN_WIDTH, d)), const((SSM_INNER, d)), const((d, d)),
                  const((1, d)), perb, perb, const((2, d, 128)), const((1, 128))],
        out_specs=[pl.BlockSpec((tm, d), lambda i: (i, 0)),
                   pl.BlockSpec((tm, d // 2), lambda i: (i, 0)),
                   pl.BlockSpec((ROUTE_ROWS, tm), lambda i: (0, i)),
                   pl.BlockSpec((tm, 128), lambda i: (i, 0)),
                   pl.BlockSpec((N_EXPERTS, 128), lambda i: (0, 0))],
        out_shape=[jax.ShapeDtypeStruct((t, d), jnp.float32),
                   jax.ShapeDtypeStruct((t, d // 2), jnp.int32),
                   jax.ShapeDtypeStruct((ROUTE_ROWS, t), jnp.float32),
                   jax.ShapeDtypeStruct((t, 128), jnp.float32),
                   jax.ShapeDtypeStruct((N_EXPERTS, 128), jnp.float32)],
        compiler_params=pltpu.CompilerParams(dimension_semantics=("arbitrary",), vmem_limit_bytes=VMEM_LIMIT),
        name="mix_out",
    )(attn2, ssd2, proj, x2, g_m[:, None, :], w_attn_o.astype(bf), w_ssm_o.astype(bf), w_out.astype(bf),
      norm_ffn.reshape(1, d), sc_f[:, None, :], sh_f[:, None, :], wr, br)


def _moe_kernel(be_ref, nb_ref, x_ref, wgu_ref, bgu_ref, wdn_ref, bdn_ref, *rest, first_block):
    o_ref, wgu_bf, wdn_bf = rest[-3:]
    i = pl.program_id(0)
    j = i + first_block

    @pl.when((i == 0) | (be_ref[j] != be_ref[jnp.maximum(j - 1, 0)]))
    def _():
        wgu_bf[...] = wgu_ref[0, 0].astype(jnp.bfloat16)
        wdn_bf[...] = wdn_ref[0, 0].astype(jnp.bfloat16)

    @pl.when(j < nb_ref[0])
    def _():
        words = x_ref[...]
        x_hi = pltpu.bitcast(words & jnp.int32(-65536), jnp.float32).astype(jnp.bfloat16)
        x_lo = pltpu.bitcast(words << 16, jnp.float32).astype(jnp.bfloat16)
        x = jnp.concatenate([x_hi, x_lo], axis=1)
        gu = jnp.dot(x, wgu_bf[...], preferred_element_type=jnp.float32) + bgu_ref[0, 0]
        g = jnp.minimum(gu[:, :D_EXPERT], SWIGLU_LIMIT)
        u = jnp.clip(gu[:, D_EXPERT:], -SWIGLU_LIMIT, SWIGLU_LIMIT)
        act = (u + 1.0) * (g * jax.nn.sigmoid(SWIGLU_ALPHA * g))
        out = jnp.dot(act.astype(jnp.bfloat16), wdn_bf[...], preferred_element_type=jnp.float32) + bdn_ref[0, 0]
        o_ref[...] = out.astype(o_ref.dtype)

    @pl.when(j >= nb_ref[0])
    def _():
        o_ref[...] = jnp.zeros_like(o_ref)


def _moe_ffn(xs_parts, blk_exp, n_used, w_gu, b_gu, w_dn, b_dn, layer):
    d = D_MODEL
    tm = MOE_TM
    n_rows = sum(xs.shape[0] for xs in xs_parts)
    out, first = None, 0
    for xs in xs_parts:
        nblk = xs.shape[0] // tm
        wmap = lambda i, be, nb, first=first: (layer, be[i + first], 0, 0)
        in_specs = [pl.BlockSpec((tm, d // 2), lambda i, be, nb: (i, 0)),
                    pl.BlockSpec((1, 1, d, 2 * D_EXPERT), wmap),
                    pl.BlockSpec((1, 1, 1, 2 * D_EXPERT), wmap),
                    pl.BlockSpec((1, 1, D_EXPERT, d), wmap),
                    pl.BlockSpec((1, 1, 1, d), wmap)]
        args = [blk_exp, n_used, xs, w_gu, b_gu[:, :, None, :], w_dn, b_dn[:, :, None, :]]
        aliases = {}
        if out is not None:
            in_specs.append(pl.BlockSpec(memory_space=pl.ANY))
            args.append(out)
            aliases = {len(args) - 1: 0}
        grid_spec = pltpu.PrefetchScalarGridSpec(
            num_scalar_prefetch=2,
            grid=(nblk,),
            in_specs=in_specs,
            out_specs=pl.BlockSpec((tm, d), lambda i, be, nb, first=first: (i + first, 0)),
            scratch_shapes=[pltpu.VMEM((d, 2 * D_EXPERT), jnp.bfloat16), pltpu.VMEM((D_EXPERT, d), jnp.bfloat16)],
        )
        out = pl.pallas_call(
            functools.partial(_moe_kernel, first_block=first),
            grid_spec=grid_spec,
            out_shape=jax.ShapeDtypeStruct((n_rows, d), jnp.bfloat16),
            input_output_aliases=aliases,
            compiler_params=pltpu.CompilerParams(dimension_semantics=("arbitrary",), vmem_limit_bytes=VMEM_LIMIT),
            name="moe_ffn",
        )(*args)
        first += nblk
    return out


def _moe(h2, route, expert_counts, w_gu, b_gu, w_dn, b_dn, layer):
    t = h2.shape[0]
    d = D_MODEL
    tm = MOE_TM
    i32 = jnp.int32
    experts = jnp.arange(N_EXPERTS, dtype=i32)
    top_idx = route[:TOP_K].astype(i32)
    rank = route[2 * TOP_K:3 * TOP_K].astype(i32)
    n_assign = t * TOP_K
    n_rows = n_assign + N_EXPERTS * tm
    e_flat = top_idx.T.reshape(n_assign)
    counts = expert_counts[:, 0].astype(i32)
    padded = (counts + tm - 1) // tm * tm
    pad_start = jnp.cumsum(padded) - padded
    dest = rank + jnp.sum(jnp.where(top_idx[..., None] == experts, pad_start, 0), axis=-1)
    dest = dest.reshape(-1)
    filler_exp = jnp.repeat(experts, tm)
    filler_key = jnp.where(jnp.tile(jnp.arange(tm, dtype=i32), N_EXPERTS) < jnp.repeat(padded - counts, tm),
                           filler_exp, N_EXPERTS)
    rows = jnp.arange(n_rows, dtype=i32)
    assert n_rows < (1 << ROW_BITS) and (N_EXPERTS + 1) << ROW_BITS < (1 << 31)
    packed = lax.sort((jnp.concatenate([e_flat, filler_key]) << ROW_BITS) | rows)
    row_key, row_src = packed >> ROW_BITS, packed & ((1 << ROW_BITS) - 1)
    row_tok = jnp.where(row_src < n_assign, row_src // TOP_K, rows % t)
    blk_exp = jnp.minimum(row_key[::tm], N_EXPERTS - 1)
    n_used = (jnp.sum(padded, keepdims=True) // tm).astype(i32)
    slab = n_rows // MOE_SLABS
    xs_parts = [h2[row_tok[s * slab:(s + 1) * slab]] for s in range(MOE_SLABS)]
    out = _moe_ffn(xs_parts, blk_exp, n_used, w_gu, b_gu, w_dn, b_dn, layer)
    return out[dest].reshape(TOP_K, t, d)


def _combine_kernel(p_ref, r_ref, x_ref, g_ref, o_ref):
    f32 = jnp.float32
    w = [r_ref[:, TOP_K + k:TOP_K + k + 1] for k in range(TOP_K)]
    y = (w[0] * p_ref[0].astype(f32) + w[1] * p_ref[1].astype(f32)) + (w[2] * p_ref[2].astype(f32) + w[3] * p_ref[3].astype(f32))
    o_ref[...] = x_ref[...] + g_ref[0] * y


def _combine(parts, route_tok, x2, g_f, seq, tm=512):
    t, d = x2.shape
    per_b = seq // tm
    return pl.pallas_call(
        _combine_kernel,
        grid=(t // tm,),
        in_specs=[pl.BlockSpec((TOP_K, tm, d), lambda i: (0, i, 0)),
                  pl.BlockSpec((tm, 128), lambda i: (i, 0)),
                  pl.BlockSpec((tm, d), lambda i: (i, 0)),
                  pl.BlockSpec((1, 1, d), lambda i: (i // per_b, 0, 0))],
        out_specs=pl.BlockSpec((tm, d), lambda i: (i, 0)),
        out_shape=jax.ShapeDtypeStruct((t, d), jnp.float32),
        compiler_params=pltpu.CompilerParams(dimension_semantics=("parallel",), vmem_limit_bytes=VMEM_LIMIT),
        name="moe_combine",
    )(parts, route_tok, x2, g_f[:, None, :])


def kernel(x, c, rel_bias, w_ada, b_ada, norm_mix, norm_ffn, w_in, kv_norm, w_kv_up, q_norm, k_norm,
           idx_k_ln_w, idx_k_ln_b, w_attn_o, conv_w, conv_b, dt_bias, a_log, d_skip, ssm_norm, w_ssm_o,
           w_out, w_router, b_router, w_gu, b_gu, w_dn, b_dn):
    bsz, seq, d = x.shape
    t = bsz * seq
    cond = jax.nn.silu(c)
    x2 = x.reshape(t, d)
    for l in range(DEPTH):
        mod = cond @ w_ada[l] + b_ada[l]
        sh_m, sc_m, g_m, sh_f, sc_f, g_f = jnp.split(mod, 6, axis=-1)
        proj = _in_proj(x2, norm_mix[l], sc_m, sh_m, _pack_w_in(w_in[l]), seq)
        qT, k, vT, qiT, ki2, wT, kn2 = _prep(proj, bsz, seq, q_norm[l], kv_norm[l], w_kv_up[l], k_norm[l],
                                             idx_k_ln_w[l], idx_k_ln_b[l])
        attn = _dsa_attention(qT, qiT, wT, k, vT, ki2, kn2, rel_bias)
        y_ssd = _mamba2_ssd(proj, bsz, seq, conv_w[l], conv_b[l], dt_bias[l], a_log[l], d_skip[l], ssm_norm[l])
        x2, h2, route, route_tok, expert_counts = _mix_out(
            attn.reshape(t, ATTN_WIDTH), y_ssd, proj, x2, g_m, w_attn_o[l], w_ssm_o[l], w_out[l], norm_ffn[l],
            sc_f, sh_f, w_router[l], b_router[l], seq)
        parts = _moe(h2, route, expert_counts, w_gu, b_gu, w_dn, b_dn, l)
        x2 = _combine(parts, route_tok, x2, g_f, seq)
    return x2.reshape(bsz, seq, d)
```

```python
import functools
import math

import jax
import jax.numpy as jnp
import numpy as np
from jax import lax
from jax.experimental import pallas as pl
from jax.experimental.pallas import tpu as pltpu

D_MODEL = 1024
DEPTH = 2
ATTN_HEADS = 8
ATTN_HEAD_DIM = 64
ATTN_WIDTH = ATTN_HEADS * ATTN_HEAD_DIM
KV_RANK = 256
IDX_HEADS = 8
IDX_DIM = 64
TOPK_MAX = 256
N_BUCKETS = 32
MAX_DISTANCE = 128
SSM_HEADS = 16
SSM_HEAD_DIM = 64
SSM_INNER = SSM_HEADS * SSM_HEAD_DIM
SSM_GROUPS = 2
SSM_STATE = 128
CONV_WIDTH = 4
CONV_CH = SSM_INNER + 2 * SSM_GROUPS * SSM_STATE
SSD_CHUNK = 128
SSD_STEP = 256
N_EXPERTS = 32
TOP_K = 4
D_EXPERT = D_MODEL
SWIGLU_LIMIT = 7.0
SWIGLU_ALPHA = 1.702
EPS = 1e-6

COL_Q = 0
COL_KV = 512
COL_QI = 768
COL_SMALL = 1280
COL_XBC = 1536
COL_Z = 3072
COL_GATE = 4096
PROJ_COLS = 6144
PREP_COLS = 1408
SMALL_KI, SMALL_WI, SMALL_DT = 0, 64, 72

QB = 256
VROWS = 80
INT_MIN = -2 ** 31
KEY_NEG_INF = (0xFF800000 ^ 0x7FFFFFFF) - 2 ** 32
NEG = -1e30
TINY = 2.0 ** -126
LOG2E = math.log2(math.e)
NORM_SLACK = 1.02
MAX_SHIFT_ERROR = 96.0
VMEM_LIMIT = 56 * 1024 * 1024
MOE_TM = 512
MOE_SLABS = 4
COMBINE_SPANS = 4
ROW_BITS = 18
ROUTE_ROWS = 16
HIGHEST = lax.Precision.HIGHEST
NT = (((1,), (1,)), ((), ()))


def _pack_w_in(w):
    o = np.cumsum((0, ATTN_WIDTH, KV_RANK, IDX_HEADS * IDX_DIM, IDX_DIM, IDX_HEADS, SSM_INNER, CONV_CH, SSM_HEADS, 2 * D_MODEL))
    q, kv, qi, ki, wi, z, xbc, dt, gate = (w[:, int(o[n]):int(o[n + 1])] for n in range(9))
    zeros = lambda n: jnp.zeros((w.shape[0], n), w.dtype)
    small = jnp.concatenate([ki, wi, dt, zeros(128 - 88)], axis=1)
    packed = jnp.concatenate([q, kv, qi, small, zeros(COL_XBC - PREP_COLS), xbc, z, gate], axis=1)
    assert packed.shape[1] == PROJ_COLS
    return packed.astype(jnp.bfloat16)


def _in_proj_kernel(x_ref, g_ref, sc_ref, sh_ref, w_ref, o_ref, h_ref):
    @pl.when(pl.program_id(1) == 0)
    def _():
        x = x_ref[...]
        y = x * lax.rsqrt(jnp.mean(x * x, axis=-1, keepdims=True) + EPS) * g_ref[...]
        h_ref[...] = (y * (1.0 + sc_ref[0]) + sh_ref[0]).astype(jnp.bfloat16)
    o_ref[...] = jnp.dot(h_ref[...], w_ref[...], preferred_element_type=jnp.float32)


def _in_proj(x2, gain, sc, sh, w_packed, seq, tm=1024, tn=2048):
    t, d = x2.shape
    per_b = seq // tm
    return pl.pallas_call(
        _in_proj_kernel,
        grid=(t // tm, PROJ_COLS // tn),
        in_specs=[pl.BlockSpec((tm, d), lambda i, j: (i, 0)),
                  pl.BlockSpec((1, d), lambda i, j: (0, 0)),
                  pl.BlockSpec((1, 1, d), lambda i, j: (i // per_b, 0, 0)),
                  pl.BlockSpec((1, 1, d), lambda i, j: (i // per_b, 0, 0)),
                  pl.BlockSpec((d, tn), lambda i, j: (0, j))],
        out_specs=pl.BlockSpec((tm, tn), lambda i, j: (i, j)),
        out_shape=jax.ShapeDtypeStruct((t, PROJ_COLS), jnp.float32),
        scratch_shapes=[pltpu.VMEM((tm, d), jnp.bfloat16)],
        compiler_params=pltpu.CompilerParams(dimension_semantics=("parallel", "arbitrary"),
                                             vmem_limit_bytes=VMEM_LIMIT),
        name="in_proj",
    )(x2, gain.reshape(1, d), sc[:, None, :], sh[:, None, :], w_packed)


def _head_rms_t(xt):
    x3 = xt.reshape(ATTN_HEADS, ATTN_HEAD_DIM, xt.shape[1])
    return lax.rsqrt(jnp.mean(x3 * x3, axis=1, keepdims=True) + EPS)


def _prep_kernel(p_ref, qg_ref, kvg_ref, wkv_ref, kg_ref, lng_ref, lnb_ref,
                 qT_ref, k_ref, vT_ref, qiT_ref, ki_ref, wT_ref, kn2_ref):
    n = p_ref.shape[0]
    q = p_ref[:, COL_Q:COL_Q + ATTN_WIDTH]
    lat = p_ref[:, COL_KV:COL_KV + KV_RANK]
    qi = p_ref[:, COL_QI:COL_QI + IDX_HEADS * IDX_DIM]
    sm = p_ref[:, COL_SMALL:COL_SMALL + 128]

    scale = ATTN_HEAD_DIM ** -0.5 * LOG2E
    qt = q.T
    qn = qt.reshape(ATTN_HEADS, ATTN_HEAD_DIM, n) * _head_rms_t(qt)
    qT_ref[0] = (qn.reshape(ATTN_WIDTH, n) * qg_ref[...] * scale).astype(jnp.bfloat16)

    latn = lat * lax.rsqrt(jnp.mean(lat * lat, axis=-1, keepdims=True) + EPS) * kvg_ref[...]
    kv = jnp.dot(latn.astype(jnp.bfloat16), wkv_ref[...], preferred_element_type=jnp.float32)
    kt = kv[:, :ATTN_WIDTH].T
    kn = (kt.reshape(ATTN_HEADS, ATTN_HEAD_DIM, n) * _head_rms_t(kt)).reshape(ATTN_WIDTH, n) * kg_ref[...]
    k_ref[0] = kn.T.astype(jnp.bfloat16)
    kn3 = kn.reshape(ATTN_HEADS, ATTN_HEAD_DIM, n)
    kn2_ref[0] = jnp.sum(kn3 * kn3, axis=1)
    vt = kv[:, ATTN_WIDTH:].T.reshape(ATTN_HEADS, ATTN_HEAD_DIM, n)
    ones = jnp.ones((ATTN_HEADS, VROWS - ATTN_HEAD_DIM, n), jnp.float32)
    vT_ref[0] = jnp.concatenate([vt, ones], axis=1).reshape(ATTN_HEADS * VROWS, n).astype(jnp.bfloat16)

    qiT_ref[0] = (qi * (IDX_DIM ** -0.5)).T.astype(jnp.bfloat16)

    lane = lax.broadcasted_iota(jnp.int32, sm.shape, 1)
    kid = jnp.where(lane < IDX_DIM, sm, pltpu.roll(sm, IDX_DIM, 1))
    mu = jnp.mean(kid, axis=-1, keepdims=True)
    var = jnp.mean(jnp.square(kid - mu), axis=-1, keepdims=True)
    ki_ref[0] = ((kid - mu) * lax.rsqrt(var + EPS) * lng_ref[...] + lnb_ref[...]).astype(jnp.bfloat16)

    wT_ref[0] = sm.T[SMALL_WI:SMALL_WI + IDX_HEADS, :] * (IDX_HEADS ** -0.5)


def _prep(proj, bsz, seq, q_norm, kv_norm, w_kv_up, k_norm, ln_w, ln_b, tp=512):
    nb = seq // tp
    tile8 = lambda g: jnp.tile(g, ATTN_HEADS).reshape(ATTN_WIDTH, 1)
    const = lambda shape: pl.BlockSpec(shape, lambda b, i: (0,) * len(shape))
    bf = jnp.bfloat16
    return pl.pallas_call(
        _prep_kernel,
        grid=(bsz, nb),
        in_specs=[pl.BlockSpec((tp, PREP_COLS), lambda b, i: (b * nb + i, 0)),
                  const((ATTN_WIDTH, 1)), const((1, KV_RANK)), const((KV_RANK, 2 * ATTN_WIDTH)),
                  const((ATTN_WIDTH, 1)), const((1, 128)), const((1, 128))],
        out_specs=[pl.BlockSpec((1, ATTN_WIDTH, tp), lambda b, i: (b, 0, i)),
                   pl.BlockSpec((1, tp, ATTN_WIDTH), lambda b, i: (b, i, 0)),
                   pl.BlockSpec((1, ATTN_HEADS * VROWS, tp), lambda b, i: (b, 0, i)),
                   pl.BlockSpec((1, ATTN_WIDTH, tp), lambda b, i: (b, 0, i)),
                   pl.BlockSpec((1, tp, 128), lambda b, i: (b, i, 0)),
                   pl.BlockSpec((1, IDX_HEADS, tp), lambda b, i: (b, 0, i)),
                   pl.BlockSpec((1, ATTN_HEADS, tp), lambda b, i: (b, 0, i))],
        out_shape=[jax.ShapeDtypeStruct((bsz, ATTN_WIDTH, seq), bf),
                   jax.ShapeDtypeStruct((bsz, seq, ATTN_WIDTH), bf),
                   jax.ShapeDtypeStruct((bsz, ATTN_HEADS * VROWS, seq), bf),
                   jax.ShapeDtypeStruct((bsz, ATTN_WIDTH, seq), bf),
                   jax.ShapeDtypeStruct((bsz, seq, 128), bf),
                   jax.ShapeDtypeStruct((bsz, IDX_HEADS, seq), jnp.float32),
                   jax.ShapeDtypeStruct((bsz, ATTN_HEADS, seq), jnp.float32)],
        compiler_params=pltpu.CompilerParams(dimension_semantics=("parallel", "parallel"),
                                             vmem_limit_bytes=VMEM_LIMIT),
        name="attn_prep",
    )(proj, tile8(q_norm), kv_norm.reshape(1, KV_RANK), w_kv_up.astype(bf), tile8(k_norm),
      jnp.tile(ln_w, 2).reshape(1, 128), jnp.tile(ln_b, 2).reshape(1, 128))


def _t5_bucket(dist):
    n = jnp.maximum(dist, 0)
    max_exact = N_BUCKETS // 2
    nf = jnp.maximum(n, 1).astype(jnp.float32)
    large = max_exact + (jnp.log(nf / max_exact) / math.log(MAX_DISTANCE / max_exact) * (N_BUCKETS - max_exact)).astype(jnp.int32)
    large = jnp.minimum(large, N_BUCKETS - 1)
    return jnp.where(n < max_exact, n, large)


def _bias_tables(rel_bias):
    s = jnp.arange(QB, dtype=jnp.int32)[None, :, None]
    q = jnp.arange(QB, dtype=jnp.int32)[None, None, :]
    dist = q - s + jnp.array([2 * QB, QB, 0], jnp.int32)[:, None, None]
    onehot = (_t5_bucket(dist)[..., None] == jnp.arange(N_BUCKETS, dtype=jnp.int32)).astype(jnp.float32)
    b = jnp.einsum('tsqb,bh->thsq', onehot, rel_bias.astype(jnp.float32) * LOG2E, precision=HIGHEST)
    return jnp.where((dist >= 0)[:, None], b, NEG)


def _attn_kernel(qT_ref, qiT_ref, wT_ref, k_ref, vT_ref, ki_ref, kn_ref, tab_ref, bst_ref, o_ref,
                 keys_ref, hi_ref, lo_ref, msk_ref, p_ref, acc_ref, mp_ref, m_ref, *, topk):
    i = pl.program_id(1)
    n_tiles = i + 1
    row_hi = lax.broadcasted_iota(jnp.int32, (128, QB), 0) >= 64

    def head_rows(ref, h):
        pair = ref[0, (h // 2) * 128:(h // 2) * 128 + 128, :]
        return jnp.where(row_hi == bool(h % 2), pair, jnp.zeros_like(pair))

    def tile_rows(kt):
        return pl.ds(pl.multiple_of(kt * QB, QB), QB)

    def score_tile(kt, carry):
        ki = ki_ref[0, tile_rows(kt), :]
        sc = jnp.zeros((QB, QB), jnp.float32)
        for h in range(IDX_HEADS):
            d = jnp.dot(ki, head_rows(qiT_ref, h), preferred_element_type=jnp.float32)
            sc = sc + wT_ref[0, h:h + 1, :] * jnp.maximum(d, 0.0)
        srow = lax.broadcasted_iota(jnp.int32, (QB, QB), 0)
        qcol = lax.broadcasted_iota(jnp.int32, (QB, QB), 1)
        sc = jnp.where(jnp.abs(sc) < TINY, 0.0, sc)
        sc = jnp.where((kt == i) & (srow > qcol), -jnp.inf, sc)
        bits = pltpu.bitcast(sc, jnp.int32)
        keys_ref[tile_rows(kt), :] = bits ^ ((bits >> 31) & 0x7FFFFFFF)
        hi_ref[tile_rows(kt), :] = pltpu.bitcast(bits & jnp.int32(-65536), jnp.float32).astype(jnp.bfloat16)
        return carry
    lax.fori_loop(0, n_tiles, score_tile, 0)

    def count_packed_ge(ref, cb):
        one, zero = jnp.ones((), jnp.bfloat16), jnp.zeros((), jnp.bfloat16)

        def body(kt, acc):
            hit = jnp.where(ref[tile_rows(kt), :] >= cb, one, zero)
            parts = [hit[r:r + 16, :] for r in range(0, QB, 16)]
            while len(parts) > 1:
                parts = [a + b for a, b in zip(parts[::2], parts[1::2])]
            return acc + parts[0]
        acc = lax.fori_loop(0, n_tiles, body, jnp.zeros((16, QB), jnp.bfloat16))
        return jnp.sum(acc.astype(jnp.float32), axis=0, keepdims=True)

    def count_hi_ge(cand16):
        b = cand16 ^ ((cand16 >> 15) & 0x7FFF)
        snap = jnp.where(((b & 0x8000) != 0) | ((b & 0x7F) == 0), 0, 0x0080)
        b = jnp.where((b & 0x7F80) == 0, snap, b)
        return count_packed_ge(hi_ref, pltpu.bitcast(b << 16, jnp.float32).astype(jnp.bfloat16))

    def mid_code(v):
        pat = jnp.where(v >= 16384, v - 16256, 0x8000 | (16511 - v))
        return pltpu.bitcast(pat << 16, jnp.float32)

    def count(hit_of_tile):
        def body(kt, acc):
            return acc + jnp.sum(hit_of_tile(kt).reshape(QB // 8, 8, QB), axis=0)
        acc = lax.fori_loop(0, n_tiles, body, jnp.zeros((8, QB), jnp.int32))
        return jnp.sum(acc, axis=0, keepdims=True)

    def count_ge(cand):
        return count(lambda kt: jnp.where(keys_ref[tile_rows(kt), :] >= cand, 1, 0))

    def hi_step(it, r):
        cand = jnp.where(it == 0, jnp.zeros_like(r), r | (1 << (15 - it)))
        return jnp.where(count_hi_ge(cand) >= topk, cand, r)
    r16 = lax.fori_loop(0, 16, hi_step, jnp.full((1, QB), -32768, jnp.int32))

    above = count_hi_ge(r16 + 1)

    def code_tile(kt, carry):
        key = keys_ref[tile_rows(kt), :]
        code = jnp.where((key >> 16) == r16, mid_code((key >> 1) & 0x7FFF), -jnp.inf)
        lo_ref[tile_rows(kt), :] = code.astype(jnp.bfloat16)
        return carry
    lax.fori_loop(0, n_tiles, code_tile, 0)

    def mid_step(it, v):
        cand = v | (1 << (14 - it))
        cnt = above + count_packed_ge(lo_ref, mid_code(cand).astype(jnp.bfloat16))
        return jnp.where(cnt >= topk, cand, v)
    v15 = lax.fori_loop(0, 15, mid_step, jnp.zeros((1, QB), jnp.int32))
    thr = (r16 << 16) | (v15 << 1)
    thr = jnp.where(count_ge(thr | 1) >= topk, thr | 1, thr)

    cnt_gt = count_ge(thr + 1)
    cnt_ge = count_ge(thr)
    need = topk - cnt_gt
    tie = (cnt_ge - cnt_gt > need) & (thr > KEY_NEG_INF)

    @pl.when(jnp.max(tie.astype(jnp.int32)) > 0)
    def _():
        def count_eq_below(cand):
            def ind(kt):
                idx = lax.broadcasted_iota(jnp.int32, (QB, QB), 0) + kt * QB
                return jnp.where((keys_ref[tile_rows(kt), :] == thr) & (idx < cand), 1, 0)
            return count(ind)

        def idx_step(it, r):
            cand = r | (1 << (15 - it))
            return jnp.where(count_eq_below(cand) < need, cand, r)
        last = lax.fori_loop(0, 16, idx_step, jnp.zeros((1, QB), jnp.int32))

        def drop(kt, carry):
            blk = keys_ref[tile_rows(kt), :]
            idx = lax.broadcasted_iota(jnp.int32, (QB, QB), 0) + kt * QB
            keys_ref[tile_rows(kt), :] = jnp.where(tie & (blk == thr) & (idx > last), INT_MIN, blk)
            return carry
        lax.fori_loop(0, n_tiles, drop, 0)

    def logits(kt, h):
        band = jnp.clip(kt - (i - 2), 0, 2)
        kp = k_ref[0, tile_rows(kt), (h // 2) * 128:(h // 2) * 128 + 128]
        s = jnp.dot(kp, head_rows(qT_ref, h), preferred_element_type=jnp.float32)
        return s + msk_ref[...] + tab_ref[band, h]

    def set_mask(kt):
        msk_ref[...] = jnp.where(keys_ref[tile_rows(kt), :] >= thr, 0.0, NEG)

    def max_tile(kt, carry):
        set_mask(kt)
        for h in range(ATTN_HEADS):
            s = logits(kt, h)
            mp_ref[h] = jnp.maximum(mp_ref[h], jnp.max(s.reshape(QB // 8, 8, QB), axis=0))
        return carry

    seq = kn_ref.shape[2]
    in_extent = lax.broadcasted_iota(jnp.int32, (ATTN_HEADS, seq), 1) < n_tiles * QB
    k_max = jnp.max(jnp.where(in_extent, kn_ref[0], 0.0), axis=1, keepdims=True)
    spread = jnp.zeros((1, QB), jnp.float32)
    for h in range(ATTN_HEADS):
        qh = qT_ref[0, h * ATTN_HEAD_DIM:(h + 1) * ATTN_HEAD_DIM, :].astype(jnp.float32)
        reach = jnp.sqrt(jnp.sum(qh * qh, axis=0, keepdims=True) * k_max[h:h + 1, :]) * NORM_SLACK
        m_ref[h:h + 1, :] = reach + bst_ref[0, h:h + 1, :]
        spread = jnp.maximum(spread, 2.0 * reach + bst_ref[1, h:h + 1, :])
    bound_ok = jnp.max(spread) <= MAX_SHIFT_ERROR

    @pl.when(jnp.logical_not(bound_ok))
    def _():
        mp_ref[...] = jnp.full(mp_ref.shape, NEG, jnp.float32)
        lax.fori_loop(0, n_tiles, max_tile, 0)
        for h in range(ATTN_HEADS):
            m_ref[h:h + 1, :] = jnp.max(mp_ref[h], axis=0, keepdims=True)
    m = [m_ref[h:h + 1, :] for h in range(ATTN_HEADS)]

    def exp_tile(kt, carry):
        set_mask(kt)
        for h in range(ATTN_HEADS):
            p_ref[h] = jnp.exp2(logits(kt, h) - m[h]).astype(jnp.bfloat16)
        for h in range(ATTN_HEADS):
            va = vT_ref[0, h * VROWS:(h + 1) * VROWS, tile_rows(kt)]
            acc_ref[h * VROWS:(h + 1) * VROWS, :] += jnp.dot(va, p_ref[h], preferred_element_type=jnp.float32)
        return carry

    acc_ref[...] = jnp.zeros(acc_ref.shape, jnp.float32)
    lax.fori_loop(0, n_tiles, exp_tile, 0)

    outs = [acc_ref[h * VROWS:h * VROWS + ATTN_HEAD_DIM, :] / acc_ref[h * VROWS + ATTN_HEAD_DIM:h * VROWS + ATTN_HEAD_DIM + 1, :]
            for h in range(ATTN_HEADS)]
    o_ref[0] = jnp.concatenate(outs, axis=0).T


def _dsa_attention(qT, qiT, wT, k, vT, ki2, kn2, rel_bias):
    bsz, _, seq = qT.shape
    topk = min(TOPK_MAX, seq // 4)
    assert seq % QB == 0 and topk <= QB
    assert seq // 16 <= 256
    b2 = rel_bias.astype(jnp.float32) * LOG2E
    bias_stats = jnp.stack([jnp.max(b2, axis=0), jnp.max(b2, axis=0) - jnp.min(b2, axis=0)])
    bias_stats = jnp.broadcast_to(bias_stats[:, :, None], (2, ATTN_HEADS, QB))
    return pl.pallas_call(
        functools.partial(_attn_kernel, topk=topk),
        grid=(bsz, seq // QB),
        in_specs=[
            pl.BlockSpec((1, ATTN_WIDTH, QB), lambda b, i: (b, 0, i)),
            pl.BlockSpec((1, IDX_HEADS * IDX_DIM, QB), lambda b, i: (b, 0, i)),
            pl.BlockSpec((1, IDX_HEADS, QB), lambda b, i: (b, 0, i)),
            pl.BlockSpec((1, seq, ATTN_WIDTH), lambda b, i: (b, 0, 0)),
            pl.BlockSpec((1, ATTN_HEADS * VROWS, seq), lambda b, i: (b, 0, 0)),
            pl.BlockSpec((1, seq, 128), lambda b, i: (b, 0, 0)),
            pl.BlockSpec((1, ATTN_HEADS, seq), lambda b, i: (b, 0, 0)),
            pl.BlockSpec((3, ATTN_HEADS, QB, QB), lambda b, i: (0, 0, 0, 0)),
            pl.BlockSpec((2, ATTN_HEADS, QB), lambda b, i: (0, 0, 0)),
        ],
        out_specs=pl.BlockSpec((1, QB, ATTN_WIDTH), lambda b, i: (b, i, 0)),
        out_shape=jax.ShapeDtypeStruct((bsz, seq, ATTN_WIDTH), jnp.float32),
        scratch_shapes=[
            pltpu.VMEM((seq, QB), jnp.int32),
            pltpu.VMEM((seq, QB), jnp.bfloat16),
            pltpu.VMEM((seq, QB), jnp.bfloat16),
            pltpu.VMEM((QB, QB), jnp.float32),
            pltpu.VMEM((ATTN_HEADS, QB, QB), jnp.bfloat16),
            pltpu.VMEM((ATTN_HEADS * VROWS, QB), jnp.float32),
            pltpu.VMEM((ATTN_HEADS, 8, QB), jnp.float32),
            pltpu.VMEM((ATTN_HEADS, QB), jnp.float32),
        ],
        compiler_params=pltpu.CompilerParams(dimension_semantics=("parallel", "arbitrary"),
                                             vmem_limit_bytes=VMEM_LIMIT),
        name="dsa_attention",
    )(qT, qiT, wT, k, vT, ki2, kn2, _bias_tables(rel_bias), bias_stats)


def _ssd_kernel(xbc_ref, z_ref, sm_ref, cw_ref, cb_ref, dtb_ref, a_ref, dsk_ref, nw_ref, y_ref, prev_ref, st_ref):
    q = SSD_CHUNK
    bf = jnp.bfloat16

    @pl.when(pl.program_id(1) == 0)
    def _():
        prev_ref[...] = jnp.zeros(prev_ref.shape, jnp.float32)
        st_ref[...] = jnp.zeros(st_ref.shape, jnp.float32)

    prev = prev_ref[...]
    for sub in range(xbc_ref.shape[0] // q):
        rows = slice(sub * q, (sub + 1) * q)
        prev = _ssd_chunk(xbc_ref[rows, :], prev, z_ref[rows, :], sm_ref[rows, :], cw_ref, cb_ref, dtb_ref, a_ref,
                          dsk_ref, nw_ref, y_ref.at[rows, :], st_ref)
    prev_ref[...] = prev


def _ssd_chunk(cur, prev, z, sm, cw_ref, cb_ref, dtb_ref, a_ref, dsk_ref, nw_ref, y_ref, st_ref):
    q = SSD_CHUNK
    bf = jnp.bfloat16
    row = lax.broadcasted_iota(jnp.int32, cur.shape, 0)
    acc = cur * cw_ref[CONV_WIDTH - 1:CONV_WIDTH, :] + cb_ref[...]
    for s in range(1, CONV_WIDTH):
        shifted = jnp.where(row >= s, pltpu.roll(cur, s, 0), pltpu.roll(prev, s, 0))
        acc = acc + shifted * cw_ref[CONV_WIDTH - 1 - s:CONV_WIDTH - s, :]
    u = acc * jax.nn.sigmoid(acc)
    xs = u[:, :SSM_INNER]
    bm = u[:, SSM_INNER:SSM_INNER + SSM_GROUPS * SSM_STATE].astype(bf)
    cm = u[:, SSM_INNER + SSM_GROUPS * SSM_STATE:].astype(bf)

    t = sm + dtb_ref[...]
    dt = jnp.maximum(t, 0.0) + jnp.log1p(jnp.exp(-jnp.abs(t)))
    ii = lax.broadcasted_iota(jnp.int32, (q, q), 0)
    jj = lax.broadcasted_iota(jnp.int32, (q, q), 1)
    causal = ii >= jj
    acum = jnp.dot(causal.astype(jnp.float32), dt * a_ref[...], preferred_element_type=jnp.float32, precision=HIGHEST)
    acum_t = acum.T
    dt_t = dt.T
    ea = jnp.exp(acum)
    last = acum[q - 1:q, :]
    decay = jnp.exp(last - acum) * dt
    ea_last = jnp.exp(last)

    lane_hi = lax.broadcasted_iota(jnp.int32, (q, 128), 1) >= SSM_HEAD_DIM
    row_hi = lax.broadcasted_iota(jnp.int32, (128, SSM_STATE), 0) >= SSM_HEAD_DIM

    def pair_cols(v, e):
        c0, c1 = SMALL_DT + e, SMALL_DT + e + 1
        return jnp.where(lane_hi, v[:, c1:c1 + 1], v[:, c0:c0 + 1])

    for g in range(SSM_GROUPS):
        bg = bm[:, g * SSM_STATE:(g + 1) * SSM_STATE]
        cg = cm[:, g * SSM_STATE:(g + 1) * SSM_STATE]
        cb = lax.dot_general(cg, bg, NT, preferred_element_type=jnp.float32)
        for k in range(g * 4, g * 4 + 4):
            e = 2 * k
            x_pair = xs[:, k * 128:(k + 1) * 128]
            halves = []
            for h in (e, e + 1):
                c = SMALL_DT + h
                seg = acum[:, c:c + 1] - acum_t[c:c + 1, :]
                w = cb * jnp.exp(jnp.where(causal, seg, -jnp.inf)) * dt_t[c:c + 1, :]
                halves.append(jnp.dot(w.astype(bf), x_pair.astype(bf), preferred_element_type=jnp.float32))
            y_pair = jnp.where(lane_hi, halves[1], halves[0])
            state = st_ref[k]
            y_pair = y_pair + lax.dot_general(cg, state.astype(bf), NT, preferred_element_type=jnp.float32) * pair_cols(ea, e)
            y_ref[:, k * 128:(k + 1) * 128] = y_pair
            xd_t = (x_pair * pair_cols(decay, e)).T.astype(bf)
            c0 = SMALL_DT + e
            keep = jnp.where(row_hi, ea_last[:, c0 + 1:c0 + 2], ea_last[:, c0:c0 + 1])
            st_ref[k] = state * keep + jnp.dot(xd_t, bg, preferred_element_type=jnp.float32)

    y = (y_ref[...] + dsk_ref[...] * xs) * (z * jax.nn.sigmoid(z))
    half = SSM_INNER // SSM_GROUPS
    for g in range(SSM_GROUPS):
        yg = y[:, g * half:(g + 1) * half]
        yg = yg * lax.rsqrt(jnp.mean(yg * yg, axis=-1, keepdims=True) + EPS)
        y_ref[:, g * half:(g + 1) * half] = yg * nw_ref[:, g * half:(g + 1) * half]
    return cur


def _mamba2_ssd(proj, bsz, seq, conv_w, conv_b, dt_bias, a_log, d_skip, norm_w):
    q = SSD_STEP
    nc = seq // q
    lane_row = lambda v: jnp.zeros((1, 128), jnp.float32).at[0, SMALL_DT:SMALL_DT + SSM_HEADS].set(v)
    const = lambda shape: pl.BlockSpec(shape, lambda b, c: (0,) * len(shape))
    return pl.pallas_call(
        _ssd_kernel,
        grid=(bsz, nc),
        in_specs=[pl.BlockSpec((q, CONV_CH), lambda b, c: (b * nc + c, COL_XBC // CONV_CH)),
                  pl.BlockSpec((q, SSM_INNER), lambda b, c: (b * nc + c, COL_Z // SSM_INNER)),
                  pl.BlockSpec((q, 128), lambda b, c: (b * nc + c, COL_SMALL // 128)),
                  const((CONV_WIDTH, CONV_CH)), const((1, CONV_CH)), const((1, 128)), const((1, 128)),
                  const((1, SSM_INNER)), const((1, SSM_INNER))],
        out_specs=pl.BlockSpec((q, SSM_INNER), lambda b, c: (b * nc + c, 0)),
        out_shape=jax.ShapeDtypeStruct((bsz * seq, SSM_INNER), jnp.float32),
        scratch_shapes=[pltpu.VMEM((SSD_CHUNK, CONV_CH), jnp.float32),
                        pltpu.VMEM((SSM_HEADS // 2, 2 * SSM_HEAD_DIM, SSM_STATE), jnp.float32)],
        compiler_params=pltpu.CompilerParams(dimension_semantics=("parallel", "arbitrary"),
                                             vmem_limit_bytes=VMEM_LIMIT),
        name="mamba2_ssd",
    )(proj, proj, proj, conv_w, conv_b.reshape(1, CONV_CH), lane_row(dt_bias), lane_row(-jnp.exp(a_log)),
      jnp.repeat(d_skip, SSM_HEAD_DIM).reshape(1, SSM_INNER), norm_w.reshape(1, SSM_INNER))


def _mix_out_kernel(a_ref, s_ref, gl_ref, x_ref, gm_ref, wo_ref, ws_ref, wout_ref,
                    nf_ref, scf_ref, shf_ref, wr_ref, br_ref, xo_ref, h_ref, rt_ref, gt_ref, cnt_ref):
    bf = jnp.bfloat16
    ya = jnp.dot(a_ref[...].astype(bf), wo_ref[...], preferred_element_type=jnp.float32)
    ys = jnp.dot(s_ref[...].astype(bf), ws_ref[...], preferred_element_type=jnp.float32)
    mixed = jax.nn.sigmoid(gl_ref[:, :D_MODEL]) * ya + jax.nn.sigmoid(gl_ref[:, D_MODEL:]) * ys
    x = x_ref[...] + gm_ref[0] * jnp.dot(mixed.astype(bf), wout_ref[...], preferred_element_type=jnp.float32)
    xo_ref[...] = x
    y = x * lax.rsqrt(jnp.mean(x * x, axis=-1, keepdims=True) + EPS) * nf_ref[...]
    h = y * (1.0 + scf_ref[0]) + shf_ref[0]
    h_hi = h.astype(bf)
    hb = pltpu.bitcast(h_hi.astype(jnp.float32), jnp.int32)
    half = D_MODEL // 2
    h_ref[...] = (hb[:, :half] & jnp.int32(-65536)) | lax.shift_right_logical(hb[:, half:], 16)
    h_lo = (h - h_hi.astype(jnp.float32)).astype(bf)
    dot = functools.partial(jnp.dot, preferred_element_type=jnp.float32)
    lg = dot(h_hi, wr_ref[0]) + (dot(h_lo, wr_ref[0]) + dot(h_hi, wr_ref[1])) + br_ref[...]

    tm = lg.shape[0]
    work = lg.T[:N_EXPERTS, :]
    row = lax.broadcasted_iota(jnp.int32, work.shape, 0).astype(jnp.float32)
    vals, eids, hits = [], [], []
    for _ in range(TOP_K):
        mx = jnp.max(work, axis=0, keepdims=True)
        ix = jnp.min(jnp.where(work == mx, row, float(N_EXPERTS)), axis=0, keepdims=True)
        vals.append(mx)
        eids.append(ix)
        hits.append(row == ix)
        work = jnp.where(hits[-1], -jnp.inf, work)
    ex = [jnp.exp(v - vals[0]) for v in vals]
    den = (ex[0] + ex[1]) + (ex[2] + ex[3])

    @pl.when(pl.program_id(0) == 0)
    def _():
        cnt_ref[...] = jnp.zeros(cnt_ref.shape, jnp.float32)
    chosen = jnp.zeros(work.shape, jnp.float32)
    for hit in hits:
        chosen = jnp.where(hit, 1.0, chosen)
    earlier = lax.broadcasted_iota(jnp.int32, (tm, tm), 0) < lax.broadcasted_iota(jnp.int32, (tm, tm), 1)
    before = dot(chosen.astype(bf), earlier.astype(bf)) + cnt_ref[:, 0:1]
    cnt_ref[...] = cnt_ref[...] + jnp.sum(chosen, axis=1, keepdims=True)

    slot = lax.broadcasted_iota(jnp.int32, (128, tm), 0)
    route = jnp.zeros((128, tm), jnp.float32)
    for k, hit in enumerate(hits):
        rank = jnp.sum(jnp.where(hit, before, 0.0), axis=0, keepdims=True)
        route = jnp.where(slot == k, eids[k], route)
        route = jnp.where(slot == TOP_K + k, ex[k] / den, route)
        route = jnp.where(slot == 2 * TOP_K + k, rank, route)
    rt_ref[...] = route[:ROUTE_ROWS, :]
    gt_ref[...] = route.T


def _mix_out(attn2, ssd2, proj, x2, g_m, w_attn_o, w_ssm_o, w_out, norm_ffn, sc_f, sh_f, w_router, b_router, seq, tm=512):
    t, d = x2.shape
    per_b = seq // tm
    bf = jnp.bfloat16
    const = lambda shape: pl.BlockSpec(shape, lambda i: (0,) * len(shape))
    perb = pl.BlockSpec((1, 1, d), lambda i: (i // per_b, 0, 0))
    wr = jnp.pad(w_router.astype(jnp.float32), ((0, 0), (0, 128 - N_EXPERTS)))
    wr_hi = wr.astype(bf)
    wr = jnp.stack([wr_hi, (wr - wr_hi.astype(jnp.float32)).astype(bf)])
    br =jnp.pad(b_router, (0, 128 - N_EXPERTS)).reshape(1, 128)
    return pl.pallas_call(
        _mix_out_kernel,
        grid=(t // tm,),
        in_specs=[pl.BlockSpec((tm, ATTN_WIDTH), lambda i: (i, 0)),
                  pl.BlockSpec((tm, SSM_INNER), lambda i: (i, 0)),
                  pl.BlockSpec((tm, 2 * d), lambda i: (i, COL_GATE // (2 * d))),
                  pl.BlockSpec((tm, d), lambda i: (i, 0)),
                  perb,
                  const((ATTN_WIDTH, d)), const((SSM_INNER, d)), const((d, d)),
                  const((1, d)), perb, perb, const((2, d, 128)), const((1, 128))],
        out_specs=[pl.BlockSpec((tm, d), lambda i: (i, 0)),
                   pl.BlockSpec((tm, d // 2), lambda i: (i, 0)),
                   pl.BlockSpec((ROUTE_ROWS, tm), lambda i: (0, i)),
                   pl.BlockSpec((tm, 128), lambda i: (i, 0)),
                   pl.BlockSpec((N_EXPERTS, 128), lambda i: (0, 0))],
        out_shape=[jax.ShapeDtypeStruct((t, d), jnp.float32),
                   jax.ShapeDtypeStruct((t, d // 2), jnp.int32),
                   jax.ShapeDtypeStruct((ROUTE_ROWS, t), jnp.float32),
                   jax.ShapeDtypeStruct((t, 128), jnp.float32),
                   jax.ShapeDtypeStruct((N_EXPERTS, 128), jnp.float32)],
        compiler_params=pltpu.CompilerParams(dimension_semantics=("arbitrary",), vmem_limit_bytes=VMEM_LIMIT),
        name="mix_out",
    )(attn2, ssd2, proj, x2, g_m[:, None, :], w_attn_o.astype(bf), w_ssm_o.astype(bf), w_out.astype(bf),
      norm_ffn.reshape(1, d), sc_f[:, None, :], sh_f[:, None, :], wr, br)


def _moe_kernel(be_ref, nb_ref, x_ref, wgu_ref, bgu_ref, wdn_ref, bdn_ref, *rest, first_block):
    o_ref, wgu_bf, wdn_bf = rest[-3:]
    i = pl.program_id(0)
    j = i + first_block

    @pl.when((i == 0) | (be_ref[j] != be_ref[jnp.maximum(j - 1, 0)]))
    def _():
        wgu_bf[...] = wgu_ref[0, 0].astype(jnp.bfloat16)
        wdn_bf[...] = wdn_ref[0, 0].astype(jnp.bfloat16)

    @pl.when(j < nb_ref[0])
    def _():
        words = x_ref[...]
        x_hi = pltpu.bitcast(words & jnp.int32(-65536), jnp.float32).astype(jnp.bfloat16)
        x_lo = pltpu.bitcast(words << 16, jnp.float32).astype(jnp.bfloat16)
        x = jnp.concatenate([x_hi, x_lo], axis=1)
        gu = jnp.dot(x, wgu_bf[...], preferred_element_type=jnp.float32) + bgu_ref[0, 0]
        g = jnp.minimum(gu[:, :D_EXPERT], SWIGLU_LIMIT)
        u = jnp.clip(gu[:, D_EXPERT:], -SWIGLU_LIMIT, SWIGLU_LIMIT)
        act = (u + 1.0) * (g * jax.nn.sigmoid(SWIGLU_ALPHA * g))
        out = jnp.dot(act.astype(jnp.bfloat16), wdn_bf[...], preferred_element_type=jnp.float32) + bdn_ref[0, 0]
        o_ref[...] = out.astype(o_ref.dtype)

    @pl.when(j >= nb_ref[0])
    def _():
        o_ref[...] = jnp.zeros_like(o_ref)


def _moe_ffn(xs_parts, blk_exp, n_used, w_gu, b_gu, w_dn, b_dn, layer):
    d = D_MODEL
    tm = MOE_TM
    n_rows = sum(xs.shape[0] for xs in xs_parts)
    out, first = None, 0
    for xs in xs_parts:
        nblk = xs.shape[0] // tm
        wmap = lambda i, be, nb, first=first: (layer, be[i + first], 0, 0)
        in_specs = [pl.BlockSpec((tm, d // 2), lambda i, be, nb: (i, 0)),
                    pl.BlockSpec((1, 1, d, 2 * D_EXPERT), wmap),
                    pl.BlockSpec((1, 1, 1, 2 * D_EXPERT), wmap),
                    pl.BlockSpec((1, 1, D_EXPERT, d), wmap),
                    pl.BlockSpec((1, 1, 1, d), wmap)]
        args = [blk_exp, n_used, xs, w_gu, b_gu[:, :, None, :], w_dn, b_dn[:, :, None, :]]
        aliases = {}
        if out is not None:
            in_specs.append(pl.BlockSpec(memory_space=pl.ANY))
            args.append(out)
            aliases = {len(args) - 1: 0}
        grid_spec = pltpu.PrefetchScalarGridSpec(
            num_scalar_prefetch=2,
            grid=(nblk,),
            in_specs=in_specs,
            out_specs=pl.BlockSpec((tm, d), lambda i, be, nb, first=first: (i + first, 0)),
            scratch_shapes=[pltpu.VMEM((d, 2 * D_EXPERT), jnp.bfloat16), pltpu.VMEM((D_EXPERT, d), jnp.bfloat16)],
        )
        out = pl.pallas_call(
            functools.partial(_moe_kernel, first_block=first),
            grid_spec=grid_spec,
            out_shape=jax.ShapeDtypeStruct((n_rows, d), jnp.bfloat16),
            input_output_aliases=aliases,
            compiler_params=pltpu.CompilerParams(dimension_semantics=("arbitrary",), vmem_limit_bytes=VMEM_LIMIT),
            name="moe_ffn",
        )(*args)
        first += nblk
    return out


def _moe(h2, route, expert_counts, w_gu, b_gu, w_dn, b_dn, layer):
    t = h2.shape[0]
    d = D_MODEL
    tm = MOE_TM
    i32 = jnp.int32
    experts = jnp.arange(N_EXPERTS, dtype=i32)
    top_idx = route[:TOP_K].astype(i32)
    rank = route[2 * TOP_K:3 * TOP_K].astype(i32)
    n_assign = t * TOP_K
    n_rows = n_assign + N_EXPERTS * tm
    e_flat = top_idx.T.reshape(n_assign)
    counts = expert_counts[:, 0].astype(i32)
    padded = (counts + tm - 1) // tm * tm
    pad_start = jnp.cumsum(padded) - padded
    dest = rank + jnp.sum(jnp.where(top_idx[..., None] == experts, pad_start, 0), axis=-1)
    span = t // COMBINE_SPANS
    filler_exp = jnp.repeat(experts, tm)
    filler_key = jnp.where(jnp.tile(jnp.arange(tm, dtype=i32), N_EXPERTS) < jnp.repeat(padded - counts, tm),
                           filler_exp, N_EXPERTS)
    rows = jnp.arange(n_rows, dtype=i32)
    assert n_rows < (1 << ROW_BITS) and (N_EXPERTS + 1) << ROW_BITS < (1 << 31)
    packed = lax.sort((jnp.concatenate([e_flat, filler_key]) << ROW_BITS) | rows)
    row_key, row_src = packed >> ROW_BITS, packed & ((1 << ROW_BITS) - 1)
    row_tok = jnp.where(row_src < n_assign, row_src // TOP_K, rows % t)
    blk_exp = jnp.minimum(row_key[::tm], N_EXPERTS - 1)
    n_used = (jnp.sum(padded, keepdims=True) // tm).astype(i32)
    slab = n_rows // MOE_SLABS
    xs_parts = [h2[row_tok[s * slab:(s + 1) * slab]] for s in range(MOE_SLABS)]
    out = _moe_ffn(xs_parts, blk_exp, n_used, w_gu, b_gu, w_dn, b_dn, layer)
    return [out[dest[:, s * span:(s + 1) * span].reshape(-1)].reshape(TOP_K, span, d) for s in range(COMBINE_SPANS)]


def _combine_kernel(p_ref, r_ref, x_ref, g_ref, *rest):
    o_ref = rest[-1]
    f32 = jnp.float32
    w = [r_ref[:, TOP_K + k:TOP_K + k + 1] for k in range(TOP_K)]
    y = (w[0] * p_ref[0].astype(f32) + w[1] * p_ref[1].astype(f32)) + (w[2] * p_ref[2].astype(f32) + w[3] * p_ref[3].astype(f32))
    o_ref[...] = x_ref[...] + g_ref[0] * y


def _combine(span_parts, route_tok, x2, g_f, seq, tm=512):
    t, d = x2.shape
    per_b = seq // tm
    out, first = None, 0
    for parts in span_parts:
        ntile = parts.shape[1] // tm
        tile = lambda i, first=first: (i + first, 0)
        in_specs = [pl.BlockSpec((TOP_K, tm, d), lambda i: (0, i, 0)),
                    pl.BlockSpec((tm, 128), tile),
                    pl.BlockSpec((tm, d), tile),
                    pl.BlockSpec((1, 1, d), lambda i, first=first: ((i + first) // per_b, 0, 0))]
        args = [parts, route_tok, x2, g_f[:, None, :]]
        aliases = {}
        if out is not None:
            in_specs.append(pl.BlockSpec(memory_space=pl.ANY))
            args.append(out)
            aliases = {len(args) - 1: 0}
        out = pl.pallas_call(
            _combine_kernel,
            grid=(ntile,),
            in_specs=in_specs,
            out_specs=pl.BlockSpec((tm, d), tile),
            out_shape=jax.ShapeDtypeStruct((t, d), jnp.float32),
            input_output_aliases=aliases,
            compiler_params=pltpu.CompilerParams(dimension_semantics=("parallel",), vmem_limit_bytes=VMEM_LIMIT),
            name="moe_combine",
        )(*args)
        first += ntile
    return out


def kernel(x, c, rel_bias, w_ada, b_ada, norm_mix, norm_ffn, w_in, kv_norm, w_kv_up, q_norm, k_norm,
           idx_k_ln_w, idx_k_ln_b, w_attn_o, conv_w, conv_b, dt_bias, a_log, d_skip, ssm_norm, w_ssm_o,
           w_out, w_router, b_router, w_gu, b_gu, w_dn, b_dn):
    bsz, seq, d = x.shape
    t = bsz * seq
    cond = jax.nn.silu(c)
    x2 = x.reshape(t, d)
    for l in range(DEPTH):
        mod = cond @ w_ada[l] + b_ada[l]
        sh_m, sc_m, g_m, sh_f, sc_f, g_f = jnp.split(mod, 6, axis=-1)
        proj = _in_proj(x2, norm_mix[l], sc_m, sh_m, _pack_w_in(w_in[l]), seq)
        qT, k, vT, qiT, ki2, wT, kn2 = _prep(proj, bsz, seq, q_norm[l], kv_norm[l], w_kv_up[l], k_norm[l],
                                             idx_k_ln_w[l], idx_k_ln_b[l])
        attn = _dsa_attention(qT, qiT, wT, k, vT, ki2, kn2, rel_bias)
        y_ssd = _mamba2_ssd(proj, bsz, seq, conv_w[l], conv_b[l], dt_bias[l], a_log[l], d_skip[l], ssm_norm[l])
        x2, h2, route, route_tok, expert_counts = _mix_out(
            attn.reshape(t, ATTN_WIDTH), y_ssd, proj, x2, g_m, w_attn_o[l], w_ssm_o[l], w_out[l], norm_ffn[l],
            sc_f, sh_f, w_router[l], b_router[l], seq)
        parts = _moe(h2, route, expert_counts, w_gu, b_gu, w_dn, b_dn, l)
        x2 = _combine(parts, route_tok, x2, g_f, seq)
    return x2.reshape(bsz, seq, d)
```

```python
import functools
import math

import jax
import jax.numpy as jnp
import numpy as np
from jax import lax
from jax.experimental import pallas as pl
from jax.experimental.pallas import tpu as pltpu

D_MODEL = 1024
DEPTH = 2
ATTN_HEADS = 8
ATTN_HEAD_DIM = 64
ATTN_WIDTH = ATTN_HEADS * ATTN_HEAD_DIM
KV_RANK = 256
IDX_HEADS = 8
IDX_DIM = 64
TOPK_MAX = 256
N_BUCKETS = 32
MAX_DISTANCE = 128
SSM_HEADS = 16
SSM_HEAD_DIM = 64
SSM_INNER = SSM_HEADS * SSM_HEAD_DIM
SSM_GROUPS = 2
SSM_STATE = 128
CONV_WIDTH = 4
CONV_CH = SSM_INNER + 2 * SSM_GROUPS * SSM_STATE
SSD_CHUNK = 128
SSD_STEP = 256
N_EXPERTS = 32
TOP_K = 4
D_EXPERT = D_MODEL
SWIGLU_LIMIT = 7.0
SWIGLU_ALPHA = 1.702
EPS = 1e-6

COL_Q = 0
COL_KV = 512
COL_QI = 768
COL_SMALL = 1280
COL_XBC = 1536
COL_Z = 3072
COL_GATE = 4096
PROJ_COLS = 6144
PREP_COLS = 1408
SMALL_KI, SMALL_WI, SMALL_DT = 0, 64, 72

QB = 256
VROWS = 80
INT_MIN = -2 ** 31
KEY_NEG_INF = (0xFF800000 ^ 0x7FFFFFFF) - 2 ** 32
NEG = -1e30
TINY = 2.0 ** -126
LOG2E = math.log2(math.e)
NORM_SLACK = 1.02
MAX_SHIFT_ERROR = 96.0
VMEM_LIMIT = 56 * 1024 * 1024
MOE_TM = 512
MOE_SLABS = 4
ROW_BITS = 18
ROUTE_ROWS = 16
HIGHEST = lax.Precision.HIGHEST
NT = (((1,), (1,)), ((), ()))


def _pack_w_in(w):
    o = np.cumsum((0, ATTN_WIDTH, KV_RANK, IDX_HEADS * IDX_DIM, IDX_DIM, IDX_HEADS, SSM_INNER, CONV_CH, SSM_HEADS, 2 * D_MODEL))
    q, kv, qi, ki, wi, z, xbc, dt, gate = (w[:, int(o[n]):int(o[n + 1])] for n in range(9))
    zeros = lambda n: jnp.zeros((w.shape[0], n), w.dtype)
    small = jnp.concatenate([ki, wi, dt, zeros(128 - 88)], axis=1)
    packed = jnp.concatenate([q, kv, qi, small, zeros(COL_XBC - PREP_COLS), xbc, z, gate], axis=1)
    assert packed.shape[1] == PROJ_COLS
    return packed.astype(jnp.bfloat16)


def _in_proj_kernel(x_ref, g_ref, sc_ref, sh_ref, w_ref, o_ref, h_ref):
    @pl.when(pl.program_id(1) == 0)
    def _():
        x = x_ref[...]
        y = x * lax.rsqrt(jnp.mean(x * x, axis=-1, keepdims=True) + EPS) * g_ref[...]
        h_ref[...] = (y * (1.0 + sc_ref[0]) + sh_ref[0]).astype(jnp.bfloat16)
    o_ref[...] = jnp.dot(h_ref[...], w_ref[...], preferred_element_type=jnp.float32).astype(o_ref.dtype)


def _in_proj(x2, gain, sc, sh, w_packed, seq, tm=1024, tn=2048):
    t, d = x2.shape
    per_b = seq // tm
    return pl.pallas_call(
        _in_proj_kernel,
        grid=(t // tm, PROJ_COLS // tn),
        in_specs=[pl.BlockSpec((tm, d), lambda i, j: (i, 0)),
                  pl.BlockSpec((1, d), lambda i, j: (0, 0)),
                  pl.BlockSpec((1, 1, d), lambda i, j: (i // per_b, 0, 0)),
                  pl.BlockSpec((1, 1, d), lambda i, j: (i // per_b, 0, 0)),
                  pl.BlockSpec((d, tn), lambda i, j: (0, j))],
        out_specs=pl.BlockSpec((tm, tn), lambda i, j: (i, j)),
        out_shape=jax.ShapeDtypeStruct((t, PROJ_COLS), jnp.bfloat16),
        scratch_shapes=[pltpu.VMEM((tm, d), jnp.bfloat16)],
        compiler_params=pltpu.CompilerParams(dimension_semantics=("parallel", "arbitrary"),
                                             vmem_limit_bytes=VMEM_LIMIT),
        name="in_proj",
    )(x2, gain.reshape(1, d), sc[:, None, :], sh[:, None, :], w_packed)


def _head_rms_t(xt):
    x3 = xt.reshape(ATTN_HEADS, ATTN_HEAD_DIM, xt.shape[1])
    return lax.rsqrt(jnp.mean(x3 * x3, axis=1, keepdims=True) + EPS)


def _prep_kernel(p_ref, qg_ref, kvg_ref, wkv_ref, kg_ref, lng_ref, lnb_ref,
                 qT_ref, k_ref, vT_ref, qiT_ref, ki_ref, wT_ref, kn2_ref):
    n = p_ref.shape[0]
    f32 = jnp.float32
    q = p_ref[:, COL_Q:COL_Q + ATTN_WIDTH].astype(f32)
    lat = p_ref[:, COL_KV:COL_KV + KV_RANK].astype(f32)
    qi = p_ref[:, COL_QI:COL_QI + IDX_HEADS * IDX_DIM].astype(f32)
    sm = p_ref[:, COL_SMALL:COL_SMALL + 128].astype(f32)

    scale = ATTN_HEAD_DIM ** -0.5 * LOG2E
    qt = q.T
    qn = qt.reshape(ATTN_HEADS, ATTN_HEAD_DIM, n) * _head_rms_t(qt)
    qT_ref[0] = (qn.reshape(ATTN_WIDTH, n) * qg_ref[...] * scale).astype(jnp.bfloat16)

    latn = lat * lax.rsqrt(jnp.mean(lat * lat, axis=-1, keepdims=True) + EPS) * kvg_ref[...]
    kv = jnp.dot(latn.astype(jnp.bfloat16), wkv_ref[...], preferred_element_type=jnp.float32)
    kt = kv[:, :ATTN_WIDTH].T
    kn = (kt.reshape(ATTN_HEADS, ATTN_HEAD_DIM, n) * _head_rms_t(kt)).reshape(ATTN_WIDTH, n) * kg_ref[...]
    k_ref[0] = kn.T.astype(jnp.bfloat16)
    kn3 = kn.reshape(ATTN_HEADS, ATTN_HEAD_DIM, n)
    kn2_ref[0] = jnp.sum(kn3 * kn3, axis=1)
    vt = kv[:, ATTN_WIDTH:].T.reshape(ATTN_HEADS, ATTN_HEAD_DIM, n)
    ones = jnp.ones((ATTN_HEADS, VROWS - ATTN_HEAD_DIM, n), jnp.float32)
    vT_ref[0] = jnp.concatenate([vt, ones], axis=1).reshape(ATTN_HEADS * VROWS, n).astype(jnp.bfloat16)

    qiT_ref[0] = (qi * (IDX_DIM ** -0.5)).T.astype(jnp.bfloat16)

    lane = lax.broadcasted_iota(jnp.int32, sm.shape, 1)
    kid = jnp.where(lane < IDX_DIM, sm, pltpu.roll(sm, IDX_DIM, 1))
    mu = jnp.mean(kid, axis=-1, keepdims=True)
    var = jnp.mean(jnp.square(kid - mu), axis=-1, keepdims=True)
    ki_ref[0] = ((kid - mu) * lax.rsqrt(var + EPS) * lng_ref[...] + lnb_ref[...]).astype(jnp.bfloat16)

    wT_ref[0] = sm.T[SMALL_WI:SMALL_WI + IDX_HEADS, :] * (IDX_HEADS ** -0.5)


def _prep(proj, bsz, seq, q_norm, kv_norm, w_kv_up, k_norm, ln_w, ln_b, tp=512):
    nb = seq // tp
    tile8 = lambda g: jnp.tile(g, ATTN_HEADS).reshape(ATTN_WIDTH, 1)
    const = lambda shape: pl.BlockSpec(shape, lambda b, i: (0,) * len(shape))
    bf = jnp.bfloat16
    return pl.pallas_call(
        _prep_kernel,
        grid=(bsz, nb),
        in_specs=[pl.BlockSpec((tp, PREP_COLS), lambda b, i: (b * nb + i, 0)),
                  const((ATTN_WIDTH, 1)), const((1, KV_RANK)), const((KV_RANK, 2 * ATTN_WIDTH)),
                  const((ATTN_WIDTH, 1)), const((1, 128)), const((1, 128))],
        out_specs=[pl.BlockSpec((1, ATTN_WIDTH, tp), lambda b, i: (b, 0, i)),
                   pl.BlockSpec((1, tp, ATTN_WIDTH), lambda b, i: (b, i, 0)),
                   pl.BlockSpec((1, ATTN_HEADS * VROWS, tp), lambda b, i: (b, 0, i)),
                   pl.BlockSpec((1, ATTN_WIDTH, tp), lambda b, i: (b, 0, i)),
                   pl.BlockSpec((1, tp, 128), lambda b, i: (b, i, 0)),
                   pl.BlockSpec((1, IDX_HEADS, tp), lambda b, i: (b, 0, i)),
                   pl.BlockSpec((1, ATTN_HEADS, tp), lambda b, i: (b, 0, i))],
        out_shape=[jax.ShapeDtypeStruct((bsz, ATTN_WIDTH, seq), bf),
                   jax.ShapeDtypeStruct((bsz, seq, ATTN_WIDTH), bf),
                   jax.ShapeDtypeStruct((bsz, ATTN_HEADS * VROWS, seq), bf),
                   jax.ShapeDtypeStruct((bsz, ATTN_WIDTH, seq), bf),
                   jax.ShapeDtypeStruct((bsz, seq, 128), bf),
                   jax.ShapeDtypeStruct((bsz, IDX_HEADS, seq), jnp.float32),
                   jax.ShapeDtypeStruct((bsz, ATTN_HEADS, seq), jnp.float32)],
        compiler_params=pltpu.CompilerParams(dimension_semantics=("parallel", "parallel"),
                                             vmem_limit_bytes=VMEM_LIMIT),
        name="attn_prep",
    )(proj, tile8(q_norm), kv_norm.reshape(1, KV_RANK), w_kv_up.astype(bf), tile8(k_norm),
      jnp.tile(ln_w, 2).reshape(1, 128), jnp.tile(ln_b, 2).reshape(1, 128))


def _t5_bucket(dist):
    n = jnp.maximum(dist, 0)
    max_exact = N_BUCKETS // 2
    nf = jnp.maximum(n, 1).astype(jnp.float32)
    large = max_exact + (jnp.log(nf / max_exact) / math.log(MAX_DISTANCE / max_exact) * (N_BUCKETS - max_exact)).astype(jnp.int32)
    large = jnp.minimum(large, N_BUCKETS - 1)
    return jnp.where(n < max_exact, n, large)


def _bias_tables(rel_bias):
    s = jnp.arange(QB, dtype=jnp.int32)[None, :, None]
    q = jnp.arange(QB, dtype=jnp.int32)[None, None, :]
    dist = q - s + jnp.array([2 * QB, QB, 0], jnp.int32)[:, None, None]
    onehot = (_t5_bucket(dist)[..., None] == jnp.arange(N_BUCKETS, dtype=jnp.int32)).astype(jnp.float32)
    b = jnp.einsum('tsqb,bh->thsq', onehot, rel_bias.astype(jnp.float32) * LOG2E, precision=HIGHEST)
    return jnp.where((dist >= 0)[:, None], b, NEG)


def _attn_kernel(qT_ref, qiT_ref, wT_ref, k_ref, vT_ref, ki_ref, kn_ref, tab_ref, bst_ref, o_ref,
                 keys_ref, hi_ref, lo_ref, msk_ref, p_ref, acc_ref, mp_ref, m_ref, *, topk):
    i = pl.program_id(1)
    n_tiles = i + 1
    row_hi = lax.broadcasted_iota(jnp.int32, (128, QB), 0) >= 64

    def head_rows(ref, h):
        pair = ref[0, (h // 2) * 128:(h // 2) * 128 + 128, :]
        return jnp.where(row_hi == bool(h % 2), pair, jnp.zeros_like(pair))

    def tile_rows(kt):
        return pl.ds(pl.multiple_of(kt * QB, QB), QB)

    def score_tile(kt, carry):
        ki = ki_ref[0, tile_rows(kt), :]
        sc = jnp.zeros((QB, QB), jnp.float32)
        for h in range(IDX_HEADS):
            d = jnp.dot(ki, head_rows(qiT_ref, h), preferred_element_type=jnp.float32)
            sc = sc + wT_ref[0, h:h + 1, :] * jnp.maximum(d, 0.0)
        srow = lax.broadcasted_iota(jnp.int32, (QB, QB), 0)
        qcol = lax.broadcasted_iota(jnp.int32, (QB, QB), 1)
        sc = jnp.where(jnp.abs(sc) < TINY, 0.0, sc)
        sc = jnp.where((kt == i) & (srow > qcol), -jnp.inf, sc)
        bits = pltpu.bitcast(sc, jnp.int32)
        keys_ref[tile_rows(kt), :] = bits ^ ((bits >> 31) & 0x7FFFFFFF)
        hi_ref[tile_rows(kt), :] = pltpu.bitcast(bits & jnp.int32(-65536), jnp.float32).astype(jnp.bfloat16)
        return carry
    lax.fori_loop(0, n_tiles, score_tile, 0)

    def count_packed_ge(ref, cb):
        one, zero = jnp.ones((), jnp.bfloat16), jnp.zeros((), jnp.bfloat16)

        def body(kt, acc):
            hit = jnp.where(ref[tile_rows(kt), :] >= cb, one, zero)
            parts = [hit[r:r + 16, :] for r in range(0, QB, 16)]
            while len(parts) > 1:
                parts = [a + b for a, b in zip(parts[::2], parts[1::2])]
            return acc + parts[0]
        acc = lax.fori_loop(0, n_tiles, body, jnp.zeros((16, QB), jnp.bfloat16))
        return jnp.sum(acc.astype(jnp.float32), axis=0, keepdims=True)

    def count_hi_ge(cand16):
        b = cand16 ^ ((cand16 >> 15) & 0x7FFF)
        snap = jnp.where(((b & 0x8000) != 0) | ((b & 0x7F) == 0), 0, 0x0080)
        b = jnp.where((b & 0x7F80) == 0, snap, b)
        return count_packed_ge(hi_ref, pltpu.bitcast(b << 16, jnp.float32).astype(jnp.bfloat16))

    def mid_code(v):
        pat = jnp.where(v >= 16384, v - 16256, 0x8000 | (16511 - v))
        return pltpu.bitcast(pat << 16, jnp.float32)

    def count(hit_of_tile):
        def body(kt, acc):
            return acc + jnp.sum(hit_of_tile(kt).reshape(QB // 8, 8, QB), axis=0)
        acc = lax.fori_loop(0, n_tiles, body, jnp.zeros((8, QB), jnp.int32))
        return jnp.sum(acc, axis=0, keepdims=True)

    def count_ge(cand):
        return count(lambda kt: jnp.where(keys_ref[tile_rows(kt), :] >= cand, 1, 0))

    def hi_step(it, r):
        cand = jnp.where(it == 0, jnp.zeros_like(r), r | (1 << (15 - it)))
        return jnp.where(count_hi_ge(cand) >= topk, cand, r)
    r16 = lax.fori_loop(0, 16, hi_step, jnp.full((1, QB), -32768, jnp.int32))

    above = count_hi_ge(r16 + 1)

    def code_tile(kt, carry):
        key = keys_ref[tile_rows(kt), :]
        code = jnp.where((key >> 16) == r16, mid_code((key >> 1) & 0x7FFF), -jnp.inf)
        lo_ref[tile_rows(kt), :] = code.astype(jnp.bfloat16)
        return carry
    lax.fori_loop(0, n_tiles, code_tile, 0)

    def mid_step(it, v):
        cand = v | (1 << (14 - it))
        cnt = above + count_packed_ge(lo_ref, mid_code(cand).astype(jnp.bfloat16))
        return jnp.where(cnt >= topk, cand, v)
    v15 = lax.fori_loop(0, 15, mid_step, jnp.zeros((1, QB), jnp.int32))
    thr = (r16 << 16) | (v15 << 1)
    thr = jnp.where(count_ge(thr | 1) >= topk, thr | 1, thr)

    cnt_gt = count_ge(thr + 1)
    cnt_ge = count_ge(thr)
    need = topk - cnt_gt
    tie = (cnt_ge - cnt_gt > need) & (thr > KEY_NEG_INF)

    @pl.when(jnp.max(tie.astype(jnp.int32)) > 0)
    def _():
        def count_eq_below(cand):
            def ind(kt):
                idx = lax.broadcasted_iota(jnp.int32, (QB, QB), 0) + kt * QB
                return jnp.where((keys_ref[tile_rows(kt), :] == thr) & (idx < cand), 1, 0)
            return count(ind)

        def idx_step(it, r):
            cand = r | (1 << (15 - it))
            return jnp.where(count_eq_below(cand) < need, cand, r)
        last = lax.fori_loop(0, 16, idx_step, jnp.zeros((1, QB), jnp.int32))

        def drop(kt, carry):
            blk = keys_ref[tile_rows(kt), :]
            idx = lax.broadcasted_iota(jnp.int32, (QB, QB), 0) + kt * QB
            keys_ref[tile_rows(kt), :] = jnp.where(tie & (blk == thr) & (idx > last), INT_MIN, blk)
            return carry
        lax.fori_loop(0, n_tiles, drop, 0)

    def logits(kt, h):
        band = jnp.clip(kt - (i - 2), 0, 2)
        kp = k_ref[0, tile_rows(kt), (h // 2) * 128:(h // 2) * 128 + 128]
        s = jnp.dot(kp, head_rows(qT_ref, h), preferred_element_type=jnp.float32)
        return s + msk_ref[...] + tab_ref[band, h]

    def set_mask(kt):
        msk_ref[...] = jnp.where(keys_ref[tile_rows(kt), :] >= thr, 0.0, NEG)

    def max_tile(kt, carry):
        set_mask(kt)
        for h in range(ATTN_HEADS):
            s = logits(kt, h)
            mp_ref[h] = jnp.maximum(mp_ref[h], jnp.max(s.reshape(QB // 8, 8, QB), axis=0))
        return carry

    seq = kn_ref.shape[2]
    in_extent = lax.broadcasted_iota(jnp.int32, (ATTN_HEADS, seq), 1) < n_tiles * QB
    k_max = jnp.max(jnp.where(in_extent, kn_ref[0], 0.0), axis=1, keepdims=True)
    spread = jnp.zeros((1, QB), jnp.float32)
    for h in range(ATTN_HEADS):
        qh = qT_ref[0, h * ATTN_HEAD_DIM:(h + 1) * ATTN_HEAD_DIM, :].astype(jnp.float32)
        reach = jnp.sqrt(jnp.sum(qh * qh, axis=0, keepdims=True) * k_max[h:h + 1, :]) * NORM_SLACK
        m_ref[h:h + 1, :] = reach + bst_ref[0, h:h + 1, :]
        spread = jnp.maximum(spread, 2.0 * reach + bst_ref[1, h:h + 1, :])
    bound_ok = jnp.max(spread) <= MAX_SHIFT_ERROR

    @pl.when(jnp.logical_not(bound_ok))
    def _():
        mp_ref[...] = jnp.full(mp_ref.shape, NEG, jnp.float32)
        lax.fori_loop(0, n_tiles, max_tile, 0)
        for h in range(ATTN_HEADS):
            m_ref[h:h + 1, :] = jnp.max(mp_ref[h], axis=0, keepdims=True)
    m = [m_ref[h:h + 1, :] for h in range(ATTN_HEADS)]

    def exp_tile(kt, carry):
        set_mask(kt)
        for h in range(ATTN_HEADS):
            p_ref[h] = jnp.exp2(logits(kt, h) - m[h]).astype(jnp.bfloat16)
        for h in range(ATTN_HEADS):
            va = vT_ref[0, h * VROWS:(h + 1) * VROWS, tile_rows(kt)]
            acc_ref[h * VROWS:(h + 1) * VROWS, :] += jnp.dot(va, p_ref[h], preferred_element_type=jnp.float32)
        return carry

    acc_ref[...] = jnp.zeros(acc_ref.shape, jnp.float32)
    lax.fori_loop(0, n_tiles, exp_tile, 0)

    outs = [acc_ref[h * VROWS:h * VROWS + ATTN_HEAD_DIM, :] / acc_ref[h * VROWS + ATTN_HEAD_DIM:h * VROWS + ATTN_HEAD_DIM + 1, :]
            for h in range(ATTN_HEADS)]
    o_ref[0] = jnp.concatenate(outs, axis=0).T


def _dsa_attention(qT, qiT, wT, k, vT, ki2, kn2, rel_bias):
    bsz, _, seq = qT.shape
    topk = min(TOPK_MAX, seq // 4)
    assert seq % QB == 0 and topk <= QB
    assert seq // 16 <= 256
    b2 = rel_bias.astype(jnp.float32) * LOG2E
    bias_stats = jnp.stack([jnp.max(b2, axis=0), jnp.max(b2, axis=0) - jnp.min(b2, axis=0)])
    bias_stats = jnp.broadcast_to(bias_stats[:, :, None], (2, ATTN_HEADS, QB))
    return pl.pallas_call(
        functools.partial(_attn_kernel, topk=topk),
        grid=(bsz, seq // QB),
        in_specs=[
            pl.BlockSpec((1, ATTN_WIDTH, QB), lambda b, i: (b, 0, i)),
            pl.BlockSpec((1, IDX_HEADS * IDX_DIM, QB), lambda b, i: (b, 0, i)),
            pl.BlockSpec((1, IDX_HEADS, QB), lambda b, i: (b, 0, i)),
            pl.BlockSpec((1, seq, ATTN_WIDTH), lambda b, i: (b, 0, 0)),
            pl.BlockSpec((1, ATTN_HEADS * VROWS, seq), lambda b, i: (b, 0, 0)),
            pl.BlockSpec((1, seq, 128), lambda b, i: (b, 0, 0)),
            pl.BlockSpec((1, ATTN_HEADS, seq), lambda b, i: (b, 0, 0)),
            pl.BlockSpec((3, ATTN_HEADS, QB, QB), lambda b, i: (0, 0, 0, 0)),
            pl.BlockSpec((2, ATTN_HEADS, QB), lambda b, i: (0, 0, 0)),
        ],
        out_specs=pl.BlockSpec((1, QB, ATTN_WIDTH), lambda b, i: (b, i, 0)),
        out_shape=jax.ShapeDtypeStruct((bsz, seq, ATTN_WIDTH), jnp.float32),
        scratch_shapes=[
            pltpu.VMEM((seq, QB), jnp.int32),
            pltpu.VMEM((seq, QB), jnp.bfloat16),
            pltpu.VMEM((seq, QB), jnp.bfloat16),
            pltpu.VMEM((QB, QB), jnp.float32),
            pltpu.VMEM((ATTN_HEADS, QB, QB), jnp.bfloat16),
            pltpu.VMEM((ATTN_HEADS * VROWS, QB), jnp.float32),
            pltpu.VMEM((ATTN_HEADS, 8, QB), jnp.float32),
            pltpu.VMEM((ATTN_HEADS, QB), jnp.float32),
        ],
        compiler_params=pltpu.CompilerParams(dimension_semantics=("parallel", "arbitrary"),
                                             vmem_limit_bytes=VMEM_LIMIT),
        name="dsa_attention",
    )(qT, qiT, wT, k, vT, ki2, kn2, _bias_tables(rel_bias), bias_stats)


def _ssd_kernel(xbc_ref, z_ref, sm_ref, cw_ref, cb_ref, dtb_ref, a_ref, dsk_ref, nw_ref, y_ref, prev_ref, st_ref):
    q = SSD_CHUNK
    bf = jnp.bfloat16

    @pl.when(pl.program_id(1) == 0)
    def _():
        prev_ref[...] = jnp.zeros(prev_ref.shape, jnp.float32)
        st_ref[...] = jnp.zeros(st_ref.shape, jnp.float32)

    prev = prev_ref[...]
    for sub in range(xbc_ref.shape[0] // q):
        rows = slice(sub * q, (sub + 1) * q)
        f32 = jnp.float32
        prev = _ssd_chunk(xbc_ref[rows, :].astype(f32), prev, z_ref[rows, :].astype(f32),
                          sm_ref[rows, :].astype(f32), cw_ref, cb_ref, dtb_ref, a_ref,
                          dsk_ref, nw_ref, y_ref.at[rows, :], st_ref)
    prev_ref[...] = prev


def _ssd_chunk(cur, prev, z, sm, cw_ref, cb_ref, dtb_ref, a_ref, dsk_ref, nw_ref, y_ref, st_ref):
    q = SSD_CHUNK
    bf = jnp.bfloat16
    row = lax.broadcasted_iota(jnp.int32, cur.shape, 0)
    acc = cur * cw_ref[CONV_WIDTH - 1:CONV_WIDTH, :] + cb_ref[...]
    for s in range(1, CONV_WIDTH):
        shifted = jnp.where(row >= s, pltpu.roll(cur, s, 0), pltpu.roll(prev, s, 0))
        acc = acc + shifted * cw_ref[CONV_WIDTH - 1 - s:CONV_WIDTH - s, :]
    u = acc * jax.nn.sigmoid(acc)
    xs = u[:, :SSM_INNER]
    bm = u[:, SSM_INNER:SSM_INNER + SSM_GROUPS * SSM_STATE].astype(bf)
    cm = u[:, SSM_INNER + SSM_GROUPS * SSM_STATE:].astype(bf)

    t = sm + dtb_ref[...]
    dt = jnp.maximum(t, 0.0) + jnp.log1p(jnp.exp(-jnp.abs(t)))
    ii = lax.broadcasted_iota(jnp.int32, (q, q), 0)
    jj = lax.broadcasted_iota(jnp.int32, (q, q), 1)
    causal = ii >= jj
    acum = jnp.dot(causal.astype(jnp.float32), dt * a_ref[...], preferred_element_type=jnp.float32, precision=HIGHEST)
    acum_t = acum.T
    dt_t = dt.T
    ea = jnp.exp(acum)
    last = acum[q - 1:q, :]
    decay = jnp.exp(last - acum) * dt
    ea_last = jnp.exp(last)

    lane_hi = lax.broadcasted_iota(jnp.int32, (q, 128), 1) >= SSM_HEAD_DIM
    row_hi = lax.broadcasted_iota(jnp.int32, (128, SSM_STATE), 0) >= SSM_HEAD_DIM

    def pair_cols(v, e):
        c0, c1 = SMALL_DT + e, SMALL_DT + e + 1
        return jnp.where(lane_hi, v[:, c1:c1 + 1], v[:, c0:c0 + 1])

    for g in range(SSM_GROUPS):
        bg = bm[:, g * SSM_STATE:(g + 1) * SSM_STATE]
        cg = cm[:, g * SSM_STATE:(g + 1) * SSM_STATE]
        cb = lax.dot_general(cg, bg, NT, preferred_element_type=jnp.float32)
        for k in range(g * 4, g * 4 + 4):
            e = 2 * k
            x_pair = xs[:, k * 128:(k + 1) * 128]
            halves = []
            for h in (e, e + 1):
                c = SMALL_DT + h
                seg = acum[:, c:c + 1] - acum_t[c:c + 1, :]
                w = cb * jnp.exp(jnp.where(causal, seg, -jnp.inf)) * dt_t[c:c + 1, :]
                halves.append(jnp.dot(w.astype(bf), x_pair.astype(bf), preferred_element_type=jnp.float32))
            y_pair = jnp.where(lane_hi, halves[1], halves[0])
            state = st_ref[k]
            y_pair = y_pair + lax.dot_general(cg, state.astype(bf), NT, preferred_element_type=jnp.float32) * pair_cols(ea, e)
            y_ref[:, k * 128:(k + 1) * 128] = y_pair
            xd_t = (x_pair * pair_cols(decay, e)).T.astype(bf)
            c0 = SMALL_DT + e
            keep = jnp.where(row_hi, ea_last[:, c0 + 1:c0 + 2], ea_last[:, c0:c0 + 1])
            st_ref[k] = state * keep + jnp.dot(xd_t, bg, preferred_element_type=jnp.float32)

    y = (y_ref[...] + dsk_ref[...] * xs) * (z * jax.nn.sigmoid(z))
    half = SSM_INNER // SSM_GROUPS
    for g in range(SSM_GROUPS):
        yg = y[:, g * half:(g + 1) * half]
        yg = yg * lax.rsqrt(jnp.mean(yg * yg, axis=-1, keepdims=True) + EPS)
        y_ref[:, g * half:(g + 1) * half] = yg * nw_ref[:, g * half:(g + 1) * half]
    return cur


def _mamba2_ssd(proj, bsz, seq, conv_w, conv_b, dt_bias, a_log, d_skip, norm_w):
    q = SSD_STEP
    nc = seq // q
    lane_row = lambda v: jnp.zeros((1, 128), jnp.float32).at[0, SMALL_DT:SMALL_DT + SSM_HEADS].set(v)
    const = lambda shape: pl.BlockSpec(shape, lambda b, c: (0,) * len(shape))
    return pl.pallas_call(
        _ssd_kernel,
        grid=(bsz, nc),
        in_specs=[pl.BlockSpec((q, CONV_CH), lambda b, c: (b * nc + c, COL_XBC // CONV_CH)),
                  pl.BlockSpec((q, SSM_INNER), lambda b, c: (b * nc + c, COL_Z // SSM_INNER)),
                  pl.BlockSpec((q, 128), lambda b, c: (b * nc + c, COL_SMALL // 128)),
                  const((CONV_WIDTH, CONV_CH)), const((1, CONV_CH)), const((1, 128)), const((1, 128)),
                  const((1, SSM_INNER)), const((1, SSM_INNER))],
        out_specs=pl.BlockSpec((q, SSM_INNER), lambda b, c: (b * nc + c, 0)),
        out_shape=jax.ShapeDtypeStruct((bsz * seq, SSM_INNER), jnp.float32),
        scratch_shapes=[pltpu.VMEM((SSD_CHUNK, CONV_CH), jnp.float32),
                        pltpu.VMEM((SSM_HEADS // 2, 2 * SSM_HEAD_DIM, SSM_STATE), jnp.float32)],
        compiler_params=pltpu.CompilerParams(dimension_semantics=("parallel", "arbitrary"),
                                             vmem_limit_bytes=VMEM_LIMIT),
        name="mamba2_ssd",
    )(proj, proj, proj, conv_w, conv_b.reshape(1, CONV_CH), lane_row(dt_bias), lane_row(-jnp.exp(a_log)),
      jnp.repeat(d_skip, SSM_HEAD_DIM).reshape(1, SSM_INNER), norm_w.reshape(1, SSM_INNER))


def _mix_out_kernel(a_ref, s_ref, gl_ref, x_ref, gm_ref, wo_ref, ws_ref, wout_ref,
                    nf_ref, scf_ref, shf_ref, wr_ref, br_ref, xo_ref, h_ref, rt_ref, gt_ref, cnt_ref):
    bf = jnp.bfloat16
    ya = jnp.dot(a_ref[...].astype(bf), wo_ref[...], preferred_element_type=jnp.float32)
    ys = jnp.dot(s_ref[...].astype(bf), ws_ref[...], preferred_element_type=jnp.float32)
    gl = gl_ref[...].astype(jnp.float32)
    mixed = jax.nn.sigmoid(gl[:, :D_MODEL]) * ya + jax.nn.sigmoid(gl[:, D_MODEL:]) * ys
    x = x_ref[...] + gm_ref[0] * jnp.dot(mixed.astype(bf), wout_ref[...], preferred_element_type=jnp.float32)
    xo_ref[...] = x
    y = x * lax.rsqrt(jnp.mean(x * x, axis=-1, keepdims=True) + EPS) * nf_ref[...]
    h = y * (1.0 + scf_ref[0]) + shf_ref[0]
    h_hi = h.astype(bf)
    hb = pltpu.bitcast(h_hi.astype(jnp.float32), jnp.int32)
    half = D_MODEL // 2
    h_ref[...] = (hb[:, :half] & jnp.int32(-65536)) | lax.shift_right_logical(hb[:, half:], 16)
    h_lo = (h - h_hi.astype(jnp.float32)).astype(bf)
    dot = functools.partial(jnp.dot, preferred_element_type=jnp.float32)
    lg = dot(h_hi, wr_ref[0]) + (dot(h_lo, wr_ref[0]) + dot(h_hi, wr_ref[1])) + br_ref[...]

    tm = lg.shape[0]
    work = lg.T[:N_EXPERTS, :]
    row = lax.broadcasted_iota(jnp.int32, work.shape, 0).astype(jnp.float32)
    vals, eids, hits = [], [], []
    for _ in range(TOP_K):
        mx = jnp.max(work, axis=0, keepdims=True)
        ix = jnp.min(jnp.where(work == mx, row, float(N_EXPERTS)), axis=0, keepdims=True)
        vals.append(mx)
        eids.append(ix)
        hits.append(row == ix)
        work = jnp.where(hits[-1], -jnp.inf, work)
    ex = [jnp.exp(v - vals[0]) for v in vals]
    den = (ex[0] + ex[1]) + (ex[2] + ex[3])

    @pl.when(pl.program_id(0) == 0)
    def _():
        cnt_ref[...] = jnp.zeros(cnt_ref.shape, jnp.float32)
    chosen = jnp.zeros(work.shape, jnp.float32)
    for hit in hits:
        chosen = jnp.where(hit, 1.0, chosen)
    earlier = lax.broadcasted_iota(jnp.int32, (tm, tm), 0) < lax.broadcasted_iota(jnp.int32, (tm, tm), 1)
    before = dot(chosen.astype(bf), earlier.astype(bf)) + cnt_ref[:, 0:1]
    cnt_ref[...] = cnt_ref[...] + jnp.sum(chosen, axis=1, keepdims=True)

    slot = lax.broadcasted_iota(jnp.int32, (128, tm), 0)
    route = jnp.zeros((128, tm), jnp.float32)
    for k, hit in enumerate(hits):
        rank = jnp.sum(jnp.where(hit, before, 0.0), axis=0, keepdims=True)
        route = jnp.where(slot == k, eids[k], route)
        route = jnp.where(slot == TOP_K + k, ex[k] / den, route)
        route = jnp.where(slot == 2 * TOP_K + k, rank, route)
    rt_ref[...] = route[:ROUTE_ROWS, :]
    gt_ref[...] = route.T


def _mix_out(attn2, ssd2, proj, x2, g_m, w_attn_o, w_ssm_o, w_out, norm_ffn, sc_f, sh_f, w_router, b_router, seq, tm=512):
    t, d = x2.shape
    per_b = seq // tm
    bf = jnp.bfloat16
    const = lambda shape: pl.BlockSpec(shape, lambda i: (0,) * len(shape))
    perb = pl.BlockSpec((1, 1, d), lambda i: (i // per_b, 0, 0))
    wr = jnp.pad(w_router.astype(jnp.float32), ((0, 0), (0, 128 - N_EXPERTS)))
    wr_hi = wr.astype(bf)
    wr = jnp.stack([wr_hi, (wr - wr_hi.astype(jnp.float32)).astype(bf)])
    br =jnp.pad(b_router, (0, 128 - N_EXPERTS)).reshape(1, 128)
    return pl.pallas_call(
        _mix_out_kernel,
        grid=(t // tm,),
        in_specs=[pl.BlockSpec((tm, ATTN_WIDTH), lambda i: (i, 0)),
                  pl.BlockSpec((tm, SSM_INNER), lambda i: (i, 0)),
                  pl.BlockSpec((tm, 2 * d), lambda i: (i, COL_GATE // (2 * d))),
                  pl.BlockSpec((tm, d), lambda i: (i, 0)),
                  perb,
                  const((ATTN_WIDTH, d)), const((SSM_INNER, d)), const((d, d)),
                  const((1, d)), perb, perb, const((2, d, 128)), const((1, 128))],
        out_specs=[pl.BlockSpec((tm, d), lambda i: (i, 0)),
                   pl.BlockSpec((tm, d // 2), lambda i: (i, 0)),
                   pl.BlockSpec((ROUTE_ROWS, tm), lambda i: (0, i)),
                   pl.BlockSpec((tm, 128), lambda i: (i, 0)),
                   pl.BlockSpec((N_EXPERTS, 128), lambda i: (0, 0))],
        out_shape=[jax.ShapeDtypeStruct((t, d), jnp.float32),
                   jax.ShapeDtypeStruct((t, d // 2), jnp.int32),
                   jax.ShapeDtypeStruct((ROUTE_ROWS, t), jnp.float32),
                   jax.ShapeDtypeStruct((t, 128), jnp.float32),
                   jax.ShapeDtypeStruct((N_EXPERTS, 128), jnp.float32)],
        compiler_params=pltpu.CompilerParams(dimension_semantics=("arbitrary",), vmem_limit_bytes=VMEM_LIMIT),
        name="mix_out",
    )(attn2, ssd2, proj, x2, g_m[:, None, :], w_attn_o.astype(bf), w_ssm_o.astype(bf), w_out.astype(bf),
      norm_ffn.reshape(1, d), sc_f[:, None, :], sh_f[:, None, :], wr, br)


def _moe_kernel(be_ref, nb_ref, x_ref, wgu_ref, bgu_ref, wdn_ref, bdn_ref, *rest, first_block):
    o_ref, wgu_bf, wdn_bf = rest[-3:]
    i = pl.program_id(0)
    j = i + first_block

    @pl.when((i == 0) | (be_ref[j] != be_ref[jnp.maximum(j - 1, 0)]))
    def _():
        wgu_bf[...] = wgu_ref[0, 0].astype(jnp.bfloat16)
        wdn_bf[...] = wdn_ref[0, 0].astype(jnp.bfloat16)

    @pl.when(j < nb_ref[0])
    def _():
        words = x_ref[...]
        x_hi = pltpu.bitcast(words & jnp.int32(-65536), jnp.float32).astype(jnp.bfloat16)
        x_lo = pltpu.bitcast(words << 16, jnp.float32).astype(jnp.bfloat16)
        x = jnp.concatenate([x_hi, x_lo], axis=1)
        gu = jnp.dot(x, wgu_bf[...], preferred_element_type=jnp.float32) + bgu_ref[0, 0]
        g = jnp.minimum(gu[:, :D_EXPERT], SWIGLU_LIMIT)
        u = jnp.clip(gu[:, D_EXPERT:], -SWIGLU_LIMIT, SWIGLU_LIMIT)
        act = (u + 1.0) * (g * jax.nn.sigmoid(SWIGLU_ALPHA * g))
        out = jnp.dot(act.astype(jnp.bfloat16), wdn_bf[...], preferred_element_type=jnp.float32) + bdn_ref[0, 0]
        o_ref[...] = out.astype(o_ref.dtype)

    @pl.when(j >= nb_ref[0])
    def _():
        o_ref[...] = jnp.zeros_like(o_ref)


def _moe_ffn(xs_parts, blk_exp, n_used, w_gu, b_gu, w_dn, b_dn, layer):
    d = D_MODEL
    tm = MOE_TM
    n_rows = sum(xs.shape[0] for xs in xs_parts)
    out, first = None, 0
    for xs in xs_parts:
        nblk = xs.shape[0] // tm
        wmap = lambda i, be, nb, first=first: (layer, be[i + first], 0, 0)
        in_specs = [pl.BlockSpec((tm, d // 2), lambda i, be, nb: (i, 0)),
                    pl.BlockSpec((1, 1, d, 2 * D_EXPERT), wmap),
                    pl.BlockSpec((1, 1, 1, 2 * D_EXPERT), wmap),
                    pl.BlockSpec((1, 1, D_EXPERT, d), wmap),
                    pl.BlockSpec((1, 1, 1, d), wmap)]
        args = [blk_exp, n_used, xs, w_gu, b_gu[:, :, None, :], w_dn, b_dn[:, :, None, :]]
        aliases = {}
        if out is not None:
            in_specs.append(pl.BlockSpec(memory_space=pl.ANY))
            args.append(out)
            aliases = {len(args) - 1: 0}
        grid_spec = pltpu.PrefetchScalarGridSpec(
            num_scalar_prefetch=2,
            grid=(nblk,),
            in_specs=in_specs,
            out_specs=pl.BlockSpec((tm, d), lambda i, be, nb, first=first: (i + first, 0)),
            scratch_shapes=[pltpu.VMEM((d, 2 * D_EXPERT), jnp.bfloat16), pltpu.VMEM((D_EXPERT, d), jnp.bfloat16)],
        )
        out = pl.pallas_call(
            functools.partial(_moe_kernel, first_block=first),
            grid_spec=grid_spec,
            out_shape=jax.ShapeDtypeStruct((n_rows, d), jnp.bfloat16),
            input_output_aliases=aliases,
            compiler_params=pltpu.CompilerParams(dimension_semantics=("arbitrary",), vmem_limit_bytes=VMEM_LIMIT),
            name="moe_ffn",
        )(*args)
        first += nblk
    return out


def _moe(h2, route, expert_counts, w_gu, b_gu, w_dn, b_dn, layer):
    t = h2.shape[0]
    d = D_MODEL
    tm = MOE_TM
    i32 = jnp.int32
    experts = jnp.arange(N_EXPERTS, dtype=i32)
    top_idx = route[:TOP_K].astype(i32)
    rank = route[2 * TOP_K:3 * TOP_K].astype(i32)
    n_assign = t * TOP_K
    n_rows = n_assign + N_EXPERTS * tm
    e_flat = top_idx.T.reshape(n_assign)
    counts = expert_counts[:, 0].astype(i32)
    padded = (counts + tm - 1) // tm * tm
    pad_start = jnp.cumsum(padded) - padded
    dest = rank + jnp.sum(jnp.where(top_idx[..., None] == experts, pad_start, 0), axis=-1)
    dest = dest.reshape(-1)
    filler_exp = jnp.repeat(experts, tm)
    filler_key = jnp.where(jnp.tile(jnp.arange(tm, dtype=i32), N_EXPERTS) < jnp.repeat(padded - counts, tm),
                           filler_exp, N_EXPERTS)
    rows = jnp.arange(n_rows, dtype=i32)
    assert n_rows < (1 << ROW_BITS) and (N_EXPERTS + 1) << ROW_BITS < (1 << 31)
    packed = lax.sort((jnp.concatenate([e_flat, filler_key]) << ROW_BITS) | rows)
    row_key, row_src = packed >> ROW_BITS, packed & ((1 << ROW_BITS) - 1)
    row_tok = jnp.where(row_src < n_assign, row_src // TOP_K, rows % t)
    blk_exp = jnp.minimum(row_key[::tm], N_EXPERTS - 1)
    n_used = (jnp.sum(padded, keepdims=True) // tm).astype(i32)
    slab = n_rows // MOE_SLABS
    xs_parts = [h2[row_tok[s * slab:(s + 1) * slab]] for s in range(MOE_SLABS)]
    out = _moe_ffn(xs_parts, blk_exp, n_used, w_gu, b_gu, w_dn, b_dn, layer)
    return out[dest].reshape(TOP_K, t, d)


def _combine_kernel(p_ref, r_ref, x_ref, g_ref, o_ref):
    f32 = jnp.float32
    w = [r_ref[:, TOP_K + k:TOP_K + k + 1] for k in range(TOP_K)]
    y = (w[0] * p_ref[0].astype(f32) + w[1] * p_ref[1].astype(f32)) + (w[2] * p_ref[2].astype(f32) + w[3] * p_ref[3].astype(f32))
    o_ref[...] = x_ref[...] + g_ref[0] * y


def _combine(parts, route_tok, x2, g_f, seq, tm=512):
    t, d = x2.shape
    per_b = seq // tm
    return pl.pallas_call(
        _combine_kernel,
        grid=(t // tm,),
        in_specs=[pl.BlockSpec((TOP_K, tm, d), lambda i: (0, i, 0)),
                  pl.BlockSpec((tm, 128), lambda i: (i, 0)),
                  pl.BlockSpec((tm, d), lambda i: (i, 0)),
                  pl.BlockSpec((1, 1, d), lambda i: (i // per_b, 0, 0))],
        out_specs=pl.BlockSpec((tm, d), lambda i: (i, 0)),
        out_shape=jax.ShapeDtypeStruct((t, d), jnp.float32),
        compiler_params=pltpu.CompilerParams(dimension_semantics=("parallel",), vmem_limit_bytes=VMEM_LIMIT),
        name="moe_combine",
    )(parts, route_tok, x2, g_f[:, None, :])


def kernel(x, c, rel_bias, w_ada, b_ada, norm_mix, norm_ffn, w_in, kv_norm, w_kv_up, q_norm, k_norm,
           idx_k_ln_w, idx_k_ln_b, w_attn_o, conv_w, conv_b, dt_bias, a_log, d_skip, ssm_norm, w_ssm_o,
           w_out, w_router, b_router, w_gu, b_gu, w_dn, b_dn):
    bsz, seq, d = x.shape
    t = bsz * seq
    cond = jax.nn.silu(c)
    x2 = x.reshape(t, d)
    for l in range(DEPTH):
        mod = cond @ w_ada[l] + b_ada[l]
        sh_m, sc_m, g_m, sh_f, sc_f, g_f = jnp.split(mod, 6, axis=-1)
        proj = _in_proj(x2, norm_mix[l], sc_m, sh_m, _pack_w_in(w_in[l]), seq)
        qT, k, vT, qiT, ki2, wT, kn2 = _prep(proj, bsz, seq, q_norm[l], kv_norm[l], w_kv_up[l], k_norm[l],
                                             idx_k_ln_w[l], idx_k_ln_b[l])
        attn = _dsa_attention(qT, qiT, wT, k, vT, ki2, kn2, rel_bias)
        y_ssd = _mamba2_ssd(proj, bsz, seq, conv_w[l], conv_b[l], dt_bias[l], a_log[l], d_skip[l], ssm_norm[l])
        x2, h2, route, route_tok, expert_counts = _mix_out(
            attn.reshape(t, ATTN_WIDTH), y_ssd, proj, x2, g_m, w_attn_o[l], w_ssm_o[l], w_out[l], norm_ffn[l],
            sc_f, sh_f, w_router[l], b_router[l], seq)
        parts = _moe(h2, route, expert_counts, w_gu, b_gu, w_dn, b_dn, l)
        x2 = _combine(parts, route_tok, x2, g_f, seq)
    return x2.reshape(bsz, seq, d)
```
